```python
import math, functools
import jax, jax.numpy as jnp
from jax import lax
import numpy as np

D_MODEL = 2048
BATCH = 4
SEQ = 2048
DEPTH = 2

CTX_LEN = 256
GRID_W = 64
EPS = 1e-6
SHORT_CONV = 3

A_DK = 128
A_DV = 128
A_HEADS = D_MODEL // 128
A_QK = A_HEADS * A_DK
A_WIDTH = A_HEADS * A_DV
DN_CHUNK = 64
B_WIDTH = D_MODEL
C_WIDTH = D_MODEL
C_GROUPS = D_MODEL // 128
C_CHUNK = 128
D_WIDTH = D_MODEL
D_GROUPS = 4
D_GW = D_WIDTH // D_GROUPS
POOL_RADII = (1, 2, 4, 8)

N_EVEN = (DEPTH + 1) // 2
N_ODD = DEPTH // 2

QKV_W = 2 * A_QK + A_WIDTH
EVEN_SPLITS = (QKV_W, QKV_W + A_WIDTH, QKV_W + A_WIDTH + 4 * A_HEADS,
               QKV_W + A_WIDTH + 4 * A_HEADS + B_WIDTH,
               QKV_W + A_WIDTH + 4 * A_HEADS + 2 * B_WIDTH,
               QKV_W + A_WIDTH + 4 * A_HEADS + 3 * B_WIDTH)
EVEN_COLS = QKV_W + A_WIDTH + 4 * A_HEADS + 4 * B_WIDTH
ODD_SPLITS = (2 * C_WIDTH, 3 * C_WIDTH, 3 * C_WIDTH + D_WIDTH)
ODD_COLS = 3 * C_WIDTH + 2 * D_WIDTH

kernel_name = 'hybrid_deltanet_shortconv_chunkmlp_pool_prefix_dit'


def _rmsnorm(x, w):
    xf = x.astype(jnp.float32)
    y = xf * lax.rsqrt(jnp.mean(xf * xf, axis=-1, keepdims=True) + EPS)
    return (y * w).astype(x.dtype)


def _layernorm(x, w, b):
    xf = x.astype(jnp.float32)
    mu = jnp.mean(xf, axis=-1, keepdims=True)
    xc = xf - mu
    var = jnp.mean(xc * xc, axis=-1, keepdims=True)
    return (xc * lax.rsqrt(var + EPS) * w + b).astype(x.dtype)


def _l2norm(x):
    return x * lax.rsqrt(jnp.sum(x * x, axis=-1, keepdims=True) + EPS)


def _dwconv(t, w):
    K = w.shape[0]
    r = K // 2
    L = t.shape[-2]
    tp = jnp.pad(t, [(0, 0)] * (t.ndim - 2) + [(r, r), (0, 0)])
    out = tp[..., 0:L, :] * w[0]
    for i in range(1, K):
        out = out + tp[..., i:i + L, :] * w[i]
    return out


def _centred_mean(t, r):
    L = t.shape[-2]
    cs = jnp.cumsum(t.astype(jnp.float32), axis=-2)
    cs = jnp.concatenate([jnp.zeros_like(cs[..., :1, :]), cs], axis=-2)
    idx = jnp.arange(L)
    hi = jnp.minimum(idx + r + 1, L)
    lo = jnp.maximum(idx - r, 0)
    s = cs[..., hi, :] - cs[..., lo, :]
    cnt = (hi - lo).astype(jnp.float32)[:, None]
    return (s / cnt).astype(t.dtype)


def _on_rows(fn, t, arg):
    Bn, L, C = t.shape
    rows = L // GRID_W
    return fn(t.reshape(Bn, rows, GRID_W, C), arg).reshape(Bn, L, C)


_conv_latent = functools.partial(_on_rows, _dwconv)
_mean_latent = functools.partial(_on_rows, _centred_mean)


def _gdn_chunk_scan(q, k, v, g, beta, s0):
    Bn, H, L, dk = q.shape
    dv = v.shape[-1]
    C = DN_CHUNK
    n = L // C
    q = q * (dk ** -0.5)
    kb = k * beta[..., None]
    vb = v * beta[..., None]
    rs = lambda t: t.reshape((Bn, H, n, C) + t.shape[3:])
    q, k, kb, vb, g = rs(q), rs(k), rs(kb), rs(vb), rs(g)
    gc = jnp.cumsum(g, axis=-1)
    tril = jnp.tril(jnp.ones((C, C), dtype=bool))
    strict = jnp.tril(jnp.ones((C, C), dtype=bool), -1)
    decay = jnp.exp(jnp.where(tril, gc[..., :, None] - gc[..., None, :], -jnp.inf))
    m = jnp.where(strict, jnp.einsum('bhnid,bhnjd->bhnij', kb, k) * decay, 0.0)
    eye = jnp.eye(C, dtype=jnp.float32)
    t_inv = lax.linalg.triangular_solve(eye + m, jnp.broadcast_to(eye, m.shape),
                                        left_side=True, lower=True, unit_diagonal=True)
    w = jnp.einsum('bhnij,bhnjd->bhnid', t_inv, kb * jnp.exp(gc)[..., None])
    u = jnp.einsum('bhnij,bhnjd->bhnid', t_inv, vb)
    a_intra = jnp.where(tril, jnp.einsum('bhnid,bhnjd->bhnij', q, k) * decay, 0.0)
    qg = q * jnp.exp(gc)[..., None]
    kg = k * jnp.exp(gc[..., -1:] - gc)[..., None]
    g_last = jnp.exp(gc[..., -1])

    def step(S, xs):
        w_i, u_i, a_i, qg_i, kg_i, gl_i = xs
        v_new = u_i - w_i @ S
        o = qg_i @ S + a_i @ v_new
        S = S * gl_i[..., None, None] + jnp.swapaxes(kg_i, -1, -2) @ v_new
        return S, o

    mv = lambda t: jnp.moveaxis(t, 2, 0)
    S, o = lax.scan(step, s0, (mv(w), mv(u), mv(a_intra), mv(qg), mv(kg), mv(g_last)))
    o = jnp.moveaxis(o, 0, 2).reshape(Bn, H, L, dv)
    return o, S


def _even_stream(p, conv, w_conv_qkv, a_log, dt_bias, w_conv_b):
    qkv, z_a, ab, b_g, c_g, h_b, z_b = jnp.split(p, EVEN_SPLITS, axis=-1)
    Bn, L, _ = p.shape
    qkv = jax.nn.silu(conv(qkv, w_conv_qkv))
    q, k, v = jnp.split(qkv, (A_QK, 2 * A_QK), axis=-1)
    heads = lambda t, d: jnp.swapaxes(t.reshape(Bn, L, A_HEADS, d), 1, 2).astype(jnp.float32)
    q = _l2norm(heads(q, A_DK))
    k = _l2norm(heads(k, A_DK))
    v = heads(v, A_DV)
    ab = jnp.transpose(ab.astype(jnp.float32).reshape(Bn, L, 4, A_HEADS), (2, 0, 3, 1))
    a_raw, b_raw = ab[:2], ab[2:]
    g = -jnp.exp(a_log.astype(jnp.float32))[:, None, :, None] * jax.nn.softplus(
        a_raw + dt_bias.astype(jnp.float32)[:, None, :, None])
    beta = jax.nn.sigmoid(b_raw)
    y_b = b_g * conv(c_g * h_b, w_conv_b) * jax.nn.silu(z_b)
    return (q, k, v, g, beta), z_a, y_b


def _even_mixer(h_x, h_c, w_in, w_conv_qkv, a_log, dt_bias, head_norm, w_conv_b, w_out):
    (qx, kx, vx, gx, bx), zx, ybx = _even_stream(h_x @ w_in, _conv_latent, w_conv_qkv, a_log, dt_bias, w_conv_b)
    (qc, kc, vc, gcx, bc), zc, ybc = _even_stream(h_c @ w_in, _dwconv, w_conv_qkv, a_log, dt_bias, w_conv_b)
    Bn = h_x.shape[0]
    o_x = 0.0
    o_c = 0.0
    for d in range(2):
        f = (lambda t: jnp.flip(t, axis=2)) if d == 1 else (lambda t: t)
        s0 = jnp.zeros((Bn, A_HEADS, A_DK, A_DV), jnp.float32)
        oc_d, s_ctx = _gdn_chunk_scan(f(qc), f(kc), f(vc), f(gcx[d]), f(bc[d]), s0)
        ox_d, _ = _gdn_chunk_scan(f(qx), f(kx), f(vx), f(gx[d]), f(bx[d]), s_ctx)
        o_c = o_c + f(oc_d)
        o_x = o_x + f(ox_d)

    def finish(o, z, y_b):
        Bq, H, L, dv = o.shape
        o = _rmsnorm(o, head_norm)
        o = jnp.swapaxes(o, 1, 2).reshape(Bq, L, A_WIDTH).astype(z.dtype) * jax.nn.silu(z)
        return jnp.concatenate([o, y_b], axis=-1) @ w_out

    return finish(o_x, zx, ybx), finish(o_c, zc, ybc)


def _chunk_token_mix(u, v, w_s, b_s):
    Bn, L, _ = v.shape
    n = L // C_CHUNK
    vr = v.reshape(Bn, n, C_CHUNK, C_GROUPS, C_WIDTH // C_GROUPS)
    s = jnp.einsum('gpq,bnqgc->bnpgc', w_s, vr) + b_s.T[:, :, None]
    return u * s.reshape(Bn, L, C_WIDTH)


def _multi_scale_pool(p, mean_fn, pool_w, pool_scale):
    groups = jnp.split(p, D_GROUPS, axis=-1)
    diffs = jnp.stack([mean_fn(gp, r) - gp for gp, r in zip(groups, POOL_RADII)], axis=-2)
    y = jnp.einsum('blgc,gcd->blgd', diffs, pool_w).reshape(p.shape)
    return y * pool_scale


def _odd_stream(p, mean_fn, ln_w, ln_b, w_s, b_s, pool_w, pool_scale, w_out):
    uv, z_c, p_d, z_d = jnp.split(p, ODD_SPLITS, axis=-1)
    u, v = jnp.split(jax.nn.gelu(uv), 2, axis=-1)
    y_c = _chunk_token_mix(u, _layernorm(v, ln_w, ln_b), w_s, b_s) * jax.nn.silu(z_c)
    y_d = _multi_scale_pool(p_d, mean_fn, pool_w, pool_scale) * jax.nn.silu(z_d)
    return jnp.concatenate([y_c, y_d], axis=-1) @ w_out


def setup_inputs(seed: int = 0) -> dict:
    key = jax.random.key(seed)
    ks = jax.random.split(key, 24)
    nrm = lambda k, shape, s: jax.random.normal(k, shape, jnp.float32) * s
    D = D_MODEL
    x = nrm(ks[0], (BATCH, SEQ, D), 1.0)
    c = nrm(ks[1], (BATCH, D), 1.0)
    ctx = nrm(ks[2], (BATCH, CTX_LEN, D), 1.0)
    c_ctx = nrm(ks[3], (D,), 1.0)
    ada_w = nrm(ks[4], (DEPTH, D, 3 * D), 0.5 * D ** -0.5)
    ada_b = nrm(ks[5], (DEPTH, 3 * D), 0.02)
    norm_w = 1.0 + nrm(ks[6], (DEPTH, D), 0.02)
    e_w_in = nrm(ks[7], (N_EVEN, D, EVEN_COLS), D ** -0.5)
    e_conv_qkv = nrm(ks[8], (N_EVEN, SHORT_CONV, QKV_W), SHORT_CONV ** -0.5)
    e_a_log = jnp.log(jax.random.uniform(ks[9], (N_EVEN, 2, A_HEADS), jnp.float32, 1.0, 16.0))
    dt = jnp.exp(jax.random.uniform(ks[10], (N_EVEN, 2, A_HEADS), jnp.float32, math.log(1e-3), math.log(0.1)))
    e_dt_bias = dt + jnp.log(-jnp.expm1(-dt))
    e_head_norm = 1.0 + nrm(ks[11], (N_EVEN, A_DV), 0.02)
    e_conv_b = nrm(ks[12], (N_EVEN, SHORT_CONV, B_WIDTH), SHORT_CONV ** -0.5)
    e_w_out = nrm(ks[13], (N_EVEN, A_WIDTH + B_WIDTH, D), (A_WIDTH + B_WIDTH) ** -0.5)
    o_w_in = nrm(ks[14], (N_ODD, D, ODD_COLS), D ** -0.5)
    o_ln_w = 1.0 + nrm(ks[15], (N_ODD, C_WIDTH), 0.02)
    o_ln_b = nrm(ks[16], (N_ODD, C_WIDTH), 0.02)
    o_w_s = nrm(ks[17], (N_ODD, C_GROUPS, C_CHUNK, C_CHUNK), C_CHUNK ** -0.5)
    o_b_s = nrm(ks[18], (N_ODD, C_GROUPS, C_CHUNK), 0.02)
    o_pool_w = nrm(ks[19], (N_ODD, D_GROUPS, D_GW, D_GW), D_GW ** -0.5)
    o_pool_scale = 1.0 + nrm(ks[20], (N_ODD, D_WIDTH), 0.1)
    o_w_out = nrm(ks[21], (N_ODD, C_WIDTH + D_WIDTH, D), (C_WIDTH + D_WIDTH) ** -0.5)
    final_norm_w = 1.0 + nrm(ks[22], (D,), 0.02)
    return {'x': x, 'c': c, 'ctx': ctx, 'c_ctx': c_ctx, 'ada_w': ada_w, 'ada_b': ada_b, 'norm_w': norm_w,
            'e_w_in': e_w_in, 'e_conv_qkv': e_conv_qkv, 'e_a_log': e_a_log, 'e_dt_bias': e_dt_bias,
            'e_head_norm': e_head_norm, 'e_conv_b': e_conv_b, 'e_w_out': e_w_out,
            'o_w_in': o_w_in, 'o_ln_w': o_ln_w, 'o_ln_b': o_ln_b, 'o_w_s': o_w_s, 'o_b_s': o_b_s,
            'o_pool_w': o_pool_w, 'o_pool_scale': o_pool_scale, 'o_w_out': o_w_out,
            'final_norm_w': final_norm_w}


def reference(x, c, ctx, c_ctx, ada_w, ada_b, norm_w, e_w_in, e_conv_qkv, e_a_log, e_dt_bias, e_head_norm,
              e_conv_b, e_w_out, o_w_in, o_ln_w, o_ln_b, o_w_s, o_b_s, o_pool_w, o_pool_scale, o_w_out,
              final_norm_w):
    for layer in range(DEPTH):
        last = layer == DEPTH - 1
        even = layer % 2 == 0
        i = layer // 2
        need_ctx = even or not last
        mod_x = jax.nn.silu(c) @ ada_w[layer] + ada_b[layer]
        sh_x, sc_x, gt_x = jnp.split(mod_x[:, None, :], 3, axis=-1)
        h_x = _rmsnorm(x, norm_w[layer]) * (1.0 + sc_x) + sh_x
        if need_ctx:
            mod_c = jax.nn.silu(c_ctx) @ ada_w[layer] + ada_b[layer]
            sh_c, sc_c, gt_c = jnp.split(mod_c, 3, axis=-1)
            h_c = _rmsnorm(ctx, norm_w[layer]) * (1.0 + sc_c) + sh_c
        if even:
            y_x, y_c = _even_mixer(h_x, h_c, e_w_in[i], e_conv_qkv[i], e_a_log[i], e_dt_bias[i],
                                   e_head_norm[i], e_conv_b[i], e_w_out[i])
        else:
            odd_args = (o_ln_w[i], o_ln_b[i], o_w_s[i], o_b_s[i], o_pool_w[i], o_pool_scale[i], o_w_out[i])
            y_x = _odd_stream(h_x @ o_w_in[i], _mean_latent, *odd_args)
            if need_ctx:
                y_c = _odd_stream(h_c @ o_w_in[i], _centred_mean, *odd_args)
        x = x + gt_x * y_x
        if not last:
            ctx = ctx + gt_c * y_c
    return _rmsnorm(x, final_norm_w)
```

```python
import functools
import math

import numpy as np
import jax
import jax.numpy as jnp
from jax import lax
from jax.experimental import pallas as pl
from jax.experimental.pallas import tpu as pltpu

F32 = jnp.float32
BF16 = jnp.bfloat16

D_MODEL = 2048
BATCH = 4
SEQ = 2048
CTX_LEN = 256
GRID_W = 64
EPS = 1e-6
HEADS = 16
HEAD_DIM = 128
QKV_W = 3 * D_MODEL
CHUNK = 64
TOK_CHUNK = 128
POOL_GROUPS = 4
POOL_GW = D_MODEL // POOL_GROUPS
POOL_RADII = (1, 2, 4, 8)

LANES = 128
VMEM_LIMIT = 56 * 1024 * 1024


def _params(sem):
    return pltpu.CompilerParams(dimension_semantics=sem, vmem_limit_bytes=VMEM_LIMIT)


def _sigmoid(x):
    return 1.0 / (1.0 + jnp.exp(-x))


def _silu(x):
    return x * _sigmoid(x)


def _gelu_tanh(x):
    cdf = 0.5 * (1.0 + jnp.tanh(math.sqrt(2.0 / math.pi) * (x + 0.044715 * (x * x * x))))
    return x * cdf


def _modulated_norm(x, nw, sc, sh):
    ms = jnp.mean(x * x, axis=-1, keepdims=True)
    y = x * lax.rsqrt(ms + EPS) * nw
    return y * (1.0 + sc) + sh


def _row_conv3(p, cw, group):
    tm = p.shape[0]
    row = lax.broadcasted_iota(jnp.int32, (tm, 1), 0) % group
    prev = jnp.where(row == 0, 0.0, pltpu.roll(p, 1, 0))
    nxt = jnp.where(row == group - 1, 0.0, pltpu.roll(p, tm - 1, 0))
    return prev * cw[0:1] + p * cw[1:2] + nxt * cw[2:3]


def _adaln_kernel(c_ref, w_ref, b_ref, o_ref):
    a = _silu(c_ref[...]).astype(BF16)
    o_ref[0] = jnp.dot(a, w_ref[0].astype(BF16), preferred_element_type=F32) + b_ref[0]


def _adaln(cc, ada_w, ada_b):
    depth, d, n = ada_w.shape
    tn = 1024
    return pl.pallas_call(
        _adaln_kernel,
        grid=(depth, n // tn),
        in_specs=[pl.BlockSpec((8, d), lambda l, j: (0, 0)),
                  pl.BlockSpec((1, d, tn), lambda l, j: (l, 0, j)),
                  pl.BlockSpec((1, 1, tn), lambda l, j: (l, 0, j))],
        out_specs=pl.BlockSpec((1, 8, tn), lambda l, j: (l, 0, j)),
        out_shape=jax.ShapeDtypeStruct((depth, 8, n), F32),
        compiler_params=_params(("parallel", "parallel")),
        name="adaln",
    )(cc, ada_w, ada_b.reshape(depth, 1, n))


def _qkv_kernel(x_ref, nw_ref, sh_ref, sc_ref, w_ref, wg_ref, cw_ref, qkv_ref, g_ref, h_scr,
                *, group, n_tiles):
    j = pl.program_id(1)

    @pl.when(j == 0)
    def _():
        h_scr[...] = _modulated_norm(x_ref[...], nw_ref[...], sc_ref[0], sh_ref[0]).astype(BF16)

    @pl.when(j < n_tiles)
    def _():
        p = jnp.dot(h_scr[...], w_ref[...], preferred_element_type=F32)
        y = _silu(_row_conv3(p, cw_ref[...], group))
        for hh in range(y.shape[1] // HEAD_DIM):
            qkv_ref[hh] = y[:, hh * HEAD_DIM:(hh + 1) * HEAD_DIM]

    @pl.when(j == n_tiles)
    def _():
        g_ref[...] = jnp.dot(h_scr[...], wg_ref[...], preferred_element_type=F32)


def _qkv_proj(xin, mods, nw, wqkv, wg, cw, *, rows_per_batch, mod_row0, group):
    r, d = xin.shape
    tm, tn = 1024, 768
    n_tiles = QKV_W // tn
    hpt = tn // HEAD_DIM
    tiles_per_batch = max(rows_per_batch // tm, 1)

    def mod_idx(part):
        if rows_per_batch >= tm:
            return lambda i, j: (mod_row0 + i // tiles_per_batch, 0, part)
        return lambda i, j: (mod_row0, 0, part)

    last = n_tiles - 1
    return pl.pallas_call(
        functools.partial(_qkv_kernel, group=group, n_tiles=n_tiles),
        grid=(r // tm, n_tiles + 1),
        in_specs=[pl.BlockSpec((tm, d), lambda i, j: (i, 0)),
                  pl.BlockSpec((1, d), lambda i, j: (0, 0)),
                  pl.BlockSpec((1, 1, d), mod_idx(0)),
                  pl.BlockSpec((1, 1, d), mod_idx(1)),
                  pl.BlockSpec((d, tn), lambda i, j: (0, jnp.minimum(j, last))),
                  pl.BlockSpec((d, LANES), lambda i, j: (0, 0)),
                  pl.BlockSpec((3, tn), lambda i, j: (0, jnp.minimum(j, last)))],
        out_specs=[pl.BlockSpec((hpt, tm, HEAD_DIM), lambda i, j: (jnp.minimum(j, last), i, 0)),
                   pl.BlockSpec((tm, LANES), lambda i, j: (i, 0))],
        out_shape=[jax.ShapeDtypeStruct((3 * HEADS, r, HEAD_DIM), F32),
                   jax.ShapeDtypeStruct((r, LANES), F32)],
        scratch_shapes=[pltpu.VMEM((tm, d), BF16)],
        compiler_params=_params(("parallel", "arbitrary")),
        name="qkv_proj",
    )(xin, nw, mods, mods, wqkv, wg, cw)


def _mixb_kernel(x_ref, nw_ref, sh_ref, sc_ref, wz_ref, wb_ref, wc_ref, wh_ref, wzb_ref, cw_ref,
                 gz_ref, yb_ref, h_scr, *, group):
    j = pl.program_id(1)

    @pl.when(j == 0)
    def _():
        h_scr[...] = _modulated_norm(x_ref[...], nw_ref[...], sc_ref[0], sh_ref[0]).astype(BF16)

    h = h_scr[...]
    dot = lambda w: jnp.dot(h, w[...], preferred_element_type=F32)
    gz_ref[...] = _silu(dot(wz_ref))
    conv = _row_conv3(dot(wc_ref) * dot(wh_ref), cw_ref[...], group)
    yb_ref[...] = dot(wb_ref) * conv * _silu(dot(wzb_ref))


def _mixb_proj(xin, mods, nw, w5, cw, *, rows_per_batch, mod_row0, group):
    r, d = xin.shape
    tm, tn = 1024, 256
    nj = d // tn
    tiles_per_batch = max(rows_per_batch // tm, 1)

    def mod_idx(part):
        if rows_per_batch >= tm:
            return lambda i, j: (mod_row0 + i // tiles_per_batch, 0, part)
        return lambda i, j: (mod_row0, 0, part)

    wspec = lambda reg: pl.BlockSpec((d, tn), lambda i, j: (0, reg * nj + j))
    return pl.pallas_call(
        functools.partial(_mixb_kernel, group=group),
        grid=(r // tm, nj),
        in_specs=[pl.BlockSpec((tm, d), lambda i, j: (i, 0)),
                  pl.BlockSpec((1, d), lambda i, j: (0, 0)),
                  pl.BlockSpec((1, 1, d), mod_idx(0)),
                  pl.BlockSpec((1, 1, d), mod_idx(1)),
                  wspec(0), wspec(1), wspec(2), wspec(3), wspec(4),
                  pl.BlockSpec((3, tn), lambda i, j: (0, j))],
        out_specs=[pl.BlockSpec((tm, tn), lambda i, j: (i, j)),
                   pl.BlockSpec((tm, tn), lambda i, j: (i, j))],
        out_shape=[jax.ShapeDtypeStruct((r, d), F32), jax.ShapeDtypeStruct((r, d), F32)],
        scratch_shapes=[pltpu.VMEM((tm, d), BF16)],
        compiler_params=_params(("parallel", "arbitrary")),
        name="mixb_proj",
    )(xin, nw, mods, mods, w5, w5, w5, w5, w5, cw)


def _dot_exact01(m01, x):
    x1 = x.astype(BF16)
    r1 = x - x1.astype(F32)
    x2 = r1.astype(BF16)
    x3 = (r1 - x2.astype(F32)).astype(BF16)
    d = lambda t: jnp.dot(m01, t, preferred_element_type=F32)
    return (d(x1) + d(x2)) + d(x3)


def _unit_tri_inverse_minus_eye(m, lower):
    n = m.shape[0]
    ri = lax.broadcasted_iota(jnp.int32, (n, n), 0)
    ci = lax.broadcasted_iota(jnp.int32, (n, n), 1)
    if not lower:
        ri, ci = ci, ri
    y = None
    s = 1
    while s < n:
        shift = s.bit_length() - 1
        br = lax.shift_right_logical(ri, shift)
        bc = lax.shift_right_logical(ci, shift)
        cm = jnp.where(((br & 1) == 1) & (bc == br - 1), m, 0.0)
        if y is None:
            y = -cm
        else:
            yb = y.astype(BF16)
            p = cm + jnp.dot(yb, cm.astype(BF16), preferred_element_type=F32)
            y = y - p - jnp.dot(p.astype(BF16), yb, preferred_element_type=F32)
        s *= 2
    return y


def _gdn_kernel(alog_ref, dtb_ref, qc_ref, kc_ref, vc_ref, qx_ref, kx_ref, vx_ref, gc_ref, gx_ref,
                hn_ref, oc_ref, ox_ref,
                qs, ks, vs, gsc, bsc, wq_s, u_s, ak_s, gl_s, oacc):
    h = pl.program_id(1)
    tc = qc_ref.shape[1]
    tx = qx_ref.shape[1]
    t = tc + tx
    nchunks = t // CHUNK
    nc_ctx = tc // CHUNK
    c = CHUNK

    def l2n(a):
        return a * lax.rsqrt(jnp.sum(a * a, axis=-1, keepdims=True) + EPS)

    qscale = HEAD_DIM ** -0.5
    qs[0:tc] = l2n(qc_ref[0]) * qscale
    qs[tc:t] = l2n(qx_ref[0]) * qscale
    ks[0:tc] = l2n(kc_ref[0])
    ks[tc:t] = l2n(kx_ref[0])
    vs[0:tc] = vc_ref[0]
    vs[tc:t] = vx_ref[0]
    oacc[...] = jnp.zeros_like(oacc)

    lane = lax.broadcasted_iota(jnp.int32, (1, LANES), 1)
    for d in range(2):
        neg_a = -jnp.exp(jnp.zeros((1, LANES), F32) + alog_ref[d, h])
        dt_b = dtb_ref[d, h]
        for ref, lo, hi in ((gc_ref, 0, tc), (gx_ref, tc, t)):
            gates = ref[...]
            a_raw = jnp.sum(jnp.where(lane == d * HEADS + h, gates, 0.0), axis=-1, keepdims=True)
            b_raw = jnp.sum(jnp.where(lane == 2 * HEADS + d * HEADS + h, gates, 0.0), axis=-1,
                            keepdims=True)
            z = a_raw + dt_b
            softplus = jnp.maximum(z, 0.0) + jnp.log1p(jnp.exp(-jnp.abs(z)))
            gsc[d, lo:hi] = jnp.broadcast_to(neg_a * softplus, (hi - lo, LANES))
            bsc[d, lo:hi] = jnp.broadcast_to(_sigmoid(b_raw), (hi - lo, LANES))

    ri = lax.broadcasted_iota(jnp.int32, (c, c), 0)
    ci = lax.broadcasted_iota(jnp.int32, (c, c), 1)
    ri2 = lax.broadcasted_iota(jnp.int32, (c, LANES), 0)
    ci2 = lax.broadcasted_iota(jnp.int32, (c, LANES), 1)

    def prep(n, carry):
        rows = pl.ds(pl.multiple_of(n * c, c), c)
        q = qs[rows]
        k = ks[rows]
        v = vs[rows]
        kbf = k.astype(BF16)
        betas = [bsc[0, rows], bsc[1, rows]]
        kbs = [k * betas[0], k * betas[1]]
        lhs = jnp.concatenate([kbs[0], kbs[1], q], axis=0).astype(BF16)
        gk = lax.dot_general(lhs, kbf, (((1,), (1,)), ((), ())), preferred_element_type=F32)
        for d in range(2):
            g = gsc[d, rows]
            if d == 0:
                cum01 = (ci <= ri)
                sel = (ri2 > ci2) & (ci2 < c)
                incl = ci <= ri
                strict = ci < ri
                edge = c - 1
            else:
                cum01 = (ci >= ri)
                sel = (ri2 < ci2) & (ci2 < c)
                incl = ci >= ri
                strict = ci > ri
                edge = 0
            rhs = jnp.concatenate([jnp.where(sel, g, 0.0), g], axis=1)
            dg = _dot_exact01(jnp.where(cum01, 1.0, 0.0).astype(BF16), rhs)
            decay = jnp.exp(dg[:, 0:c])
            gcb = dg[:, LANES:2 * LANES]
            m = jnp.where(strict, gk[d * c:(d + 1) * c] * decay, 0.0)
            a = jnp.where(incl, gk[2 * c:3 * c] * decay, 0.0)
            y = _unit_tri_inverse_minus_eye(m, lower=(d == 0))
            egc = jnp.exp(gcb)
            r = jnp.concatenate([kbs[d] * egc, v * betas[d]], axis=1)
            wu = r + jnp.dot(y.astype(BF16), r.astype(BF16), preferred_element_type=F32)
            g_edge = gcb[edge:edge + 1]
            kg = k * jnp.exp(g_edge - gcb)
            idx = 2 * n + d
            wq_s[idx] = jnp.concatenate([wu[:, 0:HEAD_DIM], q * egc], axis=0).astype(BF16)
            u_s[idx] = wu[:, HEAD_DIM:2 * HEAD_DIM]
            ak_s[idx] = jnp.concatenate([a, jnp.transpose(kg)], axis=0).astype(BF16)
            gl_s[idx] = jnp.broadcast_to(jnp.exp(g_edge), (8, LANES))
        return carry

    lax.fori_loop(0, nchunks, prep, 0)

    def scan(s, states):
        n_fwd = s
        n_rev = jnp.where(s < nc_ctx, nc_ctx - 1 - s, nchunks + nc_ctx - 1 - s)
        new_states = []
        for d, n in ((0, n_fwd), (1, n_rev)):
            st = states[d]
            idx = 2 * n + d
            ws = jnp.dot(wq_s[idx], st.astype(BF16), preferred_element_type=F32)
            v_new = u_s[idx] - ws[0:c]
            ak = jnp.dot(ak_s[idx], v_new.astype(BF16), preferred_element_type=F32)
            rows = pl.ds(pl.multiple_of(n * c, c), c)
            oacc[rows] = oacc[rows] + (ws[c:2 * c] + ak[0:c])
            new_states.append(st * gl_s[idx][0:1] + ak[c:c + HEAD_DIM])
        return tuple(new_states)

    zero = jnp.zeros((HEAD_DIM, HEAD_DIM), F32)
    lax.fori_loop(0, nchunks, scan, (zero, zero))

    o = oacc[...]
    on = o * lax.rsqrt(jnp.mean(o * o, axis=-1, keepdims=True) + EPS) * hn_ref[...]
    oc_ref[0] = on[0:tc]
    ox_ref[0] = on[tc:t]


def _gdn(a_log, dt_bias, qkv_c, qkv_x, gates_c, gates_x, head_norm):
    tc, tx = CTX_LEN, SEQ
    t = tc + tx
    nidx = 2 * (t // CHUNK)
    smem = pl.BlockSpec(memory_space=pltpu.SMEM)
    part = lambda rows, p: pl.BlockSpec((1, rows, HEAD_DIM), lambda b, h: (p * HEADS + h, b, 0))
    return pl.pallas_call(
        _gdn_kernel,
        grid=(BATCH, HEADS),
        in_specs=[smem, smem,
                  part(tc, 0), part(tc, 1), part(tc, 2),
                  part(tx, 0), part(tx, 1), part(tx, 2),
                  pl.BlockSpec((tc, LANES), lambda b, h: (b, 0)),
                  pl.BlockSpec((tx, LANES), lambda b, h: (b, 0)),
                  pl.BlockSpec((1, HEAD_DIM), lambda b, h: (0, 0))],
        out_specs=[pl.BlockSpec((1, tc, HEAD_DIM), lambda b, h: (b, 0, h)),
                   pl.BlockSpec((1, tx, HEAD_DIM), lambda b, h: (b, 0, h))],
        out_shape=[jax.ShapeDtypeStruct((BATCH, tc, D_MODEL), F32),
                   jax.ShapeDtypeStruct((BATCH, tx, D_MODEL), F32)],
        scratch_shapes=[pltpu.VMEM((t, HEAD_DIM), F32), pltpu.VMEM((t, HEAD_DIM), F32),
                        pltpu.VMEM((t, HEAD_DIM), F32),
                        pltpu.VMEM((2, t, LANES), F32), pltpu.VMEM((2, t, LANES), F32),
                        pltpu.VMEM((nidx, 2 * CHUNK, HEAD_DIM), BF16),
                        pltpu.VMEM((nidx, CHUNK, HEAD_DIM), F32),
                        pltpu.VMEM((nidx, CHUNK + HEAD_DIM, CHUNK), BF16),
                        pltpu.VMEM((nidx, 8, LANES), F32),
                        pltpu.VMEM((t, HEAD_DIM), F32)],
        compiler_params=_params(("parallel", "arbitrary")),
        name="gdn",
    )(a_log, dt_bias, qkv_c, qkv_c, qkv_c, qkv_x, qkv_x, qkv_x, gates_c, gates_x, head_norm)


def _even_out_kernel(x_ref, gt_ref, o_ref, gz_ref, yb_ref, w_ref, out_ref, acc, *, nk_a):
    kk = pl.program_id(1)

    @pl.when(kk == 0)
    def _():
        acc[...] = jnp.zeros_like(acc)

    @pl.when(kk < nk_a)
    def _():
        a = (o_ref[...] * gz_ref[...]).astype(BF16)
        acc[...] += jnp.dot(a, w_ref[...], preferred_element_type=F32)

    @pl.when(kk >= nk_a)
    def _():
        acc[...] += jnp.dot(yb_ref[...].astype(BF16), w_ref[...], preferred_element_type=F32)

    @pl.when(kk == 2 * nk_a - 1)
    def _():
        out_ref[...] = x_ref[...] + gt_ref[0] * acc[...]


def _even_out(xin, mods, o, gz, yb, w_out, *, rows_per_batch, mod_row0):
    r, d = xin.shape
    tm = min(512, rows_per_batch)
    tk = 512
    nk_a = d // tk
    tiles_per_batch = rows_per_batch // tm
    fixed = mod_row0 >= BATCH
    gt_idx = (lambda i, kk: (mod_row0, 0, 2)) if fixed else (lambda i, kk: (i // tiles_per_batch, 0, 2))
    return pl.pallas_call(
        functools.partial(_even_out_kernel, nk_a=nk_a),
        grid=(r // tm, 2 * nk_a),
        in_specs=[pl.BlockSpec((tm, d), lambda i, kk: (i, 0)),
                  pl.BlockSpec((1, 1, d), gt_idx),
                  pl.BlockSpec((tm, tk), lambda i, kk: (i, jnp.minimum(kk, nk_a - 1))),
                  pl.BlockSpec((tm, tk), lambda i, kk: (i, jnp.minimum(kk, nk_a - 1))),
                  pl.BlockSpec((tm, tk), lambda i, kk: (i, jnp.maximum(kk - nk_a, 0))),
                  pl.BlockSpec((tk, d), lambda i, kk: (kk, 0))],
        out_specs=pl.BlockSpec((tm, d), lambda i, kk: (i, 0)),
        out_shape=jax.ShapeDtypeStruct((r, d), F32),
        scratch_shapes=[pltpu.VMEM((tm, d), F32)],
        compiler_params=_params(("parallel", "arbitrary")),
        name="even_out",
    )(xin, mods, o, gz, yb, w_out)


def _odd_in_kernel(x_ref, nw_ref, sh_ref, sc_ref, w_ref, out_ref, h_scr, *, tiles_per_region):
    j = pl.program_id(1)

    @pl.when(j == 0)
    def _():
        h_scr[...] = _modulated_norm(x_ref[...], nw_ref[...], sc_ref[0], sh_ref[0]).astype(BF16)

    p = jnp.dot(h_scr[...], w_ref[...], preferred_element_type=F32)
    region = j // tiles_per_region

    @pl.when(region <= 1)
    def _():
        out_ref[...] = _gelu_tanh(p)

    @pl.when((region == 2) | (region == 4))
    def _():
        out_ref[...] = _silu(p)

    @pl.when(region == 3)
    def _():
        out_ref[...] = p


def _odd_in(xin, mods, nw, w):
    r, d = xin.shape
    n = w.shape[1]
    tm, tn = 1024, 512
    tiles_per_batch = SEQ // tm
    mod_idx = lambda part: (lambda i, j: (i // tiles_per_batch, 0, part))
    return pl.pallas_call(
        functools.partial(_odd_in_kernel, tiles_per_region=d // tn),
        grid=(r // tm, n // tn),
        in_specs=[pl.BlockSpec((tm, d), lambda i, j: (i, 0)),
                  pl.BlockSpec((1, d), lambda i, j: (0, 0)),
                  pl.BlockSpec((1, 1, d), mod_idx(0)),
                  pl.BlockSpec((1, 1, d), mod_idx(1)),
                  pl.BlockSpec((d, tn), lambda i, j: (0, j))],
        out_specs=pl.BlockSpec((tm, tn), lambda i, j: (i, j)),
        out_shape=jax.ShapeDtypeStruct((r, n), F32),
        scratch_shapes=[pltpu.VMEM((tm, d), BF16)],
        compiler_params=_params(("parallel", "arbitrary")),
        name="odd_in",
    )(xin, nw, mods, mods, w)


def _odd_mix_kernel(x_ref, gt_ref, u_ref, v_ref, zc_ref, pd_ref, zd_ref, lnw_ref, lnb_ref, ws_ref,
                    bs_ref, band_ref, icnt_ref, pw_ref, ps_ref, wo_ref, fnw_ref, out_ref,
                    acc, ln_scr, *, nk_c):
    kk = pl.program_id(1)
    tm = x_ref.shape[0]
    tk = u_ref.shape[1]
    gpb = tk // HEAD_DIM

    @pl.when(kk == 0)
    def _():
        v = v_ref[...]
        mu = jnp.mean(v, axis=-1, keepdims=True)
        xc = v - mu
        var = jnp.mean(xc * xc, axis=-1, keepdims=True)
        ln = (xc * lax.rsqrt(var + EPS) * lnw_ref[...] + lnb_ref[...]).astype(BF16)
        for blk in range(nk_c):
            ln_scr[blk] = ln[:, blk * tk:(blk + 1) * tk]
        acc[...] = jnp.zeros_like(acc)

    @pl.when(kk < nk_c)
    def _():
        ln = ln_scr[kk]
        cols = []
        for gi in range(gpb):
            w_g = ws_ref[gi]
            b_g = bs_ref[gi]
            rows = []
            for cc in range(tm // TOK_CHUNK):
                blk = ln[cc * TOK_CHUNK:(cc + 1) * TOK_CHUNK, gi * HEAD_DIM:(gi + 1) * HEAD_DIM]
                rows.append(jnp.dot(w_g, blk, preferred_element_type=F32) + b_g)
            cols.append(jnp.concatenate(rows, axis=0))
        s = jnp.concatenate(cols, axis=1)
        yc = u_ref[...] * s * zc_ref[...]
        acc[...] += jnp.dot(yc.astype(BF16), wo_ref[...], preferred_element_type=F32)

    @pl.when(kk >= nk_c)
    def _():
        pd = pd_ref[...]
        hi = pd.astype(BF16)
        mid = (pd - hi.astype(F32)).astype(BF16)
        band = band_ref[0]
        wsum = (jnp.dot(band, hi, preferred_element_type=F32)
                + jnp.dot(band, mid, preferred_element_type=F32))
        icnt = icnt_ref[0]
        icnt = jnp.concatenate([icnt] * (tk // LANES), axis=1)
        diff = wsum * icnt - pd
        y = jnp.dot(diff.astype(BF16), pw_ref[0], preferred_element_type=F32)
        yd = y * ps_ref[...] * zd_ref[...]
        acc[...] += jnp.dot(yd.astype(BF16), wo_ref[...], preferred_element_type=F32)

    @pl.when(kk == pl.num_programs(1) - 1)
    def _():
        xn = x_ref[...] + gt_ref[0] * acc[...]
        ms = jnp.mean(xn * xn, axis=-1, keepdims=True)
        out_ref[...] = xn * lax.rsqrt(ms + EPS) * fnw_ref[...]


def _pool_constants(tm):
    bands = np.zeros((POOL_GROUPS, tm, tm), np.float32)
    icnt = np.zeros((POOL_GROUPS, tm, LANES), np.float32)
    pos = np.arange(tm)
    row = pos // GRID_W
    col = pos % GRID_W
    for g, rad in enumerate(POOL_RADII):
        same = row[:, None] == row[None, :]
        bands[g] = (same & (np.abs(col[:, None] - col[None, :]) <= rad)).astype(np.float32)
        cnt = np.minimum(col + rad + 1, GRID_W) - np.maximum(col - rad, 0)
        icnt[g] = (1.0 / cnt.astype(np.float32))[:, None]
    return jnp.asarray(bands, BF16), jnp.asarray(icnt, F32)


def _odd_mix(xin, mods, p, ln_w, ln_b, w_s, b_s, pool_w, pool_scale, w_out, fnw):
    r, d = xin.shape
    tm, tk = 512, 512
    nk_c = d // tk
    nk = 2 * nk_c
    gpb = tk // HEAD_DIM
    tiles_per_batch = SEQ // tm
    bands, icnt = _pool_constants(tm)
    lo = lambda kk: jnp.minimum(kk, nk_c - 1)
    hi = lambda kk: jnp.maximum(kk - nk_c, 0)
    cblk = d // tk
    return pl.pallas_call(
        functools.partial(_odd_mix_kernel, nk_c=nk_c),
        grid=(r // tm, nk),
        in_specs=[pl.BlockSpec((tm, d), lambda i, kk: (i, 0)),
                  pl.BlockSpec((1, 1, d), lambda i, kk: (i // tiles_per_batch, 0, 2)),
                  pl.BlockSpec((tm, tk), lambda i, kk: (i, lo(kk))),
                  pl.BlockSpec((tm, d), lambda i, kk: (i, 1)),
                  pl.BlockSpec((tm, tk), lambda i, kk: (i, 2 * cblk + lo(kk))),
                  pl.BlockSpec((tm, tk), lambda i, kk: (i, 3 * cblk + hi(kk))),
                  pl.BlockSpec((tm, tk), lambda i, kk: (i, 4 * cblk + hi(kk))),
                  pl.BlockSpec((1, d), lambda i, kk: (0, 0)),
                  pl.BlockSpec((1, d), lambda i, kk: (0, 0)),
                  pl.BlockSpec((gpb, TOK_CHUNK, TOK_CHUNK), lambda i, kk: (lo(kk), 0, 0)),
                  pl.BlockSpec((gpb, TOK_CHUNK, LANES), lambda i, kk: (lo(kk), 0, 0)),
                  pl.BlockSpec((1, tm, tm), lambda i, kk: (hi(kk), 0, 0)),
                  pl.BlockSpec((1, tm, LANES), lambda i, kk: (hi(kk), 0, 0)),
                  pl.BlockSpec((1, POOL_GW, POOL_GW), lambda i, kk: (hi(kk), 0, 0)),
                  pl.BlockSpec((1, tk), lambda i, kk: (0, hi(kk))),
                  pl.BlockSpec((tk, d), lambda i, kk: (kk, 0)),
                  pl.BlockSpec((1, d), lambda i, kk: (0, 0))],
        out_specs=pl.BlockSpec((tm, d), lambda i, kk: (i, 0)),
        out_shape=jax.ShapeDtypeStruct((r, d), F32),
        scratch_shapes=[pltpu.VMEM((tm, d), F32), pltpu.VMEM((nk_c, tm, tk), BF16)],
        compiler_params=_params(("parallel", "arbitrary")),
        name="odd_mix",
    )(xin, mods, p, p, p, p, p, ln_w, ln_b, w_s, b_s, bands, icnt, pool_w, pool_scale, w_out, fnw)


def kernel(x, c, ctx, c_ctx, ada_w, ada_b, norm_w, e_w_in, e_conv_qkv, e_a_log, e_dt_bias, e_head_norm,
           e_conv_b, e_w_out, o_w_in, o_ln_w, o_ln_b, o_w_s, o_b_s, o_pool_w, o_pool_scale, o_w_out,
           final_norm_w):
    bsz, seq, d = x.shape
    rx = bsz * seq
    rc = bsz * ctx.shape[1]
    x2 = x.reshape(rx, d)
    ctx2 = ctx.reshape(rc, d)

    cc = jnp.concatenate([c, c_ctx[None, :], jnp.zeros((8 - bsz - 1, d), F32)], axis=0)
    mods = _adaln(cc, ada_w, ada_b)
    ctx_row = bsz

    w_in = e_w_in[0]
    gate_lo = QKV_W + D_MODEL
    gate_hi = gate_lo + 4 * HEADS
    wqkv = w_in[:, :QKV_W].astype(BF16)
    wg = jnp.pad(w_in[:, gate_lo:gate_hi], ((0, 0), (0, LANES - 4 * HEADS))).astype(BF16)
    w5 = jnp.concatenate([w_in[:, QKV_W:gate_lo], w_in[:, gate_hi:]], axis=1).astype(BF16)
    m0 = mods[0].reshape(8, 1, 3 * d)
    nw0 = norm_w[0].reshape(1, d)

    qkv_x, gates_x = _qkv_proj(x2, m0, nw0, wqkv, wg, e_conv_qkv[0],
                               rows_per_batch=seq, mod_row0=0, group=GRID_W)
    qkv_c, gates_c = _qkv_proj(ctx2, m0, nw0, wqkv, wg, e_conv_qkv[0],
                               rows_per_batch=ctx.shape[1], mod_row0=ctx_row, group=ctx.shape[1])
    gz_x, yb_x = _mixb_proj(x2, m0, nw0, w5, e_conv_b[0],
                            rows_per_batch=seq, mod_row0=0, group=GRID_W)
    gz_c, yb_c = _mixb_proj(ctx2, m0, nw0, w5, e_conv_b[0],
                            rows_per_batch=ctx.shape[1], mod_row0=ctx_row, group=ctx.shape[1])
    o_c, o_x = _gdn(e_a_log[0], e_dt_bias[0], qkv_c, qkv_x, gates_c, gates_x,
                    e_head_norm[0].reshape(1, HEAD_DIM))
    w_out0 = e_w_out[0].astype(BF16)
    x2 = _even_out(x2, m0, o_x.reshape(rx, d), gz_x, yb_x, w_out0, rows_per_batch=seq, mod_row0=0)
    ctx2 = _even_out(ctx2, m0, o_c.reshape(rc, d), gz_c, yb_c, w_out0,
                     rows_per_batch=ctx.shape[1], mod_row0=ctx_row)

    m1 = mods[1].reshape(8, 1, 3 * d)
    p = _odd_in(x2, m1, norm_w[1].reshape(1, d), o_w_in[0].astype(BF16))
    b_s = jnp.broadcast_to(o_b_s[0][:, :, None], (o_b_s.shape[1], TOK_CHUNK, LANES))
    out = _odd_mix(x2, m1, p, o_ln_w[0].reshape(1, d), o_ln_b[0].reshape(1, d),
                   o_w_s[0].astype(BF16), b_s, o_pool_w[0].astype(BF16),
                   o_pool_scale[0].reshape(1, d), o_w_out[0].astype(BF16),
                   final_norm_w.reshape(1, d))
    return out.reshape(bsz, seq, d)
```

```python
import functools
import math

import numpy as np
import jax
import jax.numpy as jnp
from jax import lax
from jax.experimental import pallas as pl
from jax.experimental.pallas import tpu as pltpu

F32 = jnp.float32
BF16 = jnp.bfloat16

D_MODEL = 2048
BATCH = 4
SEQ = 2048
CTX_LEN = 256
GRID_W = 64
EPS = 1e-6
HEADS = 16
HEAD_DIM = 128
QKV_W = 3 * D_MODEL
CHUNK = 64
PREP_UNROLL = 12
TOK_CHUNK = 128
POOL_GROUPS = 4
POOL_GW = D_MODEL // POOL_GROUPS
POOL_RADII = (1, 2, 4, 8)

LANES = 128
VMEM_LIMIT = 56 * 1024 * 1024


def _params(sem):
    return pltpu.CompilerParams(dimension_semantics=sem, vmem_limit_bytes=VMEM_LIMIT)


def _sigmoid(x):
    return 1.0 / (1.0 + jnp.exp(-x))


def _silu(x):
    return x * _sigmoid(x)


def _gelu_tanh(x):
    cdf = 0.5 * (1.0 + jnp.tanh(math.sqrt(2.0 / math.pi) * (x + 0.044715 * (x * x * x))))
    return x * cdf


def _modulated_norm(x, nw, sc, sh):
    ms = jnp.mean(x * x, axis=-1, keepdims=True)
    y = x * lax.rsqrt(ms + EPS) * nw
    return y * (1.0 + sc) + sh


def _row_conv3(p, cw, group):
    tm = p.shape[0]
    row = lax.broadcasted_iota(jnp.int32, (tm, 1), 0) % group
    prev = jnp.where(row == 0, 0.0, pltpu.roll(p, 1, 0))
    nxt = jnp.where(row == group - 1, 0.0, pltpu.roll(p, tm - 1, 0))
    return prev * cw[0:1] + p * cw[1:2] + nxt * cw[2:3]


def _adaln_kernel(c_ref, w_ref, b_ref, o_ref):
    a = _silu(c_ref[...]).astype(BF16)
    o_ref[0] = jnp.dot(a, w_ref[0].astype(BF16), preferred_element_type=F32) + b_ref[0]


def _adaln(cc, ada_w, ada_b):
    depth, d, n = ada_w.shape
    tn = 1024
    return pl.pallas_call(
        _adaln_kernel,
        grid=(depth, n // tn),
        in_specs=[pl.BlockSpec((8, d), lambda l, j: (0, 0)),
                  pl.BlockSpec((1, d, tn), lambda l, j: (l, 0, j)),
                  pl.BlockSpec((1, 1, tn), lambda l, j: (l, 0, j))],
        out_specs=pl.BlockSpec((1, 8, tn), lambda l, j: (l, 0, j)),
        out_shape=jax.ShapeDtypeStruct((depth, 8, n), F32),
        compiler_params=_params(("parallel", "parallel")),
        name="adaln",
    )(cc, ada_w, ada_b.reshape(depth, 1, n))


def _qkv_kernel(x_ref, nw_ref, sh_ref, sc_ref, w_ref, wg_ref, cw_ref, qkv_ref, g_ref, h_scr,
                *, group, n_tiles):
    j = pl.program_id(1)

    @pl.when(j == 0)
    def _():
        h_scr[...] = _modulated_norm(x_ref[...], nw_ref[...], sc_ref[0], sh_ref[0]).astype(BF16)

    @pl.when(j < n_tiles)
    def _():
        p = jnp.dot(h_scr[...], w_ref[...], preferred_element_type=F32)
        y = _silu(_row_conv3(p, cw_ref[...], group))
        for hh in range(y.shape[1] // HEAD_DIM):
            qkv_ref[hh] = y[:, hh * HEAD_DIM:(hh + 1) * HEAD_DIM]

    @pl.when(j == n_tiles)
    def _():
        g_ref[...] = jnp.dot(h_scr[...], wg_ref[...], preferred_element_type=F32)


def _qkv_proj(xin, mods, nw, wqkv, wg, cw, *, rows_per_batch, mod_row0, group):
    r, d = xin.shape
    tm, tn = 1024, 768
    n_tiles = QKV_W // tn
    hpt = tn // HEAD_DIM
    tiles_per_batch = max(rows_per_batch // tm, 1)

    def mod_idx(part):
        if rows_per_batch >= tm:
            return lambda i, j: (mod_row0 + i // tiles_per_batch, 0, part)
        return lambda i, j: (mod_row0, 0, part)

    last = n_tiles - 1
    return pl.pallas_call(
        functools.partial(_qkv_kernel, group=group, n_tiles=n_tiles),
        grid=(r // tm, n_tiles + 1),
        in_specs=[pl.BlockSpec((tm, d), lambda i, j: (i, 0)),
                  pl.BlockSpec((1, d), lambda i, j: (0, 0)),
                  pl.BlockSpec((1, 1, d), mod_idx(0)),
                  pl.BlockSpec((1, 1, d), mod_idx(1)),
                  pl.BlockSpec((d, tn), lambda i, j: (0, jnp.minimum(j, last))),
                  pl.BlockSpec((d, LANES), lambda i, j: (0, 0)),
                  pl.BlockSpec((3, tn), lambda i, j: (0, jnp.minimum(j, last)))],
        out_specs=[pl.BlockSpec((hpt, tm, HEAD_DIM), lambda i, j: (jnp.minimum(j, last), i, 0)),
                   pl.BlockSpec((tm, LANES), lambda i, j: (i, 0))],
        out_shape=[jax.ShapeDtypeStruct((3 * HEADS, r, HEAD_DIM), F32),
                   jax.ShapeDtypeStruct((r, LANES), F32)],
        scratch_shapes=[pltpu.VMEM((tm, d), BF16)],
        compiler_params=_params(("parallel", "arbitrary")),
        name="qkv_proj",
    )(xin, nw, mods, mods, wqkv, wg, cw)


def _mixb_kernel(x_ref, nw_ref, sh_ref, sc_ref, wz_ref, wb_ref, wc_ref, wh_ref, wzb_ref, cw_ref,
                 gz_ref, yb_ref, h_scr, *, group):
    j = pl.program_id(1)

    @pl.when(j == 0)
    def _():
        h_scr[...] = _modulated_norm(x_ref[...], nw_ref[...], sc_ref[0], sh_ref[0]).astype(BF16)

    h = h_scr[...]
    dot = lambda w: jnp.dot(h, w[...], preferred_element_type=F32)
    gz_ref[...] = _silu(dot(wz_ref))
    conv = _row_conv3(dot(wc_ref) * dot(wh_ref), cw_ref[...], group)
    yb_ref[...] = dot(wb_ref) * conv * _silu(dot(wzb_ref))


def _mixb_proj(xin, mods, nw, w5, cw, *, rows_per_batch, mod_row0, group):
    r, d = xin.shape
    tm, tn = 1024, 256
    nj = d // tn
    tiles_per_batch = max(rows_per_batch // tm, 1)

    def mod_idx(part):
        if rows_per_batch >= tm:
            return lambda i, j: (mod_row0 + i // tiles_per_batch, 0, part)
        return lambda i, j: (mod_row0, 0, part)

    wspec = lambda reg: pl.BlockSpec((d, tn), lambda i, j: (0, reg * nj + j))
    return pl.pallas_call(
        functools.partial(_mixb_kernel, group=group),
        grid=(r // tm, nj),
        in_specs=[pl.BlockSpec((tm, d), lambda i, j: (i, 0)),
                  pl.BlockSpec((1, d), lambda i, j: (0, 0)),
                  pl.BlockSpec((1, 1, d), mod_idx(0)),
                  pl.BlockSpec((1, 1, d), mod_idx(1)),
                  wspec(0), wspec(1), wspec(2), wspec(3), wspec(4),
                  pl.BlockSpec((3, tn), lambda i, j: (0, j))],
        out_specs=[pl.BlockSpec((tm, tn), lambda i, j: (i, j)),
                   pl.BlockSpec((tm, tn), lambda i, j: (i, j))],
        out_shape=[jax.ShapeDtypeStruct((r, d), F32), jax.ShapeDtypeStruct((r, d), F32)],
        scratch_shapes=[pltpu.VMEM((tm, d), BF16)],
        compiler_params=_params(("parallel", "arbitrary")),
        name="mixb_proj",
    )(xin, nw, mods, mods, w5, w5, w5, w5, w5, cw)


def _split3_bf16(x):
    x1 = x.astype(BF16)
    r1 = x - x1.astype(F32)
    x2 = r1.astype(BF16)
    x3 = (r1 - x2.astype(F32)).astype(BF16)
    return x1, x2, x3


def _dots_exact01(m01s, xs):
    pieces = [_split3_bf16(x) for x in xs]
    prods = [[jnp.dot(m01, t, preferred_element_type=F32) for t in p] for m01, p in zip(m01s, pieces)]
    return [(p[0] + p[1]) + p[2] for p in prods]


def _unit_tri_inverse_minus_eye(ms, lowers):
    n = ms[0].shape[0]
    ri = lax.broadcasted_iota(jnp.int32, (n, n), 0)
    ci = lax.broadcasted_iota(jnp.int32, (n, n), 1)
    ys = None
    s = 1
    while s < n:
        shift = s.bit_length() - 1
        br = lax.shift_right_logical(ri, shift)
        bc = lax.shift_right_logical(ci, shift)
        mask_lo = ((br & 1) == 1) & (bc == br - 1)
        mask_up = ((bc & 1) == 1) & (br == bc - 1)
        cms = [jnp.where(mask_lo if lo else mask_up, m, 0.0) for m, lo in zip(ms, lowers)]
        if ys is None:
            ys = [-cm for cm in cms]
        else:
            ybs = [y.astype(BF16) for y in ys]
            ps = [cm + jnp.dot(yb, cm.astype(BF16), preferred_element_type=F32)
                  for cm, yb in zip(cms, ybs)]
            ys = [y - p - jnp.dot(p.astype(BF16), yb, preferred_element_type=F32)
                  for y, p, yb in zip(ys, ps, ybs)]
        s *= 2
    return ys


def _gdn_kernel(alog_ref, dtb_ref, qc_ref, kc_ref, vc_ref, qx_ref, kx_ref, vx_ref, gc_ref, gx_ref,
                hn_ref, oc_ref, ox_ref,
                qs, ks, vs, gsc, bsc, wq_s, u_s, ak_s, gl_s, oacc):
    h = pl.program_id(1)
    tc = qc_ref.shape[1]
    tx = qx_ref.shape[1]
    t = tc + tx
    nchunks = t // CHUNK
    nc_ctx = tc // CHUNK
    c = CHUNK

    def l2n(a):
        return a * lax.rsqrt(jnp.sum(a * a, axis=-1, keepdims=True) + EPS)

    qscale = HEAD_DIM ** -0.5
    qs[0:tc] = l2n(qc_ref[0]) * qscale
    qs[tc:t] = l2n(qx_ref[0]) * qscale
    ks[0:tc] = l2n(kc_ref[0])
    ks[tc:t] = l2n(kx_ref[0])
    vs[0:tc] = vc_ref[0]
    vs[tc:t] = vx_ref[0]
    oacc[...] = jnp.zeros_like(oacc)

    lane = lax.broadcasted_iota(jnp.int32, (1, LANES), 1)
    for d in range(2):
        neg_a = -jnp.exp(jnp.zeros((1, LANES), F32) + alog_ref[d, h])
        dt_b = dtb_ref[d, h]
        for ref, lo, hi in ((gc_ref, 0, tc), (gx_ref, tc, t)):
            gates = ref[...]
            a_raw = jnp.sum(jnp.where(lane == d * HEADS + h, gates, 0.0), axis=-1, keepdims=True)
            b_raw = jnp.sum(jnp.where(lane == 2 * HEADS + d * HEADS + h, gates, 0.0), axis=-1,
                            keepdims=True)
            z = a_raw + dt_b
            softplus = jnp.maximum(z, 0.0) + jnp.log1p(jnp.exp(-jnp.abs(z)))
            gsc[d, lo:hi] = jnp.broadcast_to(neg_a * softplus, (hi - lo, LANES))
            bsc[d, lo:hi] = jnp.broadcast_to(_sigmoid(b_raw), (hi - lo, LANES))

    ri = lax.broadcasted_iota(jnp.int32, (c, c), 0)
    ci = lax.broadcasted_iota(jnp.int32, (c, c), 1)
    ri2 = lax.broadcasted_iota(jnp.int32, (c, LANES), 0)
    ci2 = lax.broadcasted_iota(jnp.int32, (c, LANES), 1)

    cum_lo = jnp.where(ci <= ri, 1.0, 0.0).astype(BF16)
    cum_up = jnp.where(ci >= ri, 1.0, 0.0).astype(BF16)
    sel_lo = (ri2 > ci2) & (ci2 < c)
    sel_up = (ri2 < ci2) & (ci2 < c)

    def prep_group(i, carry):
        ns = [i * PREP_UNROLL + uu for uu in range(PREP_UNROLL)]
        rows = [pl.ds(pl.multiple_of(n * c, c), c) for n in ns]
        qv = [qs[r] for r in rows]
        kv = [ks[r] for r in rows]
        vv = [vs[r] for r in rows]
        chains = [(uu, d) for uu in range(PREP_UNROLL) for d in range(2)]
        beta = {(uu, d): bsc[d, rows[uu]] for uu, d in chains}
        kb = {ch: kv[ch[0]] * beta[ch] for ch in chains}
        gk = [lax.dot_general(
                  jnp.concatenate([kb[(uu, 0)], kb[(uu, 1)], qv[uu]], axis=0).astype(BF16),
                  kv[uu].astype(BF16), (((1,), (1,)), ((), ())), preferred_element_type=F32)
              for uu in range(PREP_UNROLL)]
        rhs = []
        for uu, d in chains:
            g = gsc[d, rows[uu]]
            rhs.append(jnp.concatenate([jnp.where(sel_lo if d == 0 else sel_up, g, 0.0), g], axis=1))
        dgs = _dots_exact01([cum_lo if d == 0 else cum_up for _, d in chains], rhs)
        ms, avs, gcbs = [], [], []
        for (uu, d), dg in zip(chains, dgs):
            decay = jnp.exp(dg[:, 0:c])
            gcbs.append(dg[:, LANES:2 * LANES])
            strict = (ci < ri) if d == 0 else (ci > ri)
            incl = (ci <= ri) if d == 0 else (ci >= ri)
            ms.append(jnp.where(strict, gk[uu][d * c:(d + 1) * c] * decay, 0.0))
            avs.append(jnp.where(incl, gk[uu][2 * c:3 * c] * decay, 0.0))
        ys = _unit_tri_inverse_minus_eye(ms, [d == 0 for _, d in chains])
        egcs = [jnp.exp(gcb) for gcb in gcbs]
        rs = [jnp.concatenate([kb[ch] * egc, vv[ch[0]] * beta[ch]], axis=1)
              for ch, egc in zip(chains, egcs)]
        wus = [r + jnp.dot(y.astype(BF16), r.astype(BF16), preferred_element_type=F32)
               for y, r in zip(ys, rs)]
        for (uu, d), wu, gcb, egc, av in zip(chains, wus, gcbs, egcs, avs):
            edge = c - 1 if d == 0 else 0
            g_edge = gcb[edge:edge + 1]
            kg = kv[uu] * jnp.exp(g_edge - gcb)
            idx = 2 * ns[uu] + d
            wq_s[idx] = jnp.concatenate([wu[:, 0:HEAD_DIM], qv[uu] * egc], axis=0).astype(BF16)
            u_s[idx] = wu[:, HEAD_DIM:2 * HEAD_DIM]
            ak_s[idx] = jnp.concatenate([av, jnp.transpose(kg)], axis=0).astype(BF16)
            gl_s[idx] = jnp.broadcast_to(jnp.exp(g_edge), (8, LANES))
        return carry

    lax.fori_loop(0, nchunks // PREP_UNROLL, prep_group, 0)

    def scan(s, states):
        n_rev = jnp.where(s < nc_ctx, nc_ctx - 1 - s, nchunks + nc_ctx - 1 - s)
        idxs = (2 * s, 2 * n_rev + 1)
        ws = [jnp.dot(wq_s[idx], st.astype(BF16), preferred_element_type=F32)
              for idx, st in zip(idxs, states)]
        v_new = [u_s[idx] - w[0:c] for idx, w in zip(idxs, ws)]
        ak = [jnp.dot(ak_s[idx], vn.astype(BF16), preferred_element_type=F32)
              for idx, vn in zip(idxs, v_new)]
        for n, w, a in zip((s, n_rev), ws, ak):
            rows = pl.ds(pl.multiple_of(n * c, c), c)
            oacc[rows] = oacc[rows] + (w[c:2 * c] + a[0:c])
        return tuple(st * gl_s[idx][0:1] + a[c:c + HEAD_DIM]
                     for idx, st, a in zip(idxs, states, ak))

    zero = jnp.zeros((HEAD_DIM, HEAD_DIM), F32)
    lax.fori_loop(0, nchunks, scan, (zero, zero))

    o = oacc[...]
    on = o * lax.rsqrt(jnp.mean(o * o, axis=-1, keepdims=True) + EPS) * hn_ref[...]
    oc_ref[0] = on[0:tc]
    ox_ref[0] = on[tc:t]


def _gdn(a_log, dt_bias, qkv_c, qkv_x, gates_c, gates_x, head_norm):
    tc, tx = CTX_LEN, SEQ
    t = tc + tx
    nidx = 2 * (t // CHUNK)
    smem = pl.BlockSpec(memory_space=pltpu.SMEM)
    part = lambda rows, p: pl.BlockSpec((1, rows, HEAD_DIM), lambda b, h: (p * HEADS + h, b, 0))
    return pl.pallas_call(
        _gdn_kernel,
        grid=(BATCH, HEADS),
        in_specs=[smem, smem,
                  part(tc, 0), part(tc, 1), part(tc, 2),
                  part(tx, 0), part(tx, 1), part(tx, 2),
                  pl.BlockSpec((tc, LANES), lambda b, h: (b, 0)),
                  pl.BlockSpec((tx, LANES), lambda b, h: (b, 0)),
                  pl.BlockSpec((1, HEAD_DIM), lambda b, h: (0, 0))],
        out_specs=[pl.BlockSpec((1, tc, HEAD_DIM), lambda b, h: (b, 0, h)),
                   pl.BlockSpec((1, tx, HEAD_DIM), lambda b, h: (b, 0, h))],
        out_shape=[jax.ShapeDtypeStruct((BATCH, tc, D_MODEL), F32),
                   jax.ShapeDtypeStruct((BATCH, tx, D_MODEL), F32)],
        scratch_shapes=[pltpu.VMEM((t, HEAD_DIM), F32), pltpu.VMEM((t, HEAD_DIM), F32),
                        pltpu.VMEM((t, HEAD_DIM), F32),
                        pltpu.VMEM((2, t, LANES), F32), pltpu.VMEM((2, t, LANES), F32),
                        pltpu.VMEM((nidx, 2 * CHUNK, HEAD_DIM), BF16),
                        pltpu.VMEM((nidx, CHUNK, HEAD_DIM), F32),
                        pltpu.VMEM((nidx, CHUNK + HEAD_DIM, CHUNK), BF16),
                        pltpu.VMEM((nidx, 8, LANES), F32),
                        pltpu.VMEM((t, HEAD_DIM), F32)],
        compiler_params=_params(("parallel", "arbitrary")),
        name="gdn",
    )(a_log, dt_bias, qkv_c, qkv_c, qkv_c, qkv_x, qkv_x, qkv_x, gates_c, gates_x, head_norm)


def _even_out_kernel(x_ref, gt_ref, o_ref, gz_ref, yb_ref, w_ref, out_ref, acc, *, nk_a):
    kk = pl.program_id(1)

    @pl.when(kk == 0)
    def _():
        acc[...] = jnp.zeros_like(acc)

    @pl.when(kk < nk_a)
    def _():
        a = (o_ref[...] * gz_ref[...]).astype(BF16)
        acc[...] += jnp.dot(a, w_ref[...], preferred_element_type=F32)

    @pl.when(kk >= nk_a)
    def _():
        acc[...] += jnp.dot(yb_ref[...].astype(BF16), w_ref[...], preferred_element_type=F32)

    @pl.when(kk == 2 * nk_a - 1)
    def _():
        out_ref[...] = x_ref[...] + gt_ref[0] * acc[...]


def _even_out(xin, mods, o, gz, yb, w_out, *, rows_per_batch, mod_row0):
    r, d = xin.shape
    tm = min(512, rows_per_batch)
    tk = 512
    nk_a = d // tk
    tiles_per_batch = rows_per_batch // tm
    fixed = mod_row0 >= BATCH
    gt_idx = (lambda i, kk: (mod_row0, 0, 2)) if fixed else (lambda i, kk: (i // tiles_per_batch, 0, 2))
    return pl.pallas_call(
        functools.partial(_even_out_kernel, nk_a=nk_a),
        grid=(r // tm, 2 * nk_a),
        in_specs=[pl.BlockSpec((tm, d), lambda i, kk: (i, 0)),
                  pl.BlockSpec((1, 1, d), gt_idx),
                  pl.BlockSpec((tm, tk), lambda i, kk: (i, jnp.minimum(kk, nk_a - 1))),
                  pl.BlockSpec((tm, tk), lambda i, kk: (i, jnp.minimum(kk, nk_a - 1))),
                  pl.BlockSpec((tm, tk), lambda i, kk: (i, jnp.maximum(kk - nk_a, 0))),
                  pl.BlockSpec((tk, d), lambda i, kk: (kk, 0))],
        out_specs=pl.BlockSpec((tm, d), lambda i, kk: (i, 0)),
        out_shape=jax.ShapeDtypeStruct((r, d), F32),
        scratch_shapes=[pltpu.VMEM((tm, d), F32)],
        compiler_params=_params(("parallel", "arbitrary")),
        name="even_out",
    )(xin, mods, o, gz, yb, w_out)


def _odd_in_kernel(x_ref, nw_ref, sh_ref, sc_ref, w_ref, out_ref, h_scr, *, tiles_per_region):
    j = pl.program_id(1)

    @pl.when(j == 0)
    def _():
        h_scr[...] = _modulated_norm(x_ref[...], nw_ref[...], sc_ref[0], sh_ref[0]).astype(BF16)

    p = jnp.dot(h_scr[...], w_ref[...], preferred_element_type=F32)
    region = j // tiles_per_region

    @pl.when(region <= 1)
    def _():
        out_ref[...] = _gelu_tanh(p)

    @pl.when((region == 2) | (region == 4))
    def _():
        out_ref[...] = _silu(p)

    @pl.when(region == 3)
    def _():
        out_ref[...] = p


def _odd_in(xin, mods, nw, w):
    r, d = xin.shape
    n = w.shape[1]
    tm, tn = 1024, 512
    tiles_per_batch = SEQ // tm
    mod_idx = lambda part: (lambda i, j: (i // tiles_per_batch, 0, part))
    return pl.pallas_call(
        functools.partial(_odd_in_kernel, tiles_per_region=d // tn),
        grid=(r // tm, n // tn),
        in_specs=[pl.BlockSpec((tm, d), lambda i, j: (i, 0)),
                  pl.BlockSpec((1, d), lambda i, j: (0, 0)),
                  pl.BlockSpec((1, 1, d), mod_idx(0)),
                  pl.BlockSpec((1, 1, d), mod_idx(1)),
                  pl.BlockSpec((d, tn), lambda i, j: (0, j))],
        out_specs=pl.BlockSpec((tm, tn), lambda i, j: (i, j)),
        out_shape=jax.ShapeDtypeStruct((r, n), F32),
        scratch_shapes=[pltpu.VMEM((tm, d), BF16)],
        compiler_params=_params(("parallel", "arbitrary")),
        name="odd_in",
    )(xin, nw, mods, mods, w)


def _odd_mix_kernel(x_ref, gt_ref, u_ref, v_ref, zc_ref, pd_ref, zd_ref, lnw_ref, lnb_ref, ws_ref,
                    bs_ref, band_ref, icnt_ref, pw_ref, ps_ref, wo_ref, fnw_ref, out_ref,
                    acc, ln_scr, *, nk_c):
    kk = pl.program_id(1)
    tm = x_ref.shape[0]
    tk = u_ref.shape[1]
    gpb = tk // HEAD_DIM

    @pl.when(kk == 0)
    def _():
        v = v_ref[...]
        mu = jnp.mean(v, axis=-1, keepdims=True)
        xc = v - mu
        var = jnp.mean(xc * xc, axis=-1, keepdims=True)
        ln = (xc * lax.rsqrt(var + EPS) * lnw_ref[...] + lnb_ref[...]).astype(BF16)
        for blk in range(nk_c):
            ln_scr[blk] = ln[:, blk * tk:(blk + 1) * tk]
        acc[...] = jnp.zeros_like(acc)

    @pl.when(kk < nk_c)
    def _():
        ln = ln_scr[kk]
        cols = []
        for gi in range(gpb):
            w_g = ws_ref[gi]
            b_g = bs_ref[gi]
            rows = []
            for cc in range(tm // TOK_CHUNK):
                blk = ln[cc * TOK_CHUNK:(cc + 1) * TOK_CHUNK, gi * HEAD_DIM:(gi + 1) * HEAD_DIM]
                rows.append(jnp.dot(w_g, blk, preferred_element_type=F32) + b_g)
            cols.append(jnp.concatenate(rows, axis=0))
        s = jnp.concatenate(cols, axis=1)
        yc = u_ref[...] * s * zc_ref[...]
        acc[...] += jnp.dot(yc.astype(BF16), wo_ref[...], preferred_element_type=F32)

    @pl.when(kk >= nk_c)
    def _():
        pd = pd_ref[...]
        hi = pd.astype(BF16)
        mid = (pd - hi.astype(F32)).astype(BF16)
        band = band_ref[0]
        wsum = (jnp.dot(band, hi, preferred_element_type=F32)
                + jnp.dot(band, mid, preferred_element_type=F32))
        icnt = icnt_ref[0]
        icnt = jnp.concatenate([icnt] * (tk // LANES), axis=1)
        diff = wsum * icnt - pd
        y = jnp.dot(diff.astype(BF16), pw_ref[0], preferred_element_type=F32)
        yd = y * ps_ref[...] * zd_ref[...]
        acc[...] += jnp.dot(yd.astype(BF16), wo_ref[...], preferred_element_type=F32)

    @pl.when(kk == pl.num_programs(1) - 1)
    def _():
        xn = x_ref[...] + gt_ref[0] * acc[...]
        ms = jnp.mean(xn * xn, axis=-1, keepdims=True)
        out_ref[...] = xn * lax.rsqrt(ms + EPS) * fnw_ref[...]


def _pool_constants(tm):
    bands = np.zeros((POOL_GROUPS, tm, tm), np.float32)
    icnt = np.zeros((POOL_GROUPS, tm, LANES), np.float32)
    pos = np.arange(tm)
    row = pos // GRID_W
    col = pos % GRID_W
    for g, rad in enumerate(POOL_RADII):
        same = row[:, None] == row[None, :]
        bands[g] = (same & (np.abs(col[:, None] - col[None, :]) <= rad)).astype(np.float32)
        cnt = np.minimum(col + rad + 1, GRID_W) - np.maximum(col - rad, 0)
        icnt[g] = (1.0 / cnt.astype(np.float32))[:, None]
    return jnp.asarray(bands, BF16), jnp.asarray(icnt, F32)


def _odd_mix(xin, mods, p, ln_w, ln_b, w_s, b_s, pool_w, pool_scale, w_out, fnw):
    r, d = xin.shape
    tm, tk = 512, 512
    nk_c = d // tk
    nk = 2 * nk_c
    gpb = tk // HEAD_DIM
    tiles_per_batch = SEQ // tm
    bands, icnt = _pool_constants(tm)
    lo = lambda kk: jnp.minimum(kk, nk_c - 1)
    hi = lambda kk: jnp.maximum(kk - nk_c, 0)
    cblk = d // tk
    return pl.pallas_call(
        functools.partial(_odd_mix_kernel, nk_c=nk_c),
        grid=(r // tm, nk),
        in_specs=[pl.BlockSpec((tm, d), lambda i, kk: (i, 0)),
                  pl.BlockSpec((1, 1, d), lambda i, kk: (i // tiles_per_batch, 0, 2)),
                  pl.BlockSpec((tm, tk), lambda i, kk: (i, lo(kk))),
                  pl.BlockSpec((tm, d), lambda i, kk: (i, 1)),
                  pl.BlockSpec((tm, tk), lambda i, kk: (i, 2 * cblk + lo(kk))),
                  pl.BlockSpec((tm, tk), lambda i, kk: (i, 3 * cblk + hi(kk))),
                  pl.BlockSpec((tm, tk), lambda i, kk: (i, 4 * cblk + hi(kk))),
                  pl.BlockSpec((1, d), lambda i, kk: (0, 0)),
                  pl.BlockSpec((1, d), lambda i, kk: (0, 0)),
                  pl.BlockSpec((gpb, TOK_CHUNK, TOK_CHUNK), lambda i, kk: (lo(kk), 0, 0)),
                  pl.BlockSpec((gpb, TOK_CHUNK, LANES), lambda i, kk: (lo(kk), 0, 0)),
                  pl.BlockSpec((1, tm, tm), lambda i, kk: (hi(kk), 0, 0)),
                  pl.BlockSpec((1, tm, LANES), lambda i, kk: (hi(kk), 0, 0)),
                  pl.BlockSpec((1, POOL_GW, POOL_GW), lambda i, kk: (hi(kk), 0, 0)),
                  pl.BlockSpec((1, tk), lambda i, kk: (0, hi(kk))),
                  pl.BlockSpec((tk, d), lambda i, kk: (kk, 0)),
                  pl.BlockSpec((1, d), lambda i, kk: (0, 0))],
        out_specs=pl.BlockSpec((tm, d), lambda i, kk: (i, 0)),
        out_shape=jax.ShapeDtypeStruct((r, d), F32),
        scratch_shapes=[pltpu.VMEM((tm, d), F32), pltpu.VMEM((nk_c, tm, tk), BF16)],
        compiler_params=_params(("parallel", "arbitrary")),
        name="odd_mix",
    )(xin, mods, p, p, p, p, p, ln_w, ln_b, w_s, b_s, bands, icnt, pool_w, pool_scale, w_out, fnw)


def kernel(x, c, ctx, c_ctx, ada_w, ada_b, norm_w, e_w_in, e_conv_qkv, e_a_log, e_dt_bias, e_head_norm,
           e_conv_b, e_w_out, o_w_in, o_ln_w, o_ln_b, o_w_s, o_b_s, o_pool_w, o_pool_scale, o_w_out,
           final_norm_w):
    bsz, seq, d = x.shape
    rx = bsz * seq
    rc = bsz * ctx.shape[1]
    x2 = x.reshape(rx, d)
    ctx2 = ctx.reshape(rc, d)

    cc = jnp.concatenate([c, c_ctx[None, :], jnp.zeros((8 - bsz - 1, d), F32)], axis=0)
    mods = _adaln(cc, ada_w, ada_b)
    ctx_row = bsz

    w_in = e_w_in[0]
    gate_lo = QKV_W + D_MODEL
    gate_hi = gate_lo + 4 * HEADS
    wqkv = w_in[:, :QKV_W].astype(BF16)
    wg = jnp.pad(w_in[:, gate_lo:gate_hi], ((0, 0), (0, LANES - 4 * HEADS))).astype(BF16)
    w5 = jnp.concatenate([w_in[:, QKV_W:gate_lo], w_in[:, gate_hi:]], axis=1).astype(BF16)
    m0 = mods[0].reshape(8, 1, 3 * d)
    nw0 = norm_w[0].reshape(1, d)

    qkv_x, gates_x = _qkv_proj(x2, m0, nw0, wqkv, wg, e_conv_qkv[0],
                               rows_per_batch=seq, mod_row0=0, group=GRID_W)
    qkv_c, gates_c = _qkv_proj(ctx2, m0, nw0, wqkv, wg, e_conv_qkv[0],
                               rows_per_batch=ctx.shape[1], mod_row0=ctx_row, group=ctx.shape[1])
    gz_x, yb_x = _mixb_proj(x2, m0, nw0, w5, e_conv_b[0],
                            rows_per_batch=seq, mod_row0=0, group=GRID_W)
    gz_c, yb_c = _mixb_proj(ctx2, m0, nw0, w5, e_conv_b[0],
                            rows_per_batch=ctx.shape[1], mod_row0=ctx_row, group=ctx.shape[1])
    o_c, o_x = _gdn(e_a_log[0], e_dt_bias[0], qkv_c, qkv_x, gates_c, gates_x,
                    e_head_norm[0].reshape(1, HEAD_DIM))
    w_out0 = e_w_out[0].astype(BF16)
    x2 = _even_out(x2, m0, o_x.reshape(rx, d), gz_x, yb_x, w_out0, rows_per_batch=seq, mod_row0=0)
    ctx2 = _even_out(ctx2, m0, o_c.reshape(rc, d), gz_c, yb_c, w_out0,
                     rows_per_batch=ctx.shape[1], mod_row0=ctx_row)

    m1 = mods[1].reshape(8, 1, 3 * d)
    p = _odd_in(x2, m1, norm_w[1].reshape(1, d), o_w_in[0].astype(BF16))
    b_s = jnp.broadcast_to(o_b_s[0][:, :, None], (o_b_s.shape[1], TOK_CHUNK, LANES))
    out = _odd_mix(x2, m1, p, o_ln_w[0].reshape(1, d), o_ln_b[0].reshape(1, d),
                   o_w_s[0].astype(BF16), b_s, o_pool_w[0].astype(BF16),
                   o_pool_scale[0].reshape(1, d), o_w_out[0].astype(BF16),
                   final_norm_w.reshape(1, d))
    return out.reshape(bsz, seq, d)
```

```python
import functools
import math

import numpy as np
import jax
import jax.numpy as jnp
from jax import lax
from jax.experimental import pallas as pl
from jax.experimental.pallas import tpu as pltpu

F32 = jnp.float32
BF16 = jnp.bfloat16

D_MODEL = 2048
BATCH = 4
SEQ = 2048
CTX_LEN = 256
GRID_W = 64
EPS = 1e-6
HEADS = 16
HEAD_DIM = 128
QKV_W = 3 * D_MODEL
CHUNK = 64
PREP_UNROLL = 18
TOK_CHUNK = 128
POOL_GROUPS = 4
POOL_GW = D_MODEL // POOL_GROUPS
POOL_RADII = (1, 2, 4, 8)

LANES = 128
VMEM_LIMIT = 56 * 1024 * 1024


def _params(sem):
    return pltpu.CompilerParams(dimension_semantics=sem, vmem_limit_bytes=VMEM_LIMIT)


def _sigmoid(x):
    return 1.0 / (1.0 + jnp.exp(-x))


def _silu(x):
    return x * _sigmoid(x)


def _gelu_tanh(x):
    cdf = 0.5 * (1.0 + jnp.tanh(math.sqrt(2.0 / math.pi) * (x + 0.044715 * (x * x * x))))
    return x * cdf


def _modulated_norm(x, nw, sc, sh):
    ms = jnp.mean(x * x, axis=-1, keepdims=True)
    y = x * lax.rsqrt(ms + EPS) * nw
    return y * (1.0 + sc) + sh


def _row_conv3(p, cw, group):
    tm = p.shape[0]
    row = lax.broadcasted_iota(jnp.int32, (tm, 1), 0) % group
    prev = jnp.where(row == 0, 0.0, pltpu.roll(p, 1, 0))
    nxt = jnp.where(row == group - 1, 0.0, pltpu.roll(p, tm - 1, 0))
    return prev * cw[0:1] + p * cw[1:2] + nxt * cw[2:3]


def _adaln_kernel(c_ref, w_ref, b_ref, o_ref):
    a = _silu(c_ref[...]).astype(BF16)
    o_ref[0] = jnp.dot(a, w_ref[0].astype(BF16), preferred_element_type=F32) + b_ref[0]


def _adaln(cc, ada_w, ada_b):
    depth, d, n = ada_w.shape
    tn = 1024
    return pl.pallas_call(
        _adaln_kernel,
        grid=(depth, n // tn),
        in_specs=[pl.BlockSpec((8, d), lambda l, j: (0, 0)),
                  pl.BlockSpec((1, d, tn), lambda l, j: (l, 0, j)),
                  pl.BlockSpec((1, 1, tn), lambda l, j: (l, 0, j))],
        out_specs=pl.BlockSpec((1, 8, tn), lambda l, j: (l, 0, j)),
        out_shape=jax.ShapeDtypeStruct((depth, 8, n), F32),
        compiler_params=_params(("parallel", "parallel")),
        name="adaln",
    )(cc, ada_w, ada_b.reshape(depth, 1, n))


def _qkv_kernel(x_ref, nw_ref, sh_ref, sc_ref, w_ref, wg_ref, cw_ref, gp_ref, qkv_ref, g_ref, h_scr,
                *, group, n_tiles):
    j = pl.program_id(1)

    @pl.when(j == 0)
    def _():
        h_scr[...] = _modulated_norm(x_ref[...], nw_ref[...], sc_ref[0], sh_ref[0]).astype(BF16)

    @pl.when(j < n_tiles)
    def _():
        p = jnp.dot(h_scr[...], w_ref[...], preferred_element_type=F32)
        y = _silu(_row_conv3(p, cw_ref[...], group))
        for hh in range(y.shape[1] // HEAD_DIM):
            qkv_ref[hh] = y[:, hh * HEAD_DIM:(hh + 1) * HEAD_DIM]

    @pl.when(j == n_tiles)
    def _():
        raw = jnp.dot(h_scr[...], wg_ref[...], preferred_element_type=F32)
        z = raw + gp_ref[1:2]
        softplus = jnp.maximum(z, 0.0) + jnp.log1p(jnp.exp(-jnp.abs(z)))
        lane = lax.broadcasted_iota(jnp.int32, (1, LANES), 1)
        g_ref[...] = jnp.where(lane < 2 * HEADS, -jnp.exp(gp_ref[0:1]) * softplus, _sigmoid(raw))


def _qkv_proj(xin, mods, nw, wqkv, wg, cw, gate_par, *, rows_per_batch, mod_row0, group):
    r, d = xin.shape
    tm, tn = 1024, 768
    n_tiles = QKV_W // tn
    hpt = tn // HEAD_DIM
    tiles_per_batch = max(rows_per_batch // tm, 1)

    def mod_idx(part):
        if rows_per_batch >= tm:
            return lambda i, j: (mod_row0 + i // tiles_per_batch, 0, part)
        return lambda i, j: (mod_row0, 0, part)

    last = n_tiles - 1
    return pl.pallas_call(
        functools.partial(_qkv_kernel, group=group, n_tiles=n_tiles),
        grid=(r // tm, n_tiles + 1),
        in_specs=[pl.BlockSpec((tm, d), lambda i, j: (i, 0)),
                  pl.BlockSpec((1, d), lambda i, j: (0, 0)),
                  pl.BlockSpec((1, 1, d), mod_idx(0)),
                  pl.BlockSpec((1, 1, d), mod_idx(1)),
                  pl.BlockSpec((d, tn), lambda i, j: (0, jnp.minimum(j, last))),
                  pl.BlockSpec((d, LANES), lambda i, j: (0, 0)),
                  pl.BlockSpec((3, tn), lambda i, j: (0, jnp.minimum(j, last))),
                  pl.BlockSpec((2, LANES), lambda i, j: (0, 0))],
        out_specs=[pl.BlockSpec((hpt, tm, HEAD_DIM), lambda i, j: (jnp.minimum(j, last), i, 0)),
                   pl.BlockSpec((tm, LANES), lambda i, j: (i, 0))],
        out_shape=[jax.ShapeDtypeStruct((3 * HEADS, r, HEAD_DIM), F32),
                   jax.ShapeDtypeStruct((r, LANES), F32)],
        scratch_shapes=[pltpu.VMEM((tm, d), BF16)],
        compiler_params=_params(("parallel", "arbitrary")),
        name="qkv_proj",
    )(xin, nw, mods, mods, wqkv, wg, cw, gate_par)


def _mixb_kernel(x_ref, nw_ref, sh_ref, sc_ref, wz_ref, wb_ref, wc_ref, wh_ref, wzb_ref, cw_ref,
                 gz_ref, yb_ref, h_scr, *, group):
    j = pl.program_id(1)

    @pl.when(j == 0)
    def _():
        h_scr[...] = _modulated_norm(x_ref[...], nw_ref[...], sc_ref[0], sh_ref[0]).astype(BF16)

    h = h_scr[...]
    dot = lambda w: jnp.dot(h, w[...], preferred_element_type=F32)
    gz_ref[...] = _silu(dot(wz_ref))
    conv = _row_conv3(dot(wc_ref) * dot(wh_ref), cw_ref[...], group)
    yb_ref[...] = dot(wb_ref) * conv * _silu(dot(wzb_ref))


def _mixb_proj(xin, mods, nw, w5, cw, *, rows_per_batch, mod_row0, group):
    r, d = xin.shape
    tm, tn = 1024, 256
    nj = d // tn
    tiles_per_batch = max(rows_per_batch // tm, 1)

    def mod_idx(part):
        if rows_per_batch >= tm:
            return lambda i, j: (mod_row0 + i // tiles_per_batch, 0, part)
        return lambda i, j: (mod_row0, 0, part)

    wspec = lambda reg: pl.BlockSpec((d, tn), lambda i, j: (0, reg * nj + j))
    return pl.pallas_call(
        functools.partial(_mixb_kernel, group=group),
        grid=(r // tm, nj),
        in_specs=[pl.BlockSpec((tm, d), lambda i, j: (i, 0)),
                  pl.BlockSpec((1, d), lambda i, j: (0, 0)),
                  pl.BlockSpec((1, 1, d), mod_idx(0)),
                  pl.BlockSpec((1, 1, d), mod_idx(1)),
                  wspec(0), wspec(1), wspec(2), wspec(3), wspec(4),
                  pl.BlockSpec((3, tn), lambda i, j: (0, j))],
        out_specs=[pl.BlockSpec((tm, tn), lambda i, j: (i, j)),
                   pl.BlockSpec((tm, tn), lambda i, j: (i, j))],
        out_shape=[jax.ShapeDtypeStruct((r, d), F32), jax.ShapeDtypeStruct((r, d), F32)],
        scratch_shapes=[pltpu.VMEM((tm, d), BF16)],
        compiler_params=_params(("parallel", "arbitrary")),
        name="mixb_proj",
    )(xin, nw, mods, mods, w5, w5, w5, w5, w5, cw)


def _split3_bf16(x):
    x1 = x.astype(BF16)
    r1 = x - x1.astype(F32)
    x2 = r1.astype(BF16)
    x3 = (r1 - x2.astype(F32)).astype(BF16)
    return x1, x2, x3


def _dots_exact01(m01s, xs):
    pieces = [_split3_bf16(x) for x in xs]
    prods = [[jnp.dot(m01, t, preferred_element_type=F32) for t in p] for m01, p in zip(m01s, pieces)]
    return [(p[0] + p[1]) + p[2] for p in prods]


def _block_diag2(x, left):
    return jnp.concatenate([jnp.where(left, x, 0.0), jnp.where(left, 0.0, x)], axis=0).astype(BF16)


def _unit_tri_inverse_minus_eye(ms, left, ri, cj):
    n = ms[0].shape[0]
    ys = None
    s = 1
    while s < n:
        shift = s.bit_length() - 1
        br = lax.shift_right_logical(ri, shift)
        bc = lax.shift_right_logical(cj, shift)
        mask_lo = ((br & 1) == 1) & (bc == br - 1)
        mask_up = ((bc & 1) == 1) & (br == bc - 1)
        mask = (left & mask_lo) | (jnp.logical_not(left) & mask_up)
        cms = [jnp.where(mask, m, 0.0) for m in ms]
        if ys is None:
            ys = [-cm for cm in cms]
        else:
            ps = [cm + jnp.dot(y.astype(BF16), _block_diag2(cm, left), preferred_element_type=F32)
                  for cm, y in zip(cms, ys)]
            ys = [y - p - jnp.dot(p.astype(BF16), _block_diag2(y, left), preferred_element_type=F32)
                  for y, p in zip(ys, ps)]
        s *= 2
    return ys


def _gdn_kernel(qc_ref, kc_ref, vc_ref, qx_ref, kx_ref, vx_ref, gc_ref, gx_ref,
                hn_ref, oc_ref, ox_ref,
                qs, ks, vs, gsc, bsc, wq_s, u_s, ak_s, gl_s, oacc):
    h = pl.program_id(1)
    tc = qc_ref.shape[1]
    tx = qx_ref.shape[1]
    t = tc + tx
    nchunks = t // CHUNK
    nc_ctx = tc // CHUNK
    c = CHUNK

    def l2n(a):
        return a * lax.rsqrt(jnp.sum(a * a, axis=-1, keepdims=True) + EPS)

    qscale = HEAD_DIM ** -0.5
    qs[0:tc] = l2n(qc_ref[0]) * qscale
    qs[tc:t] = l2n(qx_ref[0]) * qscale
    ks[0:tc] = l2n(kc_ref[0])
    ks[tc:t] = l2n(kx_ref[0])
    vs[0:tc] = vc_ref[0]
    vs[tc:t] = vx_ref[0]
    oacc[...] = jnp.zeros_like(oacc)

    lane = lax.broadcasted_iota(jnp.int32, (1, LANES), 1)
    for d in range(2):
        for ref, lo, hi in ((gc_ref, 0, tc), (gx_ref, tc, t)):
            gates = ref[...]
            g = jnp.sum(jnp.where(lane == d * HEADS + h, gates, 0.0), axis=-1, keepdims=True)
            b = jnp.sum(jnp.where(lane == 2 * HEADS + d * HEADS + h, gates, 0.0), axis=-1,
                        keepdims=True)
            gsc[d, lo:hi] = jnp.broadcast_to(g, (hi - lo, LANES))
            bsc[d, lo:hi] = jnp.broadcast_to(b, (hi - lo, LANES))

    ri1 = lax.broadcasted_iota(jnp.int32, (c, c), 0)
    ci1 = lax.broadcasted_iota(jnp.int32, (c, c), 1)
    cum_lo = jnp.where(ci1 <= ri1, 1.0, 0.0).astype(BF16)
    cum_up = jnp.where(ci1 >= ri1, 1.0, 0.0).astype(BF16)
    ri = lax.broadcasted_iota(jnp.int32, (c, 2 * c), 0)
    lane2 = lax.broadcasted_iota(jnp.int32, (c, 2 * c), 1)
    left = lane2 < c
    cj = lane2 & (c - 1)
    strict = (left & (cj < ri)) | (jnp.logical_not(left) & (cj > ri))
    incl = (left & (cj <= ri)) | (jnp.logical_not(left) & (cj >= ri))
    left_sq = lax.broadcasted_iota(jnp.int32, (2 * c, 2 * c), 1) < c

    def prep_group(i, carry):
        units = range(PREP_UNROLL)
        ns = [i * PREP_UNROLL + uu for uu in units]
        rows = [pl.ds(pl.multiple_of(n * c, c), c) for n in ns]
        qv = [qs[r] for r in rows]
        kv = [ks[r] for r in rows]
        vv = [vs[r] for r in rows]
        beta = [[bsc[d, r] for d in range(2)] for r in rows]
        kb = [[kv[uu] * beta[uu][d] for d in range(2)] for uu in units]
        gk = [lax.dot_general(
                  jnp.concatenate([kb[uu][0], kb[uu][1], qv[uu]], axis=0).astype(BF16),
                  jnp.concatenate([kv[uu], kv[uu]], axis=0).astype(BF16),
                  (((1,), (1,)), ((), ())), preferred_element_type=F32)
              for uu in units]
        gcb = _dots_exact01([cum_lo, cum_up] * PREP_UNROLL,
                            [gsc[d, r] for r in rows for d in range(2)])
        gcb = [gcb[2 * uu:2 * uu + 2] for uu in units]
        ms, avs = [], []
        for uu in units:
            col = jnp.where(left, gcb[uu][0], gcb[uu][1])
            row = jnp.transpose(jnp.concatenate(gcb[uu], axis=0))[0:c]
            decay = jnp.exp(jnp.where(incl, col - row, 0.0))
            ms.append(jnp.where(strict, jnp.where(left, gk[uu][0:c], gk[uu][c:2 * c]) * decay, 0.0))
            avs.append(jnp.where(incl, gk[uu][2 * c:3 * c] * decay, 0.0))
        ys = _unit_tri_inverse_minus_eye(ms, left, ri, cj)
        egc = [[jnp.exp(g) for g in gcb[uu]] for uu in units]
        rs = [[jnp.concatenate([kb[uu][d] * egc[uu][d], vv[uu] * beta[uu][d]], axis=1)
               for d in range(2)] for uu in units]
        corr = [jnp.dot(_block_diag2(ys[uu], left),
                        jnp.concatenate(rs[uu], axis=0).astype(BF16), preferred_element_type=F32)
                for uu in units]
        for uu in units:
            kgs = []
            for d in range(2):
                edge = c - 1 if d == 0 else 0
                g_edge = gcb[uu][d][edge:edge + 1]
                kgs.append(kv[uu] * jnp.exp(g_edge - gcb[uu][d]))
                wu = rs[uu][d] + corr[uu][d * c:(d + 1) * c]
                idx = 2 * ns[uu] + d
                wq_s[idx] = jnp.concatenate([wu[:, 0:HEAD_DIM], qv[uu] * egc[uu][d]],
                                            axis=0).astype(BF16)
                u_s[idx] = wu[:, HEAD_DIM:2 * HEAD_DIM]
                gl_s[idx] = jnp.broadcast_to(jnp.exp(g_edge), (8, LANES))
            kgt = jnp.transpose(jnp.concatenate(kgs, axis=0))
            ak_s[2 * ns[uu]] = jnp.concatenate(
                [jnp.where(left, avs[uu], 0.0), jnp.where(left_sq, kgt, 0.0)], axis=0).astype(BF16)
            ak_s[2 * ns[uu] + 1] = jnp.concatenate(
                [jnp.where(left, 0.0, avs[uu]), jnp.where(left_sq, 0.0, kgt)], axis=0).astype(BF16)
        return carry

    lax.fori_loop(0, nchunks // PREP_UNROLL, prep_group, 0)

    def scan(s, states):
        n_rev = jnp.where(s < nc_ctx, nc_ctx - 1 - s, nchunks + nc_ctx - 1 - s)
        idxs = (2 * s, 2 * n_rev + 1)
        ws = [jnp.dot(wq_s[idx], st.astype(BF16), preferred_element_type=F32)
              for idx, st in zip(idxs, states)]
        v_new = [u_s[idx] - w[0:c] for idx, w in zip(idxs, ws)]
        v_st = jnp.concatenate(v_new, axis=0).astype(BF16)
        ak = [jnp.dot(ak_s[idx], v_st, preferred_element_type=F32) for idx in idxs]
        for n, w, a in zip((s, n_rev), ws, ak):
            rows = pl.ds(pl.multiple_of(n * c, c), c)
            oacc[rows] = oacc[rows] + (w[c:2 * c] + a[0:c])
        return tuple(st * gl_s[idx][0:1] + a[c:c + HEAD_DIM]
                     for idx, st, a in zip(idxs, states, ak))

    zero = jnp.zeros((HEAD_DIM, HEAD_DIM), F32)
    lax.fori_loop(0, nchunks, scan, (zero, zero))

    o = oacc[...]
    on = o * lax.rsqrt(jnp.mean(o * o, axis=-1, keepdims=True) + EPS) * hn_ref[...]
    oc_ref[0] = on[0:tc]
    ox_ref[0] = on[tc:t]


def _gdn(qkv_c, qkv_x, gates_c, gates_x, head_norm):
    tc, tx = CTX_LEN, SEQ
    t = tc + tx
    nidx = 2 * (t // CHUNK)
    part = lambda rows, p: pl.BlockSpec((1, rows, HEAD_DIM), lambda b, h: (p * HEADS + h, b, 0))
    return pl.pallas_call(
        _gdn_kernel,
        grid=(BATCH, HEADS),
        in_specs=[part(tc, 0), part(tc, 1), part(tc, 2),
                  part(tx, 0), part(tx, 1), part(tx, 2),
                  pl.BlockSpec((tc, LANES), lambda b, h: (b, 0)),
                  pl.BlockSpec((tx, LANES), lambda b, h: (b, 0)),
                  pl.BlockSpec((1, HEAD_DIM), lambda b, h: (0, 0))],
        out_specs=[pl.BlockSpec((1, tc, HEAD_DIM), lambda b, h: (b, 0, h)),
                   pl.BlockSpec((1, tx, HEAD_DIM), lambda b, h: (b, 0, h))],
        out_shape=[jax.ShapeDtypeStruct((BATCH, tc, D_MODEL), F32),
                   jax.ShapeDtypeStruct((BATCH, tx, D_MODEL), F32)],
        scratch_shapes=[pltpu.VMEM((t, HEAD_DIM), F32), pltpu.VMEM((t, HEAD_DIM), F32),
                        pltpu.VMEM((t, HEAD_DIM), F32),
                        pltpu.VMEM((2, t, LANES), F32), pltpu.VMEM((2, t, LANES), F32),
                        pltpu.VMEM((nidx, 2 * CHUNK, HEAD_DIM), BF16),
                        pltpu.VMEM((nidx, CHUNK, HEAD_DIM), F32),
                        pltpu.VMEM((nidx, CHUNK + HEAD_DIM, 2 * CHUNK), BF16),
                        pltpu.VMEM((nidx, 8, LANES), F32),
                        pltpu.VMEM((t, HEAD_DIM), F32)],
        compiler_params=_params(("parallel", "arbitrary")),
        name="gdn",
    )(qkv_c, qkv_c, qkv_c, qkv_x, qkv_x, qkv_x, gates_c, gates_x, head_norm)


def _even_out_kernel(x_ref, gt_ref, o_ref, gz_ref, yb_ref, w_ref, out_ref, acc, *, nk_a):
    kk = pl.program_id(1)

    @pl.when(kk == 0)
    def _():
        acc[...] = jnp.zeros_like(acc)

    @pl.when(kk < nk_a)
    def _():
        a = (o_ref[...] * gz_ref[...]).astype(BF16)
        acc[...] += jnp.dot(a, w_ref[...], preferred_element_type=F32)

    @pl.when(kk >= nk_a)
    def _():
        acc[...] += jnp.dot(yb_ref[...].astype(BF16), w_ref[...], preferred_element_type=F32)

    @pl.when(kk == 2 * nk_a - 1)
    def _():
        out_ref[...] = x_ref[...] + gt_ref[0] * acc[...]


def _even_out(xin, mods, o, gz, yb, w_out, *, rows_per_batch, mod_row0):
    r, d = xin.shape
    tm = min(512, rows_per_batch)
    tk = 512
    nk_a = d // tk
    tiles_per_batch = rows_per_batch // tm
    fixed = mod_row0 >= BATCH
    gt_idx = (lambda i, kk: (mod_row0, 0, 2)) if fixed else (lambda i, kk: (i // tiles_per_batch, 0, 2))
    return pl.pallas_call(
        functools.partial(_even_out_kernel, nk_a=nk_a),
        grid=(r // tm, 2 * nk_a),
        in_specs=[pl.BlockSpec((tm, d), lambda i, kk: (i, 0)),
                  pl.BlockSpec((1, 1, d), gt_idx),
                  pl.BlockSpec((tm, tk), lambda i, kk: (i, jnp.minimum(kk, nk_a - 1))),
                  pl.BlockSpec((tm, tk), lambda i, kk: (i, jnp.minimum(kk, nk_a - 1))),
                  pl.BlockSpec((tm, tk), lambda i, kk: (i, jnp.maximum(kk - nk_a, 0))),
                  pl.BlockSpec((tk, d), lambda i, kk: (kk, 0))],
        out_specs=pl.BlockSpec((tm, d), lambda i, kk: (i, 0)),
        out_shape=jax.ShapeDtypeStruct((r, d), F32),
        scratch_shapes=[pltpu.VMEM((tm, d), F32)],
        compiler_params=_params(("parallel", "arbitrary")),
        name="even_out",
    )(xin, mods, o, gz, yb, w_out)


def _odd_in_kernel(x_ref, nw_ref, sh_ref, sc_ref, w_ref, out_ref, h_scr, *, tiles_per_region):
    j = pl.program_id(1)

    @pl.when(j == 0)
    def _():
        h_scr[...] = _modulated_norm(x_ref[...], nw_ref[...], sc_ref[0], sh_ref[0]).astype(BF16)

    p = jnp.dot(h_scr[...], w_ref[...], preferred_element_type=F32)
    region = j // tiles_per_region

    @pl.when(region <= 1)
    def _():
        out_ref[...] = _gelu_tanh(p)

    @pl.when((region == 2) | (region == 4))
    def _():
        out_ref[...] = _silu(p)

    @pl.when(region == 3)
    def _():
        out_ref[...] = p


def _odd_in(xin, mods, nw, w):
    r, d = xin.shape
    n = w.shape[1]
    tm, tn = 1024, 512
    tiles_per_batch = SEQ // tm
    mod_idx = lambda part: (lambda i, j: (i // tiles_per_batch, 0, part))
    return pl.pallas_call(
        functools.partial(_odd_in_kernel, tiles_per_region=d // tn),
        grid=(r // tm, n // tn),
        in_specs=[pl.BlockSpec((tm, d), lambda i, j: (i, 0)),
                  pl.BlockSpec((1, d), lambda i, j: (0, 0)),
                  pl.BlockSpec((1, 1, d), mod_idx(0)),
                  pl.BlockSpec((1, 1, d), mod_idx(1)),
                  pl.BlockSpec((d, tn), lambda i, j: (0, j))],
        out_specs=pl.BlockSpec((tm, tn), lambda i, j: (i, j)),
        out_shape=jax.ShapeDtypeStruct((r, n), F32),
        scratch_shapes=[pltpu.VMEM((tm, d), BF16)],
        compiler_params=_params(("parallel", "arbitrary")),
        name="odd_in",
    )(xin, nw, mods, mods, w)


def _odd_mix_kernel(x_ref, gt_ref, u_ref, v_ref, zc_ref, pd_ref, zd_ref, lnw_ref, lnb_ref, ws_ref,
                    bs_ref, band_ref, icnt_ref, pw_ref, ps_ref, wo_ref, fnw_ref, out_ref,
                    acc, ln_scr, *, nk_c):
    kk = pl.program_id(1)
    tm = x_ref.shape[0]
    tk = u_ref.shape[1]
    gpb = tk // HEAD_DIM

    @pl.when(kk == 0)
    def _():
        v = v_ref[...]
        mu = jnp.mean(v, axis=-1, keepdims=True)
        xc = v - mu
        var = jnp.mean(xc * xc, axis=-1, keepdims=True)
        ln = (xc * lax.rsqrt(var + EPS) * lnw_ref[...] + lnb_ref[...]).astype(BF16)
        for blk in range(nk_c):
            ln_scr[blk] = ln[:, blk * tk:(blk + 1) * tk]
        acc[...] = jnp.zeros_like(acc)

    @pl.when(kk < nk_c)
    def _():
        ln = ln_scr[kk]
        cols = []
        for gi in range(gpb):
            w_g = ws_ref[gi]
            b_g = bs_ref[gi]
            rows = []
            for cc in range(tm // TOK_CHUNK):
                blk = ln[cc * TOK_CHUNK:(cc + 1) * TOK_CHUNK, gi * HEAD_DIM:(gi + 1) * HEAD_DIM]
                rows.append(jnp.dot(w_g, blk, preferred_element_type=F32) + b_g)
            cols.append(jnp.concatenate(rows, axis=0))
        s = jnp.concatenate(cols, axis=1)
        yc = u_ref[...] * s * zc_ref[...]
        acc[...] += jnp.dot(yc.astype(BF16), wo_ref[...], preferred_element_type=F32)

    @pl.when(kk >= nk_c)
    def _():
        pd = pd_ref[...]
        hi = pd.astype(BF16)
        mid = (pd - hi.astype(F32)).astype(BF16)
        band = band_ref[0]
        wsum = (jnp.dot(band, hi, preferred_element_type=F32)
                + jnp.dot(band, mid, preferred_element_type=F32))
        icnt = icnt_ref[0]
        icnt = jnp.concatenate([icnt] * (tk // LANES), axis=1)
        diff = wsum * icnt - pd
        y = jnp.dot(diff.astype(BF16), pw_ref[0], preferred_element_type=F32)
        yd = y * ps_ref[...] * zd_ref[...]
        acc[...] += jnp.dot(yd.astype(BF16), wo_ref[...], preferred_element_type=F32)

    @pl.when(kk == pl.num_programs(1) - 1)
    def _():
        xn = x_ref[...] + gt_ref[0] * acc[...]
        ms = jnp.mean(xn * xn, axis=-1, keepdims=True)
        out_ref[...] = xn * lax.rsqrt(ms + EPS) * fnw_ref[...]


def _pool_constants(tm):
    bands = np.zeros((POOL_GROUPS, tm, tm), np.float32)
    icnt = np.zeros((POOL_GROUPS, tm, LANES), np.float32)
    pos = np.arange(tm)
    row = pos // GRID_W
    col = pos % GRID_W
    for g, rad in enumerate(POOL_RADII):
        same = row[:, None] == row[None, :]
        bands[g] = (same & (np.abs(col[:, None] - col[None, :]) <= rad)).astype(np.float32)
        cnt = np.minimum(col + rad + 1, GRID_W) - np.maximum(col - rad, 0)
        icnt[g] = (1.0 / cnt.astype(np.float32))[:, None]
    return jnp.asarray(bands, BF16), jnp.asarray(icnt, F32)


def _odd_mix(xin, mods, p, ln_w, ln_b, w_s, b_s, pool_w, pool_scale, w_out, fnw):
    r, d = xin.shape
    tm, tk = 512, 512
    nk_c = d // tk
    nk = 2 * nk_c
    gpb = tk // HEAD_DIM
    tiles_per_batch = SEQ // tm
    bands, icnt = _pool_constants(tm)
    lo = lambda kk: jnp.minimum(kk, nk_c - 1)
    hi = lambda kk: jnp.maximum(kk - nk_c, 0)
    cblk = d // tk
    return pl.pallas_call(
        functools.partial(_odd_mix_kernel, nk_c=nk_c),
        grid=(r // tm, nk),
        in_specs=[pl.BlockSpec((tm, d), lambda i, kk: (i, 0)),
                  pl.BlockSpec((1, 1, d), lambda i, kk: (i // tiles_per_batch, 0, 2)),
                  pl.BlockSpec((tm, tk), lambda i, kk: (i, lo(kk))),
                  pl.BlockSpec((tm, d), lambda i, kk: (i, 1)),
                  pl.BlockSpec((tm, tk), lambda i, kk: (i, 2 * cblk + lo(kk))),
                  pl.BlockSpec((tm, tk), lambda i, kk: (i, 3 * cblk + hi(kk))),
                  pl.BlockSpec((tm, tk), lambda i, kk: (i, 4 * cblk + hi(kk))),
                  pl.BlockSpec((1, d), lambda i, kk: (0, 0)),
                  pl.BlockSpec((1, d), lambda i, kk: (0, 0)),
                  pl.BlockSpec((gpb, TOK_CHUNK, TOK_CHUNK), lambda i, kk: (lo(kk), 0, 0)),
                  pl.BlockSpec((gpb, TOK_CHUNK, LANES), lambda i, kk: (lo(kk), 0, 0)),
                  pl.BlockSpec((1, tm, tm), lambda i, kk: (hi(kk), 0, 0)),
                  pl.BlockSpec((1, tm, LANES), lambda i, kk: (hi(kk), 0, 0)),
                  pl.BlockSpec((1, POOL_GW, POOL_GW), lambda i, kk: (hi(kk), 0, 0)),
                  pl.BlockSpec((1, tk), lambda i, kk: (0, hi(kk))),
                  pl.BlockSpec((tk, d), lambda i, kk: (kk, 0)),
                  pl.BlockSpec((1, d), lambda i, kk: (0, 0))],
        out_specs=pl.BlockSpec((tm, d), lambda i, kk: (i, 0)),
        out_shape=jax.ShapeDtypeStruct((r, d), F32),
        scratch_shapes=[pltpu.VMEM((tm, d), F32), pltpu.VMEM((nk_c, tm, tk), BF16)],
        compiler_params=_params(("parallel", "arbitrary")),
        name="odd_mix",
    )(xin, mods, p, p, p, p, p, ln_w, ln_b, w_s, b_s, bands, icnt, pool_w, pool_scale, w_out, fnw)


def kernel(x, c, ctx, c_ctx, ada_w, ada_b, norm_w, e_w_in, e_conv_qkv, e_a_log, e_dt_bias, e_head_norm,
           e_conv_b, e_w_out, o_w_in, o_ln_w, o_ln_b, o_w_s, o_b_s, o_pool_w, o_pool_scale, o_w_out,
           final_norm_w):
    bsz, seq, d = x.shape
    rx = bsz * seq
    rc = bsz * ctx.shape[1]
    x2 = x.reshape(rx, d)
    ctx2 = ctx.reshape(rc, d)

    cc = jnp.concatenate([c, c_ctx[None, :], jnp.zeros((8 - bsz - 1, d), F32)], axis=0)
    mods = _adaln(cc, ada_w, ada_b)
    ctx_row = bsz

    w_in = e_w_in[0]
    gate_lo = QKV_W + D_MODEL
    gate_hi = gate_lo + 4 * HEADS
    wqkv = w_in[:, :QKV_W].astype(BF16)
    wg = jnp.pad(w_in[:, gate_lo:gate_hi], ((0, 0), (0, LANES - 4 * HEADS))).astype(BF16)
    w5 = jnp.concatenate([w_in[:, QKV_W:gate_lo], w_in[:, gate_hi:]], axis=1).astype(BF16)
    m0 = mods[0].reshape(8, 1, 3 * d)
    nw0 = norm_w[0].reshape(1, d)

    gate_par = jnp.pad(jnp.stack([e_a_log[0].reshape(-1), e_dt_bias[0].reshape(-1)]),
                       ((0, 0), (0, LANES - 2 * HEADS)))
    qkv_x, gates_x = _qkv_proj(x2, m0, nw0, wqkv, wg, e_conv_qkv[0], gate_par,
                               rows_per_batch=seq, mod_row0=0, group=GRID_W)
    qkv_c, gates_c = _qkv_proj(ctx2, m0, nw0, wqkv, wg, e_conv_qkv[0], gate_par,
                               rows_per_batch=ctx.shape[1], mod_row0=ctx_row, group=ctx.shape[1])
    gz_x, yb_x = _mixb_proj(x2, m0, nw0, w5, e_conv_b[0],
                            rows_per_batch=seq, mod_row0=0, group=GRID_W)
    gz_c, yb_c = _mixb_proj(ctx2, m0, nw0, w5, e_conv_b[0],
                            rows_per_batch=ctx.shape[1], mod_row0=ctx_row, group=ctx.shape[1])
    o_c, o_x = _gdn(qkv_c, qkv_x, gates_c, gates_x,
                    e_head_norm[0].reshape(1, HEAD_DIM))
    w_out0 = e_w_out[0].astype(BF16)
    x2 = _even_out(x2, m0, o_x.reshape(rx, d), gz_x, yb_x, w_out0, rows_per_batch=seq, mod_row0=0)
    ctx2 = _even_out(ctx2, m0, o_c.reshape(rc, d), gz_c, yb_c, w_out0,
                     rows_per_batch=ctx.shape[1], mod_row0=ctx_row)

    m1 = mods[1].reshape(8, 1, 3 * d)
    p = _odd_in(x2, m1, norm_w[1].reshape(1, d), o_w_in[0].astype(BF16))
    b_s = jnp.broadcast_to(o_b_s[0][:, :, None], (o_b_s.shape[1], TOK_CHUNK, LANES))
    out = _odd_mix(x2, m1, p, o_ln_w[0].reshape(1, d), o_ln_b[0].reshape(1, d),
                   o_w_s[0].astype(BF16), b_s, o_pool_w[0].astype(BF16),
                   o_pool_scale[0].reshape(1, d), o_w_out[0].astype(BF16),
                   final_norm_w.reshape(1, d))
    return out.reshape(bsz, seq, d)
```

```python
import functools
import math

import numpy as np
import jax
import jax.numpy as jnp
from jax import lax
from jax.experimental import pallas as pl
from jax.experimental.pallas import tpu as pltpu

F32 = jnp.float32
BF16 = jnp.bfloat16

D_MODEL = 2048
BATCH = 4
SEQ = 2048
CTX_LEN = 256
GRID_W = 64
EPS = 1e-6
HEADS = 16
HEAD_DIM = 128
QKV_W = 3 * D_MODEL
CHUNK = 64
PREP_UNROLL = 18
TOK_CHUNK = 128
POOL_GROUPS = 4
POOL_GW = D_MODEL // POOL_GROUPS
POOL_RADII = (1, 2, 4, 8)

LANES = 128
VMEM_LIMIT = 56 * 1024 * 1024


def _params(sem):
    return pltpu.CompilerParams(dimension_semantics=sem, vmem_limit_bytes=VMEM_LIMIT)


def _sigmoid(x):
    return 1.0 / (1.0 + jnp.exp(-x))


def _silu(x):
    return x * _sigmoid(x)


def _gelu_tanh(x):
    cdf = 0.5 * (1.0 + jnp.tanh(math.sqrt(2.0 / math.pi) * (x + 0.044715 * (x * x * x))))
    return x * cdf


def _modulated_norm(x, nw, sc, sh):
    ms = jnp.mean(x * x, axis=-1, keepdims=True)
    y = x * lax.rsqrt(ms + EPS) * nw
    return y * (1.0 + sc) + sh


def _row_conv3(p, cw, group):
    tm = p.shape[0]
    row = lax.broadcasted_iota(jnp.int32, (tm, 1), 0) % group
    prev = jnp.where(row == 0, 0.0, pltpu.roll(p, 1, 0))
    nxt = jnp.where(row == group - 1, 0.0, pltpu.roll(p, tm - 1, 0))
    return prev * cw[0:1] + p * cw[1:2] + nxt * cw[2:3]


def _adaln_kernel(c_ref, w_ref, b_ref, o_ref):
    a = _silu(c_ref[...]).astype(BF16)
    o_ref[0] = jnp.dot(a, w_ref[0].astype(BF16), preferred_element_type=F32) + b_ref[0]


def _adaln(cc, ada_w, ada_b):
    depth, d, n = ada_w.shape
    tn = 1024
    return pl.pallas_call(
        _adaln_kernel,
        grid=(depth, n // tn),
        in_specs=[pl.BlockSpec((8, d), lambda l, j: (0, 0)),
                  pl.BlockSpec((1, d, tn), lambda l, j: (l, 0, j)),
                  pl.BlockSpec((1, 1, tn), lambda l, j: (l, 0, j))],
        out_specs=pl.BlockSpec((1, 8, tn), lambda l, j: (l, 0, j)),
        out_shape=jax.ShapeDtypeStruct((depth, 8, n), F32),
        compiler_params=_params(("parallel", "parallel")),
        name="adaln",
    )(cc, ada_w, ada_b.reshape(depth, 1, n))


def _qkv_kernel(x_ref, nw_ref, sh_ref, sc_ref, w_ref, wg_ref, cw_ref, gp_ref, qkv_ref, g_ref, h_scr,
                *, group, n_tiles):
    j = pl.program_id(1)

    @pl.when(j == 0)
    def _():
        h_scr[...] = _modulated_norm(x_ref[...], nw_ref[...], sc_ref[0], sh_ref[0]).astype(BF16)

    @pl.when(j < n_tiles)
    def _():
        p = jnp.dot(h_scr[...], w_ref[...], preferred_element_type=F32)
        y = _silu(_row_conv3(p, cw_ref[...], group))
        for hh in range(y.shape[1] // HEAD_DIM):
            qkv_ref[hh] = y[:, hh * HEAD_DIM:(hh + 1) * HEAD_DIM]

    @pl.when(j == n_tiles)
    def _():
        raw = jnp.dot(h_scr[...], wg_ref[...], preferred_element_type=F32)
        z = raw + gp_ref[1:2]
        softplus = jnp.maximum(z, 0.0) + jnp.log1p(jnp.exp(-jnp.abs(z)))
        lane = lax.broadcasted_iota(jnp.int32, (1, LANES), 1)
        g_ref[...] = jnp.where(lane < 2 * HEADS, -jnp.exp(gp_ref[0:1]) * softplus, _sigmoid(raw))


def _qkv_proj(xin, mods, nw, wqkv, wg, cw, gate_par, *, rows_per_batch, mod_row0, group):
    r, d = xin.shape
    tm, tn = 1024, 768
    n_tiles = QKV_W // tn
    hpt = tn // HEAD_DIM
    tiles_per_batch = max(rows_per_batch // tm, 1)

    def mod_idx(part):
        if rows_per_batch >= tm:
            return lambda i, j: (mod_row0 + i // tiles_per_batch, 0, part)
        return lambda i, j: (mod_row0, 0, part)

    last = n_tiles - 1
    return pl.pallas_call(
        functools.partial(_qkv_kernel, group=group, n_tiles=n_tiles),
        grid=(r // tm, n_tiles + 1),
        in_specs=[pl.BlockSpec((tm, d), lambda i, j: (i, 0)),
                  pl.BlockSpec((1, d), lambda i, j: (0, 0)),
                  pl.BlockSpec((1, 1, d), mod_idx(0)),
                  pl.BlockSpec((1, 1, d), mod_idx(1)),
                  pl.BlockSpec((d, tn), lambda i, j: (0, jnp.minimum(j, last))),
                  pl.BlockSpec((d, LANES), lambda i, j: (0, 0)),
                  pl.BlockSpec((3, tn), lambda i, j: (0, jnp.minimum(j, last))),
                  pl.BlockSpec((2, LANES), lambda i, j: (0, 0))],
        out_specs=[pl.BlockSpec((hpt, tm, HEAD_DIM), lambda i, j: (jnp.minimum(j, last), i, 0)),
                   pl.BlockSpec((tm, LANES), lambda i, j: (i, 0))],
        out_shape=[jax.ShapeDtypeStruct((3 * HEADS, r, HEAD_DIM), F32),
                   jax.ShapeDtypeStruct((r, LANES), F32)],
        scratch_shapes=[pltpu.VMEM((tm, d), BF16)],
        compiler_params=_params(("parallel", "arbitrary")),
        name="qkv_proj",
    )(xin, nw, mods, mods, wqkv, wg, cw, gate_par)


def _mixb_kernel(x_ref, nw_ref, sh_ref, sc_ref, wz_ref, wb_ref, wc_ref, wh_ref, wzb_ref, cw_ref,
                 gz_ref, yb_ref, h_scr, *, group):
    j = pl.program_id(1)

    @pl.when(j == 0)
    def _():
        h_scr[...] = _modulated_norm(x_ref[...], nw_ref[...], sc_ref[0], sh_ref[0]).astype(BF16)

    h = h_scr[...]
    dot = lambda w: jnp.dot(h, w[...], preferred_element_type=F32)
    gz_ref[...] = _silu(dot(wz_ref))
    conv = _row_conv3(dot(wc_ref) * dot(wh_ref), cw_ref[...], group)
    yb_ref[...] = dot(wb_ref) * conv * _silu(dot(wzb_ref))


def _mixb_proj(xin, mods, nw, w5, cw, *, rows_per_batch, mod_row0, group):
    r, d = xin.shape
    tm, tn = 1024, 256
    nj = d // tn
    tiles_per_batch = max(rows_per_batch // tm, 1)

    def mod_idx(part):
        if rows_per_batch >= tm:
            return lambda i, j: (mod_row0 + i // tiles_per_batch, 0, part)
        return lambda i, j: (mod_row0, 0, part)

    wspec = lambda reg: pl.BlockSpec((d, tn), lambda i, j: (0, reg * nj + j))
    return pl.pallas_call(
        functools.partial(_mixb_kernel, group=group),
        grid=(r // tm, nj),
        in_specs=[pl.BlockSpec((tm, d), lambda i, j: (i, 0)),
                  pl.BlockSpec((1, d), lambda i, j: (0, 0)),
                  pl.BlockSpec((1, 1, d), mod_idx(0)),
                  pl.BlockSpec((1, 1, d), mod_idx(1)),
                  wspec(0), wspec(1), wspec(2), wspec(3), wspec(4),
                  pl.BlockSpec((3, tn), lambda i, j: (0, j))],
        out_specs=[pl.BlockSpec((tm, tn), lambda i, j: (i, j)),
                   pl.BlockSpec((tm, tn), lambda i, j: (i, j))],
        out_shape=[jax.ShapeDtypeStruct((r, d), F32), jax.ShapeDtypeStruct((r, d), F32)],
        scratch_shapes=[pltpu.VMEM((tm, d), BF16)],
        compiler_params=_params(("parallel", "arbitrary")),
        name="mixb_proj",
    )(xin, nw, mods, mods, w5, w5, w5, w5, w5, cw)


def _split3_bf16(x):
    x1 = x.astype(BF16)
    r1 = x - x1.astype(F32)
    x2 = r1.astype(BF16)
    x3 = (r1 - x2.astype(F32)).astype(BF16)
    return x1, x2, x3


def _dots_exact01(m01s, xs):
    pieces = [_split3_bf16(x) for x in xs]
    prods = [[jnp.dot(m01, t, preferred_element_type=F32) for t in p] for m01, p in zip(m01s, pieces)]
    return [(p[0] + p[1]) + p[2] for p in prods]


def _block_diag2(x, left):
    return jnp.concatenate([jnp.where(left, x, 0.0), jnp.where(left, 0.0, x)], axis=0).astype(BF16)


def _unit_tri_inverse_minus_eye(ms, left, ri, cj):
    n = ms[0].shape[0]
    ys = None
    s = 1
    while s < n:
        shift = s.bit_length() - 1
        br = lax.shift_right_logical(ri, shift)
        bc = lax.shift_right_logical(cj, shift)
        mask_lo = ((br & 1) == 1) & (bc == br - 1)
        mask_up = ((bc & 1) == 1) & (br == bc - 1)
        mask = (left & mask_lo) | (jnp.logical_not(left) & mask_up)
        cms = [jnp.where(mask, m, 0.0) for m in ms]
        if ys is None:
            ys = [-cm for cm in cms]
        else:
            ps = [cm + jnp.dot(y.astype(BF16), _block_diag2(cm, left), preferred_element_type=F32)
                  for cm, y in zip(cms, ys)]
            ys = [y - p - jnp.dot(p.astype(BF16), _block_diag2(y, left), preferred_element_type=F32)
                  for y, p in zip(ys, ps)]
        s *= 2
    return ys


def _gdn_kernel(qc_ref, kc_ref, vc_ref, qx_ref, kx_ref, vx_ref, gc_ref, gx_ref,
                hn_ref, oc_ref, ox_ref,
                qs, ks, vs, gsc, bsc, wq_s, u_s, ak_s, gl_s, oacc):
    h = pl.program_id(1)
    tc = qc_ref.shape[1]
    tx = qx_ref.shape[1]
    t = tc + tx
    nchunks = t // CHUNK
    nc_ctx = tc // CHUNK
    c = CHUNK

    def l2n(a):
        return a * lax.rsqrt(jnp.sum(a * a, axis=-1, keepdims=True) + EPS)

    qscale = HEAD_DIM ** -0.5
    qs[0:tc] = l2n(qc_ref[0]) * qscale
    qs[tc:t] = l2n(qx_ref[0]) * qscale
    ks[0:tc] = l2n(kc_ref[0])
    ks[tc:t] = l2n(kx_ref[0])
    vs[0:tc] = vc_ref[0]
    vs[tc:t] = vx_ref[0]
    oacc[...] = jnp.zeros_like(oacc)

    lane = lax.broadcasted_iota(jnp.int32, (1, LANES), 1)
    for d in range(2):
        for ref, lo, hi in ((gc_ref, 0, tc), (gx_ref, tc, t)):
            gates = ref[...]
            g = jnp.sum(jnp.where(lane == d * HEADS + h, gates, 0.0), axis=-1, keepdims=True)
            b = jnp.sum(jnp.where(lane == 2 * HEADS + d * HEADS + h, gates, 0.0), axis=-1,
                        keepdims=True)
            gsc[d, lo:hi] = jnp.broadcast_to(g, (hi - lo, LANES))
            bsc[d, lo:hi] = jnp.broadcast_to(b, (hi - lo, LANES))

    ri1 = lax.broadcasted_iota(jnp.int32, (c, c), 0)
    ci1 = lax.broadcasted_iota(jnp.int32, (c, c), 1)
    cum_lo = jnp.where(ci1 <= ri1, 1.0, 0.0).astype(BF16)
    cum_up = jnp.where(ci1 >= ri1, 1.0, 0.0).astype(BF16)
    ri = lax.broadcasted_iota(jnp.int32, (c, 2 * c), 0)
    lane2 = lax.broadcasted_iota(jnp.int32, (c, 2 * c), 1)
    left = lane2 < c
    cj = lane2 & (c - 1)
    strict = (left & (cj < ri)) | (jnp.logical_not(left) & (cj > ri))
    incl = (left & (cj <= ri)) | (jnp.logical_not(left) & (cj >= ri))
    left_sq = lax.broadcasted_iota(jnp.int32, (2 * c, 2 * c), 1) < c

    def prep_group(i, carry):
        units = range(PREP_UNROLL)
        ns = [i * PREP_UNROLL + uu for uu in units]
        rows = [pl.ds(pl.multiple_of(n * c, c), c) for n in ns]
        qv = [qs[r] for r in rows]
        kv = [ks[r] for r in rows]
        vv = [vs[r] for r in rows]
        beta = [[bsc[d, r] for d in range(2)] for r in rows]
        kb = [[kv[uu] * beta[uu][d] for d in range(2)] for uu in units]
        gk = [lax.dot_general(
                  jnp.concatenate([kb[uu][0], kb[uu][1], qv[uu]], axis=0).astype(BF16),
                  jnp.concatenate([kv[uu], kv[uu]], axis=0).astype(BF16),
                  (((1,), (1,)), ((), ())), preferred_element_type=F32)
              for uu in units]
        gcb = _dots_exact01([cum_lo, cum_up] * PREP_UNROLL,
                            [gsc[d, r] for r in rows for d in range(2)])
        gcb = [gcb[2 * uu:2 * uu + 2] for uu in units]
        ms, avs = [], []
        for uu in units:
            col = jnp.where(left, gcb[uu][0], gcb[uu][1])
            row = jnp.transpose(jnp.concatenate(gcb[uu], axis=0))[0:c]
            decay = jnp.exp(jnp.where(incl, col - row, 0.0))
            ms.append(jnp.where(strict, jnp.where(left, gk[uu][0:c], gk[uu][c:2 * c]) * decay, 0.0))
            avs.append(jnp.where(incl, gk[uu][2 * c:3 * c] * decay, 0.0))
        ys = _unit_tri_inverse_minus_eye(ms, left, ri, cj)
        egc = [[jnp.exp(g) for g in gcb[uu]] for uu in units]
        rs = [[jnp.concatenate([kb[uu][d] * egc[uu][d], vv[uu] * beta[uu][d]], axis=1)
               for d in range(2)] for uu in units]
        corr = [jnp.dot(_block_diag2(ys[uu], left),
                        jnp.concatenate(rs[uu], axis=0).astype(BF16), preferred_element_type=F32)
                for uu in units]
        for uu in units:
            kgs = []
            for d in range(2):
                edge = c - 1 if d == 0 else 0
                g_edge = gcb[uu][d][edge:edge + 1]
                kgs.append(kv[uu] * jnp.exp(g_edge - gcb[uu][d]))
                wu = rs[uu][d] + corr[uu][d * c:(d + 1) * c]
                idx = 2 * ns[uu] + d
                wq_s[idx] = jnp.concatenate([wu[:, 0:HEAD_DIM], qv[uu] * egc[uu][d]],
                                            axis=0).astype(BF16)
                u_s[idx] = wu[:, HEAD_DIM:2 * HEAD_DIM]
                gl_s[idx] = jnp.broadcast_to(jnp.exp(g_edge), (8, LANES))
            kgt = jnp.transpose(jnp.concatenate(kgs, axis=0))
            ak_s[2 * ns[uu]] = jnp.concatenate(
                [jnp.where(left, avs[uu], 0.0), jnp.where(left_sq, kgt, 0.0)], axis=0).astype(BF16)
            ak_s[2 * ns[uu] + 1] = jnp.concatenate(
                [jnp.where(left, 0.0, avs[uu]), jnp.where(left_sq, 0.0, kgt)], axis=0).astype(BF16)
        return carry

    lax.fori_loop(0, nchunks // PREP_UNROLL, prep_group, 0)

    def scan(s, states):
        n_rev = jnp.where(s < nc_ctx, nc_ctx - 1 - s, nchunks + nc_ctx - 1 - s)
        idxs = (2 * s, 2 * n_rev + 1)
        ws = [jnp.dot(wq_s[idx], st.astype(BF16), preferred_element_type=F32)
              for idx, st in zip(idxs, states)]
        v_new = [u_s[idx] - w[0:c] for idx, w in zip(idxs, ws)]
        v_st = jnp.concatenate(v_new, axis=0).astype(BF16)
        ak = [jnp.dot(ak_s[idx], v_st, preferred_element_type=F32) for idx in idxs]
        for n, w, a in zip((s, n_rev), ws, ak):
            rows = pl.ds(pl.multiple_of(n * c, c), c)
            oacc[rows] = oacc[rows] + (w[c:2 * c] + a[0:c])
        return tuple(st * gl_s[idx][0:1] + a[c:c + HEAD_DIM]
                     for idx, st, a in zip(idxs, states, ak))

    zero = jnp.zeros((HEAD_DIM, HEAD_DIM), F32)
    lax.fori_loop(0, nchunks, scan, (zero, zero))

    o = oacc[...]
    on = o * lax.rsqrt(jnp.mean(o * o, axis=-1, keepdims=True) + EPS) * hn_ref[...]
    oc_ref[0] = on[0:tc]
    ox_ref[0] = on[tc:t]


def _gdn(qkv_c, qkv_x, gates_c, gates_x, head_norm):
    tc, tx = CTX_LEN, SEQ
    t = tc + tx
    nidx = 2 * (t // CHUNK)
    part = lambda rows, p: pl.BlockSpec((1, rows, HEAD_DIM), lambda b, h: (p * HEADS + h, b, 0))
    return pl.pallas_call(
        _gdn_kernel,
        grid=(BATCH, HEADS),
        in_specs=[part(tc, 0), part(tc, 1), part(tc, 2),
                  part(tx, 0), part(tx, 1), part(tx, 2),
                  pl.BlockSpec((tc, LANES), lambda b, h: (b, 0)),
                  pl.BlockSpec((tx, LANES), lambda b, h: (b, 0)),
                  pl.BlockSpec((1, HEAD_DIM), lambda b, h: (0, 0))],
        out_specs=[pl.BlockSpec((1, tc, HEAD_DIM), lambda b, h: (b, 0, h)),
                   pl.BlockSpec((1, tx, HEAD_DIM), lambda b, h: (b, 0, h))],
        out_shape=[jax.ShapeDtypeStruct((BATCH, tc, D_MODEL), F32),
                   jax.ShapeDtypeStruct((BATCH, tx, D_MODEL), F32)],
        scratch_shapes=[pltpu.VMEM((t, HEAD_DIM), F32), pltpu.VMEM((t, HEAD_DIM), F32),
                        pltpu.VMEM((t, HEAD_DIM), F32),
                        pltpu.VMEM((2, t, LANES), F32), pltpu.VMEM((2, t, LANES), F32),
                        pltpu.VMEM((nidx, 2 * CHUNK, HEAD_DIM), BF16),
                        pltpu.VMEM((nidx, CHUNK, HEAD_DIM), F32),
                        pltpu.VMEM((nidx, CHUNK + HEAD_DIM, 2 * CHUNK), BF16),
                        pltpu.VMEM((nidx, 8, LANES), F32),
                        pltpu.VMEM((t, HEAD_DIM), F32)],
        compiler_params=_params(("parallel", "arbitrary")),
        name="gdn",
    )(qkv_c, qkv_c, qkv_c, qkv_x, qkv_x, qkv_x, gates_c, gates_x, head_norm)


def _even_out_kernel(x_ref, gt_ref, o_ref, gz_ref, yb_ref, w_ref, out_ref, a_scr):
    j = pl.program_id(1)

    @pl.when(j == 0)
    def _():
        d = o_ref.shape[1]
        a_scr[:, 0:d] = (o_ref[...] * gz_ref[...]).astype(BF16)
        a_scr[:, d:2 * d] = yb_ref[...].astype(BF16)

    y = jnp.dot(a_scr[...], w_ref[...], preferred_element_type=F32)
    out_ref[...] = x_ref[...] + gt_ref[0] * y


def _even_out(xin, mods, o, gz, yb, w_out, *, rows_per_batch, mod_row0):
    r, d = xin.shape
    tm = min(512, rows_per_batch)
    tn = 512
    nj = d // tn
    tiles_per_batch = rows_per_batch // tm
    fixed = mod_row0 >= BATCH
    gt_idx = ((lambda i, j: (mod_row0, 0, 2 * nj + j)) if fixed
              else (lambda i, j: (i // tiles_per_batch, 0, 2 * nj + j)))
    return pl.pallas_call(
        _even_out_kernel,
        grid=(r // tm, nj),
        in_specs=[pl.BlockSpec((tm, tn), lambda i, j: (i, j)),
                  pl.BlockSpec((1, 1, tn), gt_idx),
                  pl.BlockSpec((tm, d), lambda i, j: (i, 0)),
                  pl.BlockSpec((tm, d), lambda i, j: (i, 0)),
                  pl.BlockSpec((tm, d), lambda i, j: (i, 0)),
                  pl.BlockSpec((2 * d, tn), lambda i, j: (0, j))],
        out_specs=pl.BlockSpec((tm, tn), lambda i, j: (i, j)),
        out_shape=jax.ShapeDtypeStruct((r, d), F32),
        scratch_shapes=[pltpu.VMEM((tm, 2 * d), BF16)],
        compiler_params=_params(("parallel", "arbitrary")),
        name="even_out",
    )(xin, mods, o, gz, yb, w_out)


def _odd_in_kernel(x_ref, nw_ref, sh_ref, sc_ref, w_ref, out_ref, h_scr, *, tiles_per_region):
    j = pl.program_id(1)

    @pl.when(j == 0)
    def _():
        h_scr[...] = _modulated_norm(x_ref[...], nw_ref[...], sc_ref[0], sh_ref[0]).astype(BF16)

    p = jnp.dot(h_scr[...], w_ref[...], preferred_element_type=F32)
    region = j // tiles_per_region

    @pl.when(region <= 1)
    def _():
        out_ref[...] = _gelu_tanh(p)

    @pl.when((region == 2) | (region == 4))
    def _():
        out_ref[...] = _silu(p)

    @pl.when(region == 3)
    def _():
        out_ref[...] = p


def _odd_in(xin, mods, nw, w):
    r, d = xin.shape
    n = w.shape[1]
    tm, tn = 1024, 512
    tiles_per_batch = SEQ // tm
    mod_idx = lambda part: (lambda i, j: (i // tiles_per_batch, 0, part))
    return pl.pallas_call(
        functools.partial(_odd_in_kernel, tiles_per_region=d // tn),
        grid=(r // tm, n // tn),
        in_specs=[pl.BlockSpec((tm, d), lambda i, j: (i, 0)),
                  pl.BlockSpec((1, d), lambda i, j: (0, 0)),
                  pl.BlockSpec((1, 1, d), mod_idx(0)),
                  pl.BlockSpec((1, 1, d), mod_idx(1)),
                  pl.BlockSpec((d, tn), lambda i, j: (0, j))],
        out_specs=pl.BlockSpec((tm, tn), lambda i, j: (i, j)),
        out_shape=jax.ShapeDtypeStruct((r, n), F32),
        scratch_shapes=[pltpu.VMEM((tm, d), BF16)],
        compiler_params=_params(("parallel", "arbitrary")),
        name="odd_in",
    )(xin, nw, mods, mods, w)


def _odd_mix_kernel(x_ref, gt_ref, u_ref, v_ref, zc_ref, pd_ref, zd_ref, lnw_ref, lnb_ref, ws_ref,
                    bs_ref, band_ref, icnt_ref, pw_ref, ps_ref, wo_ref, fnw_ref, out_ref,
                    acc, ln_scr, *, nk_c):
    kk = pl.program_id(1)
    tm = x_ref.shape[0]
    tk = u_ref.shape[1]
    gpb = tk // HEAD_DIM

    @pl.when(kk == 0)
    def _():
        v = v_ref[...]
        mu = jnp.mean(v, axis=-1, keepdims=True)
        xc = v - mu
        var = jnp.mean(xc * xc, axis=-1, keepdims=True)
        ln = (xc * lax.rsqrt(var + EPS) * lnw_ref[...] + lnb_ref[...]).astype(BF16)
        for blk in range(nk_c):
            ln_scr[blk] = ln[:, blk * tk:(blk + 1) * tk]
        acc[...] = jnp.zeros_like(acc)

    @pl.when(kk < nk_c)
    def _():
        ln = ln_scr[kk]
        cols = []
        for gi in range(gpb):
            w_g = ws_ref[gi]
            b_g = bs_ref[gi]
            rows = []
            for cc in range(tm // TOK_CHUNK):
                blk = ln[cc * TOK_CHUNK:(cc + 1) * TOK_CHUNK, gi * HEAD_DIM:(gi + 1) * HEAD_DIM]
                rows.append(jnp.dot(w_g, blk, preferred_element_type=F32) + b_g)
            cols.append(jnp.concatenate(rows, axis=0))
        s = jnp.concatenate(cols, axis=1)
        yc = u_ref[...] * s * zc_ref[...]
        acc[...] += jnp.dot(yc.astype(BF16), wo_ref[...], preferred_element_type=F32)

    @pl.when(kk >= nk_c)
    def _():
        pd = pd_ref[...]
        hi = pd.astype(BF16)
        mid = (pd - hi.astype(F32)).astype(BF16)
        band = band_ref[0]
        wsum = (jnp.dot(band, hi, preferred_element_type=F32)
                + jnp.dot(band, mid, preferred_element_type=F32))
        icnt = icnt_ref[0]
        icnt = jnp.concatenate([icnt] * (tk // LANES), axis=1)
        diff = wsum * icnt - pd
        y = jnp.dot(diff.astype(BF16), pw_ref[0], preferred_element_type=F32)
        yd = y * ps_ref[...] * zd_ref[...]
        acc[...] += jnp.dot(yd.astype(BF16), wo_ref[...], preferred_element_type=F32)

    @pl.when(kk == pl.num_programs(1) - 1)
    def _():
        xn = x_ref[...] + gt_ref[0] * acc[...]
        ms = jnp.mean(xn * xn, axis=-1, keepdims=True)
        out_ref[...] = xn * lax.rsqrt(ms + EPS) * fnw_ref[...]


def _pool_constants(tm):
    bands = np.zeros((POOL_GROUPS, tm, tm), np.float32)
    icnt = np.zeros((POOL_GROUPS, tm, LANES), np.float32)
    pos = np.arange(tm)
    row = pos // GRID_W
    col = pos % GRID_W
    for g, rad in enumerate(POOL_RADII):
        same = row[:, None] == row[None, :]
        bands[g] = (same & (np.abs(col[:, None] - col[None, :]) <= rad)).astype(np.float32)
        cnt = np.minimum(col + rad + 1, GRID_W) - np.maximum(col - rad, 0)
        icnt[g] = (1.0 / cnt.astype(np.float32))[:, None]
    return jnp.asarray(bands, BF16), jnp.asarray(icnt, F32)


def _odd_mix(xin, mods, p, ln_w, ln_b, w_s, b_s, pool_w, pool_scale, w_out, fnw):
    r, d = xin.shape
    tm, tk = 512, 512
    nk_c = d // tk
    nk = 2 * nk_c
    gpb = tk // HEAD_DIM
    tiles_per_batch = SEQ // tm
    bands, icnt = _pool_constants(tm)
    lo = lambda kk: jnp.minimum(kk, nk_c - 1)
    hi = lambda kk: jnp.maximum(kk - nk_c, 0)
    cblk = d // tk
    return pl.pallas_call(
        functools.partial(_odd_mix_kernel, nk_c=nk_c),
        grid=(r // tm, nk),
        in_specs=[pl.BlockSpec((tm, d), lambda i, kk: (i, 0)),
                  pl.BlockSpec((1, 1, d), lambda i, kk: (i // tiles_per_batch, 0, 2)),
                  pl.BlockSpec((tm, tk), lambda i, kk: (i, lo(kk))),
                  pl.BlockSpec((tm, d), lambda i, kk: (i, 1)),
                  pl.BlockSpec((tm, tk), lambda i, kk: (i, 2 * cblk + lo(kk))),
                  pl.BlockSpec((tm, tk), lambda i, kk: (i, 3 * cblk + hi(kk))),
                  pl.BlockSpec((tm, tk), lambda i, kk: (i, 4 * cblk + hi(kk))),
                  pl.BlockSpec((1, d), lambda i, kk: (0, 0)),
                  pl.BlockSpec((1, d), lambda i, kk: (0, 0)),
                  pl.BlockSpec((gpb, TOK_CHUNK, TOK_CHUNK), lambda i, kk: (lo(kk), 0, 0)),
                  pl.BlockSpec((gpb, TOK_CHUNK, LANES), lambda i, kk: (lo(kk), 0, 0)),
                  pl.BlockSpec((1, tm, tm), lambda i, kk: (hi(kk), 0, 0)),
                  pl.BlockSpec((1, tm, LANES), lambda i, kk: (hi(kk), 0, 0)),
                  pl.BlockSpec((1, POOL_GW, POOL_GW), lambda i, kk: (hi(kk), 0, 0)),
                  pl.BlockSpec((1, tk), lambda i, kk: (0, hi(kk))),
                  pl.BlockSpec((tk, d), lambda i, kk: (kk, 0)),
                  pl.BlockSpec((1, d), lambda i, kk: (0, 0))],
        out_specs=pl.BlockSpec((tm, d), lambda i, kk: (i, 0)),
        out_shape=jax.ShapeDtypeStruct((r, d), F32),
        scratch_shapes=[pltpu.VMEM((tm, d), F32), pltpu.VMEM((nk_c, tm, tk), BF16)],
        compiler_params=_params(("parallel", "arbitrary")),
        name="odd_mix",
    )(xin, mods, p, p, p, p, p, ln_w, ln_b, w_s, b_s, bands, icnt, pool_w, pool_scale, w_out, fnw)


def kernel(x, c, ctx, c_ctx, ada_w, ada_b, norm_w, e_w_in, e_conv_qkv, e_a_log, e_dt_bias, e_head_norm,
           e_conv_b, e_w_out, o_w_in, o_ln_w, o_ln_b, o_w_s, o_b_s, o_pool_w, o_pool_scale, o_w_out,
           final_norm_w):
    bsz, seq, d = x.shape
    rx = bsz * seq
    rc = bsz * ctx.shape[1]
    x2 = x.reshape(rx, d)
    ctx2 = ctx.reshape(rc, d)

    cc = jnp.concatenate([c, c_ctx[None, :], jnp.zeros((8 - bsz - 1, d), F32)], axis=0)
    mods = _adaln(cc, ada_w, ada_b)
    ctx_row = bsz

    w_in = e_w_in[0]
    gate_lo = QKV_W + D_MODEL
    gate_hi = gate_lo + 4 * HEADS
    wqkv = w_in[:, :QKV_W].astype(BF16)
    wg = jnp.pad(w_in[:, gate_lo:gate_hi], ((0, 0), (0, LANES - 4 * HEADS))).astype(BF16)
    w5 = jnp.concatenate([w_in[:, QKV_W:gate_lo], w_in[:, gate_hi:]], axis=1).astype(BF16)
    m0 = mods[0].reshape(8, 1, 3 * d)
    nw0 = norm_w[0].reshape(1, d)

    gate_par = jnp.pad(jnp.stack([e_a_log[0].reshape(-1), e_dt_bias[0].reshape(-1)]),
                       ((0, 0), (0, LANES - 2 * HEADS)))
    qkv_x, gates_x = _qkv_proj(x2, m0, nw0, wqkv, wg, e_conv_qkv[0], gate_par,
                               rows_per_batch=seq, mod_row0=0, group=GRID_W)
    qkv_c, gates_c = _qkv_proj(ctx2, m0, nw0, wqkv, wg, e_conv_qkv[0], gate_par,
                               rows_per_batch=ctx.shape[1], mod_row0=ctx_row, group=ctx.shape[1])
    gz_x, yb_x = _mixb_proj(x2, m0, nw0, w5, e_conv_b[0],
                            rows_per_batch=seq, mod_row0=0, group=GRID_W)
    gz_c, yb_c = _mixb_proj(ctx2, m0, nw0, w5, e_conv_b[0],
                            rows_per_batch=ctx.shape[1], mod_row0=ctx_row, group=ctx.shape[1])
    o_c, o_x = _gdn(qkv_c, qkv_x, gates_c, gates_x,
                    e_head_norm[0].reshape(1, HEAD_DIM))
    w_out0 = e_w_out[0].astype(BF16)
    x2 = _even_out(x2, m0, o_x.reshape(rx, d), gz_x, yb_x, w_out0, rows_per_batch=seq, mod_row0=0)
    ctx2 = _even_out(ctx2, m0, o_c.reshape(rc, d), gz_c, yb_c, w_out0,
                     rows_per_batch=ctx.shape[1], mod_row0=ctx_row)

    m1 = mods[1].reshape(8, 1, 3 * d)
    p = _odd_in(x2, m1, norm_w[1].reshape(1, d), o_w_in[0].astype(BF16))
    b_s = jnp.broadcast_to(o_b_s[0][:, :, None], (o_b_s.shape[1], TOK_CHUNK, LANES))
    out = _odd_mix(x2, m1, p, o_ln_w[0].reshape(1, d), o_ln_b[0].reshape(1, d),
                   o_w_s[0].astype(BF16), b_s, o_pool_w[0].astype(BF16),
                   o_pool_scale[0].reshape(1, d), o_w_out[0].astype(BF16),
                   final_norm_w.reshape(1, d))
    return out.reshape(bsz, seq, d)
```

```python
import functools
import math

import numpy as np
import jax
import jax.numpy as jnp
from jax import lax
from jax.experimental import pallas as pl
from jax.experimental.pallas import tpu as pltpu

F32 = jnp.float32
BF16 = jnp.bfloat16

D_MODEL = 2048
BATCH = 4
SEQ = 2048
CTX_LEN = 256
GRID_W = 64
EPS = 1e-6
HEADS = 16
HEAD_DIM = 128
QKV_W = 3 * D_MODEL
CHUNK = 64
PREP_UNROLL = 36
TOK_CHUNK = 128
POOL_GROUPS = 4
POOL_GW = D_MODEL // POOL_GROUPS
POOL_RADII = (1, 2, 4, 8)

LANES = 128
VMEM_LIMIT = 56 * 1024 * 1024


def _params(sem):
    return pltpu.CompilerParams(dimension_semantics=sem, vmem_limit_bytes=VMEM_LIMIT)


def _sigmoid(x):
    return 1.0 / (1.0 + jnp.exp(-x))


def _silu(x):
    return x * _sigmoid(x)


def _gelu_tanh(x):
    cdf = 0.5 * (1.0 + jnp.tanh(math.sqrt(2.0 / math.pi) * (x + 0.044715 * (x * x * x))))
    return x * cdf


def _modulated_norm(x, nw, sc, sh):
    ms = jnp.mean(x * x, axis=-1, keepdims=True)
    y = x * lax.rsqrt(ms + EPS) * nw
    return y * (1.0 + sc) + sh


def _row_conv3(p, cw, group):
    tm = p.shape[0]
    row = lax.broadcasted_iota(jnp.int32, (tm, 1), 0) % group
    prev = jnp.where(row == 0, 0.0, pltpu.roll(p, 1, 0))
    nxt = jnp.where(row == group - 1, 0.0, pltpu.roll(p, tm - 1, 0))
    return prev * cw[0:1] + p * cw[1:2] + nxt * cw[2:3]


def _adaln_kernel(c_ref, w_ref, b_ref, o_ref):
    a = _silu(c_ref[...]).astype(BF16)
    o_ref[0] = jnp.dot(a, w_ref[0].astype(BF16), preferred_element_type=F32) + b_ref[0]


def _adaln(cc, ada_w, ada_b):
    depth, d, n = ada_w.shape
    tn = 1024
    return pl.pallas_call(
        _adaln_kernel,
        grid=(depth, n // tn),
        in_specs=[pl.BlockSpec((8, d), lambda l, j: (0, 0)),
                  pl.BlockSpec((1, d, tn), lambda l, j: (l, 0, j)),
                  pl.BlockSpec((1, 1, tn), lambda l, j: (l, 0, j))],
        out_specs=pl.BlockSpec((1, 8, tn), lambda l, j: (l, 0, j)),
        out_shape=jax.ShapeDtypeStruct((depth, 8, n), F32),
        compiler_params=_params(("parallel", "parallel")),
        name="adaln",
    )(cc, ada_w, ada_b.reshape(depth, 1, n))


def _qkv_kernel(x_ref, nw_ref, sh_ref, sc_ref, w_ref, wg_ref, cw_ref, gp_ref, qkv_ref, g_ref, h_scr,
                *, group, n_tiles):
    j = pl.program_id(1)

    @pl.when(j == 0)
    def _():
        h_scr[...] = _modulated_norm(x_ref[...], nw_ref[...], sc_ref[0], sh_ref[0]).astype(BF16)

    @pl.when(j < n_tiles)
    def _():
        p = jnp.dot(h_scr[...], w_ref[...], preferred_element_type=F32)
        y = _silu(_row_conv3(p, cw_ref[...], group))
        for hh in range(y.shape[1] // HEAD_DIM):
            qkv_ref[hh] = y[:, hh * HEAD_DIM:(hh + 1) * HEAD_DIM]

    @pl.when(j == n_tiles)
    def _():
        raw = jnp.dot(h_scr[...], wg_ref[...], preferred_element_type=F32)
        z = raw + gp_ref[1:2]
        softplus = jnp.maximum(z, 0.0) + jnp.log1p(jnp.exp(-jnp.abs(z)))
        lane = lax.broadcasted_iota(jnp.int32, (1, LANES), 1)
        g_ref[...] = jnp.where(lane < 2 * HEADS, -jnp.exp(gp_ref[0:1]) * softplus, _sigmoid(raw))


def _qkv_proj(xin, mods, nw, wqkv, wg, cw, gate_par, *, rows_per_batch, mod_row0, group):
    r, d = xin.shape
    tm, tn = 1024, 768
    n_tiles = QKV_W // tn
    hpt = tn // HEAD_DIM
    tiles_per_batch = max(rows_per_batch // tm, 1)

    def mod_idx(part):
        if rows_per_batch >= tm:
            return lambda i, j: (mod_row0 + i // tiles_per_batch, 0, part)
        return lambda i, j: (mod_row0, 0, part)

    last = n_tiles - 1
    return pl.pallas_call(
        functools.partial(_qkv_kernel, group=group, n_tiles=n_tiles),
        grid=(r // tm, n_tiles + 1),
        in_specs=[pl.BlockSpec((tm, d), lambda i, j: (i, 0)),
                  pl.BlockSpec((1, d), lambda i, j: (0, 0)),
                  pl.BlockSpec((1, 1, d), mod_idx(0)),
                  pl.BlockSpec((1, 1, d), mod_idx(1)),
                  pl.BlockSpec((d, tn), lambda i, j: (0, jnp.minimum(j, last))),
                  pl.BlockSpec((d, LANES), lambda i, j: (0, 0)),
                  pl.BlockSpec((3, tn), lambda i, j: (0, jnp.minimum(j, last))),
                  pl.BlockSpec((2, LANES), lambda i, j: (0, 0))],
        out_specs=[pl.BlockSpec((hpt, tm, HEAD_DIM), lambda i, j: (jnp.minimum(j, last), i, 0)),
                   pl.BlockSpec((tm, LANES), lambda i, j: (i, 0))],
        out_shape=[jax.ShapeDtypeStruct((3 * HEADS, r, HEAD_DIM), F32),
                   jax.ShapeDtypeStruct((r, LANES), F32)],
        scratch_shapes=[pltpu.VMEM((tm, d), BF16)],
        compiler_params=_params(("parallel", "arbitrary")),
        name="qkv_proj",
    )(xin, nw, mods, mods, wqkv, wg, cw, gate_par)


def _mixb_kernel(x_ref, nw_ref, sh_ref, sc_ref, wz_ref, wb_ref, wc_ref, wh_ref, wzb_ref, cw_ref,
                 gz_ref, yb_ref, h_scr, *, group):
    j = pl.program_id(1)

    @pl.when(j == 0)
    def _():
        h_scr[...] = _modulated_norm(x_ref[...], nw_ref[...], sc_ref[0], sh_ref[0]).astype(BF16)

    h = h_scr[...]
    dot = lambda w: jnp.dot(h, w[...], preferred_element_type=F32)
    gz_ref[...] = _silu(dot(wz_ref))
    conv = _row_conv3(dot(wc_ref) * dot(wh_ref), cw_ref[...], group)
    yb_ref[...] = dot(wb_ref) * conv * _silu(dot(wzb_ref))


def _mixb_proj(xin, mods, nw, w5, cw, *, rows_per_batch, mod_row0, group):
    r, d = xin.shape
    tm, tn = 1024, 256
    nj = d // tn
    tiles_per_batch = max(rows_per_batch // tm, 1)

    def mod_idx(part):
        if rows_per_batch >= tm:
            return lambda i, j: (mod_row0 + i // tiles_per_batch, 0, part)
        return lambda i, j: (mod_row0, 0, part)

    wspec = lambda reg: pl.BlockSpec((d, tn), lambda i, j: (0, reg * nj + j))
    return pl.pallas_call(
        functools.partial(_mixb_kernel, group=group),
        grid=(r // tm, nj),
        in_specs=[pl.BlockSpec((tm, d), lambda i, j: (i, 0)),
                  pl.BlockSpec((1, d), lambda i, j: (0, 0)),
                  pl.BlockSpec((1, 1, d), mod_idx(0)),
                  pl.BlockSpec((1, 1, d), mod_idx(1)),
                  wspec(0), wspec(1), wspec(2), wspec(3), wspec(4),
                  pl.BlockSpec((3, tn), lambda i, j: (0, j))],
        out_specs=[pl.BlockSpec((tm, tn), lambda i, j: (i, j)),
                   pl.BlockSpec((tm, tn), lambda i, j: (i, j))],
        out_shape=[jax.ShapeDtypeStruct((r, d), F32), jax.ShapeDtypeStruct((r, d), F32)],
        scratch_shapes=[pltpu.VMEM((tm, d), BF16)],
        compiler_params=_params(("parallel", "arbitrary")),
        name="mixb_proj",
    )(xin, nw, mods, mods, w5, w5, w5, w5, w5, cw)


def _split3_bf16(x):
    x1 = x.astype(BF16)
    r1 = x - x1.astype(F32)
    x2 = r1.astype(BF16)
    x3 = (r1 - x2.astype(F32)).astype(BF16)
    return x1, x2, x3


def _dots_exact01(m01s, xs):
    pieces = [_split3_bf16(x) for x in xs]
    prods = [[jnp.dot(m01, t, preferred_element_type=F32) for t in p] for m01, p in zip(m01s, pieces)]
    return [(p[0] + p[1]) + p[2] for p in prods]


def _block_diag2(x, left):
    return jnp.concatenate([jnp.where(left, x, 0.0), jnp.where(left, 0.0, x)], axis=0).astype(BF16)


def _unit_tri_inverse_minus_eye(ms, left, ri, cj):
    n = ms[0].shape[0]
    ys = None
    s = 1
    while s < n:
        shift = s.bit_length() - 1
        br = lax.shift_right_logical(ri, shift)
        bc = lax.shift_right_logical(cj, shift)
        mask_lo = ((br & 1) == 1) & (bc == br - 1)
        mask_up = ((bc & 1) == 1) & (br == bc - 1)
        mask = (left & mask_lo) | (jnp.logical_not(left) & mask_up)
        cms = [jnp.where(mask, m, 0.0) for m in ms]
        if ys is None:
            ys = [-cm for cm in cms]
        else:
            ps = [cm + jnp.dot(y.astype(BF16), _block_diag2(cm, left), preferred_element_type=F32)
                  for cm, y in zip(cms, ys)]
            ys = [y - p - jnp.dot(p.astype(BF16), _block_diag2(y, left), preferred_element_type=F32)
                  for y, p in zip(ys, ps)]
        s *= 2
    return ys


def _gdn_kernel(qc_ref, kc_ref, vc_ref, qx_ref, kx_ref, vx_ref, gc_ref, gx_ref,
                hn_ref, oc_ref, ox_ref,
                qs, ks, vs, gsc, bsc, nq_s, b_s, op_s, gl_s, oacc):
    h = pl.program_id(1)
    tc = qc_ref.shape[1]
    tx = qx_ref.shape[1]
    t = tc + tx
    nchunks = t // CHUNK
    nc_ctx = tc // CHUNK
    c = CHUNK

    def l2n(a):
        return a * lax.rsqrt(jnp.sum(a * a, axis=-1, keepdims=True) + EPS)

    qscale = HEAD_DIM ** -0.5
    qs[0:tc] = l2n(qc_ref[0]) * qscale
    qs[tc:t] = l2n(qx_ref[0]) * qscale
    ks[0:tc] = l2n(kc_ref[0])
    ks[tc:t] = l2n(kx_ref[0])
    vs[0:tc] = vc_ref[0]
    vs[tc:t] = vx_ref[0]
    oacc[...] = jnp.zeros_like(oacc)

    lane = lax.broadcasted_iota(jnp.int32, (1, LANES), 1)
    for d in range(2):
        for ref, lo, hi in ((gc_ref, 0, tc), (gx_ref, tc, t)):
            gates = ref[...]
            g = jnp.sum(jnp.where(lane == d * HEADS + h, gates, 0.0), axis=-1, keepdims=True)
            b = jnp.sum(jnp.where(lane == 2 * HEADS + d * HEADS + h, gates, 0.0), axis=-1,
                        keepdims=True)
            gsc[d, lo:hi] = jnp.broadcast_to(g, (hi - lo, LANES))
            bsc[d, lo:hi] = jnp.broadcast_to(b, (hi - lo, LANES))

    ri1 = lax.broadcasted_iota(jnp.int32, (c, c), 0)
    ci1 = lax.broadcasted_iota(jnp.int32, (c, c), 1)
    cum_lo = jnp.where(ci1 <= ri1, 1.0, 0.0).astype(BF16)
    cum_up = jnp.where(ci1 >= ri1, 1.0, 0.0).astype(BF16)
    ri = lax.broadcasted_iota(jnp.int32, (c, 2 * c), 0)
    lane2 = lax.broadcasted_iota(jnp.int32, (c, 2 * c), 1)
    left = lane2 < c
    cj = lane2 & (c - 1)
    strict = (left & (cj < ri)) | (jnp.logical_not(left) & (cj > ri))
    incl = (left & (cj <= ri)) | (jnp.logical_not(left) & (cj >= ri))
    left_sq = lax.broadcasted_iota(jnp.int32, (2 * c, 2 * c), 1) < c

    def prep_group(i, carry):
        units = range(PREP_UNROLL)
        ns = [i * PREP_UNROLL + uu for uu in units]
        rows = [pl.ds(pl.multiple_of(n * c, c), c) for n in ns]
        qv = [qs[r] for r in rows]
        kv = [ks[r] for r in rows]
        vv = [vs[r] for r in rows]
        beta = [[bsc[d, r] for d in range(2)] for r in rows]
        kb = [[kv[uu] * beta[uu][d] for d in range(2)] for uu in units]
        gk = [lax.dot_general(
                  jnp.concatenate([kb[uu][0], kb[uu][1], qv[uu]], axis=0).astype(BF16),
                  jnp.concatenate([kv[uu], kv[uu]], axis=0).astype(BF16),
                  (((1,), (1,)), ((), ())), preferred_element_type=F32)
              for uu in units]
        gcb = _dots_exact01([cum_lo, cum_up] * PREP_UNROLL,
                            [gsc[d, r] for r in rows for d in range(2)])
        gcb = [gcb[2 * uu:2 * uu + 2] for uu in units]
        ms, avs = [], []
        for uu in units:
            col = jnp.where(left, gcb[uu][0], gcb[uu][1])
            row = jnp.transpose(jnp.concatenate(gcb[uu], axis=0))[0:c]
            decay = jnp.exp(jnp.where(incl, col - row, 0.0))
            ms.append(jnp.where(strict, jnp.where(left, gk[uu][0:c], gk[uu][c:2 * c]) * decay, 0.0))
            avs.append(jnp.where(incl, gk[uu][2 * c:3 * c] * decay, 0.0))
        ys = _unit_tri_inverse_minus_eye(ms, left, ri, cj)
        egc = [[jnp.exp(g) for g in gcb[uu]] for uu in units]
        rs = [[jnp.concatenate([kb[uu][d] * egc[uu][d], vv[uu] * beta[uu][d]], axis=1)
               for d in range(2)] for uu in units]
        corr = [jnp.dot(_block_diag2(ys[uu], left),
                        jnp.concatenate(rs[uu], axis=0).astype(BF16), preferred_element_type=F32)
                for uu in units]
        wus, aks, g_edges = [], [], []
        for uu in units:
            kgs, edges = [], []
            for d in range(2):
                edge = c - 1 if d == 0 else 0
                edges.append(gcb[uu][d][edge:edge + 1])
                kgs.append(kv[uu] * jnp.exp(edges[d] - gcb[uu][d]))
            g_edges.append(edges)
            wus.append([rs[uu][d] + corr[uu][d * c:(d + 1) * c] for d in range(2)])
            kgt = jnp.transpose(jnp.concatenate(kgs, axis=0))
            aks.append(jnp.concatenate(
                [jnp.where(left, avs[uu], 0.0), jnp.where(left_sq, kgt, 0.0),
                 jnp.where(left, 0.0, avs[uu]), jnp.where(left_sq, 0.0, kgt)], axis=0).astype(BF16))
        prods = [jnp.dot(aks[uu], jnp.concatenate(wus[uu], axis=0).astype(BF16),
                         preferred_element_type=F32) for uu in units]
        for uu in units:
            for d in range(2):
                x = prods[uu][d * (c + HEAD_DIM):(d + 1) * (c + HEAD_DIM)]
                a_w, a_u = x[0:c, 0:HEAD_DIM], x[0:c, HEAD_DIM:2 * HEAD_DIM]
                k_w, k_u = x[c:c + HEAD_DIM, 0:HEAD_DIM], x[c:c + HEAD_DIM, HEAD_DIM:2 * HEAD_DIM]
                idx = 2 * ns[uu] + d
                nq_s[idx] = jnp.concatenate([-k_w, qv[uu] * egc[uu][d] - a_w], axis=0).astype(BF16)
                b_s[idx] = k_u
                op_s[idx] = a_u
                gl_s[idx] = jnp.broadcast_to(jnp.exp(g_edges[uu][d]), (8, LANES))
        return carry

    lax.fori_loop(0, nchunks // PREP_UNROLL, prep_group, 0)

    def scan(s, states):
        n_rev = jnp.where(s < nc_ctx, nc_ctx - 1 - s, nchunks + nc_ctx - 1 - s)
        idxs = (2 * s, 2 * n_rev + 1)
        xs = [jnp.dot(nq_s[idx], st.astype(BF16), preferred_element_type=F32)
              for idx, st in zip(idxs, states)]
        for n, idx, x in zip((s, n_rev), idxs, xs):
            rows = pl.ds(pl.multiple_of(n * c, c), c)
            oacc[rows] = oacc[rows] + (x[HEAD_DIM:HEAD_DIM + c] + op_s[idx])
        return tuple(st * gl_s[idx][0:1] + x[0:HEAD_DIM] + b_s[idx]
                     for idx, st, x in zip(idxs, states, xs))

    zero = jnp.zeros((HEAD_DIM, HEAD_DIM), F32)
    lax.fori_loop(0, nchunks, scan, (zero, zero))

    o = oacc[...]
    on = o * lax.rsqrt(jnp.mean(o * o, axis=-1, keepdims=True) + EPS) * hn_ref[...]
    oc_ref[0] = on[0:tc]
    ox_ref[0] = on[tc:t]


def _gdn(qkv_c, qkv_x, gates_c, gates_x, head_norm):
    tc, tx = CTX_LEN, SEQ
    t = tc + tx
    nidx = 2 * (t // CHUNK)
    part = lambda rows, p: pl.BlockSpec((1, rows, HEAD_DIM), lambda b, h: (p * HEADS + h, b, 0))
    return pl.pallas_call(
        _gdn_kernel,
        grid=(BATCH, HEADS),
        in_specs=[part(tc, 0), part(tc, 1), part(tc, 2),
                  part(tx, 0), part(tx, 1), part(tx, 2),
                  pl.BlockSpec((tc, LANES), lambda b, h: (b, 0)),
                  pl.BlockSpec((tx, LANES), lambda b, h: (b, 0)),
                  pl.BlockSpec((1, HEAD_DIM), lambda b, h: (0, 0))],
        out_specs=[pl.BlockSpec((1, tc, HEAD_DIM), lambda b, h: (b, 0, h)),
                   pl.BlockSpec((1, tx, HEAD_DIM), lambda b, h: (b, 0, h))],
        out_shape=[jax.ShapeDtypeStruct((BATCH, tc, D_MODEL), F32),
                   jax.ShapeDtypeStruct((BATCH, tx, D_MODEL), F32)],
        scratch_shapes=[pltpu.VMEM((t, HEAD_DIM), F32), pltpu.VMEM((t, HEAD_DIM), F32),
                        pltpu.VMEM((t, HEAD_DIM), F32),
                        pltpu.VMEM((2, t, LANES), F32), pltpu.VMEM((2, t, LANES), F32),
                        pltpu.VMEM((nidx, HEAD_DIM + CHUNK, HEAD_DIM), BF16),
                        pltpu.VMEM((nidx, HEAD_DIM, HEAD_DIM), F32),
                        pltpu.VMEM((nidx, CHUNK, HEAD_DIM), F32),
                        pltpu.VMEM((nidx, 8, LANES), F32),
                        pltpu.VMEM((t, HEAD_DIM), F32)],
        compiler_params=_params(("parallel", "arbitrary")),
        name="gdn",
    )(qkv_c, qkv_c, qkv_c, qkv_x, qkv_x, qkv_x, gates_c, gates_x, head_norm)


def _even_out_kernel(x_ref, gt_ref, o_ref, gz_ref, yb_ref, w_ref, out_ref, a_scr):
    j = pl.program_id(1)

    @pl.when(j == 0)
    def _():
        d = o_ref.shape[1]
        a_scr[:, 0:d] = (o_ref[...] * gz_ref[...]).astype(BF16)
        a_scr[:, d:2 * d] = yb_ref[...].astype(BF16)

    y = jnp.dot(a_scr[...], w_ref[...], preferred_element_type=F32)
    out_ref[...] = x_ref[...] + gt_ref[0] * y


def _even_out(xin, mods, o, gz, yb, w_out, *, rows_per_batch, mod_row0):
    r, d = xin.shape
    tm = min(512, rows_per_batch)
    tn = 512
    nj = d // tn
    tiles_per_batch = rows_per_batch // tm
    fixed = mod_row0 >= BATCH
    gt_idx = ((lambda i, j: (mod_row0, 0, 2 * nj + j)) if fixed
              else (lambda i, j: (i // tiles_per_batch, 0, 2 * nj + j)))
    return pl.pallas_call(
        _even_out_kernel,
        grid=(r // tm, nj),
        in_specs=[pl.BlockSpec((tm, tn), lambda i, j: (i, j)),
                  pl.BlockSpec((1, 1, tn), gt_idx),
                  pl.BlockSpec((tm, d), lambda i, j: (i, 0)),
                  pl.BlockSpec((tm, d), lambda i, j: (i, 0)),
                  pl.BlockSpec((tm, d), lambda i, j: (i, 0)),
                  pl.BlockSpec((2 * d, tn), lambda i, j: (0, j))],
        out_specs=pl.BlockSpec((tm, tn), lambda i, j: (i, j)),
        out_shape=jax.ShapeDtypeStruct((r, d), F32),
        scratch_shapes=[pltpu.VMEM((tm, 2 * d), BF16)],
        compiler_params=_params(("parallel", "arbitrary")),
        name="even_out",
    )(xin, mods, o, gz, yb, w_out)


def _odd_in_kernel(x_ref, nw_ref, sh_ref, sc_ref, w_ref, out_ref, h_scr, *, tiles_per_region):
    j = pl.program_id(1)

    @pl.when(j == 0)
    def _():
        h_scr[...] = _modulated_norm(x_ref[...], nw_ref[...], sc_ref[0], sh_ref[0]).astype(BF16)

    p = jnp.dot(h_scr[...], w_ref[...], preferred_element_type=F32)
    region = j // tiles_per_region

    @pl.when(region <= 1)
    def _():
        out_ref[...] = _gelu_tanh(p)

    @pl.when((region == 2) | (region == 4))
    def _():
        out_ref[...] = _silu(p)

    @pl.when(region == 3)
    def _():
        out_ref[...] = p


def _odd_in(xin, mods, nw, w):
    r, d = xin.shape
    n = w.shape[1]
    tm, tn = 1024, 512
    tiles_per_batch = SEQ // tm
    mod_idx = lambda part: (lambda i, j: (i // tiles_per_batch, 0, part))
    return pl.pallas_call(
        functools.partial(_odd_in_kernel, tiles_per_region=d // tn),
        grid=(r // tm, n // tn),
        in_specs=[pl.BlockSpec((tm, d), lambda i, j: (i, 0)),
                  pl.BlockSpec((1, d), lambda i, j: (0, 0)),
                  pl.BlockSpec((1, 1, d), mod_idx(0)),
                  pl.BlockSpec((1, 1, d), mod_idx(1)),
                  pl.BlockSpec((d, tn), lambda i, j: (0, j))],
        out_specs=pl.BlockSpec((tm, tn), lambda i, j: (i, j)),
        out_shape=jax.ShapeDtypeStruct((r, n), F32),
        scratch_shapes=[pltpu.VMEM((tm, d), BF16)],
        compiler_params=_params(("parallel", "arbitrary")),
        name="odd_in",
    )(xin, nw, mods, mods, w)


def _odd_mix_kernel(x_ref, gt_ref, u_ref, v_ref, zc_ref, pd_ref, zd_ref, lnw_ref, lnb_ref, ws_ref,
                    bs_ref, band_ref, icnt_ref, pw_ref, ps_ref, wo_ref, fnw_ref, out_ref,
                    acc, ln_scr, *, nk_c):
    kk = pl.program_id(1)
    tm = x_ref.shape[0]
    tk = u_ref.shape[1]
    gpb = tk // HEAD_DIM

    @pl.when(kk == 0)
    def _():
        v = v_ref[...]
        mu = jnp.mean(v, axis=-1, keepdims=True)
        xc = v - mu
        var = jnp.mean(xc * xc, axis=-1, keepdims=True)
        ln = (xc * lax.rsqrt(var + EPS) * lnw_ref[...] + lnb_ref[...]).astype(BF16)
        for blk in range(nk_c):
            ln_scr[blk] = ln[:, blk * tk:(blk + 1) * tk]
        acc[...] = jnp.zeros_like(acc)

    @pl.when(kk < nk_c)
    def _():
        ln = ln_scr[kk]
        cols = []
        for gi in range(gpb):
            w_g = ws_ref[gi]
            b_g = bs_ref[gi]
            rows = []
            for cc in range(tm // TOK_CHUNK):
                blk = ln[cc * TOK_CHUNK:(cc + 1) * TOK_CHUNK, gi * HEAD_DIM:(gi + 1) * HEAD_DIM]
                rows.append(jnp.dot(w_g, blk, preferred_element_type=F32) + b_g)
            cols.append(jnp.concatenate(rows, axis=0))
        s = jnp.concatenate(cols, axis=1)
        yc = u_ref[...] * s * zc_ref[...]
        acc[...] += jnp.dot(yc.astype(BF16), wo_ref[...], preferred_element_type=F32)

    @pl.when(kk >= nk_c)
    def _():
        pd = pd_ref[...]
        hi = pd.astype(BF16)
        mid = (pd - hi.astype(F32)).astype(BF16)
        band = band_ref[0]
        wsum = (jnp.dot(band, hi, preferred_element_type=F32)
                + jnp.dot(band, mid, preferred_element_type=F32))
        icnt = icnt_ref[0]
        icnt = jnp.concatenate([icnt] * (tk // LANES), axis=1)
        diff = wsum * icnt - pd
        y = jnp.dot(diff.astype(BF16), pw_ref[0], preferred_element_type=F32)
        yd = y * ps_ref[...] * zd_ref[...]
        acc[...] += jnp.dot(yd.astype(BF16), wo_ref[...], preferred_element_type=F32)

    @pl.when(kk == pl.num_programs(1) - 1)
    def _():
        xn = x_ref[...] + gt_ref[0] * acc[...]
        ms = jnp.mean(xn * xn, axis=-1, keepdims=True)
        out_ref[...] = xn * lax.rsqrt(ms + EPS) * fnw_ref[...]


def _pool_constants(tm):
    bands = np.zeros((POOL_GROUPS, tm, tm), np.float32)
    icnt = np.zeros((POOL_GROUPS, tm, LANES), np.float32)
    pos = np.arange(tm)
    row = pos // GRID_W
    col = pos % GRID_W
    for g, rad in enumerate(POOL_RADII):
        same = row[:, None] == row[None, :]
        bands[g] = (same & (np.abs(col[:, None] - col[None, :]) <= rad)).astype(np.float32)
        cnt = np.minimum(col + rad + 1, GRID_W) - np.maximum(col - rad, 0)
        icnt[g] = (1.0 / cnt.astype(np.float32))[:, None]
    return jnp.asarray(bands, BF16), jnp.asarray(icnt, F32)


def _odd_mix(xin, mods, p, ln_w, ln_b, w_s, b_s, pool_w, pool_scale, w_out, fnw):
    r, d = xin.shape
    tm, tk = 512, 512
    nk_c = d // tk
    nk = 2 * nk_c
    gpb = tk // HEAD_DIM
    tiles_per_batch = SEQ // tm
    bands, icnt = _pool_constants(tm)
    lo = lambda kk: jnp.minimum(kk, nk_c - 1)
    hi = lambda kk: jnp.maximum(kk - nk_c, 0)
    cblk = d // tk
    return pl.pallas_call(
        functools.partial(_odd_mix_kernel, nk_c=nk_c),
        grid=(r // tm, nk),
        in_specs=[pl.BlockSpec((tm, d), lambda i, kk: (i, 0)),
                  pl.BlockSpec((1, 1, d), lambda i, kk: (i // tiles_per_batch, 0, 2)),
                  pl.BlockSpec((tm, tk), lambda i, kk: (i, lo(kk))),
                  pl.BlockSpec((tm, d), lambda i, kk: (i, 1)),
                  pl.BlockSpec((tm, tk), lambda i, kk: (i, 2 * cblk + lo(kk))),
                  pl.BlockSpec((tm, tk), lambda i, kk: (i, 3 * cblk + hi(kk))),
                  pl.BlockSpec((tm, tk), lambda i, kk: (i, 4 * cblk + hi(kk))),
                  pl.BlockSpec((1, d), lambda i, kk: (0, 0)),
                  pl.BlockSpec((1, d), lambda i, kk: (0, 0)),
                  pl.BlockSpec((gpb, TOK_CHUNK, TOK_CHUNK), lambda i, kk: (lo(kk), 0, 0)),
                  pl.BlockSpec((gpb, TOK_CHUNK, LANES), lambda i, kk: (lo(kk), 0, 0)),
                  pl.BlockSpec((1, tm, tm), lambda i, kk: (hi(kk), 0, 0)),
                  pl.BlockSpec((1, tm, LANES), lambda i, kk: (hi(kk), 0, 0)),
                  pl.BlockSpec((1, POOL_GW, POOL_GW), lambda i, kk: (hi(kk), 0, 0)),
                  pl.BlockSpec((1, tk), lambda i, kk: (0, hi(kk))),
                  pl.BlockSpec((tk, d), lambda i, kk: (kk, 0)),
                  pl.BlockSpec((1, d), lambda i, kk: (0, 0))],
        out_specs=pl.BlockSpec((tm, d), lambda i, kk: (i, 0)),
        out_shape=jax.ShapeDtypeStruct((r, d), F32),
        scratch_shapes=[pltpu.VMEM((tm, d), F32), pltpu.VMEM((nk_c, tm, tk), BF16)],
        compiler_params=_params(("parallel", "arbitrary")),
        name="odd_mix",
    )(xin, mods, p, p, p, p, p, ln_w, ln_b, w_s, b_s, bands, icnt, pool_w, pool_scale, w_out, fnw)


def kernel(x, c, ctx, c_ctx, ada_w, ada_b, norm_w, e_w_in, e_conv_qkv, e_a_log, e_dt_bias, e_head_norm,
           e_conv_b, e_w_out, o_w_in, o_ln_w, o_ln_b, o_w_s, o_b_s, o_pool_w, o_pool_scale, o_w_out,
           final_norm_w):
    bsz, seq, d = x.shape
    rx = bsz * seq
    rc = bsz * ctx.shape[1]
    x2 = x.reshape(rx, d)
    ctx2 = ctx.reshape(rc, d)

    cc = jnp.concatenate([c, c_ctx[None, :], jnp.zeros((8 - bsz - 1, d), F32)], axis=0)
    mods = _adaln(cc, ada_w, ada_b)
    ctx_row = bsz

    w_in = e_w_in[0]
    gate_lo = QKV_W + D_MODEL
    gate_hi = gate_lo + 4 * HEADS
    wqkv = w_in[:, :QKV_W].astype(BF16)
    wg = jnp.pad(w_in[:, gate_lo:gate_hi], ((0, 0), (0, LANES - 4 * HEADS))).astype(BF16)
    w5 = jnp.concatenate([w_in[:, QKV_W:gate_lo], w_in[:, gate_hi:]], axis=1).astype(BF16)
    m0 = mods[0].reshape(8, 1, 3 * d)
    nw0 = norm_w[0].reshape(1, d)

    gate_par = jnp.pad(jnp.stack([e_a_log[0].reshape(-1), e_dt_bias[0].reshape(-1)]),
                       ((0, 0), (0, LANES - 2 * HEADS)))
    qkv_x, gates_x = _qkv_proj(x2, m0, nw0, wqkv, wg, e_conv_qkv[0], gate_par,
                               rows_per_batch=seq, mod_row0=0, group=GRID_W)
    qkv_c, gates_c = _qkv_proj(ctx2, m0, nw0, wqkv, wg, e_conv_qkv[0], gate_par,
                               rows_per_batch=ctx.shape[1], mod_row0=ctx_row, group=ctx.shape[1])
    gz_x, yb_x = _mixb_proj(x2, m0, nw0, w5, e_conv_b[0],
                            rows_per_batch=seq, mod_row0=0, group=GRID_W)
    gz_c, yb_c = _mixb_proj(ctx2, m0, nw0, w5, e_conv_b[0],
                            rows_per_batch=ctx.shape[1], mod_row0=ctx_row, group=ctx.shape[1])
    o_c, o_x = _gdn(qkv_c, qkv_x, gates_c, gates_x,
                    e_head_norm[0].reshape(1, HEAD_DIM))
    w_out0 = e_w_out[0].astype(BF16)
    x2 = _even_out(x2, m0, o_x.reshape(rx, d), gz_x, yb_x, w_out0, rows_per_batch=seq, mod_row0=0)
    ctx2 = _even_out(ctx2, m0, o_c.reshape(rc, d), gz_c, yb_c, w_out0,
                     rows_per_batch=ctx.shape[1], mod_row0=ctx_row)

    m1 = mods[1].reshape(8, 1, 3 * d)
    p = _odd_in(x2, m1, norm_w[1].reshape(1, d), o_w_in[0].astype(BF16))
    b_s = jnp.broadcast_to(o_b_s[0][:, :, None], (o_b_s.shape[1], TOK_CHUNK, LANES))
    out = _odd_mix(x2, m1, p, o_ln_w[0].reshape(1, d), o_ln_b[0].reshape(1, d),
                   o_w_s[0].astype(BF16), b_s, o_pool_w[0].astype(BF16),
                   o_pool_scale[0].reshape(1, d), o_w_out[0].astype(BF16),
                   final_norm_w.reshape(1, d))
    return out.reshape(bsz, seq, d)
```

```python
import functools
import math

import numpy as np
import jax
import jax.numpy as jnp
from jax import lax
from jax.experimental import pallas as pl
from jax.experimental.pallas import tpu as pltpu

F32 = jnp.float32
BF16 = jnp.bfloat16

D_MODEL = 2048
BATCH = 4
SEQ = 2048
CTX_LEN = 256
GRID_W = 64
EPS = 1e-6
HEADS = 16
HEAD_DIM = 128
QKV_W = 3 * D_MODEL
CHUNK = 64
SCAN_GROUP = 12
TOK_CHUNK = 128
POOL_GROUPS = 4
POOL_GW = D_MODEL // POOL_GROUPS
POOL_RADII = (1, 2, 4, 8)

LANES = 128
VMEM_LIMIT = 56 * 1024 * 1024


def _params(sem):
    return pltpu.CompilerParams(dimension_semantics=sem, vmem_limit_bytes=VMEM_LIMIT)


def _sigmoid(x):
    return 1.0 / (1.0 + jnp.exp(-x))


def _silu(x):
    return x * _sigmoid(x)


def _gelu_tanh(x):
    cdf = 0.5 * (1.0 + jnp.tanh(math.sqrt(2.0 / math.pi) * (x + 0.044715 * (x * x * x))))
    return x * cdf


def _modulated_norm(x, nw, sc, sh):
    ms = jnp.mean(x * x, axis=-1, keepdims=True)
    y = x * lax.rsqrt(ms + EPS) * nw
    return y * (1.0 + sc) + sh


def _row_conv3(p, cw, group):
    tm = p.shape[0]
    row = lax.broadcasted_iota(jnp.int32, (tm, 1), 0) % group
    prev = jnp.where(row == 0, 0.0, pltpu.roll(p, 1, 0))
    nxt = jnp.where(row == group - 1, 0.0, pltpu.roll(p, tm - 1, 0))
    return prev * cw[0:1] + p * cw[1:2] + nxt * cw[2:3]


def _adaln_kernel(c_ref, w_ref, b_ref, o_ref):
    a = _silu(c_ref[...]).astype(BF16)
    o_ref[0] = jnp.dot(a, w_ref[0].astype(BF16), preferred_element_type=F32) + b_ref[0]


def _adaln(cc, ada_w, ada_b):
    depth, d, n = ada_w.shape
    tn = 1024
    return pl.pallas_call(
        _adaln_kernel,
        grid=(depth, n // tn),
        in_specs=[pl.BlockSpec((8, d), lambda l, j: (0, 0)),
                  pl.BlockSpec((1, d, tn), lambda l, j: (l, 0, j)),
                  pl.BlockSpec((1, 1, tn), lambda l, j: (l, 0, j))],
        out_specs=pl.BlockSpec((1, 8, tn), lambda l, j: (l, 0, j)),
        out_shape=jax.ShapeDtypeStruct((depth, 8, n), F32),
        compiler_params=_params(("parallel", "parallel")),
        name="adaln",
    )(cc, ada_w, ada_b.reshape(depth, 1, n))


def _qkv_kernel(x_ref, nw_ref, sh_ref, sc_ref, w_ref, wg_ref, cw_ref, gp_ref, qkv_ref, g_ref, h_scr,
                *, group, n_tiles):
    j = pl.program_id(1)

    @pl.when(j == 0)
    def _():
        h_scr[...] = _modulated_norm(x_ref[...], nw_ref[...], sc_ref[0], sh_ref[0]).astype(BF16)

    @pl.when(j < n_tiles)
    def _():
        p = jnp.dot(h_scr[...], w_ref[...], preferred_element_type=F32)
        y = _silu(_row_conv3(p, cw_ref[...], group))
        for hh in range(y.shape[1] // HEAD_DIM):
            qkv_ref[hh] = y[:, hh * HEAD_DIM:(hh + 1) * HEAD_DIM]

    @pl.when(j == n_tiles)
    def _():
        raw = jnp.dot(h_scr[...], wg_ref[...], preferred_element_type=F32)
        z = raw + gp_ref[1:2]
        softplus = jnp.maximum(z, 0.0) + jnp.log1p(jnp.exp(-jnp.abs(z)))
        lane = lax.broadcasted_iota(jnp.int32, (1, LANES), 1)
        g_ref[...] = jnp.where(lane < 2 * HEADS, -jnp.exp(gp_ref[0:1]) * softplus, _sigmoid(raw))


def _qkv_proj(xin, mods, nw, wqkv, wg, cw, gate_par, *, rows_per_batch, mod_row0, group):
    r, d = xin.shape
    tm, tn = 1024, 768
    n_tiles = QKV_W // tn
    hpt = tn // HEAD_DIM
    tiles_per_batch = max(rows_per_batch // tm, 1)

    def mod_idx(part):
        if rows_per_batch >= tm:
            return lambda i, j: (mod_row0 + i // tiles_per_batch, 0, part)
        return lambda i, j: (mod_row0, 0, part)

    last = n_tiles - 1
    return pl.pallas_call(
        functools.partial(_qkv_kernel, group=group, n_tiles=n_tiles),
        grid=(r // tm, n_tiles + 1),
        in_specs=[pl.BlockSpec((tm, d), lambda i, j: (i, 0)),
                  pl.BlockSpec((1, d), lambda i, j: (0, 0)),
                  pl.BlockSpec((1, 1, d), mod_idx(0)),
                  pl.BlockSpec((1, 1, d), mod_idx(1)),
                  pl.BlockSpec((d, tn), lambda i, j: (0, jnp.minimum(j, last))),
                  pl.BlockSpec((d, LANES), lambda i, j: (0, 0)),
                  pl.BlockSpec((3, tn), lambda i, j: (0, jnp.minimum(j, last))),
                  pl.BlockSpec((2, LANES), lambda i, j: (0, 0))],
        out_specs=[pl.BlockSpec((hpt, tm, HEAD_DIM), lambda i, j: (jnp.minimum(j, last), i, 0)),
                   pl.BlockSpec((tm, LANES), lambda i, j: (i, 0))],
        out_shape=[jax.ShapeDtypeStruct((3 * HEADS, r, HEAD_DIM), F32),
                   jax.ShapeDtypeStruct((r, LANES), F32)],
        scratch_shapes=[pltpu.VMEM((tm, d), BF16)],
        compiler_params=_params(("parallel", "arbitrary")),
        name="qkv_proj",
    )(xin, nw, mods, mods, wqkv, wg, cw, gate_par)


def _mixb_kernel(x_ref, nw_ref, sh_ref, sc_ref, wz_ref, wb_ref, wc_ref, wh_ref, wzb_ref, cw_ref,
                 gz_ref, yb_ref, h_scr, *, group):
    j = pl.program_id(1)

    @pl.when(j == 0)
    def _():
        h_scr[...] = _modulated_norm(x_ref[...], nw_ref[...], sc_ref[0], sh_ref[0]).astype(BF16)

    h = h_scr[...]
    dot = lambda w: jnp.dot(h, w[...], preferred_element_type=F32)
    gz_ref[...] = _silu(dot(wz_ref))
    conv = _row_conv3(dot(wc_ref) * dot(wh_ref), cw_ref[...], group)
    yb_ref[...] = dot(wb_ref) * conv * _silu(dot(wzb_ref))


def _mixb_proj(xin, mods, nw, w5, cw, *, rows_per_batch, mod_row0, group):
    r, d = xin.shape
    tm, tn = 1024, 256
    nj = d // tn
    tiles_per_batch = max(rows_per_batch // tm, 1)

    def mod_idx(part):
        if rows_per_batch >= tm:
            return lambda i, j: (mod_row0 + i // tiles_per_batch, 0, part)
        return lambda i, j: (mod_row0, 0, part)

    wspec = lambda reg: pl.BlockSpec((d, tn), lambda i, j: (0, reg * nj + j))
    return pl.pallas_call(
        functools.partial(_mixb_kernel, group=group),
        grid=(r // tm, nj),
        in_specs=[pl.BlockSpec((tm, d), lambda i, j: (i, 0)),
                  pl.BlockSpec((1, d), lambda i, j: (0, 0)),
                  pl.BlockSpec((1, 1, d), mod_idx(0)),
                  pl.BlockSpec((1, 1, d), mod_idx(1)),
                  wspec(0), wspec(1), wspec(2), wspec(3), wspec(4),
                  pl.BlockSpec((3, tn), lambda i, j: (0, j))],
        out_specs=[pl.BlockSpec((tm, tn), lambda i, j: (i, j)),
                   pl.BlockSpec((tm, tn), lambda i, j: (i, j))],
        out_shape=[jax.ShapeDtypeStruct((r, d), F32), jax.ShapeDtypeStruct((r, d), F32)],
        scratch_shapes=[pltpu.VMEM((tm, d), BF16)],
        compiler_params=_params(("parallel", "arbitrary")),
        name="mixb_proj",
    )(xin, nw, mods, mods, w5, w5, w5, w5, w5, cw)


def _split3_bf16(x):
    x1 = x.astype(BF16)
    r1 = x - x1.astype(F32)
    x2 = r1.astype(BF16)
    x3 = (r1 - x2.astype(F32)).astype(BF16)
    return x1, x2, x3


def _dots_exact01(m01s, xs):
    pieces = [_split3_bf16(x) for x in xs]
    prods = [[jnp.dot(m01, t, preferred_element_type=F32) for t in p] for m01, p in zip(m01s, pieces)]
    yield
    return [(p[0] + p[1]) + p[2] for p in prods]


def _block_diag2(x, left):
    return jnp.concatenate([jnp.where(left, x, 0.0), jnp.where(left, 0.0, x)], axis=0).astype(BF16)


def _unit_tri_inverse_minus_eye(ms, left, ri, cj):
    n = ms[0].shape[0]
    ys = None
    s = 1
    while s < n:
        shift = s.bit_length() - 1
        br = lax.shift_right_logical(ri, shift)
        bc = lax.shift_right_logical(cj, shift)
        mask_lo = ((br & 1) == 1) & (bc == br - 1)
        mask_up = ((bc & 1) == 1) & (br == bc - 1)
        mask = (left & mask_lo) | (jnp.logical_not(left) & mask_up)
        cms = [jnp.where(mask, m, 0.0) for m in ms]
        if ys is None:
            ys = [-cm for cm in cms]
        else:
            yc = [jnp.dot(y.astype(BF16), _block_diag2(cm, left), preferred_element_type=F32)
                  for cm, y in zip(cms, ys)]
            yield
            ps = [cm + t for cm, t in zip(cms, yc)]
            py = [jnp.dot(p.astype(BF16), _block_diag2(y, left), preferred_element_type=F32)
                  for y, p in zip(ys, ps)]
            yield
            ys = [y - p - t for y, p, t in zip(ys, ps, py)]
        s *= 2
    return ys


def _run_interleaved(*gens):
    active = list(gens)
    while active:
        for g in list(active):
            try:
                next(g)
            except StopIteration:
                active.remove(g)


def _gdn_kernel(qc_ref, kc_ref, vc_ref, qx_ref, kx_ref, vx_ref, gc_ref, gx_ref,
                hn_ref, oc_ref, ox_ref,
                qs, ks, vs, gsc, bsc, nq_s, b_s, op_s, gl_s, oacc):
    h = pl.program_id(1)
    tc = qc_ref.shape[1]
    tx = qx_ref.shape[1]
    t = tc + tx
    nsteps = t // CHUNK
    nc_ctx = tc // CHUNK
    c = CHUNK

    def l2n(a):
        return a * lax.rsqrt(jnp.sum(a * a, axis=-1, keepdims=True) + EPS)

    qscale = HEAD_DIM ** -0.5
    qs[0:tc] = l2n(qc_ref[0]) * qscale
    qs[tc:t] = l2n(qx_ref[0]) * qscale
    ks[0:tc] = l2n(kc_ref[0])
    ks[tc:t] = l2n(kx_ref[0])
    vs[0:tc] = vc_ref[0]
    vs[tc:t] = vx_ref[0]

    lane = lax.broadcasted_iota(jnp.int32, (1, LANES), 1)
    for d in range(2):
        for ref, lo, hi in ((gc_ref, 0, tc), (gx_ref, tc, t)):
            gates = ref[...]
            g = jnp.sum(jnp.where(lane == d * HEADS + h, gates, 0.0), axis=-1, keepdims=True)
            b = jnp.sum(jnp.where(lane == 2 * HEADS + d * HEADS + h, gates, 0.0), axis=-1,
                        keepdims=True)
            gsc[d, lo:hi] = jnp.broadcast_to(g, (hi - lo, LANES))
            bsc[d, lo:hi] = jnp.broadcast_to(b, (hi - lo, LANES))

    def rev_chunk(s):
        return nc_ctx - 1 - s if s < nc_ctx else nsteps + nc_ctx - 1 - s

    ri1 = lax.broadcasted_iota(jnp.int32, (c, c), 0)
    ci1 = lax.broadcasted_iota(jnp.int32, (c, c), 1)
    cum_lo = jnp.where(ci1 <= ri1, 1.0, 0.0).astype(BF16)
    cum_up = jnp.where(ci1 >= ri1, 1.0, 0.0).astype(BF16)
    ri = lax.broadcasted_iota(jnp.int32, (c, 2 * c), 0)
    lane2 = lax.broadcasted_iota(jnp.int32, (c, 2 * c), 1)
    left = lane2 < c
    cj = lane2 & (c - 1)
    strict = (left & (cj < ri)) | (jnp.logical_not(left) & (cj > ri))
    incl = (left & (cj <= ri)) | (jnp.logical_not(left) & (cj >= ri))
    left_sq = lax.broadcasted_iota(jnp.int32, (2 * c, 2 * c), 1) < c

    def prep(steps):
        units = range(len(steps))
        rows = [[slice(n * c, (n + 1) * c) for n in (s, rev_chunk(s))] for s in steps]
        qv = [[qs[r] for r in rows[uu]] for uu in units]
        kv = [[ks[r] for r in rows[uu]] for uu in units]
        vv = [[vs[r] for r in rows[uu]] for uu in units]
        beta = [[bsc[d, rows[uu][d]] for d in range(2)] for uu in units]
        kb = [[kv[uu][d] * beta[uu][d] for d in range(2)] for uu in units]
        gk = [lax.dot_general(
                  jnp.concatenate([kb[uu][0], qv[uu][0], kb[uu][1], qv[uu][1]], axis=0).astype(BF16),
                  jnp.concatenate(kv[uu], axis=0).astype(BF16),
                  (((1,), (1,)), ((), ())), preferred_element_type=F32)
              for uu in units]
        yield
        gcb = yield from _dots_exact01([cum_lo, cum_up] * len(steps),
                                       [gsc[d, rows[uu][d]] for uu in units for d in range(2)])
        gcb = [gcb[2 * uu:2 * uu + 2] for uu in units]
        ms, avs = [], []
        for uu in units:
            col = jnp.where(left, gcb[uu][0], gcb[uu][1])
            row = jnp.transpose(jnp.concatenate(gcb[uu], axis=0))[0:c]
            decay = jnp.exp(jnp.where(incl, col - row, 0.0))
            kk = jnp.where(left, gk[uu][0:c], gk[uu][2 * c:3 * c])
            qk = jnp.where(left, gk[uu][c:2 * c], gk[uu][3 * c:4 * c])
            ms.append(jnp.where(strict, kk * decay, 0.0))
            avs.append(jnp.where(incl, qk * decay, 0.0))
        ys = yield from _unit_tri_inverse_minus_eye(ms, left, ri, cj)
        egc = [[jnp.exp(g) for g in gcb[uu]] for uu in units]
        rs = [[jnp.concatenate([kb[uu][d] * egc[uu][d], vv[uu][d] * beta[uu][d]], axis=1)
               for d in range(2)] for uu in units]
        corr = [jnp.dot(_block_diag2(ys[uu], left),
                        jnp.concatenate(rs[uu], axis=0).astype(BF16), preferred_element_type=F32)
                for uu in units]
        yield
        wus, aks, g_edges = [], [], []
        for uu in units:
            kgs, edges = [], []
            for d in range(2):
                edge = c - 1 if d == 0 else 0
                edges.append(gcb[uu][d][edge:edge + 1])
                kgs.append(kv[uu][d] * jnp.exp(edges[d] - gcb[uu][d]))
            g_edges.append(edges)
            wus.append([rs[uu][d] + corr[uu][d * c:(d + 1) * c] for d in range(2)])
            kgt = jnp.transpose(jnp.concatenate(kgs, axis=0))
            aks.append(jnp.concatenate(
                [jnp.where(left, avs[uu], 0.0), jnp.where(left_sq, kgt, 0.0),
                 jnp.where(left, 0.0, avs[uu]), jnp.where(left_sq, 0.0, kgt)], axis=0).astype(BF16))
        prods = [jnp.dot(aks[uu], jnp.concatenate(wus[uu], axis=0).astype(BF16),
                         preferred_element_type=F32) for uu in units]
        yield
        for uu in units:
            for d in range(2):
                x = prods[uu][d * (c + HEAD_DIM):(d + 1) * (c + HEAD_DIM)]
                a_w, a_u = x[0:c, 0:HEAD_DIM], x[0:c, HEAD_DIM:2 * HEAD_DIM]
                k_w, k_u = x[c:c + HEAD_DIM, 0:HEAD_DIM], x[c:c + HEAD_DIM, HEAD_DIM:2 * HEAD_DIM]
                idx = 2 * steps[uu] + d
                nq_s[idx] = jnp.concatenate([-k_w, qv[uu][d] * egc[uu][d] - a_w],
                                            axis=0).astype(BF16)
                b_s[idx] = k_u
                op_s[idx] = a_u
                gl_s[idx] = jnp.broadcast_to(jnp.exp(g_edges[uu][d]), (8, LANES))

    states = [jnp.zeros((HEAD_DIM, HEAD_DIM), F32)] * 2
    written = set()

    def scan(steps):
        for s in steps:
            idxs = (2 * s, 2 * s + 1)
            xs = [jnp.dot(nq_s[idx], st.astype(BF16), preferred_element_type=F32)
                  for idx, st in zip(idxs, states)]
            yield
            for n, idx, x in zip((s, rev_chunk(s)), idxs, xs):
                o = x[HEAD_DIM:HEAD_DIM + c] + op_s[idx]
                rows = slice(n * c, (n + 1) * c)
                oacc[rows] = oacc[rows] + o if n in written else o
                written.add(n)
            states[:] = [st * gl_s[idx][0:1] + x[0:HEAD_DIM] + b_s[idx]
                         for idx, st, x in zip(idxs, states, xs)]

    groups = [list(range(i, i + SCAN_GROUP)) for i in range(0, nsteps, SCAN_GROUP)]
    _run_interleaved(prep(groups[0]))
    for gi in range(1, len(groups)):
        _run_interleaved(prep(groups[gi]), scan(groups[gi - 1]))
    _run_interleaved(scan(groups[-1]))

    o = oacc[...]
    on = o * lax.rsqrt(jnp.mean(o * o, axis=-1, keepdims=True) + EPS) * hn_ref[...]
    oc_ref[0] = on[0:tc]
    ox_ref[0] = on[tc:t]


def _gdn(qkv_c, qkv_x, gates_c, gates_x, head_norm):
    tc, tx = CTX_LEN, SEQ
    t = tc + tx
    nidx = 2 * (t // CHUNK)
    part = lambda rows, p: pl.BlockSpec((1, rows, HEAD_DIM), lambda b, h: (p * HEADS + h, b, 0))
    return pl.pallas_call(
        _gdn_kernel,
        grid=(BATCH, HEADS),
        in_specs=[part(tc, 0), part(tc, 1), part(tc, 2),
                  part(tx, 0), part(tx, 1), part(tx, 2),
                  pl.BlockSpec((tc, LANES), lambda b, h: (b, 0)),
                  pl.BlockSpec((tx, LANES), lambda b, h: (b, 0)),
                  pl.BlockSpec((1, HEAD_DIM), lambda b, h: (0, 0))],
        out_specs=[pl.BlockSpec((1, tc, HEAD_DIM), lambda b, h: (b, 0, h)),
                   pl.BlockSpec((1, tx, HEAD_DIM), lambda b, h: (b, 0, h))],
        out_shape=[jax.ShapeDtypeStruct((BATCH, tc, D_MODEL), F32),
                   jax.ShapeDtypeStruct((BATCH, tx, D_MODEL), F32)],
        scratch_shapes=[pltpu.VMEM((t, HEAD_DIM), F32), pltpu.VMEM((t, HEAD_DIM), F32),
                        pltpu.VMEM((t, HEAD_DIM), F32),
                        pltpu.VMEM((2, t, LANES), F32), pltpu.VMEM((2, t, LANES), F32),
                        pltpu.VMEM((nidx, HEAD_DIM + CHUNK, HEAD_DIM), BF16),
                        pltpu.VMEM((nidx, HEAD_DIM, HEAD_DIM), F32),
                        pltpu.VMEM((nidx, CHUNK, HEAD_DIM), F32),
                        pltpu.VMEM((nidx, 8, LANES), F32),
                        pltpu.VMEM((t, HEAD_DIM), F32)],
        compiler_params=_params(("parallel", "arbitrary")),
        name="gdn",
    )(qkv_c, qkv_c, qkv_c, qkv_x, qkv_x, qkv_x, gates_c, gates_x, head_norm)


def _even_out_kernel(x_ref, gt_ref, o_ref, gz_ref, yb_ref, w_ref, out_ref, a_scr):
    j = pl.program_id(1)

    @pl.when(j == 0)
    def _():
        d = o_ref.shape[1]
        a_scr[:, 0:d] = (o_ref[...] * gz_ref[...]).astype(BF16)
        a_scr[:, d:2 * d] = yb_ref[...].astype(BF16)

    y = jnp.dot(a_scr[...], w_ref[...], preferred_element_type=F32)
    out_ref[...] = x_ref[...] + gt_ref[0] * y


def _even_out(xin, mods, o, gz, yb, w_out, *, rows_per_batch, mod_row0):
    r, d = xin.shape
    tm = min(512, rows_per_batch)
    tn = 512
    nj = d // tn
    tiles_per_batch = rows_per_batch // tm
    fixed = mod_row0 >= BATCH
    gt_idx = ((lambda i, j: (mod_row0, 0, 2 * nj + j)) if fixed
              else (lambda i, j: (i // tiles_per_batch, 0, 2 * nj + j)))
    return pl.pallas_call(
        _even_out_kernel,
        grid=(r // tm, nj),
        in_specs=[pl.BlockSpec((tm, tn), lambda i, j: (i, j)),
                  pl.BlockSpec((1, 1, tn), gt_idx),
                  pl.BlockSpec((tm, d), lambda i, j: (i, 0)),
                  pl.BlockSpec((tm, d), lambda i, j: (i, 0)),
                  pl.BlockSpec((tm, d), lambda i, j: (i, 0)),
                  pl.BlockSpec((2 * d, tn), lambda i, j: (0, j))],
        out_specs=pl.BlockSpec((tm, tn), lambda i, j: (i, j)),
        out_shape=jax.ShapeDtypeStruct((r, d), F32),
        scratch_shapes=[pltpu.VMEM((tm, 2 * d), BF16)],
        compiler_params=_params(("parallel", "arbitrary")),
        name="even_out",
    )(xin, mods, o, gz, yb, w_out)


def _odd_in_kernel(x_ref, nw_ref, sh_ref, sc_ref, w_ref, out_ref, h_scr, *, tiles_per_region):
    j = pl.program_id(1)

    @pl.when(j == 0)
    def _():
        h_scr[...] = _modulated_norm(x_ref[...], nw_ref[...], sc_ref[0], sh_ref[0]).astype(BF16)

    p = jnp.dot(h_scr[...], w_ref[...], preferred_element_type=F32)
    region = j // tiles_per_region

    @pl.when(region <= 1)
    def _():
        out_ref[...] = _gelu_tanh(p)

    @pl.when((region == 2) | (region == 4))
    def _():
        out_ref[...] = _silu(p)

    @pl.when(region == 3)
    def _():
        out_ref[...] = p


def _odd_in(xin, mods, nw, w):
    r, d = xin.shape
    n = w.shape[1]
    tm, tn = 1024, 512
    tiles_per_batch = SEQ // tm
    mod_idx = lambda part: (lambda i, j: (i // tiles_per_batch, 0, part))
    return pl.pallas_call(
        functools.partial(_odd_in_kernel, tiles_per_region=d // tn),
        grid=(r // tm, n // tn),
        in_specs=[pl.BlockSpec((tm, d), lambda i, j: (i, 0)),
                  pl.BlockSpec((1, d), lambda i, j: (0, 0)),
                  pl.BlockSpec((1, 1, d), mod_idx(0)),
                  pl.BlockSpec((1, 1, d), mod_idx(1)),
                  pl.BlockSpec((d, tn), lambda i, j: (0, j))],
        out_specs=pl.BlockSpec((tm, tn), lambda i, j: (i, j)),
        out_shape=jax.ShapeDtypeStruct((r, n), F32),
        scratch_shapes=[pltpu.VMEM((tm, d), BF16)],
        compiler_params=_params(("parallel", "arbitrary")),
        name="odd_in",
    )(xin, nw, mods, mods, w)


def _odd_mix_kernel(x_ref, gt_ref, u_ref, v_ref, zc_ref, pd_ref, zd_ref, lnw_ref, lnb_ref, ws_ref,
                    bs_ref, band_ref, icnt_ref, pw_ref, ps_ref, wo_ref, fnw_ref, out_ref,
                    acc, ln_scr, *, nk_c):
    kk = pl.program_id(1)
    tm = x_ref.shape[0]
    tk = u_ref.shape[1]
    gpb = tk // HEAD_DIM

    @pl.when(kk == 0)
    def _():
        v = v_ref[...]
        mu = jnp.mean(v, axis=-1, keepdims=True)
        xc = v - mu
        var = jnp.mean(xc * xc, axis=-1, keepdims=True)
        ln = (xc * lax.rsqrt(var + EPS) * lnw_ref[...] + lnb_ref[...]).astype(BF16)
        for blk in range(nk_c):
            ln_scr[blk] = ln[:, blk * tk:(blk + 1) * tk]
        acc[...] = jnp.zeros_like(acc)

    @pl.when(kk < nk_c)
    def _():
        ln = ln_scr[kk]
        cols = []
        for gi in range(gpb):
            w_g = ws_ref[gi]
            b_g = bs_ref[gi]
            rows = []
            for cc in range(tm // TOK_CHUNK):
                blk = ln[cc * TOK_CHUNK:(cc + 1) * TOK_CHUNK, gi * HEAD_DIM:(gi + 1) * HEAD_DIM]
                rows.append(jnp.dot(w_g, blk, preferred_element_type=F32) + b_g)
            cols.append(jnp.concatenate(rows, axis=0))
        s = jnp.concatenate(cols, axis=1)
        yc = u_ref[...] * s * zc_ref[...]
        acc[...] += jnp.dot(yc.astype(BF16), wo_ref[...], preferred_element_type=F32)

    @pl.when(kk >= nk_c)
    def _():
        pd = pd_ref[...]
        hi = pd.astype(BF16)
        mid = (pd - hi.astype(F32)).astype(BF16)
        band = band_ref[0]
        wsum = (jnp.dot(band, hi, preferred_element_type=F32)
                + jnp.dot(band, mid, preferred_element_type=F32))
        icnt = icnt_ref[0]
        icnt = jnp.concatenate([icnt] * (tk // LANES), axis=1)
        diff = wsum * icnt - pd
        y = jnp.dot(diff.astype(BF16), pw_ref[0], preferred_element_type=F32)
        yd = y * ps_ref[...] * zd_ref[...]
        acc[...] += jnp.dot(yd.astype(BF16), wo_ref[...], preferred_element_type=F32)

    @pl.when(kk == pl.num_programs(1) - 1)
    def _():
        xn = x_ref[...] + gt_ref[0] * acc[...]
        ms = jnp.mean(xn * xn, axis=-1, keepdims=True)
        out_ref[...] = xn * lax.rsqrt(ms + EPS) * fnw_ref[...]


def _pool_constants(tm):
    bands = np.zeros((POOL_GROUPS, tm, tm), np.float32)
    icnt = np.zeros((POOL_GROUPS, tm, LANES), np.float32)
    pos = np.arange(tm)
    row = pos // GRID_W
    col = pos % GRID_W
    for g, rad in enumerate(POOL_RADII):
        same = row[:, None] == row[None, :]
        bands[g] = (same & (np.abs(col[:, None] - col[None, :]) <= rad)).astype(np.float32)
        cnt = np.minimum(col + rad + 1, GRID_W) - np.maximum(col - rad, 0)
        icnt[g] = (1.0 / cnt.astype(np.float32))[:, None]
    return jnp.asarray(bands, BF16), jnp.asarray(icnt, F32)


def _odd_mix(xin, mods, p, ln_w, ln_b, w_s, b_s, pool_w, pool_scale, w_out, fnw):
    r, d = xin.shape
    tm, tk = 512, 512
    nk_c = d // tk
    nk = 2 * nk_c
    gpb = tk // HEAD_DIM
    tiles_per_batch = SEQ // tm
    bands, icnt = _pool_constants(tm)
    lo = lambda kk: jnp.minimum(kk, nk_c - 1)
    hi = lambda kk: jnp.maximum(kk - nk_c, 0)
    cblk = d // tk
    return pl.pallas_call(
        functools.partial(_odd_mix_kernel, nk_c=nk_c),
        grid=(r // tm, nk),
        in_specs=[pl.BlockSpec((tm, d), lambda i, kk: (i, 0)),
                  pl.BlockSpec((1, 1, d), lambda i, kk: (i // tiles_per_batch, 0, 2)),
                  pl.BlockSpec((tm, tk), lambda i, kk: (i, lo(kk))),
                  pl.BlockSpec((tm, d), lambda i, kk: (i, 1)),
                  pl.BlockSpec((tm, tk), lambda i, kk: (i, 2 * cblk + lo(kk))),
                  pl.BlockSpec((tm, tk), lambda i, kk: (i, 3 * cblk + hi(kk))),
                  pl.BlockSpec((tm, tk), lambda i, kk: (i, 4 * cblk + hi(kk))),
                  pl.BlockSpec((1, d), lambda i, kk: (0, 0)),
                  pl.BlockSpec((1, d), lambda i, kk: (0, 0)),
                  pl.BlockSpec((gpb, TOK_CHUNK, TOK_CHUNK), lambda i, kk: (lo(kk), 0, 0)),
                  pl.BlockSpec((gpb, TOK_CHUNK, LANES), lambda i, kk: (lo(kk), 0, 0)),
                  pl.BlockSpec((1, tm, tm), lambda i, kk: (hi(kk), 0, 0)),
                  pl.BlockSpec((1, tm, LANES), lambda i, kk: (hi(kk), 0, 0)),
                  pl.BlockSpec((1, POOL_GW, POOL_GW), lambda i, kk: (hi(kk), 0, 0)),
                  pl.BlockSpec((1, tk), lambda i, kk: (0, hi(kk))),
                  pl.BlockSpec((tk, d), lambda i, kk: (kk, 0)),
                  pl.BlockSpec((1, d), lambda i, kk: (0, 0))],
        out_specs=pl.BlockSpec((tm, d), lambda i, kk: (i, 0)),
        out_shape=jax.ShapeDtypeStruct((r, d), F32),
        scratch_shapes=[pltpu.VMEM((tm, d), F32), pltpu.VMEM((nk_c, tm, tk), BF16)],
        compiler_params=_params(("parallel", "arbitrary")),
        name="odd_mix",
    )(xin, mods, p, p, p, p, p, ln_w, ln_b, w_s, b_s, bands, icnt, pool_w, pool_scale, w_out, fnw)


def kernel(x, c, ctx, c_ctx, ada_w, ada_b, norm_w, e_w_in, e_conv_qkv, e_a_log, e_dt_bias, e_head_norm,
           e_conv_b, e_w_out, o_w_in, o_ln_w, o_ln_b, o_w_s, o_b_s, o_pool_w, o_pool_scale, o_w_out,
           final_norm_w):
    bsz, seq, d = x.shape
    rx = bsz * seq
    rc = bsz * ctx.shape[1]
    x2 = x.reshape(rx, d)
    ctx2 = ctx.reshape(rc, d)

    cc = jnp.concatenate([c, c_ctx[None, :], jnp.zeros((8 - bsz - 1, d), F32)], axis=0)
    mods = _adaln(cc, ada_w, ada_b)
    ctx_row = bsz

    w_in = e_w_in[0]
    gate_lo = QKV_W + D_MODEL
    gate_hi = gate_lo + 4 * HEADS
    wqkv = w_in[:, :QKV_W].astype(BF16)
    wg = jnp.pad(w_in[:, gate_lo:gate_hi], ((0, 0), (0, LANES - 4 * HEADS))).astype(BF16)
    w5 = jnp.concatenate([w_in[:, QKV_W:gate_lo], w_in[:, gate_hi:]], axis=1).astype(BF16)
    m0 = mods[0].reshape(8, 1, 3 * d)
    nw0 = norm_w[0].reshape(1, d)

    gate_par = jnp.pad(jnp.stack([e_a_log[0].reshape(-1), e_dt_bias[0].reshape(-1)]),
                       ((0, 0), (0, LANES - 2 * HEADS)))
    qkv_x, gates_x = _qkv_proj(x2, m0, nw0, wqkv, wg, e_conv_qkv[0], gate_par,
                               rows_per_batch=seq, mod_row0=0, group=GRID_W)
    qkv_c, gates_c = _qkv_proj(ctx2, m0, nw0, wqkv, wg, e_conv_qkv[0], gate_par,
                               rows_per_batch=ctx.shape[1], mod_row0=ctx_row, group=ctx.shape[1])
    gz_x, yb_x = _mixb_proj(x2, m0, nw0, w5, e_conv_b[0],
                            rows_per_batch=seq, mod_row0=0, group=GRID_W)
    gz_c, yb_c = _mixb_proj(ctx2, m0, nw0, w5, e_conv_b[0],
                            rows_per_batch=ctx.shape[1], mod_row0=ctx_row, group=ctx.shape[1])
    o_c, o_x = _gdn(qkv_c, qkv_x, gates_c, gates_x,
                    e_head_norm[0].reshape(1, HEAD_DIM))
    w_out0 = e_w_out[0].astype(BF16)
    x2 = _even_out(x2, m0, o_x.reshape(rx, d), gz_x, yb_x, w_out0, rows_per_batch=seq, mod_row0=0)
    ctx2 = _even_out(ctx2, m0, o_c.reshape(rc, d), gz_c, yb_c, w_out0,
                     rows_per_batch=ctx.shape[1], mod_row0=ctx_row)

    m1 = mods[1].reshape(8, 1, 3 * d)
    p = _odd_in(x2, m1, norm_w[1].reshape(1, d), o_w_in[0].astype(BF16))
    b_s = jnp.broadcast_to(o_b_s[0][:, :, None], (o_b_s.shape[1], TOK_CHUNK, LANES))
    out = _odd_mix(x2, m1, p, o_ln_w[0].reshape(1, d), o_ln_b[0].reshape(1, d),
                   o_w_s[0].astype(BF16), b_s, o_pool_w[0].astype(BF16),
                   o_pool_scale[0].reshape(1, d), o_w_out[0].astype(BF16),
                   final_norm_w.reshape(1, d))
    return out.reshape(bsz, seq, d)
```

```python
import functools
import math

import numpy as np
import jax
import jax.numpy as jnp
from jax import lax
from jax.experimental import pallas as pl
from jax.experimental.pallas import tpu as pltpu

F32 = jnp.float32
BF16 = jnp.bfloat16

D_MODEL = 2048
BATCH = 4
SEQ = 2048
CTX_LEN = 256
GRID_W = 64
EPS = 1e-6
HEADS = 16
HEAD_DIM = 128
QKV_W = 3 * D_MODEL
CHUNK = 64
SCAN_GROUPS = (14, 12, 10)
TOK_CHUNK = 128
POOL_GROUPS = 4
POOL_GW = D_MODEL // POOL_GROUPS
POOL_RADII = (1, 2, 4, 8)

LANES = 128
VMEM_LIMIT = 56 * 1024 * 1024
SUB_N = 256


def _params(sem):
    return pltpu.CompilerParams(dimension_semantics=sem, vmem_limit_bytes=VMEM_LIMIT)


def _sigmoid(x):
    return 1.0 / (1.0 + jnp.exp(-x))


def _silu(x):
    return x * _sigmoid(x)


def _gelu_tanh(x):
    cdf = 0.5 * (1.0 + jnp.tanh(math.sqrt(2.0 / math.pi) * (x + 0.044715 * (x * x * x))))
    return x * cdf


def _modulated_norm(x, nw, sc, sh):
    ms = jnp.mean(x * x, axis=-1, keepdims=True)
    y = x * lax.rsqrt(ms + EPS) * nw
    return y * (1.0 + sc) + sh


def _row_conv3(p, cw, group):
    tm = p.shape[0]
    row = lax.broadcasted_iota(jnp.int32, (tm, 1), 0) % group
    prev = jnp.where(row == 0, 0.0, pltpu.roll(p, 1, 0))
    nxt = jnp.where(row == group - 1, 0.0, pltpu.roll(p, tm - 1, 0))
    return prev * cw[0:1] + p * cw[1:2] + nxt * cw[2:3]


def _adaln_kernel(c_ref, w_ref, b_ref, o_ref):
    a = _silu(c_ref[...]).astype(BF16)
    o_ref[0] = jnp.dot(a, w_ref[0].astype(BF16), preferred_element_type=F32) + b_ref[0]


def _adaln(cc, ada_w, ada_b):
    depth, d, n = ada_w.shape
    tn = 1024
    return pl.pallas_call(
        _adaln_kernel,
        grid=(depth, n // tn),
        in_specs=[pl.BlockSpec((8, d), lambda l, j: (0, 0)),
                  pl.BlockSpec((1, d, tn), lambda l, j: (l, 0, j)),
                  pl.BlockSpec((1, 1, tn), lambda l, j: (l, 0, j))],
        out_specs=pl.BlockSpec((1, 8, tn), lambda l, j: (l, 0, j)),
        out_shape=jax.ShapeDtypeStruct((depth, 8, n), F32),
        compiler_params=_params(("parallel", "parallel")),
        name="adaln",
    )(cc, ada_w, ada_b.reshape(depth, 1, n))


def _qkv_kernel(x_ref, nw_ref, sh_ref, sc_ref, w_ref, wg_ref, cw_ref, gp_ref, qkv_ref, g_ref, h_scr,
                *, group, n_tiles):
    j = pl.program_id(1)

    @pl.when(j == 0)
    def _():
        h_scr[...] = _modulated_norm(x_ref[...], nw_ref[...], sc_ref[0], sh_ref[0]).astype(BF16)

    @pl.when(j < n_tiles)
    def _():
        p = jnp.dot(h_scr[...], w_ref[...], preferred_element_type=F32)
        y = _silu(_row_conv3(p, cw_ref[...], group))
        for hh in range(y.shape[1] // HEAD_DIM):
            qkv_ref[hh] = y[:, hh * HEAD_DIM:(hh + 1) * HEAD_DIM]

    @pl.when(j == n_tiles)
    def _():
        raw = jnp.dot(h_scr[...], wg_ref[...], preferred_element_type=F32)
        z = raw + gp_ref[1:2]
        softplus = jnp.maximum(z, 0.0) + jnp.log1p(jnp.exp(-jnp.abs(z)))
        lane = lax.broadcasted_iota(jnp.int32, (1, LANES), 1)
        g_ref[...] = jnp.where(lane < 2 * HEADS, -jnp.exp(gp_ref[0:1]) * softplus, _sigmoid(raw))


def _qkv_proj(xin, mods, nw, wqkv, wg, cw, gate_par, *, rows_per_batch, mod_row0, group):
    r, d = xin.shape
    tm, tn = 1024, 768
    n_tiles = QKV_W // tn
    hpt = tn // HEAD_DIM
    tiles_per_batch = max(rows_per_batch // tm, 1)

    def mod_idx(part):
        if rows_per_batch >= tm:
            return lambda i, j: (mod_row0 + i // tiles_per_batch, 0, part)
        return lambda i, j: (mod_row0, 0, part)

    last = n_tiles - 1
    return pl.pallas_call(
        functools.partial(_qkv_kernel, group=group, n_tiles=n_tiles),
        grid=(r // tm, n_tiles + 1),
        in_specs=[pl.BlockSpec((tm, d), lambda i, j: (i, 0)),
                  pl.BlockSpec((1, d), lambda i, j: (0, 0)),
                  pl.BlockSpec((1, 1, d), mod_idx(0)),
                  pl.BlockSpec((1, 1, d), mod_idx(1)),
                  pl.BlockSpec((d, tn), lambda i, j: (0, jnp.minimum(j, last))),
                  pl.BlockSpec((d, LANES), lambda i, j: (0, 0)),
                  pl.BlockSpec((3, tn), lambda i, j: (0, jnp.minimum(j, last))),
                  pl.BlockSpec((2, LANES), lambda i, j: (0, 0))],
        out_specs=[pl.BlockSpec((hpt, tm, HEAD_DIM), lambda i, j: (jnp.minimum(j, last), i, 0)),
                   pl.BlockSpec((tm, LANES), lambda i, j: (i, 0))],
        out_shape=[jax.ShapeDtypeStruct((3 * HEADS, r, HEAD_DIM), F32),
                   jax.ShapeDtypeStruct((r, LANES), F32)],
        scratch_shapes=[pltpu.VMEM((tm, d), BF16)],
        compiler_params=_params(("parallel", "arbitrary")),
        name="qkv_proj",
    )(xin, nw, mods, mods, wqkv, wg, cw, gate_par)


def _mixb_kernel(x_ref, nw_ref, sh_ref, sc_ref, wz_ref, wb_ref, wc_ref, wh_ref, wzb_ref, cw_ref,
                 gz_ref, yb_ref, h_scr, *, group):
    j = pl.program_id(1)

    @pl.when(j == 0)
    def _():
        h_scr[...] = _modulated_norm(x_ref[...], nw_ref[...], sc_ref[0], sh_ref[0]).astype(BF16)

    h = h_scr[...]
    dot = lambda w: jnp.dot(h, w[...], preferred_element_type=F32)
    gz_ref[...] = _silu(dot(wz_ref)).astype(gz_ref.dtype)
    conv = _row_conv3(dot(wc_ref) * dot(wh_ref), cw_ref[...], group)
    yb_ref[...] = (dot(wb_ref) * conv * _silu(dot(wzb_ref))).astype(yb_ref.dtype)


def _mixb_proj(xin, mods, nw, w5, cw, *, rows_per_batch, mod_row0, group):
    r, d = xin.shape
    tm, tn = 1024, 256
    nj = d // tn
    tiles_per_batch = max(rows_per_batch // tm, 1)

    def mod_idx(part):
        if rows_per_batch >= tm:
            return lambda i, j: (mod_row0 + i // tiles_per_batch, 0, part)
        return lambda i, j: (mod_row0, 0, part)

    wspec = lambda reg: pl.BlockSpec((d, tn), lambda i, j: (0, reg * nj + j))
    return pl.pallas_call(
        functools.partial(_mixb_kernel, group=group),
        grid=(r // tm, nj),
        in_specs=[pl.BlockSpec((tm, d), lambda i, j: (i, 0)),
                  pl.BlockSpec((1, d), lambda i, j: (0, 0)),
                  pl.BlockSpec((1, 1, d), mod_idx(0)),
                  pl.BlockSpec((1, 1, d), mod_idx(1)),
                  wspec(0), wspec(1), wspec(2), wspec(3), wspec(4),
                  pl.BlockSpec((3, tn), lambda i, j: (0, j))],
        out_specs=[pl.BlockSpec((tm, tn), lambda i, j: (i, j)),
                   pl.BlockSpec((tm, tn), lambda i, j: (i, j))],
        out_shape=[jax.ShapeDtypeStruct((r, d), BF16), jax.ShapeDtypeStruct((r, d), BF16)],
        scratch_shapes=[pltpu.VMEM((tm, d), BF16)],
        compiler_params=_params(("parallel", "arbitrary")),
        name="mixb_proj",
    )(xin, nw, mods, mods, w5, w5, w5, w5, w5, cw)


def _split3_bf16(x):
    x1 = x.astype(BF16)
    r1 = x - x1.astype(F32)
    x2 = r1.astype(BF16)
    x3 = (r1 - x2.astype(F32)).astype(BF16)
    return x1, x2, x3


def _dots_exact01(m01s, xs):
    pieces = [_split3_bf16(x) for x in xs]
    prods = [[jnp.dot(m01, t, preferred_element_type=F32) for t in p] for m01, p in zip(m01s, pieces)]
    yield
    return [(p[0] + p[1]) + p[2] for p in prods]


def _block_diag2(x, left):
    return jnp.concatenate([jnp.where(left, x, 0.0), jnp.where(left, 0.0, x)], axis=0).astype(BF16)


def _unit_tri_inverse_minus_eye(ms, left, ri, cj):
    n = ms[0].shape[0]
    ys = None
    s = 1
    while s < n:
        shift = s.bit_length() - 1
        br = lax.shift_right_logical(ri, shift)
        bc = lax.shift_right_logical(cj, shift)
        mask_lo = ((br & 1) == 1) & (bc == br - 1)
        mask_up = ((bc & 1) == 1) & (br == bc - 1)
        mask = (left & mask_lo) | (jnp.logical_not(left) & mask_up)
        cms = [jnp.where(mask, m, 0.0) for m in ms]
        if ys is None:
            ys = [-cm for cm in cms]
        else:
            yc = [jnp.dot(y.astype(BF16), _block_diag2(cm, left), preferred_element_type=F32)
                  for cm, y in zip(cms, ys)]
            yield
            ps = [cm + t for cm, t in zip(cms, yc)]
            py = [jnp.dot(p.astype(BF16), _block_diag2(y, left), preferred_element_type=F32)
                  for y, p in zip(ys, ps)]
            yield
            ys = [y - p - t for y, p, t in zip(ys, ps, py)]
        s *= 2
    return ys


def _run_interleaved(*gens):
    active = list(gens)
    while active:
        for g in list(active):
            try:
                next(g)
            except StopIteration:
                active.remove(g)


def _gdn_kernel(qc_ref, kc_ref, vc_ref, qx_ref, kx_ref, vx_ref, gc_ref, gx_ref,
                hn_ref, oc_ref, ox_ref,
                qs, ks, vs, gsc, bsc, nq_s, b_s, op_s, gl_s, oacc):
    h = pl.program_id(1)
    tc = qc_ref.shape[1]
    tx = qx_ref.shape[1]
    t = tc + tx
    nsteps = t // CHUNK
    nc_ctx = tc // CHUNK
    c = CHUNK

    def l2n(a):
        return a * lax.rsqrt(jnp.sum(a * a, axis=-1, keepdims=True) + EPS)

    qscale = HEAD_DIM ** -0.5
    qs[0:tc] = l2n(qc_ref[0]) * qscale
    qs[tc:t] = l2n(qx_ref[0]) * qscale
    ks[0:tc] = l2n(kc_ref[0])
    ks[tc:t] = l2n(kx_ref[0])
    vs[0:tc] = vc_ref[0]
    vs[tc:t] = vx_ref[0]

    lane = lax.broadcasted_iota(jnp.int32, (1, LANES), 1)
    for d in range(2):
        for ref, lo, hi in ((gc_ref, 0, tc), (gx_ref, tc, t)):
            gates = ref[...]
            g = jnp.sum(jnp.where(lane == d * HEADS + h, gates, 0.0), axis=-1, keepdims=True)
            b = jnp.sum(jnp.where(lane == 2 * HEADS + d * HEADS + h, gates, 0.0), axis=-1,
                        keepdims=True)
            gsc[d, lo:hi] = jnp.broadcast_to(g, (hi - lo, LANES))
            bsc[d, lo:hi] = jnp.broadcast_to(b, (hi - lo, LANES))

    def rev_chunk(s):
        return nc_ctx - 1 - s if s < nc_ctx else nsteps + nc_ctx - 1 - s

    ri1 = lax.broadcasted_iota(jnp.int32, (c, c), 0)
    ci1 = lax.broadcasted_iota(jnp.int32, (c, c), 1)
    cum_lo = jnp.where(ci1 <= ri1, 1.0, 0.0).astype(BF16)
    cum_up = jnp.where(ci1 >= ri1, 1.0, 0.0).astype(BF16)
    ri = lax.broadcasted_iota(jnp.int32, (c, 2 * c), 0)
    lane2 = lax.broadcasted_iota(jnp.int32, (c, 2 * c), 1)
    left = lane2 < c
    cj = lane2 & (c - 1)
    strict = (left & (cj < ri)) | (jnp.logical_not(left) & (cj > ri))
    incl = (left & (cj <= ri)) | (jnp.logical_not(left) & (cj >= ri))
    left_sq = lax.broadcasted_iota(jnp.int32, (2 * c, 2 * c), 1) < c

    def prep(steps):
        units = range(len(steps))
        rows = [[slice(n * c, (n + 1) * c) for n in (s, rev_chunk(s))] for s in steps]
        qv = [[qs[r] for r in rows[uu]] for uu in units]
        kv = [[ks[r] for r in rows[uu]] for uu in units]
        vv = [[vs[r] for r in rows[uu]] for uu in units]
        beta = [[bsc[d, rows[uu][d]] for d in range(2)] for uu in units]
        kb = [[kv[uu][d] * beta[uu][d] for d in range(2)] for uu in units]
        gk = [lax.dot_general(
                  jnp.concatenate([kb[uu][0], qv[uu][0], kb[uu][1], qv[uu][1]], axis=0).astype(BF16),
                  jnp.concatenate(kv[uu], axis=0).astype(BF16),
                  (((1,), (1,)), ((), ())), preferred_element_type=F32)
              for uu in units]
        yield
        gcb = yield from _dots_exact01([cum_lo, cum_up] * len(steps),
                                       [gsc[d, rows[uu][d]] for uu in units for d in range(2)])
        gcb = [gcb[2 * uu:2 * uu + 2] for uu in units]
        ms, avs = [], []
        for uu in units:
            col = jnp.where(left, gcb[uu][0], gcb[uu][1])
            row = jnp.transpose(jnp.concatenate(gcb[uu], axis=0))[0:c]
            decay = jnp.exp(jnp.where(incl, col - row, 0.0))
            kk = jnp.where(left, gk[uu][0:c], gk[uu][2 * c:3 * c])
            qk = jnp.where(left, gk[uu][c:2 * c], gk[uu][3 * c:4 * c])
            ms.append(jnp.where(strict, kk * decay, 0.0))
            avs.append(jnp.where(incl, qk * decay, 0.0))
        ys = yield from _unit_tri_inverse_minus_eye(ms, left, ri, cj)
        egc = [[jnp.exp(g) for g in gcb[uu]] for uu in units]
        rs = [[jnp.concatenate([kb[uu][d] * egc[uu][d], vv[uu][d] * beta[uu][d]], axis=1)
               for d in range(2)] for uu in units]
        corr = [jnp.dot(_block_diag2(ys[uu], left),
                        jnp.concatenate(rs[uu], axis=0).astype(BF16), preferred_element_type=F32)
                for uu in units]
        yield
        wus, aks, g_edges = [], [], []
        for uu in units:
            kgs, edges = [], []
            for d in range(2):
                edge = c - 1 if d == 0 else 0
                edges.append(gcb[uu][d][edge:edge + 1])
                kgs.append(kv[uu][d] * jnp.exp(edges[d] - gcb[uu][d]))
            g_edges.append(edges)
            wus.append([rs[uu][d] + corr[uu][d * c:(d + 1) * c] for d in range(2)])
            kgt = jnp.transpose(jnp.concatenate(kgs, axis=0))
            aks.append(jnp.concatenate(
                [jnp.where(left, avs[uu], 0.0), jnp.where(left_sq, kgt, 0.0),
                 jnp.where(left, 0.0, avs[uu]), jnp.where(left_sq, 0.0, kgt)], axis=0).astype(BF16))
        prods = [jnp.dot(aks[uu], jnp.concatenate(wus[uu], axis=0).astype(BF16),
                         preferred_element_type=F32) for uu in units]
        yield
        for uu in units:
            for d in range(2):
                x = prods[uu][d * (c + HEAD_DIM):(d + 1) * (c + HEAD_DIM)]
                a_w, a_u = x[0:c, 0:HEAD_DIM], x[0:c, HEAD_DIM:2 * HEAD_DIM]
                k_w, k_u = x[c:c + HEAD_DIM, 0:HEAD_DIM], x[c:c + HEAD_DIM, HEAD_DIM:2 * HEAD_DIM]
                idx = 2 * steps[uu] + d
                nq_s[idx] = jnp.concatenate([-k_w, qv[uu][d] * egc[uu][d] - a_w],
                                            axis=0).astype(BF16)
                b_s[idx] = k_u
                op_s[idx] = a_u
                gl_s[idx] = jnp.broadcast_to(jnp.exp(g_edges[uu][d]), (8, LANES))

    states = [jnp.zeros((HEAD_DIM, HEAD_DIM), F32)] * 2
    written = set()

    def scan(steps):
        for s in steps:
            idxs = (2 * s, 2 * s + 1)
            xs = [jnp.dot(nq_s[idx], st.astype(BF16), preferred_element_type=F32)
                  for idx, st in zip(idxs, states)]
            yield
            for n, idx, x in zip((s, rev_chunk(s)), idxs, xs):
                o = x[HEAD_DIM:HEAD_DIM + c] + op_s[idx]
                rows = slice(n * c, (n + 1) * c)
                oacc[rows] = oacc[rows] + o if n in written else o
                written.add(n)
            states[:] = [st * gl_s[idx][0:1] + x[0:HEAD_DIM] + b_s[idx]
                         for idx, st, x in zip(idxs, states, xs)]

    bounds = np.cumsum((0,) + SCAN_GROUPS)
    assert bounds[-1] == nsteps
    groups = [list(range(lo, hi)) for lo, hi in zip(bounds[:-1], bounds[1:])]
    _run_interleaved(prep(groups[0]))
    for gi in range(1, len(groups)):
        _run_interleaved(prep(groups[gi]), scan(groups[gi - 1]))
    _run_interleaved(scan(groups[-1]))

    o = oacc[...]
    on = o * lax.rsqrt(jnp.mean(o * o, axis=-1, keepdims=True) + EPS) * hn_ref[...]
    oc_ref[0] = on[0:tc].astype(oc_ref.dtype)
    ox_ref[0] = on[tc:t].astype(ox_ref.dtype)


def _gdn(qkv_c, qkv_x, gates_c, gates_x, head_norm):
    tc, tx = CTX_LEN, SEQ
    t = tc + tx
    nidx = 2 * (t // CHUNK)
    part = lambda rows, p: pl.BlockSpec((1, rows, HEAD_DIM), lambda b, h: (p * HEADS + h, b, 0))
    return pl.pallas_call(
        _gdn_kernel,
        grid=(BATCH, HEADS),
        in_specs=[part(tc, 0), part(tc, 1), part(tc, 2),
                  part(tx, 0), part(tx, 1), part(tx, 2),
                  pl.BlockSpec((tc, LANES), lambda b, h: (b, 0)),
                  pl.BlockSpec((tx, LANES), lambda b, h: (b, 0)),
                  pl.BlockSpec((1, HEAD_DIM), lambda b, h: (0, 0))],
        out_specs=[pl.BlockSpec((1, tc, HEAD_DIM), lambda b, h: (b, 0, h)),
                   pl.BlockSpec((1, tx, HEAD_DIM), lambda b, h: (b, 0, h))],
        out_shape=[jax.ShapeDtypeStruct((BATCH, tc, D_MODEL), BF16),
                   jax.ShapeDtypeStruct((BATCH, tx, D_MODEL), BF16)],
        scratch_shapes=[pltpu.VMEM((t, HEAD_DIM), F32), pltpu.VMEM((t, HEAD_DIM), F32),
                        pltpu.VMEM((t, HEAD_DIM), F32),
                        pltpu.VMEM((2, t, LANES), F32), pltpu.VMEM((2, t, LANES), F32),
                        pltpu.VMEM((nidx, HEAD_DIM + CHUNK, HEAD_DIM), BF16),
                        pltpu.VMEM((nidx, HEAD_DIM, HEAD_DIM), F32),
                        pltpu.VMEM((nidx, CHUNK, HEAD_DIM), F32),
                        pltpu.VMEM((nidx, 8, LANES), F32),
                        pltpu.VMEM((t, HEAD_DIM), F32)],
        compiler_params=_params(("parallel", "arbitrary")),
        name="gdn",
    )(qkv_c, qkv_c, qkv_c, qkv_x, qkv_x, qkv_x, gates_c, gates_x, head_norm)


def _even_out_kernel(x_ref, gt_ref, o_ref, gz_ref, yb_ref, w_ref, out_ref, a_scr):
    j = pl.program_id(1)

    @pl.when(j == 0)
    def _():
        d = o_ref.shape[1]
        a_scr[:, 0:d] = o_ref[...] * gz_ref[...]
        a_scr[:, d:2 * d] = yb_ref[...]

    y = jnp.dot(a_scr[...], w_ref[...], preferred_element_type=F32)
    out_ref[...] = x_ref[...] + gt_ref[0] * y


def _even_out(xin, mods, o, gz, yb, w_out, *, rows_per_batch, mod_row0):
    r, d = xin.shape
    tm = min(1024, rows_per_batch)
    tn = 512
    nj = d // tn
    tiles_per_batch = rows_per_batch // tm
    fixed = mod_row0 >= BATCH
    gt_idx = ((lambda i, j: (mod_row0, 0, 2 * nj + j)) if fixed
              else (lambda i, j: (i // tiles_per_batch, 0, 2 * nj + j)))
    return pl.pallas_call(
        _even_out_kernel,
        grid=(r // tm, nj),
        in_specs=[pl.BlockSpec((tm, tn), lambda i, j: (i, j)),
                  pl.BlockSpec((1, 1, tn), gt_idx),
                  pl.BlockSpec((tm, d), lambda i, j: (i, 0)),
                  pl.BlockSpec((tm, d), lambda i, j: (i, 0)),
                  pl.BlockSpec((tm, d), lambda i, j: (i, 0)),
                  pl.BlockSpec((2 * d, tn), lambda i, j: (0, j))],
        out_specs=pl.BlockSpec((tm, tn), lambda i, j: (i, j)),
        out_shape=jax.ShapeDtypeStruct((r, d), F32),
        scratch_shapes=[pltpu.VMEM((tm, 2 * d), BF16)],
        compiler_params=_params(("parallel", "arbitrary")),
        name="even_out",
    )(xin, mods, o, gz, yb, w_out)


def _odd_in_kernel(x_ref, nw_ref, sh_ref, sc_ref, w_ref, out_ref, h_scr, *, tiles_per_region):
    j = pl.program_id(1)

    @pl.when(j == 0)
    def _():
        h_scr[...] = _modulated_norm(x_ref[...], nw_ref[...], sc_ref[0], sh_ref[0]).astype(BF16)

    region = j // tiles_per_region

    def tile(act):
        for lo in range(0, w_ref.shape[1], SUB_N):
            p = jnp.dot(h_scr[...], w_ref[:, lo:lo + SUB_N], preferred_element_type=F32)
            out_ref[:, lo:lo + SUB_N] = act(p).astype(out_ref.dtype)

    @pl.when(region <= 1)
    def _():
        tile(_gelu_tanh)

    @pl.when((region == 2) | (region == 4))
    def _():
        tile(_silu)

    @pl.when(region == 3)
    def _():
        tile(lambda p: p)


def _odd_in(xin, mods, nw, w):
    r, d = xin.shape
    n = w.shape[1]
    tm, tn = 1024, 1024
    tiles_per_batch = SEQ // tm
    mod_idx = lambda part: (lambda i, j: (i // tiles_per_batch, 0, part))
    return pl.pallas_call(
        functools.partial(_odd_in_kernel, tiles_per_region=d // tn),
        grid=(r // tm, n // tn),
        in_specs=[pl.BlockSpec((tm, d), lambda i, j: (i, 0)),
                  pl.BlockSpec((1, d), lambda i, j: (0, 0)),
                  pl.BlockSpec((1, 1, d), mod_idx(0)),
                  pl.BlockSpec((1, 1, d), mod_idx(1)),
                  pl.BlockSpec((d, tn), lambda i, j: (0, j))],
        out_specs=pl.BlockSpec((tm, tn), lambda i, j: (i, j)),
        out_shape=jax.ShapeDtypeStruct((r, n), BF16),
        scratch_shapes=[pltpu.VMEM((tm, d), BF16)],
        compiler_params=_params(("parallel", "arbitrary")),
        name="odd_in",
    )(xin, nw, mods, mods, w)


def _odd_mix_kernel(x_ref, gt_ref, u_ref, v_ref, zc_ref, pd_ref, zd_ref, lnw_ref, lnb_ref, ws_ref,
                    bs_ref, band_ref, icnt_ref, pw_ref, ps_ref, wo_ref, fnw_ref, out_ref,
                    acc, ln_scr, *, nk_c):
    kk = pl.program_id(1)
    tm = x_ref.shape[0]
    tk = u_ref.shape[1]
    gpb = tk // HEAD_DIM

    @pl.when(kk == 0)
    def _():
        v = v_ref[...].astype(F32)
        mu = jnp.mean(v, axis=-1, keepdims=True)
        xc = v - mu
        var = jnp.mean(xc * xc, axis=-1, keepdims=True)
        ln = (xc * lax.rsqrt(var + EPS) * lnw_ref[...] + lnb_ref[...]).astype(BF16)
        for blk in range(nk_c):
            ln_scr[blk] = ln[:, blk * tk:(blk + 1) * tk]
        acc[...] = jnp.zeros_like(acc)

    @pl.when(kk < nk_c)
    def _():
        ln = ln_scr[kk]
        cols = []
        for gi in range(gpb):
            w_g = ws_ref[gi]
            b_g = bs_ref[gi]
            rows = []
            for cc in range(tm // TOK_CHUNK):
                blk = ln[cc * TOK_CHUNK:(cc + 1) * TOK_CHUNK, gi * HEAD_DIM:(gi + 1) * HEAD_DIM]
                rows.append(jnp.dot(w_g, blk, preferred_element_type=F32) + b_g)
            cols.append(jnp.concatenate(rows, axis=0))
        s = jnp.concatenate(cols, axis=1)
        yc = u_ref[...] * s * zc_ref[...]
        acc[...] += jnp.dot(yc.astype(BF16), wo_ref[...], preferred_element_type=F32)

    @pl.when(kk >= nk_c)
    def _():
        pd = pd_ref[...]
        wsum = jnp.dot(band_ref[0], pd, preferred_element_type=F32)
        icnt = icnt_ref[0]
        icnt = jnp.concatenate([icnt] * (tk // LANES), axis=1)
        diff = wsum * icnt - pd.astype(F32)
        y = jnp.dot(diff.astype(BF16), pw_ref[0], preferred_element_type=F32)
        yd = y * ps_ref[...] * zd_ref[...]
        acc[...] += jnp.dot(yd.astype(BF16), wo_ref[...], preferred_element_type=F32)

    @pl.when(kk == pl.num_programs(1) - 1)
    def _():
        xn = x_ref[...] + gt_ref[0] * acc[...]
        ms = jnp.mean(xn * xn, axis=-1, keepdims=True)
        out_ref[...] = xn * lax.rsqrt(ms + EPS) * fnw_ref[...]


def _pool_constants(tm):
    bands = np.zeros((POOL_GROUPS, tm, tm), np.float32)
    icnt = np.zeros((POOL_GROUPS, tm, LANES), np.float32)
    pos = np.arange(tm)
    row = pos // GRID_W
    col = pos % GRID_W
    for g, rad in enumerate(POOL_RADII):
        same = row[:, None] == row[None, :]
        bands[g] = (same & (np.abs(col[:, None] - col[None, :]) <= rad)).astype(np.float32)
        cnt = np.minimum(col + rad + 1, GRID_W) - np.maximum(col - rad, 0)
        icnt[g] = (1.0 / cnt.astype(np.float32))[:, None]
    return jnp.asarray(bands, BF16), jnp.asarray(icnt, F32)


def _odd_mix(xin, mods, p, ln_w, ln_b, w_s, b_s, pool_w, pool_scale, w_out, fnw):
    r, d = xin.shape
    tm, tk = 512, 512
    nk_c = d // tk
    nk = 2 * nk_c
    gpb = tk // HEAD_DIM
    tiles_per_batch = SEQ // tm
    bands, icnt = _pool_constants(tm)
    lo = lambda kk: jnp.minimum(kk, nk_c - 1)
    hi = lambda kk: jnp.maximum(kk - nk_c, 0)
    cblk = d // tk
    return pl.pallas_call(
        functools.partial(_odd_mix_kernel, nk_c=nk_c),
        grid=(r // tm, nk),
        in_specs=[pl.BlockSpec((tm, d), lambda i, kk: (i, 0)),
                  pl.BlockSpec((1, 1, d), lambda i, kk: (i // tiles_per_batch, 0, 2)),
                  pl.BlockSpec((tm, tk), lambda i, kk: (i, lo(kk))),
                  pl.BlockSpec((tm, d), lambda i, kk: (i, 1)),
                  pl.BlockSpec((tm, tk), lambda i, kk: (i, 2 * cblk + lo(kk))),
                  pl.BlockSpec((tm, tk), lambda i, kk: (i, 3 * cblk + hi(kk))),
                  pl.BlockSpec((tm, tk), lambda i, kk: (i, 4 * cblk + hi(kk))),
                  pl.BlockSpec((1, d), lambda i, kk: (0, 0)),
                  pl.BlockSpec((1, d), lambda i, kk: (0, 0)),
                  pl.BlockSpec((gpb, TOK_CHUNK, TOK_CHUNK), lambda i, kk: (lo(kk), 0, 0)),
                  pl.BlockSpec((gpb, TOK_CHUNK, LANES), lambda i, kk: (lo(kk), 0, 0)),
                  pl.BlockSpec((1, tm, tm), lambda i, kk: (hi(kk), 0, 0)),
                  pl.BlockSpec((1, tm, LANES), lambda i, kk: (hi(kk), 0, 0)),
                  pl.BlockSpec((1, POOL_GW, POOL_GW), lambda i, kk: (hi(kk), 0, 0)),
                  pl.BlockSpec((1, tk), lambda i, kk: (0, hi(kk))),
                  pl.BlockSpec((tk, d), lambda i, kk: (kk, 0)),
                  pl.BlockSpec((1, d), lambda i, kk: (0, 0))],
        out_specs=pl.BlockSpec((tm, d), lambda i, kk: (i, 0)),
        out_shape=jax.ShapeDtypeStruct((r, d), F32),
        scratch_shapes=[pltpu.VMEM((tm, d), F32), pltpu.VMEM((nk_c, tm, tk), BF16)],
        compiler_params=_params(("parallel", "arbitrary")),
        name="odd_mix",
    )(xin, mods, p, p, p, p, p, ln_w, ln_b, w_s, b_s, bands, icnt, pool_w, pool_scale, w_out, fnw)


def kernel(x, c, ctx, c_ctx, ada_w, ada_b, norm_w, e_w_in, e_conv_qkv, e_a_log, e_dt_bias, e_head_norm,
           e_conv_b, e_w_out, o_w_in, o_ln_w, o_ln_b, o_w_s, o_b_s, o_pool_w, o_pool_scale, o_w_out,
           final_norm_w):
    bsz, seq, d = x.shape
    rx = bsz * seq
    rc = bsz * ctx.shape[1]
    x2 = x.reshape(rx, d)
    ctx2 = ctx.reshape(rc, d)

    cc = jnp.concatenate([c, c_ctx[None, :], jnp.zeros((8 - bsz - 1, d), F32)], axis=0)
    mods = _adaln(cc, ada_w, ada_b)
    ctx_row = bsz

    w_in = e_w_in[0]
    gate_lo = QKV_W + D_MODEL
    gate_hi = gate_lo + 4 * HEADS
    wqkv = w_in[:, :QKV_W].astype(BF16)
    wg = jnp.pad(w_in[:, gate_lo:gate_hi], ((0, 0), (0, LANES - 4 * HEADS))).astype(BF16)
    w5 = jnp.concatenate([w_in[:, QKV_W:gate_lo], w_in[:, gate_hi:]], axis=1).astype(BF16)
    m0 = mods[0].reshape(8, 1, 3 * d)
    nw0 = norm_w[0].reshape(1, d)

    gate_par = jnp.pad(jnp.stack([e_a_log[0].reshape(-1), e_dt_bias[0].reshape(-1)]),
                       ((0, 0), (0, LANES - 2 * HEADS)))
    qkv_x, gates_x = _qkv_proj(x2, m0, nw0, wqkv, wg, e_conv_qkv[0], gate_par,
                               rows_per_batch=seq, mod_row0=0, group=GRID_W)
    qkv_c, gates_c = _qkv_proj(ctx2, m0, nw0, wqkv, wg, e_conv_qkv[0], gate_par,
                               rows_per_batch=ctx.shape[1], mod_row0=ctx_row, group=ctx.shape[1])
    gz_x, yb_x = _mixb_proj(x2, m0, nw0, w5, e_conv_b[0],
                            rows_per_batch=seq, mod_row0=0, group=GRID_W)
    gz_c, yb_c = _mixb_proj(ctx2, m0, nw0, w5, e_conv_b[0],
                            rows_per_batch=ctx.shape[1], mod_row0=ctx_row, group=ctx.shape[1])
    o_c, o_x = _gdn(qkv_c, qkv_x, gates_c, gates_x,
                    e_head_norm[0].reshape(1, HEAD_DIM))
    w_out0 = e_w_out[0].astype(BF16)
    x2 = _even_out(x2, m0, o_x.reshape(rx, d), gz_x, yb_x, w_out0, rows_per_batch=seq, mod_row0=0)
    ctx2 = _even_out(ctx2, m0, o_c.reshape(rc, d), gz_c, yb_c, w_out0,
                     rows_per_batch=ctx.shape[1], mod_row0=ctx_row)

    m1 = mods[1].reshape(8, 1, 3 * d)
    p = _odd_in(x2, m1, norm_w[1].reshape(1, d), o_w_in[0].astype(BF16))
    b_s = jnp.broadcast_to(o_b_s[0][:, :, None], (o_b_s.shape[1], TOK_CHUNK, LANES))
    out = _odd_mix(x2, m1, p, o_ln_w[0].reshape(1, d), o_ln_b[0].reshape(1, d),
                   o_w_s[0].astype(BF16), b_s, o_pool_w[0].astype(BF16),
                   o_pool_scale[0].reshape(1, d), o_w_out[0].astype(BF16),
                   final_norm_w.reshape(1, d))
    return out.reshape(bsz, seq, d)
```

```python
import functools
import math

import numpy as np
import jax
import jax.numpy as jnp
from jax import lax
from jax.experimental import pallas as pl
from jax.experimental.pallas import tpu as pltpu

F32 = jnp.float32
BF16 = jnp.bfloat16

D_MODEL = 2048
BATCH = 4
SEQ = 2048
CTX_LEN = 256
GRID_W = 64
EPS = 1e-6
HEADS = 16
HEAD_DIM = 128
QKV_W = 3 * D_MODEL
GATE_COL0 = QKV_W + D_MODEL
CHUNK = 64
SCAN_GROUPS = (14, 12, 10)
TOK_CHUNK = 128
POOL_GROUPS = 4
POOL_GW = D_MODEL // POOL_GROUPS
POOL_RADII = (1, 2, 4, 8)

LANES = 128
VMEM_LIMIT = 56 * 1024 * 1024
SUB_N = 256


def _params(sem):
    return pltpu.CompilerParams(dimension_semantics=sem, vmem_limit_bytes=VMEM_LIMIT)


def _sigmoid(x):
    return 1.0 / (1.0 + jnp.exp(-x))


def _silu(x):
    return x * _sigmoid(x)


def _gelu_tanh(x):
    cdf = 0.5 * (1.0 + jnp.tanh(math.sqrt(2.0 / math.pi) * (x + 0.044715 * (x * x * x))))
    return x * cdf


def _modulated_norm(x, nw, sc, sh):
    ms = jnp.mean(x * x, axis=-1, keepdims=True)
    y = x * lax.rsqrt(ms + EPS) * nw
    return y * (1.0 + sc) + sh


def _row_conv3(p, cw, group):
    tm = p.shape[0]
    row = lax.broadcasted_iota(jnp.int32, (tm, 1), 0) % group
    prev = jnp.where(row == 0, 0.0, pltpu.roll(p, 1, 0))
    nxt = jnp.where(row == group - 1, 0.0, pltpu.roll(p, tm - 1, 0))
    return prev * cw[0:1] + p * cw[1:2] + nxt * cw[2:3]


def _adaln_kernel(c_ref, w_ref, b_ref, o_ref):
    a = _silu(c_ref[...]).astype(BF16)
    o_ref[0] = jnp.dot(a, w_ref[0].astype(BF16), preferred_element_type=F32) + b_ref[0]


def _adaln(cc, ada_w, ada_b):
    depth, d, n = ada_w.shape
    tn = 1024
    return pl.pallas_call(
        _adaln_kernel,
        grid=(depth, n // tn),
        in_specs=[pl.BlockSpec((8, d), lambda l, j: (0, 0)),
                  pl.BlockSpec((1, d, tn), lambda l, j: (l, 0, j)),
                  pl.BlockSpec((1, 1, tn), lambda l, j: (l, 0, j))],
        out_specs=pl.BlockSpec((1, 8, tn), lambda l, j: (l, 0, j)),
        out_shape=jax.ShapeDtypeStruct((depth, 8, n), F32),
        compiler_params=_params(("parallel", "parallel")),
        name="adaln",
    )(cc, ada_w, ada_b.reshape(depth, 1, n))


def _qkv_kernel(x_ref, nw_ref, sh_ref, sc_ref, w_ref, wg_ref, cw_ref, gp_ref, qkv_ref, g_ref, h_scr,
                *, group, n_tiles):
    j = pl.program_id(1)

    @pl.when(j == 0)
    def _():
        h_scr[...] = _modulated_norm(x_ref[...], nw_ref[...], sc_ref[0], sh_ref[0]).astype(BF16)

    @pl.when(j < n_tiles)
    def _():
        p = jnp.dot(h_scr[...], w_ref[...], preferred_element_type=F32)
        y = _silu(_row_conv3(p, cw_ref[...], group))
        for hh in range(y.shape[1] // HEAD_DIM):
            qkv_ref[hh] = y[:, hh * HEAD_DIM:(hh + 1) * HEAD_DIM]

    @pl.when(j == n_tiles)
    def _():
        raw = jnp.dot(h_scr[...], wg_ref[...], preferred_element_type=F32)
        z = raw + gp_ref[1:2]
        softplus = jnp.maximum(z, 0.0) + jnp.log1p(jnp.exp(-jnp.abs(z)))
        lane = lax.broadcasted_iota(jnp.int32, (1, LANES), 1)
        g_ref[...] = jnp.where(lane < 2 * HEADS, -jnp.exp(gp_ref[0:1]) * softplus, _sigmoid(raw))


def _qkv_proj(xin, mods, nw, w_in, cw, gate_par, *, rows_per_batch, mod_row0, group):
    r, d = xin.shape
    tm, tn = 1024, 768
    n_tiles = QKV_W // tn
    hpt = tn // HEAD_DIM
    tiles_per_batch = max(rows_per_batch // tm, 1)

    def mod_idx(part):
        if rows_per_batch >= tm:
            return lambda i, j: (mod_row0 + i // tiles_per_batch, 0, part)
        return lambda i, j: (mod_row0, 0, part)

    last = n_tiles - 1
    return pl.pallas_call(
        functools.partial(_qkv_kernel, group=group, n_tiles=n_tiles),
        grid=(r // tm, n_tiles + 1),
        in_specs=[pl.BlockSpec((tm, d), lambda i, j: (i, 0)),
                  pl.BlockSpec((1, d), lambda i, j: (0, 0)),
                  pl.BlockSpec((1, 1, d), mod_idx(0)),
                  pl.BlockSpec((1, 1, d), mod_idx(1)),
                  pl.BlockSpec((d, tn), lambda i, j: (0, jnp.minimum(j, last))),
                  pl.BlockSpec((d, LANES), lambda i, j: (0, GATE_COL0 // LANES)),
                  pl.BlockSpec((3, tn), lambda i, j: (0, jnp.minimum(j, last))),
                  pl.BlockSpec((2, LANES), lambda i, j: (0, 0))],
        out_specs=[pl.BlockSpec((hpt, tm, HEAD_DIM), lambda i, j: (jnp.minimum(j, last), i, 0)),
                   pl.BlockSpec((tm, LANES), lambda i, j: (i, 0))],
        out_shape=[jax.ShapeDtypeStruct((3 * HEADS, r, HEAD_DIM), F32),
                   jax.ShapeDtypeStruct((r, LANES), F32)],
        scratch_shapes=[pltpu.VMEM((tm, d), BF16)],
        compiler_params=_params(("parallel", "arbitrary")),
        name="qkv_proj",
    )(xin, nw, mods, mods, w_in, w_in, cw, gate_par)


def _mixb_kernel(x_ref, nw_ref, sh_ref, sc_ref, wz_ref, wb_ref, wc_ref, wh_ref, wzb_ref, cw_ref,
                 gz_ref, yb_ref, h_scr, *, group):
    j = pl.program_id(1)

    @pl.when(j == 0)
    def _():
        h_scr[...] = _modulated_norm(x_ref[...], nw_ref[...], sc_ref[0], sh_ref[0]).astype(BF16)

    h = h_scr[...]
    dot = lambda w: jnp.dot(h, w[...], preferred_element_type=F32)
    gz_ref[...] = _silu(dot(wz_ref)).astype(gz_ref.dtype)
    conv = _row_conv3(dot(wc_ref) * dot(wh_ref), cw_ref[...], group)
    yb_ref[...] = (dot(wb_ref) * conv * _silu(dot(wzb_ref))).astype(yb_ref.dtype)


def _mixb_proj(xin, mods, nw, w_in, w_bchz, cw, *, rows_per_batch, mod_row0, group):
    r, d = xin.shape
    tm, tn = 1024, 256
    nj = d // tn
    za_blk0 = QKV_W // tn
    tiles_per_batch = max(rows_per_batch // tm, 1)

    def mod_idx(part):
        if rows_per_batch >= tm:
            return lambda i, j: (mod_row0 + i // tiles_per_batch, 0, part)
        return lambda i, j: (mod_row0, 0, part)

    wspec = lambda reg: pl.BlockSpec((d, tn), lambda i, j: (0, reg * nj + j))
    return pl.pallas_call(
        functools.partial(_mixb_kernel, group=group),
        grid=(r // tm, nj),
        in_specs=[pl.BlockSpec((tm, d), lambda i, j: (i, 0)),
                  pl.BlockSpec((1, d), lambda i, j: (0, 0)),
                  pl.BlockSpec((1, 1, d), mod_idx(0)),
                  pl.BlockSpec((1, 1, d), mod_idx(1)),
                  pl.BlockSpec((d, tn), lambda i, j: (0, za_blk0 + j)),
                  wspec(0), wspec(1), wspec(2), wspec(3),
                  pl.BlockSpec((3, tn), lambda i, j: (0, j))],
        out_specs=[pl.BlockSpec((tm, tn), lambda i, j: (i, j)),
                   pl.BlockSpec((tm, tn), lambda i, j: (i, j))],
        out_shape=[jax.ShapeDtypeStruct((r, d), BF16), jax.ShapeDtypeStruct((r, d), BF16)],
        scratch_shapes=[pltpu.VMEM((tm, d), BF16)],
        compiler_params=_params(("parallel", "arbitrary")),
        name="mixb_proj",
    )(xin, nw, mods, mods, w_in, w_bchz, w_bchz, w_bchz, w_bchz, cw)


def _split3_bf16(x):
    x1 = x.astype(BF16)
    r1 = x - x1.astype(F32)
    x2 = r1.astype(BF16)
    x3 = (r1 - x2.astype(F32)).astype(BF16)
    return x1, x2, x3


def _dots_exact01(m01s, xs):
    pieces = [_split3_bf16(x) for x in xs]
    prods = [[jnp.dot(m01, t, preferred_element_type=F32) for t in p] for m01, p in zip(m01s, pieces)]
    yield
    return [(p[0] + p[1]) + p[2] for p in prods]


def _block_diag2(x, left):
    return jnp.concatenate([jnp.where(left, x, 0.0), jnp.where(left, 0.0, x)], axis=0).astype(BF16)


def _unit_tri_inverse_minus_eye(ms, left, ri, cj):
    n = ms[0].shape[0]
    ys = None
    s = 1
    while s < n:
        shift = s.bit_length() - 1
        br = lax.shift_right_logical(ri, shift)
        bc = lax.shift_right_logical(cj, shift)
        mask_lo = ((br & 1) == 1) & (bc == br - 1)
        mask_up = ((bc & 1) == 1) & (br == bc - 1)
        mask = (left & mask_lo) | (jnp.logical_not(left) & mask_up)
        cms = [jnp.where(mask, m, 0.0) for m in ms]
        if ys is None:
            ys = [-cm for cm in cms]
        else:
            yc = [jnp.dot(y.astype(BF16), _block_diag2(cm, left), preferred_element_type=F32)
                  for cm, y in zip(cms, ys)]
            yield
            ps = [cm + t for cm, t in zip(cms, yc)]
            py = [jnp.dot(p.astype(BF16), _block_diag2(y, left), preferred_element_type=F32)
                  for y, p in zip(ys, ps)]
            yield
            ys = [y - p - t for y, p, t in zip(ys, ps, py)]
        s *= 2
    return ys


def _run_interleaved(*gens):
    active = list(gens)
    while active:
        for g in list(active):
            try:
                next(g)
            except StopIteration:
                active.remove(g)


def _gdn_kernel(qc_ref, kc_ref, vc_ref, qx_ref, kx_ref, vx_ref, gc_ref, gx_ref,
                hn_ref, oc_ref, ox_ref,
                qs, ks, vs, gsc, bsc, nq_s, b_s, op_s, gl_s, oacc):
    h = pl.program_id(1)
    tc = qc_ref.shape[1]
    tx = qx_ref.shape[1]
    t = tc + tx
    nsteps = t // CHUNK
    nc_ctx = tc // CHUNK
    c = CHUNK

    def l2n(a):
        return a * lax.rsqrt(jnp.sum(a * a, axis=-1, keepdims=True) + EPS)

    qscale = HEAD_DIM ** -0.5
    qs[0:tc] = l2n(qc_ref[0]) * qscale
    qs[tc:t] = l2n(qx_ref[0]) * qscale
    ks[0:tc] = l2n(kc_ref[0])
    ks[tc:t] = l2n(kx_ref[0])
    vs[0:tc] = vc_ref[0]
    vs[tc:t] = vx_ref[0]

    lane = lax.broadcasted_iota(jnp.int32, (1, LANES), 1)
    for d in range(2):
        for ref, lo, hi in ((gc_ref, 0, tc), (gx_ref, tc, t)):
            gates = ref[...]
            g = jnp.sum(jnp.where(lane == d * HEADS + h, gates, 0.0), axis=-1, keepdims=True)
            b = jnp.sum(jnp.where(lane == 2 * HEADS + d * HEADS + h, gates, 0.0), axis=-1,
                        keepdims=True)
            gsc[d, lo:hi] = jnp.broadcast_to(g, (hi - lo, LANES))
            bsc[d, lo:hi] = jnp.broadcast_to(b, (hi - lo, LANES))

    def rev_chunk(s):
        return nc_ctx - 1 - s if s < nc_ctx else nsteps + nc_ctx - 1 - s

    ri1 = lax.broadcasted_iota(jnp.int32, (c, c), 0)
    ci1 = lax.broadcasted_iota(jnp.int32, (c, c), 1)
    cum_lo = jnp.where(ci1 <= ri1, 1.0, 0.0).astype(BF16)
    cum_up = jnp.where(ci1 >= ri1, 1.0, 0.0).astype(BF16)
    ri = lax.broadcasted_iota(jnp.int32, (c, 2 * c), 0)
    lane2 = lax.broadcasted_iota(jnp.int32, (c, 2 * c), 1)
    left = lane2 < c
    cj = lane2 & (c - 1)
    strict = (left & (cj < ri)) | (jnp.logical_not(left) & (cj > ri))
    incl = (left & (cj <= ri)) | (jnp.logical_not(left) & (cj >= ri))
    left_sq = lax.broadcasted_iota(jnp.int32, (2 * c, 2 * c), 1) < c

    def prep(steps):
        units = range(len(steps))
        rows = [[slice(n * c, (n + 1) * c) for n in (s, rev_chunk(s))] for s in steps]
        qv = [[qs[r] for r in rows[uu]] for uu in units]
        kv = [[ks[r] for r in rows[uu]] for uu in units]
        vv = [[vs[r] for r in rows[uu]] for uu in units]
        beta = [[bsc[d, rows[uu][d]] for d in range(2)] for uu in units]
        kb = [[kv[uu][d] * beta[uu][d] for d in range(2)] for uu in units]
        gk = [lax.dot_general(
                  jnp.concatenate([kb[uu][0], qv[uu][0], kb[uu][1], qv[uu][1]], axis=0).astype(BF16),
                  jnp.concatenate(kv[uu], axis=0).astype(BF16),
                  (((1,), (1,)), ((), ())), preferred_element_type=F32)
              for uu in units]
        yield
        gcb = yield from _dots_exact01([cum_lo, cum_up] * len(steps),
                                       [gsc[d, rows[uu][d]] for uu in units for d in range(2)])
        gcb = [gcb[2 * uu:2 * uu + 2] for uu in units]
        ms, avs = [], []
        for uu in units:
            col = jnp.where(left, gcb[uu][0], gcb[uu][1])
            row = jnp.transpose(jnp.concatenate(gcb[uu], axis=0))[0:c]
            decay = jnp.exp(jnp.where(incl, col - row, 0.0))
            kk = jnp.where(left, gk[uu][0:c], gk[uu][2 * c:3 * c])
            qk = jnp.where(left, gk[uu][c:2 * c], gk[uu][3 * c:4 * c])
            ms.append(jnp.where(strict, kk * decay, 0.0))
            avs.append(jnp.where(incl, qk * decay, 0.0))
        ys = yield from _unit_tri_inverse_minus_eye(ms, left, ri, cj)
        egc = [[jnp.exp(g) for g in gcb[uu]] for uu in units]
        rs = [[jnp.concatenate([kb[uu][d] * egc[uu][d], vv[uu][d] * beta[uu][d]], axis=1)
               for d in range(2)] for uu in units]
        corr = [jnp.dot(_block_diag2(ys[uu], left),
                        jnp.concatenate(rs[uu], axis=0).astype(BF16), preferred_element_type=F32)
                for uu in units]
        yield
        wus, aks, g_edges = [], [], []
        for uu in units:
            kgs, edges = [], []
            for d in range(2):
                edge = c - 1 if d == 0 else 0
                edges.append(gcb[uu][d][edge:edge + 1])
                kgs.append(kv[uu][d] * jnp.exp(edges[d] - gcb[uu][d]))
            g_edges.append(edges)
            wus.append([rs[uu][d] + corr[uu][d * c:(d + 1) * c] for d in range(2)])
            kgt = jnp.transpose(jnp.concatenate(kgs, axis=0))
            aks.append(jnp.concatenate(
                [jnp.where(left, avs[uu], 0.0), jnp.where(left_sq, kgt, 0.0),
                 jnp.where(left, 0.0, avs[uu]), jnp.where(left_sq, 0.0, kgt)], axis=0).astype(BF16))
        prods = [jnp.dot(aks[uu], jnp.concatenate(wus[uu], axis=0).astype(BF16),
                         preferred_element_type=F32) for uu in units]
        yield
        for uu in units:
            for d in range(2):
                x = prods[uu][d * (c + HEAD_DIM):(d + 1) * (c + HEAD_DIM)]
                a_w, a_u = x[0:c, 0:HEAD_DIM], x[0:c, HEAD_DIM:2 * HEAD_DIM]
                k_w, k_u = x[c:c + HEAD_DIM, 0:HEAD_DIM], x[c:c + HEAD_DIM, HEAD_DIM:2 * HEAD_DIM]
                idx = 2 * steps[uu] + d
                nq_s[idx] = jnp.concatenate([-k_w, qv[uu][d] * egc[uu][d] - a_w],
                                            axis=0).astype(BF16)
                b_s[idx] = k_u
                op_s[idx] = a_u
                gl_s[idx] = jnp.broadcast_to(jnp.exp(g_edges[uu][d]), (8, LANES))

    states = [jnp.zeros((HEAD_DIM, HEAD_DIM), F32)] * 2
    written = set()

    def scan(steps):
        for s in steps:
            idxs = (2 * s, 2 * s + 1)
            xs = [jnp.dot(nq_s[idx], st.astype(BF16), preferred_element_type=F32)
                  for idx, st in zip(idxs, states)]
            yield
            for n, idx, x in zip((s, rev_chunk(s)), idxs, xs):
                o = x[HEAD_DIM:HEAD_DIM + c] + op_s[idx]
                rows = slice(n * c, (n + 1) * c)
                oacc[rows] = oacc[rows] + o if n in written else o
                written.add(n)
            states[:] = [st * gl_s[idx][0:1] + x[0:HEAD_DIM] + b_s[idx]
                         for idx, st, x in zip(idxs, states, xs)]

    bounds = np.cumsum((0,) + SCAN_GROUPS)
    assert bounds[-1] == nsteps
    groups = [list(range(lo, hi)) for lo, hi in zip(bounds[:-1], bounds[1:])]
    _run_interleaved(prep(groups[0]))
    for gi in range(1, len(groups)):
        _run_interleaved(prep(groups[gi]), scan(groups[gi - 1]))
    _run_interleaved(scan(groups[-1]))

    o = oacc[...]
    on = o * lax.rsqrt(jnp.mean(o * o, axis=-1, keepdims=True) + EPS) * hn_ref[...]
    oc_ref[0] = on[0:tc].astype(oc_ref.dtype)
    ox_ref[0] = on[tc:t].astype(ox_ref.dtype)


def _gdn(qkv_c, qkv_x, gates_c, gates_x, head_norm):
    tc, tx = CTX_LEN, SEQ
    t = tc + tx
    nidx = 2 * (t // CHUNK)
    part = lambda rows, p: pl.BlockSpec((1, rows, HEAD_DIM), lambda b, h: (p * HEADS + h, b, 0))
    return pl.pallas_call(
        _gdn_kernel,
        grid=(BATCH, HEADS),
        in_specs=[part(tc, 0), part(tc, 1), part(tc, 2),
                  part(tx, 0), part(tx, 1), part(tx, 2),
                  pl.BlockSpec((tc, LANES), lambda b, h: (b, 0)),
                  pl.BlockSpec((tx, LANES), lambda b, h: (b, 0)),
                  pl.BlockSpec((1, HEAD_DIM), lambda b, h: (0, 0))],
        out_specs=[pl.BlockSpec((1, tc, HEAD_DIM), lambda b, h: (b, 0, h)),
                   pl.BlockSpec((1, tx, HEAD_DIM), lambda b, h: (b, 0, h))],
        out_shape=[jax.ShapeDtypeStruct((BATCH, tc, D_MODEL), BF16),
                   jax.ShapeDtypeStruct((BATCH, tx, D_MODEL), BF16)],
        scratch_shapes=[pltpu.VMEM((t, HEAD_DIM), F32), pltpu.VMEM((t, HEAD_DIM), F32),
                        pltpu.VMEM((t, HEAD_DIM), F32),
                        pltpu.VMEM((2, t, LANES), F32), pltpu.VMEM((2, t, LANES), F32),
                        pltpu.VMEM((nidx, HEAD_DIM + CHUNK, HEAD_DIM), BF16),
                        pltpu.VMEM((nidx, HEAD_DIM, HEAD_DIM), F32),
                        pltpu.VMEM((nidx, CHUNK, HEAD_DIM), F32),
                        pltpu.VMEM((nidx, 8, LANES), F32),
                        pltpu.VMEM((t, HEAD_DIM), F32)],
        compiler_params=_params(("parallel", "arbitrary")),
        name="gdn",
    )(qkv_c, qkv_c, qkv_c, qkv_x, qkv_x, qkv_x, gates_c, gates_x, head_norm)


def _even_out_kernel(x_ref, gt_ref, o_ref, gz_ref, yb_ref, w_ref, out_ref, a_scr):
    j = pl.program_id(1)

    @pl.when(j == 0)
    def _():
        d = o_ref.shape[1]
        a_scr[:, 0:d] = o_ref[...] * gz_ref[...]
        a_scr[:, d:2 * d] = yb_ref[...]

    y = jnp.dot(a_scr[...], w_ref[...], preferred_element_type=F32)
    out_ref[...] = x_ref[...] + gt_ref[0] * y


def _even_out(xin, mods, o, gz, yb, w_out, *, rows_per_batch, mod_row0):
    r, d = xin.shape
    tm = min(1024, rows_per_batch)
    tn = 512
    nj = d // tn
    tiles_per_batch = rows_per_batch // tm
    fixed = mod_row0 >= BATCH
    gt_idx = ((lambda i, j: (mod_row0, 0, 2 * nj + j)) if fixed
              else (lambda i, j: (i // tiles_per_batch, 0, 2 * nj + j)))
    return pl.pallas_call(
        _even_out_kernel,
        grid=(r // tm, nj),
        in_specs=[pl.BlockSpec((tm, tn), lambda i, j: (i, j)),
                  pl.BlockSpec((1, 1, tn), gt_idx),
                  pl.BlockSpec((tm, d), lambda i, j: (i, 0)),
                  pl.BlockSpec((tm, d), lambda i, j: (i, 0)),
                  pl.BlockSpec((tm, d), lambda i, j: (i, 0)),
                  pl.BlockSpec((2 * d, tn), lambda i, j: (0, j))],
        out_specs=pl.BlockSpec((tm, tn), lambda i, j: (i, j)),
        out_shape=jax.ShapeDtypeStruct((r, d), F32),
        scratch_shapes=[pltpu.VMEM((tm, 2 * d), BF16)],
        compiler_params=_params(("parallel", "arbitrary")),
        name="even_out",
    )(xin, mods, o, gz, yb, w_out)


def _odd_in_kernel(x_ref, nw_ref, sh_ref, sc_ref, w_ref, out_ref, h_scr, *, tiles_per_region):
    j = pl.program_id(1)

    @pl.when(j == 0)
    def _():
        h_scr[...] = _modulated_norm(x_ref[...], nw_ref[...], sc_ref[0], sh_ref[0]).astype(BF16)

    region = j // tiles_per_region

    def tile(act):
        for lo in range(0, w_ref.shape[1], SUB_N):
            p = jnp.dot(h_scr[...], w_ref[:, lo:lo + SUB_N], preferred_element_type=F32)
            out_ref[:, lo:lo + SUB_N] = act(p).astype(out_ref.dtype)

    @pl.when(region <= 1)
    def _():
        tile(_gelu_tanh)

    @pl.when((region == 2) | (region == 4))
    def _():
        tile(_silu)

    @pl.when(region == 3)
    def _():
        tile(lambda p: p)


def _odd_in(xin, mods, nw, w):
    r, d = xin.shape
    n = w.shape[1]
    tm, tn = 1024, 1024
    tiles_per_batch = SEQ // tm
    mod_idx = lambda part: (lambda i, j: (i // tiles_per_batch, 0, part))
    return pl.pallas_call(
        functools.partial(_odd_in_kernel, tiles_per_region=d // tn),
        grid=(r // tm, n // tn),
        in_specs=[pl.BlockSpec((tm, d), lambda i, j: (i, 0)),
                  pl.BlockSpec((1, d), lambda i, j: (0, 0)),
                  pl.BlockSpec((1, 1, d), mod_idx(0)),
                  pl.BlockSpec((1, 1, d), mod_idx(1)),
                  pl.BlockSpec((d, tn), lambda i, j: (0, j))],
        out_specs=pl.BlockSpec((tm, tn), lambda i, j: (i, j)),
        out_shape=jax.ShapeDtypeStruct((r, n), BF16),
        scratch_shapes=[pltpu.VMEM((tm, d), BF16)],
        compiler_params=_params(("parallel", "arbitrary")),
        name="odd_in",
    )(xin, nw, mods, mods, w)


def _odd_mix_kernel(x_ref, gt_ref, u_ref, v_ref, zc_ref, pd_ref, zd_ref, lnw_ref, lnb_ref, ws_ref,
                    bs_ref, band_ref, icnt_ref, pw_ref, ps_ref, wo_ref, fnw_ref, out_ref,
                    acc, ln_scr, *, nk_c):
    kk = pl.program_id(1)
    tm = x_ref.shape[0]
    tk = u_ref.shape[1]
    gpb = tk // HEAD_DIM

    @pl.when(kk == 0)
    def _():
        v = v_ref[...].astype(F32)
        mu = jnp.mean(v, axis=-1, keepdims=True)
        xc = v - mu
        var = jnp.mean(xc * xc, axis=-1, keepdims=True)
        ln = (xc * lax.rsqrt(var + EPS) * lnw_ref[...] + lnb_ref[...]).astype(BF16)
        for blk in range(nk_c):
            ln_scr[blk] = ln[:, blk * tk:(blk + 1) * tk]
        acc[...] = jnp.zeros_like(acc)

    @pl.when(kk < nk_c)
    def _():
        ln = ln_scr[kk]
        cols = []
        for gi in range(gpb):
            w_g = ws_ref[gi]
            b_g = bs_ref[gi]
            rows = []
            for cc in range(tm // TOK_CHUNK):
                blk = ln[cc * TOK_CHUNK:(cc + 1) * TOK_CHUNK, gi * HEAD_DIM:(gi + 1) * HEAD_DIM]
                rows.append(jnp.dot(w_g, blk, preferred_element_type=F32) + b_g)
            cols.append(jnp.concatenate(rows, axis=0))
        s = jnp.concatenate(cols, axis=1)
        yc = u_ref[...] * s * zc_ref[...]
        acc[...] += jnp.dot(yc.astype(BF16), wo_ref[...], preferred_element_type=F32)

    @pl.when(kk >= nk_c)
    def _():
        pd = pd_ref[...]
        wsum = jnp.dot(band_ref[0], pd, preferred_element_type=F32)
        icnt = icnt_ref[0]
        icnt = jnp.concatenate([icnt] * (tk // LANES), axis=1)
        diff = wsum * icnt - pd.astype(F32)
        y = jnp.dot(diff.astype(BF16), pw_ref[0], preferred_element_type=F32)
        yd = y * ps_ref[...] * zd_ref[...]
        acc[...] += jnp.dot(yd.astype(BF16), wo_ref[...], preferred_element_type=F32)

    @pl.when(kk == pl.num_programs(1) - 1)
    def _():
        xn = x_ref[...] + gt_ref[0] * acc[...]
        ms = jnp.mean(xn * xn, axis=-1, keepdims=True)
        out_ref[...] = xn * lax.rsqrt(ms + EPS) * fnw_ref[...]


def _pool_constants(tm):
    bands = np.zeros((POOL_GROUPS, tm, tm), np.float32)
    icnt = np.zeros((POOL_GROUPS, tm, LANES), np.float32)
    pos = np.arange(tm)
    row = pos // GRID_W
    col = pos % GRID_W
    for g, rad in enumerate(POOL_RADII):
        same = row[:, None] == row[None, :]
        bands[g] = (same & (np.abs(col[:, None] - col[None, :]) <= rad)).astype(np.float32)
        cnt = np.minimum(col + rad + 1, GRID_W) - np.maximum(col - rad, 0)
        icnt[g] = (1.0 / cnt.astype(np.float32))[:, None]
    return jnp.asarray(bands, BF16), jnp.asarray(icnt, F32)


def _odd_mix(xin, mods, p, ln_w, ln_b, w_s, b_s, pool_w, pool_scale, w_out, fnw):
    r, d = xin.shape
    tm, tk = 512, 512
    nk_c = d // tk
    nk = 2 * nk_c
    gpb = tk // HEAD_DIM
    tiles_per_batch = SEQ // tm
    bands, icnt = _pool_constants(tm)
    lo = lambda kk: jnp.minimum(kk, nk_c - 1)
    hi = lambda kk: jnp.maximum(kk - nk_c, 0)
    cblk = d // tk
    return pl.pallas_call(
        functools.partial(_odd_mix_kernel, nk_c=nk_c),
        grid=(r // tm, nk),
        in_specs=[pl.BlockSpec((tm, d), lambda i, kk: (i, 0)),
                  pl.BlockSpec((1, 1, d), lambda i, kk: (i // tiles_per_batch, 0, 2)),
                  pl.BlockSpec((tm, tk), lambda i, kk: (i, lo(kk))),
                  pl.BlockSpec((tm, d), lambda i, kk: (i, 1)),
                  pl.BlockSpec((tm, tk), lambda i, kk: (i, 2 * cblk + lo(kk))),
                  pl.BlockSpec((tm, tk), lambda i, kk: (i, 3 * cblk + hi(kk))),
                  pl.BlockSpec((tm, tk), lambda i, kk: (i, 4 * cblk + hi(kk))),
                  pl.BlockSpec((1, d), lambda i, kk: (0, 0)),
                  pl.BlockSpec((1, d), lambda i, kk: (0, 0)),
                  pl.BlockSpec((gpb, TOK_CHUNK, TOK_CHUNK), lambda i, kk: (lo(kk), 0, 0)),
                  pl.BlockSpec((gpb, TOK_CHUNK, LANES), lambda i, kk: (lo(kk), 0, 0)),
                  pl.BlockSpec((1, tm, tm), lambda i, kk: (hi(kk), 0, 0)),
                  pl.BlockSpec((1, tm, LANES), lambda i, kk: (hi(kk), 0, 0)),
                  pl.BlockSpec((1, POOL_GW, POOL_GW), lambda i, kk: (hi(kk), 0, 0)),
                  pl.BlockSpec((1, tk), lambda i, kk: (0, hi(kk))),
                  pl.BlockSpec((tk, d), lambda i, kk: (kk, 0)),
                  pl.BlockSpec((1, d), lambda i, kk: (0, 0))],
        out_specs=pl.BlockSpec((tm, d), lambda i, kk: (i, 0)),
        out_shape=jax.ShapeDtypeStruct((r, d), F32),
        scratch_shapes=[pltpu.VMEM((tm, d), F32), pltpu.VMEM((nk_c, tm, tk), BF16)],
        compiler_params=_params(("parallel", "arbitrary")),
        name="odd_mix",
    )(xin, mods, p, p, p, p, p, ln_w, ln_b, w_s, b_s, bands, icnt, pool_w, pool_scale, w_out, fnw)


def kernel(x, c, ctx, c_ctx, ada_w, ada_b, norm_w, e_w_in, e_conv_qkv, e_a_log, e_dt_bias, e_head_norm,
           e_conv_b, e_w_out, o_w_in, o_ln_w, o_ln_b, o_w_s, o_b_s, o_pool_w, o_pool_scale, o_w_out,
           final_norm_w):
    bsz, seq, d = x.shape
    rx = bsz * seq
    rc = bsz * ctx.shape[1]
    x2 = x.reshape(rx, d)
    ctx2 = ctx.reshape(rc, d)

    cc = jnp.concatenate([c, c_ctx[None, :], jnp.zeros((8 - bsz - 1, d), F32)], axis=0)
    mods = _adaln(cc, ada_w, ada_b)
    ctx_row = bsz

    w_in = e_w_in[0].astype(BF16)
    w_bchz = w_in[:, GATE_COL0 + 4 * HEADS:]
    m0 = mods[0].reshape(8, 1, 3 * d)
    nw0 = norm_w[0].reshape(1, d)

    gate_par = jnp.pad(jnp.stack([e_a_log[0].reshape(-1), e_dt_bias[0].reshape(-1)]),
                       ((0, 0), (0, LANES - 2 * HEADS)))
    qkv_x, gates_x = _qkv_proj(x2, m0, nw0, w_in, e_conv_qkv[0], gate_par,
                               rows_per_batch=seq, mod_row0=0, group=GRID_W)
    qkv_c, gates_c = _qkv_proj(ctx2, m0, nw0, w_in, e_conv_qkv[0], gate_par,
                               rows_per_batch=ctx.shape[1], mod_row0=ctx_row, group=ctx.shape[1])
    gz_x, yb_x = _mixb_proj(x2, m0, nw0, w_in, w_bchz, e_conv_b[0],
                            rows_per_batch=seq, mod_row0=0, group=GRID_W)
    gz_c, yb_c = _mixb_proj(ctx2, m0, nw0, w_in, w_bchz, e_conv_b[0],
                            rows_per_batch=ctx.shape[1], mod_row0=ctx_row, group=ctx.shape[1])
    o_c, o_x = _gdn(qkv_c, qkv_x, gates_c, gates_x,
                    e_head_norm[0].reshape(1, HEAD_DIM))
    w_out0 = e_w_out[0].astype(BF16)
    x2 = _even_out(x2, m0, o_x.reshape(rx, d), gz_x, yb_x, w_out0, rows_per_batch=seq, mod_row0=0)
    ctx2 = _even_out(ctx2, m0, o_c.reshape(rc, d), gz_c, yb_c, w_out0,
                     rows_per_batch=ctx.shape[1], mod_row0=ctx_row)

    m1 = mods[1].reshape(8, 1, 3 * d)
    p = _odd_in(x2, m1, norm_w[1].reshape(1, d), o_w_in[0].astype(BF16))
    b_s = jnp.broadcast_to(o_b_s[0][:, :, None], (o_b_s.shape[1], TOK_CHUNK, LANES))
    out = _odd_mix(x2, m1, p, o_ln_w[0].reshape(1, d), o_ln_b[0].reshape(1, d),
                   o_w_s[0].astype(BF16), b_s, o_pool_w[0].astype(BF16),
                   o_pool_scale[0].reshape(1, d), o_w_out[0].astype(BF16),
                   final_norm_w.reshape(1, d))
    return out.reshape(bsz, seq, d)
```

```python
import functools
import math

import numpy as np
import jax
import jax.numpy as jnp
from jax import lax
from jax.experimental import pallas as pl
from jax.experimental.pallas import tpu as pltpu

F32 = jnp.float32
BF16 = jnp.bfloat16

D_MODEL = 2048
BATCH = 4
SEQ = 2048
CTX_LEN = 256
GRID_W = 64
EPS = 1e-6
HEADS = 16
HEAD_DIM = 128
QKV_W = 3 * D_MODEL
GATE_COL0 = QKV_W + D_MODEL
CHUNK = 64
SCAN_GROUPS = (14, 12, 10)
TOK_CHUNK = 128
POOL_GROUPS = 4
POOL_GW = D_MODEL // POOL_GROUPS
POOL_RADII = (1, 2, 4, 8)

LANES = 128
VMEM_LIMIT = 56 * 1024 * 1024
NORM_ROWS = 16
NORM_UNROLL = 8
SUB_N = 256


def _params(sem):
    return pltpu.CompilerParams(dimension_semantics=sem, vmem_limit_bytes=VMEM_LIMIT)


def _sigmoid(x):
    return 1.0 / (1.0 + jnp.exp(-x))


def _silu(x):
    return x * _sigmoid(x)


def _gelu_tanh(x):
    cdf = 0.5 * (1.0 + jnp.tanh(math.sqrt(2.0 / math.pi) * (x + 0.044715 * (x * x * x))))
    return x * cdf


def _modulated_norm_to(h_scr, x_ref, nw_ref, sc_ref, sh_ref):
    nw = nw_ref[...]
    sc = sc_ref[0]
    sh = sh_ref[0]

    step = NORM_ROWS * NORM_UNROLL

    def body(r, carry):
        rows = [pl.ds(pl.multiple_of(r * step + k * NORM_ROWS, NORM_ROWS), NORM_ROWS)
                for k in range(NORM_UNROLL)]
        scales = []
        for rw in rows:
            x = x_ref[rw, :]
            scales.append(lax.rsqrt(jnp.mean(x * x, axis=-1, keepdims=True) + EPS))
        for rw, rs in zip(rows, scales):
            y = x_ref[rw, :] * rs * nw
            h_scr[rw, :] = (y * (1.0 + sc) + sh).astype(h_scr.dtype)
        return carry

    lax.fori_loop(0, x_ref.shape[0] // step, body, 0)


def _row_conv3(p, cw, group):
    tm = p.shape[0]
    row = lax.broadcasted_iota(jnp.int32, (tm, 1), 0) % group
    prev = jnp.where(row == 0, 0.0, pltpu.roll(p, 1, 0))
    nxt = jnp.where(row == group - 1, 0.0, pltpu.roll(p, tm - 1, 0))
    return prev * cw[0:1] + p * cw[1:2] + nxt * cw[2:3]


def _adaln_kernel(c_ref, w_ref, b_ref, o_ref):
    a = _silu(c_ref[...]).astype(BF16)
    o_ref[0] = jnp.dot(a, w_ref[0].astype(BF16), preferred_element_type=F32) + b_ref[0]


def _adaln(cc, ada_w, ada_b):
    depth, d, n = ada_w.shape
    tn = 1024
    return pl.pallas_call(
        _adaln_kernel,
        grid=(depth, n // tn),
        in_specs=[pl.BlockSpec((8, d), lambda l, j: (0, 0)),
                  pl.BlockSpec((1, d, tn), lambda l, j: (l, 0, j)),
                  pl.BlockSpec((1, 1, tn), lambda l, j: (l, 0, j))],
        out_specs=pl.BlockSpec((1, 8, tn), lambda l, j: (l, 0, j)),
        out_shape=jax.ShapeDtypeStruct((depth, 8, n), F32),
        compiler_params=_params(("parallel", "parallel")),
        name="adaln",
    )(cc, ada_w, ada_b.reshape(depth, 1, n))


def _qkv_kernel(x_ref, nw_ref, sh_ref, sc_ref, w_ref, wg_ref, cw_ref, gp_ref, qkv_ref, g_ref, h_scr,
                *, group, n_tiles):
    j = pl.program_id(1)

    @pl.when(j == 0)
    def _():
        _modulated_norm_to(h_scr, x_ref, nw_ref, sc_ref, sh_ref)

    @pl.when(j < n_tiles)
    def _():
        p = jnp.dot(h_scr[...], w_ref[...], preferred_element_type=F32)
        y = _silu(_row_conv3(p, cw_ref[...], group))
        for hh in range(y.shape[1] // HEAD_DIM):
            qkv_ref[hh] = y[:, hh * HEAD_DIM:(hh + 1) * HEAD_DIM]

    @pl.when(j == n_tiles)
    def _():
        raw = jnp.dot(h_scr[...], wg_ref[...], preferred_element_type=F32)
        z = raw + gp_ref[1:2]
        softplus = jnp.maximum(z, 0.0) + jnp.log1p(jnp.exp(-jnp.abs(z)))
        lane = lax.broadcasted_iota(jnp.int32, (1, LANES), 1)
        gates = jnp.where(lane < 2 * HEADS, -jnp.exp(gp_ref[0:1]) * softplus, _sigmoid(raw))
        ri = lax.broadcasted_iota(jnp.int32, (LANES, LANES), 0)
        ci = lax.broadcasted_iota(jnp.int32, (LANES, LANES), 1)
        same = (ri // CHUNK) == (ci // CHUNK)
        pre01 = jnp.where(same & (ci <= ri), 1.0, 0.0).astype(BF16)
        suf01 = jnp.where(same & (ci >= ri), 1.0, 0.0).astype(BF16)
        pieces = _split3_bf16(gates)
        for r0 in range(0, gates.shape[0], LANES):
            blk = [p[r0:r0 + LANES] for p in pieces]
            pre = [jnp.dot(pre01, b, preferred_element_type=F32) for b in blk]
            suf = [jnp.dot(suf01, b, preferred_element_type=F32) for b in blk]
            pre = (pre[0] + pre[1]) + pre[2]
            suf = (suf[0] + suf[1]) + suf[2]
            g_ref[r0:r0 + LANES] = jnp.where(lane < HEADS, pre,
                                             jnp.where(lane < 2 * HEADS, suf, gates[r0:r0 + LANES]))


def _qkv_proj(xin, mods, nw, w_in, cw, gate_par, *, rows_per_batch, mod_row0, group):
    r, d = xin.shape
    tm, tn = 1024, 768
    n_tiles = QKV_W // tn
    hpt = tn // HEAD_DIM
    tiles_per_batch = max(rows_per_batch // tm, 1)

    def mod_idx(part):
        if rows_per_batch >= tm:
            return lambda i, j: (mod_row0 + i // tiles_per_batch, 0, part)
        return lambda i, j: (mod_row0, 0, part)

    last = n_tiles - 1
    return pl.pallas_call(
        functools.partial(_qkv_kernel, group=group, n_tiles=n_tiles),
        grid=(r // tm, n_tiles + 1),
        in_specs=[pl.BlockSpec((tm, d), lambda i, j: (i, 0)),
                  pl.BlockSpec((1, d), lambda i, j: (0, 0)),
                  pl.BlockSpec((1, 1, d), mod_idx(0)),
                  pl.BlockSpec((1, 1, d), mod_idx(1)),
                  pl.BlockSpec((d, tn), lambda i, j: (0, jnp.minimum(j, last))),
                  pl.BlockSpec((d, LANES), lambda i, j: (0, GATE_COL0 // LANES)),
                  pl.BlockSpec((3, tn), lambda i, j: (0, jnp.minimum(j, last))),
                  pl.BlockSpec((2, LANES), lambda i, j: (0, 0))],
        out_specs=[pl.BlockSpec((hpt, tm, HEAD_DIM), lambda i, j: (jnp.minimum(j, last), i, 0)),
                   pl.BlockSpec((tm, LANES), lambda i, j: (i, 0))],
        out_shape=[jax.ShapeDtypeStruct((3 * HEADS, r, HEAD_DIM), F32),
                   jax.ShapeDtypeStruct((r, LANES), F32)],
        scratch_shapes=[pltpu.VMEM((tm, d), BF16)],
        compiler_params=_params(("parallel", "arbitrary")),
        name="qkv_proj",
    )(xin, nw, mods, mods, w_in, w_in, cw, gate_par)


def _mixb_kernel(x_ref, nw_ref, sh_ref, sc_ref, wz_ref, wb_ref, wc_ref, wh_ref, wzb_ref, cw_ref,
                 gz_ref, yb_ref, h_scr, *, group):
    j = pl.program_id(1)

    @pl.when(j == 0)
    def _():
        _modulated_norm_to(h_scr, x_ref, nw_ref, sc_ref, sh_ref)

    h = h_scr[...]
    dot = lambda w: jnp.dot(h, w[...], preferred_element_type=F32)
    gz_ref[...] = _silu(dot(wz_ref)).astype(gz_ref.dtype)
    conv = _row_conv3(dot(wc_ref) * dot(wh_ref), cw_ref[...], group)
    yb_ref[...] = (dot(wb_ref) * conv * _silu(dot(wzb_ref))).astype(yb_ref.dtype)


def _mixb_proj(xin, mods, nw, w_in, w_bchz, cw, *, rows_per_batch, mod_row0, group):
    r, d = xin.shape
    tm, tn = 1024, 256
    nj = d // tn
    za_blk0 = QKV_W // tn
    tiles_per_batch = max(rows_per_batch // tm, 1)

    def mod_idx(part):
        if rows_per_batch >= tm:
            return lambda i, j: (mod_row0 + i // tiles_per_batch, 0, part)
        return lambda i, j: (mod_row0, 0, part)

    wspec = lambda reg: pl.BlockSpec((d, tn), lambda i, j: (0, reg * nj + j))
    return pl.pallas_call(
        functools.partial(_mixb_kernel, group=group),
        grid=(r // tm, nj),
        in_specs=[pl.BlockSpec((tm, d), lambda i, j: (i, 0)),
                  pl.BlockSpec((1, d), lambda i, j: (0, 0)),
                  pl.BlockSpec((1, 1, d), mod_idx(0)),
                  pl.BlockSpec((1, 1, d), mod_idx(1)),
                  pl.BlockSpec((d, tn), lambda i, j: (0, za_blk0 + j)),
                  wspec(0), wspec(1), wspec(2), wspec(3),
                  pl.BlockSpec((3, tn), lambda i, j: (0, j))],
        out_specs=[pl.BlockSpec((tm, tn), lambda i, j: (i, j)),
                   pl.BlockSpec((tm, tn), lambda i, j: (i, j))],
        out_shape=[jax.ShapeDtypeStruct((r, d), BF16), jax.ShapeDtypeStruct((r, d), BF16)],
        scratch_shapes=[pltpu.VMEM((tm, d), BF16)],
        compiler_params=_params(("parallel", "arbitrary")),
        name="mixb_proj",
    )(xin, nw, mods, mods, w_in, w_bchz, w_bchz, w_bchz, w_bchz, cw)


def _split3_bf16(x):
    x1 = x.astype(BF16)
    r1 = x - x1.astype(F32)
    x2 = r1.astype(BF16)
    x3 = (r1 - x2.astype(F32)).astype(BF16)
    return x1, x2, x3


def _block_diag2(x, left):
    return jnp.concatenate([jnp.where(left, x, 0.0), jnp.where(left, 0.0, x)], axis=0).astype(BF16)


def _unit_tri_inverse_minus_eye(ms, left, ri, cj):
    n = ms[0].shape[0]
    ys = None
    s = 1
    while s < n:
        shift = s.bit_length() - 1
        br = lax.shift_right_logical(ri, shift)
        bc = lax.shift_right_logical(cj, shift)
        mask_lo = ((br & 1) == 1) & (bc == br - 1)
        mask_up = ((bc & 1) == 1) & (br == bc - 1)
        mask = (left & mask_lo) | (jnp.logical_not(left) & mask_up)
        cms = [jnp.where(mask, m, 0.0) for m in ms]
        if ys is None:
            ys = [-cm for cm in cms]
        else:
            yc = [jnp.dot(y.astype(BF16), _block_diag2(cm, left), preferred_element_type=F32)
                  for cm, y in zip(cms, ys)]
            yield
            ps = [cm + t for cm, t in zip(cms, yc)]
            py = [jnp.dot(p.astype(BF16), _block_diag2(y, left), preferred_element_type=F32)
                  for y, p in zip(ys, ps)]
            yield
            ys = [y - p - t for y, p, t in zip(ys, ps, py)]
        s *= 2
    return ys


def _run_interleaved(*gens):
    active = list(gens)
    while active:
        for g in list(active):
            try:
                next(g)
            except StopIteration:
                active.remove(g)


def _gdn_kernel(qc_ref, kc_ref, vc_ref, qx_ref, kx_ref, vx_ref, gc_ref, gx_ref,
                hn_ref, oc_ref, ox_ref,
                qs, ks, vs, gsc, bsc, nq_s, b_s, op_s, gl_s, oacc):
    h = pl.program_id(1)
    tc = qc_ref.shape[1]
    tx = qx_ref.shape[1]
    t = tc + tx
    nsteps = t // CHUNK
    nc_ctx = tc // CHUNK
    c = CHUNK

    def l2n(a):
        return a * lax.rsqrt(jnp.sum(a * a, axis=-1, keepdims=True) + EPS)

    qscale = HEAD_DIM ** -0.5
    qs[0:tc] = l2n(qc_ref[0]) * qscale
    qs[tc:t] = l2n(qx_ref[0]) * qscale
    ks[0:tc] = l2n(kc_ref[0])
    ks[tc:t] = l2n(kx_ref[0])
    vs[0:tc] = vc_ref[0]
    vs[tc:t] = vx_ref[0]

    lane = lax.broadcasted_iota(jnp.int32, (1, LANES), 1)
    for d in range(2):
        for ref, lo, hi in ((gc_ref, 0, tc), (gx_ref, tc, t)):
            gates = ref[...]
            g = jnp.sum(jnp.where(lane == d * HEADS + h, gates, 0.0), axis=-1, keepdims=True)
            b = jnp.sum(jnp.where(lane == 2 * HEADS + d * HEADS + h, gates, 0.0), axis=-1,
                        keepdims=True)
            gsc[d, lo:hi] = jnp.broadcast_to(g, (hi - lo, LANES))
            bsc[d, lo:hi] = jnp.broadcast_to(b, (hi - lo, LANES))

    def rev_chunk(s):
        return nc_ctx - 1 - s if s < nc_ctx else nsteps + nc_ctx - 1 - s

    ri = lax.broadcasted_iota(jnp.int32, (c, 2 * c), 0)
    lane2 = lax.broadcasted_iota(jnp.int32, (c, 2 * c), 1)
    left = lane2 < c
    cj = lane2 & (c - 1)
    strict = (left & (cj < ri)) | (jnp.logical_not(left) & (cj > ri))
    incl = (left & (cj <= ri)) | (jnp.logical_not(left) & (cj >= ri))
    left_sq = lax.broadcasted_iota(jnp.int32, (2 * c, 2 * c), 1) < c

    def prep(steps):
        units = range(len(steps))
        rows = [[slice(n * c, (n + 1) * c) for n in (s, rev_chunk(s))] for s in steps]
        qv = [[qs[r] for r in rows[uu]] for uu in units]
        kv = [[ks[r] for r in rows[uu]] for uu in units]
        vv = [[vs[r] for r in rows[uu]] for uu in units]
        beta = [[bsc[d, rows[uu][d]] for d in range(2)] for uu in units]
        kb = [[kv[uu][d] * beta[uu][d] for d in range(2)] for uu in units]
        gk = [lax.dot_general(
                  jnp.concatenate([kb[uu][0], qv[uu][0], kb[uu][1], qv[uu][1]], axis=0).astype(BF16),
                  jnp.concatenate(kv[uu], axis=0).astype(BF16),
                  (((1,), (1,)), ((), ())), preferred_element_type=F32)
              for uu in units]
        yield
        gcb = [[gsc[d, rows[uu][d]] for d in range(2)] for uu in units]
        ms, avs = [], []
        for uu in units:
            col = jnp.where(left, gcb[uu][0], gcb[uu][1])
            row = jnp.transpose(jnp.concatenate(gcb[uu], axis=0))[0:c]
            decay = jnp.exp(jnp.where(incl, col - row, 0.0))
            kk = jnp.where(left, gk[uu][0:c], gk[uu][2 * c:3 * c])
            qk = jnp.where(left, gk[uu][c:2 * c], gk[uu][3 * c:4 * c])
            ms.append(jnp.where(strict, kk * decay, 0.0))
            avs.append(jnp.where(incl, qk * decay, 0.0))
        ys = yield from _unit_tri_inverse_minus_eye(ms, left, ri, cj)
        egc = [[jnp.exp(g) for g in gcb[uu]] for uu in units]
        rs = [[jnp.concatenate([kb[uu][d] * egc[uu][d], vv[uu][d] * beta[uu][d]], axis=1)
               for d in range(2)] for uu in units]
        corr = [jnp.dot(_block_diag2(ys[uu], left),
                        jnp.concatenate(rs[uu], axis=0).astype(BF16), preferred_element_type=F32)
                for uu in units]
        yield
        wus, aks, g_edges = [], [], []
        for uu in units:
            kgs, edges = [], []
            for d in range(2):
                edge = c - 1 if d == 0 else 0
                edges.append(gcb[uu][d][edge:edge + 1])
                kgs.append(kv[uu][d] * jnp.exp(edges[d] - gcb[uu][d]))
            g_edges.append(edges)
            wus.append([rs[uu][d] + corr[uu][d * c:(d + 1) * c] for d in range(2)])
            kgt = jnp.transpose(jnp.concatenate(kgs, axis=0))
            aks.append(jnp.concatenate(
                [jnp.where(left, avs[uu], 0.0), jnp.where(left_sq, kgt, 0.0),
                 jnp.where(left, 0.0, avs[uu]), jnp.where(left_sq, 0.0, kgt)], axis=0).astype(BF16))
        prods = [jnp.dot(aks[uu], jnp.concatenate(wus[uu], axis=0).astype(BF16),
                         preferred_element_type=F32) for uu in units]
        yield
        for uu in units:
            for d in range(2):
                x = prods[uu][d * (c + HEAD_DIM):(d + 1) * (c + HEAD_DIM)]
                a_w, a_u = x[0:c, 0:HEAD_DIM], x[0:c, HEAD_DIM:2 * HEAD_DIM]
                k_w, k_u = x[c:c + HEAD_DIM, 0:HEAD_DIM], x[c:c + HEAD_DIM, HEAD_DIM:2 * HEAD_DIM]
                idx = 2 * steps[uu] + d
                nq_s[idx] = jnp.concatenate([-k_w, qv[uu][d] * egc[uu][d] - a_w],
                                            axis=0).astype(BF16)
                b_s[idx] = k_u
                op_s[idx] = a_u
                gl_s[idx] = jnp.broadcast_to(jnp.exp(g_edges[uu][d]), (8, LANES))

    states = [jnp.zeros((HEAD_DIM, HEAD_DIM), F32)] * 2
    written = set()

    def scan(steps):
        for s in steps:
            idxs = (2 * s, 2 * s + 1)
            xs = [jnp.dot(nq_s[idx], st.astype(BF16), preferred_element_type=F32)
                  for idx, st in zip(idxs, states)]
            yield
            for n, idx, x in zip((s, rev_chunk(s)), idxs, xs):
                o = x[HEAD_DIM:HEAD_DIM + c] + op_s[idx]
                rows = slice(n * c, (n + 1) * c)
                oacc[rows] = oacc[rows] + o if n in written else o
                written.add(n)
            states[:] = [st * gl_s[idx][0:1] + x[0:HEAD_DIM] + b_s[idx]
                         for idx, st, x in zip(idxs, states, xs)]

    bounds = np.cumsum((0,) + SCAN_GROUPS)
    assert bounds[-1] == nsteps
    groups = [list(range(lo, hi)) for lo, hi in zip(bounds[:-1], bounds[1:])]
    _run_interleaved(prep(groups[0]))
    for gi in range(1, len(groups)):
        _run_interleaved(prep(groups[gi]), scan(groups[gi - 1]))
    _run_interleaved(scan(groups[-1]))

    o = oacc[...]
    on = o * lax.rsqrt(jnp.mean(o * o, axis=-1, keepdims=True) + EPS) * hn_ref[...]
    oc_ref[0] = on[0:tc].astype(oc_ref.dtype)
    ox_ref[0] = on[tc:t].astype(ox_ref.dtype)


def _gdn(qkv_c, qkv_x, gates_c, gates_x, head_norm):
    tc, tx = CTX_LEN, SEQ
    t = tc + tx
    nidx = 2 * (t // CHUNK)
    part = lambda rows, p: pl.BlockSpec((1, rows, HEAD_DIM), lambda b, h: (p * HEADS + h, b, 0))
    return pl.pallas_call(
        _gdn_kernel,
        grid=(BATCH, HEADS),
        in_specs=[part(tc, 0), part(tc, 1), part(tc, 2),
                  part(tx, 0), part(tx, 1), part(tx, 2),
                  pl.BlockSpec((tc, LANES), lambda b, h: (b, 0)),
                  pl.BlockSpec((tx, LANES), lambda b, h: (b, 0)),
                  pl.BlockSpec((1, HEAD_DIM), lambda b, h: (0, 0))],
        out_specs=[pl.BlockSpec((1, tc, HEAD_DIM), lambda b, h: (b, 0, h)),
                   pl.BlockSpec((1, tx, HEAD_DIM), lambda b, h: (b, 0, h))],
        out_shape=[jax.ShapeDtypeStruct((BATCH, tc, D_MODEL), BF16),
                   jax.ShapeDtypeStruct((BATCH, tx, D_MODEL), BF16)],
        scratch_shapes=[pltpu.VMEM((t, HEAD_DIM), F32), pltpu.VMEM((t, HEAD_DIM), F32),
                        pltpu.VMEM((t, HEAD_DIM), F32),
                        pltpu.VMEM((2, t, LANES), F32), pltpu.VMEM((2, t, LANES), F32),
                        pltpu.VMEM((nidx, HEAD_DIM + CHUNK, HEAD_DIM), BF16),
                        pltpu.VMEM((nidx, HEAD_DIM, HEAD_DIM), F32),
                        pltpu.VMEM((nidx, CHUNK, HEAD_DIM), F32),
                        pltpu.VMEM((nidx, 8, LANES), F32),
                        pltpu.VMEM((t, HEAD_DIM), F32)],
        compiler_params=_params(("parallel", "arbitrary")),
        name="gdn",
    )(qkv_c, qkv_c, qkv_c, qkv_x, qkv_x, qkv_x, gates_c, gates_x, head_norm)


def _even_out_kernel(x_ref, gt_ref, o_ref, gz_ref, yb_ref, w_ref, out_ref, a_scr):
    j = pl.program_id(1)

    @pl.when(j == 0)
    def _():
        d = o_ref.shape[1]
        a_scr[:, 0:d] = o_ref[...] * gz_ref[...]
        a_scr[:, d:2 * d] = yb_ref[...]

    y = jnp.dot(a_scr[...], w_ref[...], preferred_element_type=F32)
    out_ref[...] = x_ref[...] + gt_ref[0] * y


def _even_out(xin, mods, o, gz, yb, w_out, *, rows_per_batch, mod_row0):
    r, d = xin.shape
    tm = min(1024, rows_per_batch)
    tn = 512
    nj = d // tn
    tiles_per_batch = rows_per_batch // tm
    fixed = mod_row0 >= BATCH
    gt_idx = ((lambda i, j: (mod_row0, 0, 2 * nj + j)) if fixed
              else (lambda i, j: (i // tiles_per_batch, 0, 2 * nj + j)))
    return pl.pallas_call(
        _even_out_kernel,
        grid=(r // tm, nj),
        in_specs=[pl.BlockSpec((tm, tn), lambda i, j: (i, j)),
                  pl.BlockSpec((1, 1, tn), gt_idx),
                  pl.BlockSpec((tm, d), lambda i, j: (i, 0)),
                  pl.BlockSpec((tm, d), lambda i, j: (i, 0)),
                  pl.BlockSpec((tm, d), lambda i, j: (i, 0)),
                  pl.BlockSpec((2 * d, tn), lambda i, j: (0, j))],
        out_specs=pl.BlockSpec((tm, tn), lambda i, j: (i, j)),
        out_shape=jax.ShapeDtypeStruct((r, d), F32),
        scratch_shapes=[pltpu.VMEM((tm, 2 * d), BF16)],
        compiler_params=_params(("parallel", "arbitrary")),
        name="even_out",
    )(xin, mods, o, gz, yb, w_out)


def _odd_in_kernel(x_ref, nw_ref, sh_ref, sc_ref, w_ref, out_ref, h_scr, *, tiles_per_region):
    j = pl.program_id(1)

    @pl.when(j == 0)
    def _():
        _modulated_norm_to(h_scr, x_ref, nw_ref, sc_ref, sh_ref)

    region = j // tiles_per_region

    def tile(act):
        for lo in range(0, w_ref.shape[1], SUB_N):
            p = jnp.dot(h_scr[...], w_ref[:, lo:lo + SUB_N], preferred_element_type=F32)
            out_ref[:, lo:lo + SUB_N] = act(p).astype(out_ref.dtype)

    @pl.when(region <= 1)
    def _():
        tile(_gelu_tanh)

    @pl.when((region == 2) | (region == 4))
    def _():
        tile(_silu)

    @pl.when(region == 3)
    def _():
        tile(lambda p: p)


def _odd_in(xin, mods, nw, w):
    r, d = xin.shape
    n = w.shape[1]
    tm, tn = 1024, 1024
    tiles_per_batch = SEQ // tm
    mod_idx = lambda part: (lambda i, j: (i // tiles_per_batch, 0, part))
    return pl.pallas_call(
        functools.partial(_odd_in_kernel, tiles_per_region=d // tn),
        grid=(r // tm, n // tn),
        in_specs=[pl.BlockSpec((tm, d), lambda i, j: (i, 0)),
                  pl.BlockSpec((1, d), lambda i, j: (0, 0)),
                  pl.BlockSpec((1, 1, d), mod_idx(0)),
                  pl.BlockSpec((1, 1, d), mod_idx(1)),
                  pl.BlockSpec((d, tn), lambda i, j: (0, j))],
        out_specs=pl.BlockSpec((tm, tn), lambda i, j: (i, j)),
        out_shape=jax.ShapeDtypeStruct((r, n), BF16),
        scratch_shapes=[pltpu.VMEM((tm, d), BF16)],
        compiler_params=_params(("parallel", "arbitrary")),
        name="odd_in",
    )(xin, nw, mods, mods, w)


def _odd_mix_kernel(x_ref, gt_ref, u_ref, v_ref, zc_ref, pd_ref, zd_ref, lnw_ref, lnb_ref, ws_ref,
                    bs_ref, band_ref, icnt_ref, pw_ref, ps_ref, wo_ref, fnw_ref, out_ref,
                    acc, ln_scr, *, nk_c):
    kk = pl.program_id(1)
    tm = x_ref.shape[0]
    tk = u_ref.shape[1]
    gpb = tk // HEAD_DIM

    @pl.when(kk == 0)
    def _():
        v = v_ref[...].astype(F32)
        mu = jnp.mean(v, axis=-1, keepdims=True)
        xc = v - mu
        var = jnp.mean(xc * xc, axis=-1, keepdims=True)
        ln = (xc * lax.rsqrt(var + EPS) * lnw_ref[...] + lnb_ref[...]).astype(BF16)
        for blk in range(nk_c):
            ln_scr[blk] = ln[:, blk * tk:(blk + 1) * tk]
        acc[...] = jnp.zeros_like(acc)

    @pl.when(kk < nk_c)
    def _():
        ln = ln_scr[kk]
        cols = []
        for gi in range(gpb):
            w_g = ws_ref[gi]
            b_g = bs_ref[gi]
            rows = []
            for cc in range(tm // TOK_CHUNK):
                blk = ln[cc * TOK_CHUNK:(cc + 1) * TOK_CHUNK, gi * HEAD_DIM:(gi + 1) * HEAD_DIM]
                rows.append(jnp.dot(w_g, blk, preferred_element_type=F32) + b_g)
            cols.append(jnp.concatenate(rows, axis=0))
        s = jnp.concatenate(cols, axis=1)
        yc = u_ref[...] * s * zc_ref[...]
        acc[...] += jnp.dot(yc.astype(BF16), wo_ref[...], preferred_element_type=F32)

    @pl.when(kk >= nk_c)
    def _():
        pd = pd_ref[...]
        wsum = jnp.dot(band_ref[0], pd, preferred_element_type=F32)
        icnt = icnt_ref[0]
        icnt = jnp.concatenate([icnt] * (tk // LANES), axis=1)
        diff = wsum * icnt - pd.astype(F32)
        y = jnp.dot(diff.astype(BF16), pw_ref[0], preferred_element_type=F32)
        yd = y * ps_ref[...] * zd_ref[...]
        acc[...] += jnp.dot(yd.astype(BF16), wo_ref[...], preferred_element_type=F32)

    @pl.when(kk == pl.num_programs(1) - 1)
    def _():
        xn = x_ref[...] + gt_ref[0] * acc[...]
        ms = jnp.mean(xn * xn, axis=-1, keepdims=True)
        out_ref[...] = xn * lax.rsqrt(ms + EPS) * fnw_ref[...]


def _pool_constants(tm):
    bands = np.zeros((POOL_GROUPS, tm, tm), np.float32)
    icnt = np.zeros((POOL_GROUPS, tm, LANES), np.float32)
    pos = np.arange(tm)
    row = pos // GRID_W
    col = pos % GRID_W
    for g, rad in enumerate(POOL_RADII):
        same = row[:, None] == row[None, :]
        bands[g] = (same & (np.abs(col[:, None] - col[None, :]) <= rad)).astype(np.float32)
        cnt = np.minimum(col + rad + 1, GRID_W) - np.maximum(col - rad, 0)
        icnt[g] = (1.0 / cnt.astype(np.float32))[:, None]
    return jnp.asarray(bands, BF16), jnp.asarray(icnt, F32)


def _odd_mix(xin, mods, p, ln_w, ln_b, w_s, b_s, pool_w, pool_scale, w_out, fnw):
    r, d = xin.shape
    tm, tk = 512, 512
    nk_c = d // tk
    nk = 2 * nk_c
    gpb = tk // HEAD_DIM
    tiles_per_batch = SEQ // tm
    bands, icnt = _pool_constants(tm)
    lo = lambda kk: jnp.minimum(kk, nk_c - 1)
    hi = lambda kk: jnp.maximum(kk - nk_c, 0)
    cblk = d // tk
    return pl.pallas_call(
        functools.partial(_odd_mix_kernel, nk_c=nk_c),
        grid=(r // tm, nk),
        in_specs=[pl.BlockSpec((tm, d), lambda i, kk: (i, 0)),
                  pl.BlockSpec((1, 1, d), lambda i, kk: (i // tiles_per_batch, 0, 2)),
                  pl.BlockSpec((tm, tk), lambda i, kk: (i, lo(kk))),
                  pl.BlockSpec((tm, d), lambda i, kk: (i, 1)),
                  pl.BlockSpec((tm, tk), lambda i, kk: (i, 2 * cblk + lo(kk))),
                  pl.BlockSpec((tm, tk), lambda i, kk: (i, 3 * cblk + hi(kk))),
                  pl.BlockSpec((tm, tk), lambda i, kk: (i, 4 * cblk + hi(kk))),
                  pl.BlockSpec((1, d), lambda i, kk: (0, 0)),
                  pl.BlockSpec((1, d), lambda i, kk: (0, 0)),
                  pl.BlockSpec((gpb, TOK_CHUNK, TOK_CHUNK), lambda i, kk: (lo(kk), 0, 0)),
                  pl.BlockSpec((gpb, TOK_CHUNK, LANES), lambda i, kk: (lo(kk), 0, 0)),
                  pl.BlockSpec((1, tm, tm), lambda i, kk: (hi(kk), 0, 0)),
                  pl.BlockSpec((1, tm, LANES), lambda i, kk: (hi(kk), 0, 0)),
                  pl.BlockSpec((1, POOL_GW, POOL_GW), lambda i, kk: (hi(kk), 0, 0)),
                  pl.BlockSpec((1, tk), lambda i, kk: (0, hi(kk))),
                  pl.BlockSpec((tk, d), lambda i, kk: (kk, 0)),
                  pl.BlockSpec((1, d), lambda i, kk: (0, 0))],
        out_specs=pl.BlockSpec((tm, d), lambda i, kk: (i, 0)),
        out_shape=jax.ShapeDtypeStruct((r, d), F32),
        scratch_shapes=[pltpu.VMEM((tm, d), F32), pltpu.VMEM((nk_c, tm, tk), BF16)],
        compiler_params=_params(("parallel", "arbitrary")),
        name="odd_mix",
    )(xin, mods, p, p, p, p, p, ln_w, ln_b, w_s, b_s, bands, icnt, pool_w, pool_scale, w_out, fnw)


def kernel(x, c, ctx, c_ctx, ada_w, ada_b, norm_w, e_w_in, e_conv_qkv, e_a_log, e_dt_bias, e_head_norm,
           e_conv_b, e_w_out, o_w_in, o_ln_w, o_ln_b, o_w_s, o_b_s, o_pool_w, o_pool_scale, o_w_out,
           final_norm_w):
    bsz, seq, d = x.shape
    rx = bsz * seq
    rc = bsz * ctx.shape[1]
    x2 = x.reshape(rx, d)
    ctx2 = ctx.reshape(rc, d)

    cc = jnp.concatenate([c, c_ctx[None, :], jnp.zeros((8 - bsz - 1, d), F32)], axis=0)
    mods = _adaln(cc, ada_w, ada_b)
    ctx_row = bsz

    w_in = e_w_in[0].astype(BF16)
    w_bchz = w_in[:, GATE_COL0 + 4 * HEADS:]
    m0 = mods[0].reshape(8, 1, 3 * d)
    nw0 = norm_w[0].reshape(1, d)

    gate_par = jnp.pad(jnp.stack([e_a_log[0].reshape(-1), e_dt_bias[0].reshape(-1)]),
                       ((0, 0), (0, LANES - 2 * HEADS)))
    qkv_x, gates_x = _qkv_proj(x2, m0, nw0, w_in, e_conv_qkv[0], gate_par,
                               rows_per_batch=seq, mod_row0=0, group=GRID_W)
    qkv_c, gates_c = _qkv_proj(ctx2, m0, nw0, w_in, e_conv_qkv[0], gate_par,
                               rows_per_batch=ctx.shape[1], mod_row0=ctx_row, group=ctx.shape[1])
    gz_x, yb_x = _mixb_proj(x2, m0, nw0, w_in, w_bchz, e_conv_b[0],
                            rows_per_batch=seq, mod_row0=0, group=GRID_W)
    gz_c, yb_c = _mixb_proj(ctx2, m0, nw0, w_in, w_bchz, e_conv_b[0],
                            rows_per_batch=ctx.shape[1], mod_row0=ctx_row, group=ctx.shape[1])
    o_c, o_x = _gdn(qkv_c, qkv_x, gates_c, gates_x,
                    e_head_norm[0].reshape(1, HEAD_DIM))
    w_out0 = e_w_out[0].astype(BF16)
    x2 = _even_out(x2, m0, o_x.reshape(rx, d), gz_x, yb_x, w_out0, rows_per_batch=seq, mod_row0=0)
    ctx2 = _even_out(ctx2, m0, o_c.reshape(rc, d), gz_c, yb_c, w_out0,
                     rows_per_batch=ctx.shape[1], mod_row0=ctx_row)

    m1 = mods[1].reshape(8, 1, 3 * d)
    p = _odd_in(x2, m1, norm_w[1].reshape(1, d), o_w_in[0].astype(BF16))
    b_s = jnp.broadcast_to(o_b_s[0][:, :, None], (o_b_s.shape[1], TOK_CHUNK, LANES))
    out = _odd_mix(x2, m1, p, o_ln_w[0].reshape(1, d), o_ln_b[0].reshape(1, d),
                   o_w_s[0].astype(BF16), b_s, o_pool_w[0].astype(BF16),
                   o_pool_scale[0].reshape(1, d), o_w_out[0].astype(BF16),
                   final_norm_w.reshape(1, d))
    return out.reshape(bsz, seq, d)
```

```python
import functools
import math

import numpy as np
import jax
import jax.numpy as jnp
from jax import lax
from jax.experimental import pallas as pl
from jax.experimental.pallas import tpu as pltpu

F32 = jnp.float32
BF16 = jnp.bfloat16

D_MODEL = 2048
BATCH = 4
SEQ = 2048
CTX_LEN = 256
GRID_W = 64
EPS = 1e-6
HEADS = 16
HEAD_DIM = 128
QKV_W = 3 * D_MODEL
GATE_COL0 = QKV_W + D_MODEL
CHUNK = 64
SCAN_GROUPS = (12, 12, 12)
TOK_CHUNK = 128
POOL_GROUPS = 4
POOL_GW = D_MODEL // POOL_GROUPS
POOL_RADII = (1, 2, 4, 8)

LANES = 128
VMEM_LIMIT = 56 * 1024 * 1024
NORM_ROWS = 16
NORM_UNROLL = 8
SUB_N = 256


def _params(sem):
    return pltpu.CompilerParams(dimension_semantics=sem, vmem_limit_bytes=VMEM_LIMIT)


def _sigmoid(x):
    return 1.0 / (1.0 + jnp.exp(-x))


def _silu(x):
    return x * _sigmoid(x)


def _gelu_tanh(x):
    cdf = 0.5 * (1.0 + jnp.tanh(math.sqrt(2.0 / math.pi) * (x + 0.044715 * (x * x * x))))
    return x * cdf


def _modulated_norm_to(h_scr, x_ref, nw_ref, sc_ref, sh_ref):
    nw = nw_ref[...]
    sc = sc_ref[0]
    sh = sh_ref[0]

    step = NORM_ROWS * NORM_UNROLL

    def body(r, carry):
        rows = [pl.ds(pl.multiple_of(r * step + k * NORM_ROWS, NORM_ROWS), NORM_ROWS)
                for k in range(NORM_UNROLL)]
        scales = []
        for rw in rows:
            x = x_ref[rw, :]
            scales.append(lax.rsqrt(jnp.mean(x * x, axis=-1, keepdims=True) + EPS))
        for rw, rs in zip(rows, scales):
            y = x_ref[rw, :] * rs * nw
            h_scr[rw, :] = (y * (1.0 + sc) + sh).astype(h_scr.dtype)
        return carry

    lax.fori_loop(0, x_ref.shape[0] // step, body, 0)


def _row_conv3(p, cw, group):
    tm = p.shape[0]
    row = lax.broadcasted_iota(jnp.int32, (tm, 1), 0) % group
    prev = jnp.where(row == 0, 0.0, pltpu.roll(p, 1, 0))
    nxt = jnp.where(row == group - 1, 0.0, pltpu.roll(p, tm - 1, 0))
    return prev * cw[0:1] + p * cw[1:2] + nxt * cw[2:3]


def _adaln_kernel(c_ref, w_ref, b_ref, o_ref):
    a = _silu(c_ref[...]).astype(BF16)
    o_ref[0] = jnp.dot(a, w_ref[0].astype(BF16), preferred_element_type=F32) + b_ref[0]


def _adaln(cc, ada_w, ada_b):
    depth, d, n = ada_w.shape
    tn = 1024
    return pl.pallas_call(
        _adaln_kernel,
        grid=(depth, n // tn),
        in_specs=[pl.BlockSpec((8, d), lambda l, j: (0, 0)),
                  pl.BlockSpec((1, d, tn), lambda l, j: (l, 0, j)),
                  pl.BlockSpec((1, 1, tn), lambda l, j: (l, 0, j))],
        out_specs=pl.BlockSpec((1, 8, tn), lambda l, j: (l, 0, j)),
        out_shape=jax.ShapeDtypeStruct((depth, 8, n), F32),
        compiler_params=_params(("parallel", "parallel")),
        name="adaln",
    )(cc, ada_w, ada_b.reshape(depth, 1, n))


def _qkv_kernel(x_ref, nw_ref, sh_ref, sc_ref, w_ref, wg_ref, cw_ref, gp_ref, qkv_ref, g_ref, h_scr,
                *, group, n_tiles):
    j = pl.program_id(1)

    @pl.when(j == 0)
    def _():
        _modulated_norm_to(h_scr, x_ref, nw_ref, sc_ref, sh_ref)

    @pl.when(j < n_tiles)
    def _():
        p = jnp.dot(h_scr[...], w_ref[...], preferred_element_type=F32)
        y = _silu(_row_conv3(p, cw_ref[...], group))
        for hh in range(y.shape[1] // HEAD_DIM):
            qkv_ref[hh] = y[:, hh * HEAD_DIM:(hh + 1) * HEAD_DIM]

    @pl.when(j == n_tiles)
    def _():
        raw = jnp.dot(h_scr[...], wg_ref[...], preferred_element_type=F32)
        z = raw + gp_ref[1:2]
        softplus = jnp.maximum(z, 0.0) + jnp.log1p(jnp.exp(-jnp.abs(z)))
        lane = lax.broadcasted_iota(jnp.int32, (1, LANES), 1)
        gates = jnp.where(lane < 2 * HEADS, -jnp.exp(gp_ref[0:1]) * softplus, _sigmoid(raw))
        ri = lax.broadcasted_iota(jnp.int32, (LANES, LANES), 0)
        ci = lax.broadcasted_iota(jnp.int32, (LANES, LANES), 1)
        same = (ri // CHUNK) == (ci // CHUNK)
        pre01 = jnp.where(same & (ci <= ri), 1.0, 0.0).astype(BF16)
        suf01 = jnp.where(same & (ci >= ri), 1.0, 0.0).astype(BF16)
        pieces = _split3_bf16(gates)
        for r0 in range(0, gates.shape[0], LANES):
            blk = [p[r0:r0 + LANES] for p in pieces]
            pre = [jnp.dot(pre01, b, preferred_element_type=F32) for b in blk]
            suf = [jnp.dot(suf01, b, preferred_element_type=F32) for b in blk]
            pre = (pre[0] + pre[1]) + pre[2]
            suf = (suf[0] + suf[1]) + suf[2]
            g_ref[r0:r0 + LANES] = jnp.where(lane < HEADS, pre,
                                             jnp.where(lane < 2 * HEADS, suf, gates[r0:r0 + LANES]))


def _qkv_proj(xin, mods, nw, w_in, cw, gate_par, *, rows_per_batch, mod_row0, group):
    r, d = xin.shape
    tm, tn = 1024, 768
    n_tiles = QKV_W // tn
    hpt = tn // HEAD_DIM
    tiles_per_batch = max(rows_per_batch // tm, 1)

    def mod_idx(part):
        if rows_per_batch >= tm:
            return lambda i, j: (mod_row0 + i // tiles_per_batch, 0, part)
        return lambda i, j: (mod_row0, 0, part)

    last = n_tiles - 1
    return pl.pallas_call(
        functools.partial(_qkv_kernel, group=group, n_tiles=n_tiles),
        grid=(r // tm, n_tiles + 1),
        in_specs=[pl.BlockSpec((tm, d), lambda i, j: (i, 0)),
                  pl.BlockSpec((1, d), lambda i, j: (0, 0)),
                  pl.BlockSpec((1, 1, d), mod_idx(0)),
                  pl.BlockSpec((1, 1, d), mod_idx(1)),
                  pl.BlockSpec((d, tn), lambda i, j: (0, jnp.minimum(j, last))),
                  pl.BlockSpec((d, LANES), lambda i, j: (0, GATE_COL0 // LANES)),
                  pl.BlockSpec((3, tn), lambda i, j: (0, jnp.minimum(j, last))),
                  pl.BlockSpec((2, LANES), lambda i, j: (0, 0))],
        out_specs=[pl.BlockSpec((hpt, tm, HEAD_DIM), lambda i, j: (jnp.minimum(j, last), i, 0)),
                   pl.BlockSpec((tm, LANES), lambda i, j: (i, 0))],
        out_shape=[jax.ShapeDtypeStruct((3 * HEADS, r, HEAD_DIM), F32),
                   jax.ShapeDtypeStruct((r, LANES), F32)],
        scratch_shapes=[pltpu.VMEM((tm, d), BF16)],
        compiler_params=_params(("parallel", "arbitrary")),
        name="qkv_proj",
    )(xin, nw, mods, mods, w_in, w_in, cw, gate_par)


def _mixb_kernel(x_ref, nw_ref, sh_ref, sc_ref, wz_ref, wb_ref, wc_ref, wh_ref, wzb_ref, cw_ref,
                 gz_ref, yb_ref, h_scr, *, group):
    j = pl.program_id(1)

    @pl.when(j == 0)
    def _():
        _modulated_norm_to(h_scr, x_ref, nw_ref, sc_ref, sh_ref)

    h = h_scr[...]
    dot = lambda w: jnp.dot(h, w[...], preferred_element_type=F32)
    gz_ref[...] = _silu(dot(wz_ref)).astype(gz_ref.dtype)
    conv = _row_conv3(dot(wc_ref) * dot(wh_ref), cw_ref[...], group)
    yb_ref[...] = (dot(wb_ref) * conv * _silu(dot(wzb_ref))).astype(yb_ref.dtype)


def _mixb_proj(xin, mods, nw, w_in, w_bchz, cw, *, rows_per_batch, mod_row0, group):
    r, d = xin.shape
    tm, tn = 1024, 256
    nj = d // tn
    za_blk0 = QKV_W // tn
    tiles_per_batch = max(rows_per_batch // tm, 1)

    def mod_idx(part):
        if rows_per_batch >= tm:
            return lambda i, j: (mod_row0 + i // tiles_per_batch, 0, part)
        return lambda i, j: (mod_row0, 0, part)

    wspec = lambda reg: pl.BlockSpec((d, tn), lambda i, j: (0, reg * nj + j))
    return pl.pallas_call(
        functools.partial(_mixb_kernel, group=group),
        grid=(r // tm, nj),
        in_specs=[pl.BlockSpec((tm, d), lambda i, j: (i, 0)),
                  pl.BlockSpec((1, d), lambda i, j: (0, 0)),
                  pl.BlockSpec((1, 1, d), mod_idx(0)),
                  pl.BlockSpec((1, 1, d), mod_idx(1)),
                  pl.BlockSpec((d, tn), lambda i, j: (0, za_blk0 + j)),
                  wspec(0), wspec(1), wspec(2), wspec(3),
                  pl.BlockSpec((3, tn), lambda i, j: (0, j))],
        out_specs=[pl.BlockSpec((tm, tn), lambda i, j: (i, j)),
                   pl.BlockSpec((tm, tn), lambda i, j: (i, j))],
        out_shape=[jax.ShapeDtypeStruct((r, d), BF16), jax.ShapeDtypeStruct((r, d), BF16)],
        scratch_shapes=[pltpu.VMEM((tm, d), BF16)],
        compiler_params=_params(("parallel", "arbitrary")),
        name="mixb_proj",
    )(xin, nw, mods, mods, w_in, w_bchz, w_bchz, w_bchz, w_bchz, cw)


def _split3_bf16(x):
    x1 = x.astype(BF16)
    r1 = x - x1.astype(F32)
    x2 = r1.astype(BF16)
    x3 = (r1 - x2.astype(F32)).astype(BF16)
    return x1, x2, x3


def _block_diag2(x, left):
    return jnp.concatenate([jnp.where(left, x, 0.0), jnp.where(left, 0.0, x)], axis=0).astype(BF16)


def _unit_tri_inverse_minus_eye(ms, left, ri, cj):
    n = ms[0].shape[0]
    ys = None
    s = 1
    while s < n:
        shift = s.bit_length() - 1
        br = lax.shift_right_logical(ri, shift)
        bc = lax.shift_right_logical(cj, shift)
        mask_lo = ((br & 1) == 1) & (bc == br - 1)
        mask_up = ((bc & 1) == 1) & (br == bc - 1)
        mask = (left & mask_lo) | (jnp.logical_not(left) & mask_up)
        cms = [jnp.where(mask, m, 0.0) for m in ms]
        if ys is None:
            ys = [-cm for cm in cms]
        else:
            yc = [jnp.dot(y.astype(BF16), _block_diag2(cm, left), preferred_element_type=F32)
                  for cm, y in zip(cms, ys)]
            yield
            ps = [cm + t for cm, t in zip(cms, yc)]
            py = [jnp.dot(p.astype(BF16), _block_diag2(y, left), preferred_element_type=F32)
                  for y, p in zip(ys, ps)]
            yield
            ys = [y - p - t for y, p, t in zip(ys, ps, py)]
        s *= 2
    return ys


def _run_interleaved(*gens):
    active = list(gens)
    while active:
        for g in list(active):
            try:
                next(g)
            except StopIteration:
                active.remove(g)


def _gdn_kernel(qc_ref, kc_ref, vc_ref, qx_ref, kx_ref, vx_ref, gc_ref, gx_ref,
                hn_ref, oc_ref, ox_ref,
                qs, ks, vs, gsc, bsc, nq_s, b_s, op_s, gl_s, oacc, st_scr):
    k = pl.program_id(0)
    h = jnp.minimum(k, pl.num_programs(0) - 2) % HEADS
    cur_slot = k % 2
    prev_slot = 1 - cur_slot
    tc = qc_ref.shape[1]
    tx = qx_ref.shape[1]
    t = tc + tx
    nsteps = t // CHUNK
    nc_ctx = tc // CHUNK
    c = CHUNK

    def l2n(a):
        return a * lax.rsqrt(jnp.sum(a * a, axis=-1, keepdims=True) + EPS)

    @pl.when(k == 0)
    def _():
        for ref in (nq_s, b_s, op_s, gl_s, oacc, st_scr):
            ref[...] = jnp.zeros_like(ref)

    qscale = HEAD_DIM ** -0.5
    qs[0:tc] = l2n(qc_ref[0]) * qscale
    qs[tc:t] = l2n(qx_ref[0]) * qscale
    ks[0:tc] = l2n(kc_ref[0])
    ks[tc:t] = l2n(kx_ref[0])
    vs[0:tc] = vc_ref[0]
    vs[tc:t] = vx_ref[0]

    lane = lax.broadcasted_iota(jnp.int32, (1, LANES), 1)
    for d in range(2):
        for ref, lo, hi in ((gc_ref, 0, tc), (gx_ref, tc, t)):
            gates = ref[...]
            g = jnp.sum(jnp.where(lane == d * HEADS + h, gates, 0.0), axis=-1, keepdims=True)
            b = jnp.sum(jnp.where(lane == 2 * HEADS + d * HEADS + h, gates, 0.0), axis=-1,
                        keepdims=True)
            gsc[d, lo:hi] = jnp.broadcast_to(g, (hi - lo, LANES))
            bsc[d, lo:hi] = jnp.broadcast_to(b, (hi - lo, LANES))

    def rev_chunk(s):
        return nc_ctx - 1 - s if s < nc_ctx else nsteps + nc_ctx - 1 - s

    ri = lax.broadcasted_iota(jnp.int32, (c, 2 * c), 0)
    lane2 = lax.broadcasted_iota(jnp.int32, (c, 2 * c), 1)
    left = lane2 < c
    cj = lane2 & (c - 1)
    strict = (left & (cj < ri)) | (jnp.logical_not(left) & (cj > ri))
    incl = (left & (cj <= ri)) | (jnp.logical_not(left) & (cj >= ri))
    left_sq = lax.broadcasted_iota(jnp.int32, (2 * c, 2 * c), 1) < c

    def prep(steps):
        units = range(len(steps))
        rows = [[slice(n * c, (n + 1) * c) for n in (s, rev_chunk(s))] for s in steps]
        qv = [[qs[r] for r in rows[uu]] for uu in units]
        kv = [[ks[r] for r in rows[uu]] for uu in units]
        vv = [[vs[r] for r in rows[uu]] for uu in units]
        beta = [[bsc[d, rows[uu][d]] for d in range(2)] for uu in units]
        kb = [[kv[uu][d] * beta[uu][d] for d in range(2)] for uu in units]
        gk = [lax.dot_general(
                  jnp.concatenate([kb[uu][0], qv[uu][0], kb[uu][1], qv[uu][1]], axis=0).astype(BF16),
                  jnp.concatenate(kv[uu], axis=0).astype(BF16),
                  (((1,), (1,)), ((), ())), preferred_element_type=F32)
              for uu in units]
        yield
        gcb = [[gsc[d, rows[uu][d]] for d in range(2)] for uu in units]
        ms, avs = [], []
        for uu in units:
            col = jnp.where(left, gcb[uu][0], gcb[uu][1])
            row = jnp.transpose(jnp.concatenate(gcb[uu], axis=0))[0:c]
            decay = jnp.exp(jnp.where(incl, col - row, 0.0))
            kk = jnp.where(left, gk[uu][0:c], gk[uu][2 * c:3 * c])
            qk = jnp.where(left, gk[uu][c:2 * c], gk[uu][3 * c:4 * c])
            ms.append(jnp.where(strict, kk * decay, 0.0))
            avs.append(jnp.where(incl, qk * decay, 0.0))
        ys = yield from _unit_tri_inverse_minus_eye(ms, left, ri, cj)
        egc = [[jnp.exp(g) for g in gcb[uu]] for uu in units]
        rs = [[jnp.concatenate([kb[uu][d] * egc[uu][d], vv[uu][d] * beta[uu][d]], axis=1)
               for d in range(2)] for uu in units]
        corr = [jnp.dot(_block_diag2(ys[uu], left),
                        jnp.concatenate(rs[uu], axis=0).astype(BF16), preferred_element_type=F32)
                for uu in units]
        yield
        wus, aks, g_edges = [], [], []
        for uu in units:
            kgs, edges = [], []
            for d in range(2):
                edge = c - 1 if d == 0 else 0
                edges.append(gcb[uu][d][edge:edge + 1])
                kgs.append(kv[uu][d] * jnp.exp(edges[d] - gcb[uu][d]))
            g_edges.append(edges)
            wus.append([rs[uu][d] + corr[uu][d * c:(d + 1) * c] for d in range(2)])
            kgt = jnp.transpose(jnp.concatenate(kgs, axis=0))
            aks.append(jnp.concatenate(
                [jnp.where(left, avs[uu], 0.0), jnp.where(left_sq, kgt, 0.0),
                 jnp.where(left, 0.0, avs[uu]), jnp.where(left_sq, 0.0, kgt)], axis=0).astype(BF16))
        prods = [jnp.dot(aks[uu], jnp.concatenate(wus[uu], axis=0).astype(BF16),
                         preferred_element_type=F32) for uu in units]
        yield
        for uu in units:
            for d in range(2):
                x = prods[uu][d * (c + HEAD_DIM):(d + 1) * (c + HEAD_DIM)]
                a_w, a_u = x[0:c, 0:HEAD_DIM], x[0:c, HEAD_DIM:2 * HEAD_DIM]
                k_w, k_u = x[c:c + HEAD_DIM, 0:HEAD_DIM], x[c:c + HEAD_DIM, HEAD_DIM:2 * HEAD_DIM]
                idx = 2 * steps[uu] + d
                nq_s[idx] = jnp.concatenate([-k_w, qv[uu][d] * egc[uu][d] - a_w],
                                            axis=0).astype(BF16)
                b_s[idx] = k_u
                op_s[idx] = a_u
                gl_s[idx] = jnp.broadcast_to(jnp.exp(g_edges[uu][d]), (8, LANES))

    def scan(steps, states, slot, written):
        for s in steps:
            idxs = (2 * s, 2 * s + 1)
            xs = [jnp.dot(nq_s[idx], st.astype(BF16), preferred_element_type=F32)
                  for idx, st in zip(idxs, states)]
            yield
            for n, idx, x in zip((s, rev_chunk(s)), idxs, xs):
                o = x[HEAD_DIM:HEAD_DIM + c] + op_s[idx]
                rows = slice(n * c, (n + 1) * c)
                oacc[slot, rows] = oacc[slot, rows] + o if n in written else o
                written.add(n)
            states[:] = [st * gl_s[idx][0:1] + x[0:HEAD_DIM] + b_s[idx]
                         for idx, st, x in zip(idxs, states, xs)]

    bounds = np.cumsum((0,) + SCAN_GROUPS)
    assert bounds[-1] == nsteps
    groups = [list(range(lo, hi)) for lo, hi in zip(bounds[:-1], bounds[1:])]
    early = [s for g in groups[:-1] for s in g]
    written_early = {n for s in early for n in (s, rev_chunk(s))}

    prev_states = [st_scr[0], st_scr[1]]
    _run_interleaved(prep(groups[0]), scan(groups[-1], prev_states, prev_slot, set(written_early)))
    o = oacc[prev_slot]
    on = o * lax.rsqrt(jnp.mean(o * o, axis=-1, keepdims=True) + EPS) * hn_ref[...]
    oc_ref[0] = on[0:tc].astype(oc_ref.dtype)
    ox_ref[0] = on[tc:t].astype(ox_ref.dtype)

    states = [jnp.zeros((HEAD_DIM, HEAD_DIM), F32)] * 2
    written = set()
    for gi in range(1, len(groups)):
        _run_interleaved(prep(groups[gi]), scan(groups[gi - 1], states, cur_slot, written))
    assert written == written_early
    st_scr[0] = states[0]
    st_scr[1] = states[1]


def _gdn(qkv_c, qkv_x, gates_c, gates_x, head_norm):
    tc, tx = CTX_LEN, SEQ
    t = tc + tx
    nidx = 2 * (t // CHUNK)
    nprob = BATCH * HEADS
    cur = lambda k: jnp.minimum(k, nprob - 1)
    prev = lambda k: jnp.maximum(k - 1, 0)
    part = lambda rows, p: pl.BlockSpec(
        (1, rows, HEAD_DIM), lambda k: (p * HEADS + cur(k) % HEADS, cur(k) // HEADS, 0))
    return pl.pallas_call(
        _gdn_kernel,
        grid=(nprob + 1,),
        in_specs=[part(tc, 0), part(tc, 1), part(tc, 2),
                  part(tx, 0), part(tx, 1), part(tx, 2),
                  pl.BlockSpec((tc, LANES), lambda k: (cur(k) // HEADS, 0)),
                  pl.BlockSpec((tx, LANES), lambda k: (cur(k) // HEADS, 0)),
                  pl.BlockSpec((1, HEAD_DIM), lambda k: (0, 0))],
        out_specs=[pl.BlockSpec((1, tc, HEAD_DIM), lambda k: (prev(k) // HEADS, 0, prev(k) % HEADS)),
                   pl.BlockSpec((1, tx, HEAD_DIM), lambda k: (prev(k) // HEADS, 0, prev(k) % HEADS))],
        out_shape=[jax.ShapeDtypeStruct((BATCH, tc, D_MODEL), BF16),
                   jax.ShapeDtypeStruct((BATCH, tx, D_MODEL), BF16)],
        scratch_shapes=[pltpu.VMEM((t, HEAD_DIM), F32), pltpu.VMEM((t, HEAD_DIM), F32),
                        pltpu.VMEM((t, HEAD_DIM), F32),
                        pltpu.VMEM((2, t, LANES), F32), pltpu.VMEM((2, t, LANES), F32),
                        pltpu.VMEM((nidx, HEAD_DIM + CHUNK, HEAD_DIM), BF16),
                        pltpu.VMEM((nidx, HEAD_DIM, HEAD_DIM), F32),
                        pltpu.VMEM((nidx, CHUNK, HEAD_DIM), F32),
                        pltpu.VMEM((nidx, 8, LANES), F32),
                        pltpu.VMEM((2, t, HEAD_DIM), F32),
                        pltpu.VMEM((2, HEAD_DIM, HEAD_DIM), F32)],
        compiler_params=_params(("arbitrary",)),
        name="gdn",
    )(qkv_c, qkv_c, qkv_c, qkv_x, qkv_x, qkv_x, gates_c, gates_x, head_norm)


def _even_out_kernel(x_ref, gt_ref, o_ref, gz_ref, yb_ref, w_ref, out_ref, a_scr):
    j = pl.program_id(1)

    @pl.when(j == 0)
    def _():
        d = o_ref.shape[1]
        a_scr[:, 0:d] = o_ref[...] * gz_ref[...]
        a_scr[:, d:2 * d] = yb_ref[...]

    y = jnp.dot(a_scr[...], w_ref[...], preferred_element_type=F32)
    out_ref[...] = x_ref[...] + gt_ref[0] * y


def _even_out(xin, mods, o, gz, yb, w_out, *, rows_per_batch, mod_row0):
    r, d = xin.shape
    tm = min(1024, rows_per_batch)
    tn = 512
    nj = d // tn
    tiles_per_batch = rows_per_batch // tm
    fixed = mod_row0 >= BATCH
    gt_idx = ((lambda i, j: (mod_row0, 0, 2 * nj + j)) if fixed
              else (lambda i, j: (i // tiles_per_batch, 0, 2 * nj + j)))
    return pl.pallas_call(
        _even_out_kernel,
        grid=(r // tm, nj),
        in_specs=[pl.BlockSpec((tm, tn), lambda i, j: (i, j)),
                  pl.BlockSpec((1, 1, tn), gt_idx),
                  pl.BlockSpec((tm, d), lambda i, j: (i, 0)),
                  pl.BlockSpec((tm, d), lambda i, j: (i, 0)),
                  pl.BlockSpec((tm, d), lambda i, j: (i, 0)),
                  pl.BlockSpec((2 * d, tn), lambda i, j: (0, j))],
        out_specs=pl.BlockSpec((tm, tn), lambda i, j: (i, j)),
        out_shape=jax.ShapeDtypeStruct((r, d), F32),
        scratch_shapes=[pltpu.VMEM((tm, 2 * d), BF16)],
        compiler_params=_params(("parallel", "arbitrary")),
        name="even_out",
    )(xin, mods, o, gz, yb, w_out)


def _odd_in_kernel(x_ref, nw_ref, sh_ref, sc_ref, w_ref, out_ref, h_scr, *, tiles_per_region):
    j = pl.program_id(1)

    @pl.when(j == 0)
    def _():
        _modulated_norm_to(h_scr, x_ref, nw_ref, sc_ref, sh_ref)

    region = j // tiles_per_region

    def tile(act):
        for lo in range(0, w_ref.shape[1], SUB_N):
            p = jnp.dot(h_scr[...], w_ref[:, lo:lo + SUB_N], preferred_element_type=F32)
            out_ref[:, lo:lo + SUB_N] = act(p).astype(out_ref.dtype)

    @pl.when(region <= 1)
    def _():
        tile(_gelu_tanh)

    @pl.when((region == 2) | (region == 4))
    def _():
        tile(_silu)

    @pl.when(region == 3)
    def _():
        tile(lambda p: p)


def _odd_in(xin, mods, nw, w):
    r, d = xin.shape
    n = w.shape[1]
    tm, tn = 1024, 1024
    tiles_per_batch = SEQ // tm
    mod_idx = lambda part: (lambda i, j: (i // tiles_per_batch, 0, part))
    return pl.pallas_call(
        functools.partial(_odd_in_kernel, tiles_per_region=d // tn),
        grid=(r // tm, n // tn),
        in_specs=[pl.BlockSpec((tm, d), lambda i, j: (i, 0)),
                  pl.BlockSpec((1, d), lambda i, j: (0, 0)),
                  pl.BlockSpec((1, 1, d), mod_idx(0)),
                  pl.BlockSpec((1, 1, d), mod_idx(1)),
                  pl.BlockSpec((d, tn), lambda i, j: (0, j))],
        out_specs=pl.BlockSpec((tm, tn), lambda i, j: (i, j)),
        out_shape=jax.ShapeDtypeStruct((r, n), BF16),
        scratch_shapes=[pltpu.VMEM((tm, d), BF16)],
        compiler_params=_params(("parallel", "arbitrary")),
        name="odd_in",
    )(xin, nw, mods, mods, w)


def _odd_mix_kernel(x_ref, gt_ref, u_ref, v_ref, zc_ref, pd_ref, zd_ref, lnw_ref, lnb_ref, ws_ref,
                    bs_ref, band_ref, icnt_ref, pw_ref, ps_ref, wo_ref, fnw_ref, out_ref,
                    acc, ln_scr, *, nk_c):
    kk = pl.program_id(1)
    tm = x_ref.shape[0]
    tk = u_ref.shape[1]
    gpb = tk // HEAD_DIM

    @pl.when(kk == 0)
    def _():
        v = v_ref[...].astype(F32)
        mu = jnp.mean(v, axis=-1, keepdims=True)
        xc = v - mu
        var = jnp.mean(xc * xc, axis=-1, keepdims=True)
        ln = (xc * lax.rsqrt(var + EPS) * lnw_ref[...] + lnb_ref[...]).astype(BF16)
        for blk in range(nk_c):
            ln_scr[blk] = ln[:, blk * tk:(blk + 1) * tk]
        acc[...] = jnp.zeros_like(acc)

    @pl.when(kk < nk_c)
    def _():
        ln = ln_scr[kk]
        cols = []
        for gi in range(gpb):
            w_g = ws_ref[gi]
            b_g = bs_ref[gi]
            rows = []
            for cc in range(tm // TOK_CHUNK):
                blk = ln[cc * TOK_CHUNK:(cc + 1) * TOK_CHUNK, gi * HEAD_DIM:(gi + 1) * HEAD_DIM]
                rows.append(jnp.dot(w_g, blk, preferred_element_type=F32) + b_g)
            cols.append(jnp.concatenate(rows, axis=0))
        s = jnp.concatenate(cols, axis=1)
        yc = u_ref[...] * s * zc_ref[...]
        acc[...] += jnp.dot(yc.astype(BF16), wo_ref[...], preferred_element_type=F32)

    @pl.when(kk >= nk_c)
    def _():
        pd = pd_ref[...]
        wsum = jnp.dot(band_ref[0], pd, preferred_element_type=F32)
        icnt = icnt_ref[0]
        icnt = jnp.concatenate([icnt] * (tk // LANES), axis=1)
        diff = wsum * icnt - pd.astype(F32)
        y = jnp.dot(diff.astype(BF16), pw_ref[0], preferred_element_type=F32)
        yd = y * ps_ref[...] * zd_ref[...]
        acc[...] += jnp.dot(yd.astype(BF16), wo_ref[...], preferred_element_type=F32)

    @pl.when(kk == pl.num_programs(1) - 1)
    def _():
        xn = x_ref[...] + gt_ref[0] * acc[...]
        ms = jnp.mean(xn * xn, axis=-1, keepdims=True)
        out_ref[...] = xn * lax.rsqrt(ms + EPS) * fnw_ref[...]


def _pool_constants(tm):
    bands = np.zeros((POOL_GROUPS, tm, tm), np.float32)
    icnt = np.zeros((POOL_GROUPS, tm, LANES), np.float32)
    pos = np.arange(tm)
    row = pos // GRID_W
    col = pos % GRID_W
    for g, rad in enumerate(POOL_RADII):
        same = row[:, None] == row[None, :]
        bands[g] = (same & (np.abs(col[:, None] - col[None, :]) <= rad)).astype(np.float32)
        cnt = np.minimum(col + rad + 1, GRID_W) - np.maximum(col - rad, 0)
        icnt[g] = (1.0 / cnt.astype(np.float32))[:, None]
    return jnp.asarray(bands, BF16), jnp.asarray(icnt, F32)


def _odd_mix(xin, mods, p, ln_w, ln_b, w_s, b_s, pool_w, pool_scale, w_out, fnw):
    r, d = xin.shape
    tm, tk = 512, 512
    nk_c = d // tk
    nk = 2 * nk_c
    gpb = tk // HEAD_DIM
    tiles_per_batch = SEQ // tm
    bands, icnt = _pool_constants(tm)
    lo = lambda kk: jnp.minimum(kk, nk_c - 1)
    hi = lambda kk: jnp.maximum(kk - nk_c, 0)
    cblk = d // tk
    return pl.pallas_call(
        functools.partial(_odd_mix_kernel, nk_c=nk_c),
        grid=(r // tm, nk),
        in_specs=[pl.BlockSpec((tm, d), lambda i, kk: (i, 0)),
                  pl.BlockSpec((1, 1, d), lambda i, kk: (i // tiles_per_batch, 0, 2)),
                  pl.BlockSpec((tm, tk), lambda i, kk: (i, lo(kk))),
                  pl.BlockSpec((tm, d), lambda i, kk: (i, 1)),
                  pl.BlockSpec((tm, tk), lambda i, kk: (i, 2 * cblk + lo(kk))),
                  pl.BlockSpec((tm, tk), lambda i, kk: (i, 3 * cblk + hi(kk))),
                  pl.BlockSpec((tm, tk), lambda i, kk: (i, 4 * cblk + hi(kk))),
                  pl.BlockSpec((1, d), lambda i, kk: (0, 0)),
                  pl.BlockSpec((1, d), lambda i, kk: (0, 0)),
                  pl.BlockSpec((gpb, TOK_CHUNK, TOK_CHUNK), lambda i, kk: (lo(kk), 0, 0)),
                  pl.BlockSpec((gpb, TOK_CHUNK, LANES), lambda i, kk: (lo(kk), 0, 0)),
                  pl.BlockSpec((1, tm, tm), lambda i, kk: (hi(kk), 0, 0)),
                  pl.BlockSpec((1, tm, LANES), lambda i, kk: (hi(kk), 0, 0)),
                  pl.BlockSpec((1, POOL_GW, POOL_GW), lambda i, kk: (hi(kk), 0, 0)),
                  pl.BlockSpec((1, tk), lambda i, kk: (0, hi(kk))),
                  pl.BlockSpec((tk, d), lambda i, kk: (kk, 0)),
                  pl.BlockSpec((1, d), lambda i, kk: (0, 0))],
        out_specs=pl.BlockSpec((tm, d), lambda i, kk: (i, 0)),
        out_shape=jax.ShapeDtypeStruct((r, d), F32),
        scratch_shapes=[pltpu.VMEM((tm, d), F32), pltpu.VMEM((nk_c, tm, tk), BF16)],
        compiler_params=_params(("parallel", "arbitrary")),
        name="odd_mix",
    )(xin, mods, p, p, p, p, p, ln_w, ln_b, w_s, b_s, bands, icnt, pool_w, pool_scale, w_out, fnw)


def kernel(x, c, ctx, c_ctx, ada_w, ada_b, norm_w, e_w_in, e_conv_qkv, e_a_log, e_dt_bias, e_head_norm,
           e_conv_b, e_w_out, o_w_in, o_ln_w, o_ln_b, o_w_s, o_b_s, o_pool_w, o_pool_scale, o_w_out,
           final_norm_w):
    bsz, seq, d = x.shape
    rx = bsz * seq
    rc = bsz * ctx.shape[1]
    x2 = x.reshape(rx, d)
    ctx2 = ctx.reshape(rc, d)

    cc = jnp.concatenate([c, c_ctx[None, :], jnp.zeros((8 - bsz - 1, d), F32)], axis=0)
    mods = _adaln(cc, ada_w, ada_b)
    ctx_row = bsz

    w_in = e_w_in[0].astype(BF16)
    w_bchz = w_in[:, GATE_COL0 + 4 * HEADS:]
    m0 = mods[0].reshape(8, 1, 3 * d)
    nw0 = norm_w[0].reshape(1, d)

    gate_par = jnp.pad(jnp.stack([e_a_log[0].reshape(-1), e_dt_bias[0].reshape(-1)]),
                       ((0, 0), (0, LANES - 2 * HEADS)))
    qkv_x, gates_x = _qkv_proj(x2, m0, nw0, w_in, e_conv_qkv[0], gate_par,
                               rows_per_batch=seq, mod_row0=0, group=GRID_W)
    qkv_c, gates_c = _qkv_proj(ctx2, m0, nw0, w_in, e_conv_qkv[0], gate_par,
                               rows_per_batch=ctx.shape[1], mod_row0=ctx_row, group=ctx.shape[1])
    gz_x, yb_x = _mixb_proj(x2, m0, nw0, w_in, w_bchz, e_conv_b[0],
                            rows_per_batch=seq, mod_row0=0, group=GRID_W)
    gz_c, yb_c = _mixb_proj(ctx2, m0, nw0, w_in, w_bchz, e_conv_b[0],
                            rows_per_batch=ctx.shape[1], mod_row0=ctx_row, group=ctx.shape[1])
    o_c, o_x = _gdn(qkv_c, qkv_x, gates_c, gates_x,
                    e_head_norm[0].reshape(1, HEAD_DIM))
    w_out0 = e_w_out[0].astype(BF16)
    x2 = _even_out(x2, m0, o_x.reshape(rx, d), gz_x, yb_x, w_out0, rows_per_batch=seq, mod_row0=0)
    ctx2 = _even_out(ctx2, m0, o_c.reshape(rc, d), gz_c, yb_c, w_out0,
                     rows_per_batch=ctx.shape[1], mod_row0=ctx_row)

    m1 = mods[1].reshape(8, 1, 3 * d)
    p = _odd_in(x2, m1, norm_w[1].reshape(1, d), o_w_in[0].astype(BF16))
    b_s = jnp.broadcast_to(o_b_s[0][:, :, None], (o_b_s.shape[1], TOK_CHUNK, LANES))
    out = _odd_mix(x2, m1, p, o_ln_w[0].reshape(1, d), o_ln_b[0].reshape(1, d),
                   o_w_s[0].astype(BF16), b_s, o_pool_w[0].astype(BF16),
                   o_pool_scale[0].reshape(1, d), o_w_out[0].astype(BF16),
                   final_norm_w.reshape(1, d))
    return out.reshape(bsz, seq, d)
```

```python
import functools
import math

import numpy as np
import jax
import jax.numpy as jnp
from jax import lax
from jax.experimental import pallas as pl
from jax.experimental.pallas import tpu as pltpu

F32 = jnp.float32
BF16 = jnp.bfloat16

D_MODEL = 2048
BATCH = 4
SEQ = 2048
CTX_LEN = 256
GRID_W = 64
EPS = 1e-6
HEADS = 16
HEAD_DIM = 128
QKV_W = 3 * D_MODEL
GATE_COL0 = QKV_W + D_MODEL
CHUNK = 64
SCAN_GROUPS = (12, 12, 12)
TOK_CHUNK = 128
POOL_GROUPS = 4
POOL_GW = D_MODEL // POOL_GROUPS
POOL_RADII = (1, 2, 4, 8)

LANES = 128
VMEM_LIMIT = 56 * 1024 * 1024
NORM_ROWS = 16
NORM_UNROLL = 8
SUB_N = 256


def _params(sem):
    return pltpu.CompilerParams(dimension_semantics=sem, vmem_limit_bytes=VMEM_LIMIT)


def _sigmoid(x):
    return 1.0 / (1.0 + jnp.exp(-x))


def _silu(x):
    return x * _sigmoid(x)


def _gelu_tanh(x):
    cdf = 0.5 * (1.0 + jnp.tanh(math.sqrt(2.0 / math.pi) * (x + 0.044715 * (x * x * x))))
    return x * cdf


def _modulated_norm_to(h_scr, x_ref, nw_ref, sc_ref, sh_ref):
    nw = nw_ref[...]
    sc = sc_ref[0]
    sh = sh_ref[0]

    step = NORM_ROWS * NORM_UNROLL

    def body(r, carry):
        rows = [pl.ds(pl.multiple_of(r * step + k * NORM_ROWS, NORM_ROWS), NORM_ROWS)
                for k in range(NORM_UNROLL)]
        scales = []
        for rw in rows:
            x = x_ref[rw, :]
            scales.append(lax.rsqrt(jnp.mean(x * x, axis=-1, keepdims=True) + EPS))
        for rw, rs in zip(rows, scales):
            y = x_ref[rw, :] * rs * nw
            h_scr[rw, :] = (y * (1.0 + sc) + sh).astype(h_scr.dtype)
        return carry

    lax.fori_loop(0, x_ref.shape[0] // step, body, 0)


def _row_conv3(p, cw, group):
    tm = p.shape[0]
    row = lax.broadcasted_iota(jnp.int32, (tm, 1), 0) % group
    prev = jnp.where(row == 0, 0.0, pltpu.roll(p, 1, 0))
    nxt = jnp.where(row == group - 1, 0.0, pltpu.roll(p, tm - 1, 0))
    return prev * cw[0:1] + p * cw[1:2] + nxt * cw[2:3]


def _adaln_kernel(c_ref, w_ref, b_ref, o_ref):
    a = _silu(c_ref[...]).astype(BF16)
    o_ref[0] = jnp.dot(a, w_ref[0].astype(BF16), preferred_element_type=F32) + b_ref[0]


def _adaln(cc, ada_w, ada_b):
    depth, d, n = ada_w.shape
    tn = 1024
    return pl.pallas_call(
        _adaln_kernel,
        grid=(depth, n // tn),
        in_specs=[pl.BlockSpec((8, d), lambda l, j: (0, 0)),
                  pl.BlockSpec((1, d, tn), lambda l, j: (l, 0, j)),
                  pl.BlockSpec((1, 1, tn), lambda l, j: (l, 0, j))],
        out_specs=pl.BlockSpec((1, 8, tn), lambda l, j: (l, 0, j)),
        out_shape=jax.ShapeDtypeStruct((depth, 8, n), F32),
        compiler_params=_params(("parallel", "parallel")),
        name="adaln",
    )(cc, ada_w, ada_b.reshape(depth, 1, n))


def _realign_kernel(a_ref, b_ref, o_ref, *, shift):
    both = jnp.concatenate([a_ref[...], b_ref[...]], axis=1)
    o_ref[...] = both[:, shift:shift + o_ref.shape[1]].astype(o_ref.dtype)


def _mixb_weights(w_in):
    d, n_all = w_in.shape
    col0 = GATE_COL0 + 4 * HEADS
    n = n_all - col0
    tn = 512
    shift = col0 % LANES
    base = col0 - shift
    return pl.pallas_call(
        functools.partial(_realign_kernel, shift=shift),
        grid=(n // tn,),
        in_specs=[pl.BlockSpec((d, tn), lambda j: (0, base // tn + j)),
                  pl.BlockSpec((d, LANES), lambda j: (0, (base + tn) // LANES + j * (tn // LANES)))],
        out_specs=pl.BlockSpec((d, tn), lambda j: (0, j)),
        out_shape=jax.ShapeDtypeStruct((d, n), BF16),
        compiler_params=_params(("parallel",)),
        name="mixb_weights",
    )(w_in, w_in)


def _qkv_kernel(x_ref, nw_ref, sh_ref, sc_ref, w_ref, wg_ref, cw_ref, gp_ref, qkv_ref, g_ref, h_scr,
                *, group, n_tiles):
    j = pl.program_id(1)

    @pl.when(j == 0)
    def _():
        _modulated_norm_to(h_scr, x_ref, nw_ref, sc_ref, sh_ref)

    @pl.when(j < n_tiles)
    def _():
        p = jnp.dot(h_scr[...], w_ref[...], preferred_element_type=F32)
        y = _silu(_row_conv3(p, cw_ref[...], group))
        for hh in range(y.shape[1] // HEAD_DIM):
            qkv_ref[hh] = y[:, hh * HEAD_DIM:(hh + 1) * HEAD_DIM]

    @pl.when(j == n_tiles)
    def _():
        raw = jnp.dot(h_scr[...], wg_ref[...], preferred_element_type=F32)
        z = raw + gp_ref[1:2]
        softplus = jnp.maximum(z, 0.0) + jnp.log1p(jnp.exp(-jnp.abs(z)))
        lane = lax.broadcasted_iota(jnp.int32, (1, LANES), 1)
        gates = jnp.where(lane < 2 * HEADS, -jnp.exp(gp_ref[0:1]) * softplus, _sigmoid(raw))
        ri = lax.broadcasted_iota(jnp.int32, (LANES, LANES), 0)
        ci = lax.broadcasted_iota(jnp.int32, (LANES, LANES), 1)
        same = (ri // CHUNK) == (ci // CHUNK)
        pre01 = jnp.where(same & (ci <= ri), 1.0, 0.0).astype(BF16)
        suf01 = jnp.where(same & (ci >= ri), 1.0, 0.0).astype(BF16)
        pieces = _split3_bf16(gates)
        for r0 in range(0, gates.shape[0], LANES):
            blk = [p[r0:r0 + LANES] for p in pieces]
            pre = [jnp.dot(pre01, b, preferred_element_type=F32) for b in blk]
            suf = [jnp.dot(suf01, b, preferred_element_type=F32) for b in blk]
            pre = (pre[0] + pre[1]) + pre[2]
            suf = (suf[0] + suf[1]) + suf[2]
            g_ref[r0:r0 + LANES] = jnp.where(lane < HEADS, pre,
                                             jnp.where(lane < 2 * HEADS, suf, gates[r0:r0 + LANES]))


def _qkv_proj(xin, mods, nw, w_in, cw, gate_par, *, rows_per_batch, mod_row0, group):
    r, d = xin.shape
    tm, tn = 1024, 768
    n_tiles = QKV_W // tn
    hpt = tn // HEAD_DIM
    tiles_per_batch = max(rows_per_batch // tm, 1)

    def mod_idx(part):
        if rows_per_batch >= tm:
            return lambda i, j: (mod_row0 + i // tiles_per_batch, 0, part)
        return lambda i, j: (mod_row0, 0, part)

    last = n_tiles - 1
    return pl.pallas_call(
        functools.partial(_qkv_kernel, group=group, n_tiles=n_tiles),
        grid=(r // tm, n_tiles + 1),
        in_specs=[pl.BlockSpec((tm, d), lambda i, j: (i, 0)),
                  pl.BlockSpec((1, d), lambda i, j: (0, 0)),
                  pl.BlockSpec((1, 1, d), mod_idx(0)),
                  pl.BlockSpec((1, 1, d), mod_idx(1)),
                  pl.BlockSpec((d, tn), lambda i, j: (0, jnp.minimum(j, last))),
                  pl.BlockSpec((d, LANES), lambda i, j: (0, GATE_COL0 // LANES)),
                  pl.BlockSpec((3, tn), lambda i, j: (0, jnp.minimum(j, last))),
                  pl.BlockSpec((2, LANES), lambda i, j: (0, 0))],
        out_specs=[pl.BlockSpec((hpt, tm, HEAD_DIM), lambda i, j: (jnp.minimum(j, last), i, 0)),
                   pl.BlockSpec((tm, LANES), lambda i, j: (i, 0))],
        out_shape=[jax.ShapeDtypeStruct((3 * HEADS, r, HEAD_DIM), F32),
                   jax.ShapeDtypeStruct((r, LANES), F32)],
        scratch_shapes=[pltpu.VMEM((tm, d), BF16)],
        compiler_params=_params(("parallel", "arbitrary")),
        name="qkv_proj",
    )(xin, nw, mods, mods, w_in, w_in, cw, gate_par)


def _mixb_kernel(x_ref, nw_ref, sh_ref, sc_ref, wz_ref, wb_ref, wc_ref, wh_ref, wzb_ref, cw_ref,
                 gz_ref, yb_ref, h_scr, *, group):
    j = pl.program_id(1)

    @pl.when(j == 0)
    def _():
        _modulated_norm_to(h_scr, x_ref, nw_ref, sc_ref, sh_ref)

    h = h_scr[...]
    dot = lambda w: jnp.dot(h, w[...], preferred_element_type=F32)
    gz_ref[...] = _silu(dot(wz_ref)).astype(gz_ref.dtype)
    conv = _row_conv3(dot(wc_ref) * dot(wh_ref), cw_ref[...], group)
    yb_ref[...] = (dot(wb_ref) * conv * _silu(dot(wzb_ref))).astype(yb_ref.dtype)


def _mixb_proj(xin, mods, nw, w_in, w_bchz, cw, *, rows_per_batch, mod_row0, group):
    r, d = xin.shape
    tm, tn = 1024, 256
    nj = d // tn
    za_blk0 = QKV_W // tn
    tiles_per_batch = max(rows_per_batch // tm, 1)

    def mod_idx(part):
        if rows_per_batch >= tm:
            return lambda i, j: (mod_row0 + i // tiles_per_batch, 0, part)
        return lambda i, j: (mod_row0, 0, part)

    wspec = lambda reg: pl.BlockSpec((d, tn), lambda i, j: (0, reg * nj + j))
    return pl.pallas_call(
        functools.partial(_mixb_kernel, group=group),
        grid=(r // tm, nj),
        in_specs=[pl.BlockSpec((tm, d), lambda i, j: (i, 0)),
                  pl.BlockSpec((1, d), lambda i, j: (0, 0)),
                  pl.BlockSpec((1, 1, d), mod_idx(0)),
                  pl.BlockSpec((1, 1, d), mod_idx(1)),
                  pl.BlockSpec((d, tn), lambda i, j: (0, za_blk0 + j)),
                  wspec(0), wspec(1), wspec(2), wspec(3),
                  pl.BlockSpec((3, tn), lambda i, j: (0, j))],
        out_specs=[pl.BlockSpec((tm, tn), lambda i, j: (i, j)),
                   pl.BlockSpec((tm, tn), lambda i, j: (i, j))],
        out_shape=[jax.ShapeDtypeStruct((r, d), BF16), jax.ShapeDtypeStruct((r, d), BF16)],
        scratch_shapes=[pltpu.VMEM((tm, d), BF16)],
        compiler_params=_params(("parallel", "arbitrary")),
        name="mixb_proj",
    )(xin, nw, mods, mods, w_in, w_bchz, w_bchz, w_bchz, w_bchz, cw)


def _split3_bf16(x):
    x1 = x.astype(BF16)
    r1 = x - x1.astype(F32)
    x2 = r1.astype(BF16)
    x3 = (r1 - x2.astype(F32)).astype(BF16)
    return x1, x2, x3


def _block_diag2(x, left):
    return jnp.concatenate([jnp.where(left, x, 0.0), jnp.where(left, 0.0, x)], axis=0).astype(BF16)


def _unit_tri_inverse_minus_eye(ms, left, ri, cj):
    n = ms[0].shape[0]
    ys = None
    s = 1
    while s < n:
        shift = s.bit_length() - 1
        br = lax.shift_right_logical(ri, shift)
        bc = lax.shift_right_logical(cj, shift)
        mask_lo = ((br & 1) == 1) & (bc == br - 1)
        mask_up = ((bc & 1) == 1) & (br == bc - 1)
        mask = (left & mask_lo) | (jnp.logical_not(left) & mask_up)
        cms = [jnp.where(mask, m, 0.0) for m in ms]
        if ys is None:
            ys = [-cm for cm in cms]
        else:
            yc = [jnp.dot(y.astype(BF16), _block_diag2(cm, left), preferred_element_type=F32)
                  for cm, y in zip(cms, ys)]
            yield
            ps = [cm + t for cm, t in zip(cms, yc)]
            py = [jnp.dot(p.astype(BF16), _block_diag2(y, left), preferred_element_type=F32)
                  for y, p in zip(ys, ps)]
            yield
            ys = [y - p - t for y, p, t in zip(ys, ps, py)]
        s *= 2
    return ys


def _run_interleaved(*gens):
    active = list(gens)
    while active:
        for g in list(active):
            try:
                next(g)
            except StopIteration:
                active.remove(g)


def _gdn_kernel(qc_ref, kc_ref, vc_ref, qx_ref, kx_ref, vx_ref, gc_ref, gx_ref,
                hn_ref, oc_ref, ox_ref,
                qs, ks, vs, gsc, bsc, nq_s, b_s, op_s, gl_s, oacc, st_scr):
    k = pl.program_id(0)
    h = jnp.minimum(k, pl.num_programs(0) - 2) % HEADS
    cur_slot = k % 2
    prev_slot = 1 - cur_slot
    tc = qc_ref.shape[1]
    tx = qx_ref.shape[1]
    t = tc + tx
    nsteps = t // CHUNK
    nc_ctx = tc // CHUNK
    c = CHUNK

    def l2n(a):
        return a * lax.rsqrt(jnp.sum(a * a, axis=-1, keepdims=True) + EPS)

    @pl.when(k == 0)
    def _():
        for ref in (nq_s, b_s, op_s, gl_s, oacc, st_scr):
            ref[...] = jnp.zeros_like(ref)

    qscale = HEAD_DIM ** -0.5
    qs[0:tc] = l2n(qc_ref[0]) * qscale
    qs[tc:t] = l2n(qx_ref[0]) * qscale
    ks[0:tc] = l2n(kc_ref[0])
    ks[tc:t] = l2n(kx_ref[0])
    vs[0:tc] = vc_ref[0]
    vs[tc:t] = vx_ref[0]

    lane = lax.broadcasted_iota(jnp.int32, (1, LANES), 1)
    for d in range(2):
        for ref, lo, hi in ((gc_ref, 0, tc), (gx_ref, tc, t)):
            gates = ref[...]
            g = jnp.sum(jnp.where(lane == d * HEADS + h, gates, 0.0), axis=-1, keepdims=True)
            b = jnp.sum(jnp.where(lane == 2 * HEADS + d * HEADS + h, gates, 0.0), axis=-1,
                        keepdims=True)
            gsc[d, lo:hi] = jnp.broadcast_to(g, (hi - lo, LANES))
            bsc[d, lo:hi] = jnp.broadcast_to(b, (hi - lo, LANES))

    def rev_chunk(s):
        return nc_ctx - 1 - s if s < nc_ctx else nsteps + nc_ctx - 1 - s

    ri = lax.broadcasted_iota(jnp.int32, (c, 2 * c), 0)
    lane2 = lax.broadcasted_iota(jnp.int32, (c, 2 * c), 1)
    left = lane2 < c
    cj = lane2 & (c - 1)
    strict = (left & (cj < ri)) | (jnp.logical_not(left) & (cj > ri))
    incl = (left & (cj <= ri)) | (jnp.logical_not(left) & (cj >= ri))
    left_sq = lax.broadcasted_iota(jnp.int32, (2 * c, 2 * c), 1) < c

    def prep(steps):
        units = range(len(steps))
        rows = [[slice(n * c, (n + 1) * c) for n in (s, rev_chunk(s))] for s in steps]
        qv = [[qs[r] for r in rows[uu]] for uu in units]
        kv = [[ks[r] for r in rows[uu]] for uu in units]
        vv = [[vs[r] for r in rows[uu]] for uu in units]
        beta = [[bsc[d, rows[uu][d]] for d in range(2)] for uu in units]
        kb = [[kv[uu][d] * beta[uu][d] for d in range(2)] for uu in units]
        gk = [lax.dot_general(
                  jnp.concatenate([kb[uu][0], qv[uu][0], kb[uu][1], qv[uu][1]], axis=0).astype(BF16),
                  jnp.concatenate(kv[uu], axis=0).astype(BF16),
                  (((1,), (1,)), ((), ())), preferred_element_type=F32)
              for uu in units]
        yield
        gcb = [[gsc[d, rows[uu][d]] for d in range(2)] for uu in units]
        ms, avs = [], []
        for uu in units:
            col = jnp.where(left, gcb[uu][0], gcb[uu][1])
            row = jnp.transpose(jnp.concatenate(gcb[uu], axis=0))[0:c]
            decay = jnp.exp(jnp.where(incl, col - row, 0.0))
            kk = jnp.where(left, gk[uu][0:c], gk[uu][2 * c:3 * c])
            qk = jnp.where(left, gk[uu][c:2 * c], gk[uu][3 * c:4 * c])
            ms.append(jnp.where(strict, kk * decay, 0.0))
            avs.append(jnp.where(incl, qk * decay, 0.0))
        ys = yield from _unit_tri_inverse_minus_eye(ms, left, ri, cj)
        egc = [[jnp.exp(g) for g in gcb[uu]] for uu in units]
        rs = [[jnp.concatenate([kb[uu][d] * egc[uu][d], vv[uu][d] * beta[uu][d]], axis=1)
               for d in range(2)] for uu in units]
        corr = [jnp.dot(_block_diag2(ys[uu], left),
                        jnp.concatenate(rs[uu], axis=0).astype(BF16), preferred_element_type=F32)
                for uu in units]
        yield
        wus, aks, g_edges = [], [], []
        for uu in units:
            kgs, edges = [], []
            for d in range(2):
                edge = c - 1 if d == 0 else 0
                edges.append(gcb[uu][d][edge:edge + 1])
                kgs.append(kv[uu][d] * jnp.exp(edges[d] - gcb[uu][d]))
            g_edges.append(edges)
            wus.append([rs[uu][d] + corr[uu][d * c:(d + 1) * c] for d in range(2)])
            kgt = jnp.transpose(jnp.concatenate(kgs, axis=0))
            aks.append(jnp.concatenate(
                [jnp.where(left, avs[uu], 0.0), jnp.where(left_sq, kgt, 0.0),
                 jnp.where(left, 0.0, avs[uu]), jnp.where(left_sq, 0.0, kgt)], axis=0).astype(BF16))
        prods = [jnp.dot(aks[uu], jnp.concatenate(wus[uu], axis=0).astype(BF16),
                         preferred_element_type=F32) for uu in units]
        yield
        for uu in units:
            for d in range(2):
                x = prods[uu][d * (c + HEAD_DIM):(d + 1) * (c + HEAD_DIM)]
                a_w, a_u = x[0:c, 0:HEAD_DIM], x[0:c, HEAD_DIM:2 * HEAD_DIM]
                k_w, k_u = x[c:c + HEAD_DIM, 0:HEAD_DIM], x[c:c + HEAD_DIM, HEAD_DIM:2 * HEAD_DIM]
                idx = 2 * steps[uu] + d
                nq_s[idx] = jnp.concatenate([-k_w, qv[uu][d] * egc[uu][d] - a_w],
                                            axis=0).astype(BF16)
                b_s[idx] = k_u
                op_s[idx] = a_u
                gl_s[idx] = jnp.broadcast_to(jnp.exp(g_edges[uu][d]), (8, LANES))

    def scan(steps, states, slot, written):
        for s in steps:
            idxs = (2 * s, 2 * s + 1)
            xs = [jnp.dot(nq_s[idx], st.astype(BF16), preferred_element_type=F32)
                  for idx, st in zip(idxs, states)]
            yield
            for n, idx, x in zip((s, rev_chunk(s)), idxs, xs):
                o = x[HEAD_DIM:HEAD_DIM + c] + op_s[idx]
                rows = slice(n * c, (n + 1) * c)
                oacc[slot, rows] = oacc[slot, rows] + o if n in written else o
                written.add(n)
            states[:] = [st * gl_s[idx][0:1] + x[0:HEAD_DIM] + b_s[idx]
                         for idx, st, x in zip(idxs, states, xs)]

    bounds = np.cumsum((0,) + SCAN_GROUPS)
    assert bounds[-1] == nsteps
    groups = [list(range(lo, hi)) for lo, hi in zip(bounds[:-1], bounds[1:])]
    early = [s for g in groups[:-1] for s in g]
    written_early = {n for s in early for n in (s, rev_chunk(s))}

    prev_states = [st_scr[0], st_scr[1]]
    _run_interleaved(prep(groups[0]), scan(groups[-1], prev_states, prev_slot, set(written_early)))
    o = oacc[prev_slot]
    on = o * lax.rsqrt(jnp.mean(o * o, axis=-1, keepdims=True) + EPS) * hn_ref[...]
    oc_ref[0] = on[0:tc].astype(oc_ref.dtype)
    ox_ref[0] = on[tc:t].astype(ox_ref.dtype)

    states = [jnp.zeros((HEAD_DIM, HEAD_DIM), F32)] * 2
    written = set()
    for gi in range(1, len(groups)):
        _run_interleaved(prep(groups[gi]), scan(groups[gi - 1], states, cur_slot, written))
    assert written == written_early
    st_scr[0] = states[0]
    st_scr[1] = states[1]


def _gdn(qkv_c, qkv_x, gates_c, gates_x, head_norm):
    tc, tx = CTX_LEN, SEQ
    t = tc + tx
    nidx = 2 * (t // CHUNK)
    nprob = BATCH * HEADS
    cur = lambda k: jnp.minimum(k, nprob - 1)
    prev = lambda k: jnp.maximum(k - 1, 0)
    part = lambda rows, p: pl.BlockSpec(
        (1, rows, HEAD_DIM), lambda k: (p * HEADS + cur(k) % HEADS, cur(k) // HEADS, 0))
    return pl.pallas_call(
        _gdn_kernel,
        grid=(nprob + 1,),
        in_specs=[part(tc, 0), part(tc, 1), part(tc, 2),
                  part(tx, 0), part(tx, 1), part(tx, 2),
                  pl.BlockSpec((tc, LANES), lambda k: (cur(k) // HEADS, 0)),
                  pl.BlockSpec((tx, LANES), lambda k: (cur(k) // HEADS, 0)),
                  pl.BlockSpec((1, HEAD_DIM), lambda k: (0, 0))],
        out_specs=[pl.BlockSpec((1, tc, HEAD_DIM), lambda k: (prev(k) // HEADS, 0, prev(k) % HEADS)),
                   pl.BlockSpec((1, tx, HEAD_DIM), lambda k: (prev(k) // HEADS, 0, prev(k) % HEADS))],
        out_shape=[jax.ShapeDtypeStruct((BATCH, tc, D_MODEL), BF16),
                   jax.ShapeDtypeStruct((BATCH, tx, D_MODEL), BF16)],
        scratch_shapes=[pltpu.VMEM((t, HEAD_DIM), F32), pltpu.VMEM((t, HEAD_DIM), F32),
                        pltpu.VMEM((t, HEAD_DIM), F32),
                        pltpu.VMEM((2, t, LANES), F32), pltpu.VMEM((2, t, LANES), F32),
                        pltpu.VMEM((nidx, HEAD_DIM + CHUNK, HEAD_DIM), BF16),
                        pltpu.VMEM((nidx, HEAD_DIM, HEAD_DIM), F32),
                        pltpu.VMEM((nidx, CHUNK, HEAD_DIM), F32),
                        pltpu.VMEM((nidx, 8, LANES), F32),
                        pltpu.VMEM((2, t, HEAD_DIM), F32),
                        pltpu.VMEM((2, HEAD_DIM, HEAD_DIM), F32)],
        compiler_params=_params(("arbitrary",)),
        name="gdn",
    )(qkv_c, qkv_c, qkv_c, qkv_x, qkv_x, qkv_x, gates_c, gates_x, head_norm)


def _even_out_kernel(x_ref, gt_ref, o_ref, gz_ref, yb_ref, w_ref, out_ref, a_scr):
    j = pl.program_id(1)

    @pl.when(j == 0)
    def _():
        d = o_ref.shape[1]
        a_scr[:, 0:d] = o_ref[...] * gz_ref[...]
        a_scr[:, d:2 * d] = yb_ref[...]

    y = jnp.dot(a_scr[...], w_ref[...], preferred_element_type=F32)
    out_ref[...] = x_ref[...] + gt_ref[0] * y


def _even_out(xin, mods, o, gz, yb, w_out, *, rows_per_batch, mod_row0):
    r, d = xin.shape
    tm = min(1024, rows_per_batch)
    tn = 512
    nj = d // tn
    tiles_per_batch = rows_per_batch // tm
    fixed = mod_row0 >= BATCH
    gt_idx = ((lambda i, j: (mod_row0, 0, 2 * nj + j)) if fixed
              else (lambda i, j: (i // tiles_per_batch, 0, 2 * nj + j)))
    return pl.pallas_call(
        _even_out_kernel,
        grid=(r // tm, nj),
        in_specs=[pl.BlockSpec((tm, tn), lambda i, j: (i, j)),
                  pl.BlockSpec((1, 1, tn), gt_idx),
                  pl.BlockSpec((tm, d), lambda i, j: (i, 0)),
                  pl.BlockSpec((tm, d), lambda i, j: (i, 0)),
                  pl.BlockSpec((tm, d), lambda i, j: (i, 0)),
                  pl.BlockSpec((2 * d, tn), lambda i, j: (0, j))],
        out_specs=pl.BlockSpec((tm, tn), lambda i, j: (i, j)),
        out_shape=jax.ShapeDtypeStruct((r, d), F32),
        scratch_shapes=[pltpu.VMEM((tm, 2 * d), BF16)],
        compiler_params=_params(("parallel", "arbitrary")),
        name="even_out",
    )(xin, mods, o, gz, yb, w_out)


def _odd_in_kernel(x_ref, nw_ref, sh_ref, sc_ref, w_ref, out_ref, h_scr, *, tiles_per_region):
    j = pl.program_id(1)

    @pl.when(j == 0)
    def _():
        _modulated_norm_to(h_scr, x_ref, nw_ref, sc_ref, sh_ref)

    region = j // tiles_per_region

    def tile(act):
        for lo in range(0, w_ref.shape[1], SUB_N):
            p = jnp.dot(h_scr[...], w_ref[:, lo:lo + SUB_N], preferred_element_type=F32)
            out_ref[:, lo:lo + SUB_N] = act(p).astype(out_ref.dtype)

    @pl.when(region <= 1)
    def _():
        tile(_gelu_tanh)

    @pl.when((region == 2) | (region == 4))
    def _():
        tile(_silu)

    @pl.when(region == 3)
    def _():
        tile(lambda p: p)


def _odd_in(xin, mods, nw, w):
    r, d = xin.shape
    n = w.shape[1]
    tm, tn = 1024, 1024
    tiles_per_batch = SEQ // tm
    mod_idx = lambda part: (lambda i, j: (i // tiles_per_batch, 0, part))
    return pl.pallas_call(
        functools.partial(_odd_in_kernel, tiles_per_region=d // tn),
        grid=(r // tm, n // tn),
        in_specs=[pl.BlockSpec((tm, d), lambda i, j: (i, 0)),
                  pl.BlockSpec((1, d), lambda i, j: (0, 0)),
                  pl.BlockSpec((1, 1, d), mod_idx(0)),
                  pl.BlockSpec((1, 1, d), mod_idx(1)),
                  pl.BlockSpec((d, tn), lambda i, j: (0, j))],
        out_specs=pl.BlockSpec((tm, tn), lambda i, j: (i, j)),
        out_shape=jax.ShapeDtypeStruct((r, n), BF16),
        scratch_shapes=[pltpu.VMEM((tm, d), BF16)],
        compiler_params=_params(("parallel", "arbitrary")),
        name="odd_in",
    )(xin, nw, mods, mods, w)


def _odd_mix_kernel(x_ref, gt_ref, u_ref, v_ref, zc_ref, pd_ref, zd_ref, lnw_ref, lnb_ref, ws_ref,
                    bs_ref, band_ref, icnt_ref, pw_ref, ps_ref, wo_ref, fnw_ref, out_ref,
                    acc, ln_scr, *, nk_c):
    kk = pl.program_id(1)
    tm = x_ref.shape[0]
    tk = u_ref.shape[1]
    gpb = tk // HEAD_DIM

    @pl.when(kk == 0)
    def _():
        v = v_ref[...].astype(F32)
        mu = jnp.mean(v, axis=-1, keepdims=True)
        xc = v - mu
        var = jnp.mean(xc * xc, axis=-1, keepdims=True)
        ln = (xc * lax.rsqrt(var + EPS) * lnw_ref[...] + lnb_ref[...]).astype(BF16)
        for blk in range(nk_c):
            ln_scr[blk] = ln[:, blk * tk:(blk + 1) * tk]
        acc[...] = jnp.zeros_like(acc)

    @pl.when(kk < nk_c)
    def _():
        ln = ln_scr[kk]
        cols = []
        for gi in range(gpb):
            w_g = ws_ref[gi]
            b_g = bs_ref[gi]
            rows = []
            for cc in range(tm // TOK_CHUNK):
                blk = ln[cc * TOK_CHUNK:(cc + 1) * TOK_CHUNK, gi * HEAD_DIM:(gi + 1) * HEAD_DIM]
                rows.append(jnp.dot(w_g, blk, preferred_element_type=F32) + b_g)
            cols.append(jnp.concatenate(rows, axis=0))
        s = jnp.concatenate(cols, axis=1)
        yc = u_ref[...] * s * zc_ref[...]
        acc[...] += jnp.dot(yc.astype(BF16), wo_ref[...], preferred_element_type=F32)

    @pl.when(kk >= nk_c)
    def _():
        pd = pd_ref[...]
        wsum = jnp.dot(band_ref[0], pd, preferred_element_type=F32)
        icnt = icnt_ref[0]
        icnt = jnp.concatenate([icnt] * (tk // LANES), axis=1)
        diff = wsum * icnt - pd.astype(F32)
        y = jnp.dot(diff.astype(BF16), pw_ref[0], preferred_element_type=F32)
        yd = y * ps_ref[...] * zd_ref[...]
        acc[...] += jnp.dot(yd.astype(BF16), wo_ref[...], preferred_element_type=F32)

    @pl.when(kk == pl.num_programs(1) - 1)
    def _():
        xn = x_ref[...] + gt_ref[0] * acc[...]
        ms = jnp.mean(xn * xn, axis=-1, keepdims=True)
        out_ref[...] = xn * lax.rsqrt(ms + EPS) * fnw_ref[...]


def _pool_constants(tm):
    bands = np.zeros((POOL_GROUPS, tm, tm), np.float32)
    icnt = np.zeros((POOL_GROUPS, tm, LANES), np.float32)
    pos = np.arange(tm)
    row = pos // GRID_W
    col = pos % GRID_W
    for g, rad in enumerate(POOL_RADII):
        same = row[:, None] == row[None, :]
        bands[g] = (same & (np.abs(col[:, None] - col[None, :]) <= rad)).astype(np.float32)
        cnt = np.minimum(col + rad + 1, GRID_W) - np.maximum(col - rad, 0)
        icnt[g] = (1.0 / cnt.astype(np.float32))[:, None]
    return jnp.asarray(bands, BF16), jnp.asarray(icnt, F32)


def _odd_mix(xin, mods, p, ln_w, ln_b, w_s, b_s, pool_w, pool_scale, w_out, fnw):
    r, d = xin.shape
    tm, tk = 512, 512
    nk_c = d // tk
    nk = 2 * nk_c
    gpb = tk // HEAD_DIM
    tiles_per_batch = SEQ // tm
    bands, icnt = _pool_constants(tm)
    lo = lambda kk: jnp.minimum(kk, nk_c - 1)
    hi = lambda kk: jnp.maximum(kk - nk_c, 0)
    cblk = d // tk
    return pl.pallas_call(
        functools.partial(_odd_mix_kernel, nk_c=nk_c),
        grid=(r // tm, nk),
        in_specs=[pl.BlockSpec((tm, d), lambda i, kk: (i, 0)),
                  pl.BlockSpec((1, 1, d), lambda i, kk: (i // tiles_per_batch, 0, 2)),
                  pl.BlockSpec((tm, tk), lambda i, kk: (i, lo(kk))),
                  pl.BlockSpec((tm, d), lambda i, kk: (i, 1)),
                  pl.BlockSpec((tm, tk), lambda i, kk: (i, 2 * cblk + lo(kk))),
                  pl.BlockSpec((tm, tk), lambda i, kk: (i, 3 * cblk + hi(kk))),
                  pl.BlockSpec((tm, tk), lambda i, kk: (i, 4 * cblk + hi(kk))),
                  pl.BlockSpec((1, d), lambda i, kk: (0, 0)),
                  pl.BlockSpec((1, d), lambda i, kk: (0, 0)),
                  pl.BlockSpec((gpb, TOK_CHUNK, TOK_CHUNK), lambda i, kk: (lo(kk), 0, 0)),
                  pl.BlockSpec((gpb, TOK_CHUNK, LANES), lambda i, kk: (lo(kk), 0, 0)),
                  pl.BlockSpec((1, tm, tm), lambda i, kk: (hi(kk), 0, 0)),
                  pl.BlockSpec((1, tm, LANES), lambda i, kk: (hi(kk), 0, 0)),
                  pl.BlockSpec((1, POOL_GW, POOL_GW), lambda i, kk: (hi(kk), 0, 0)),
                  pl.BlockSpec((1, tk), lambda i, kk: (0, hi(kk))),
                  pl.BlockSpec((tk, d), lambda i, kk: (kk, 0)),
                  pl.BlockSpec((1, d), lambda i, kk: (0, 0))],
        out_specs=pl.BlockSpec((tm, d), lambda i, kk: (i, 0)),
        out_shape=jax.ShapeDtypeStruct((r, d), F32),
        scratch_shapes=[pltpu.VMEM((tm, d), F32), pltpu.VMEM((nk_c, tm, tk), BF16)],
        compiler_params=_params(("parallel", "arbitrary")),
        name="odd_mix",
    )(xin, mods, p, p, p, p, p, ln_w, ln_b, w_s, b_s, bands, icnt, pool_w, pool_scale, w_out, fnw)


def kernel(x, c, ctx, c_ctx, ada_w, ada_b, norm_w, e_w_in, e_conv_qkv, e_a_log, e_dt_bias, e_head_norm,
           e_conv_b, e_w_out, o_w_in, o_ln_w, o_ln_b, o_w_s, o_b_s, o_pool_w, o_pool_scale, o_w_out,
           final_norm_w):
    bsz, seq, d = x.shape
    rx = bsz * seq
    rc = bsz * ctx.shape[1]
    x2 = x.reshape(rx, d)
    ctx2 = ctx.reshape(rc, d)

    cc = jnp.concatenate([c, c_ctx[None, :], jnp.zeros((8 - bsz - 1, d), F32)], axis=0)
    mods = _adaln(cc, ada_w, ada_b)
    ctx_row = bsz

    w_in = e_w_in[0, :, :GATE_COL0 + LANES].astype(BF16)
    w_bchz = _mixb_weights(e_w_in[0])
    m0 = mods[0].reshape(8, 1, 3 * d)
    nw0 = norm_w[0].reshape(1, d)

    gate_par = jnp.pad(jnp.stack([e_a_log[0].reshape(-1), e_dt_bias[0].reshape(-1)]),
                       ((0, 0), (0, LANES - 2 * HEADS)))
    qkv_x, gates_x = _qkv_proj(x2, m0, nw0, w_in, e_conv_qkv[0], gate_par,
                               rows_per_batch=seq, mod_row0=0, group=GRID_W)
    qkv_c, gates_c = _qkv_proj(ctx2, m0, nw0, w_in, e_conv_qkv[0], gate_par,
                               rows_per_batch=ctx.shape[1], mod_row0=ctx_row, group=ctx.shape[1])
    gz_x, yb_x = _mixb_proj(x2, m0, nw0, w_in, w_bchz, e_conv_b[0],
                            rows_per_batch=seq, mod_row0=0, group=GRID_W)
    gz_c, yb_c = _mixb_proj(ctx2, m0, nw0, w_in, w_bchz, e_conv_b[0],
                            rows_per_batch=ctx.shape[1], mod_row0=ctx_row, group=ctx.shape[1])
    o_c, o_x = _gdn(qkv_c, qkv_x, gates_c, gates_x,
                    e_head_norm[0].reshape(1, HEAD_DIM))
    w_out0 = e_w_out[0].astype(BF16)
    x2 = _even_out(x2, m0, o_x.reshape(rx, d), gz_x, yb_x, w_out0, rows_per_batch=seq, mod_row0=0)
    ctx2 = _even_out(ctx2, m0, o_c.reshape(rc, d), gz_c, yb_c, w_out0,
                     rows_per_batch=ctx.shape[1], mod_row0=ctx_row)

    m1 = mods[1].reshape(8, 1, 3 * d)
    p = _odd_in(x2, m1, norm_w[1].reshape(1, d), o_w_in[0].astype(BF16))
    b_s = jnp.broadcast_to(o_b_s[0][:, :, None], (o_b_s.shape[1], TOK_CHUNK, LANES))
    out = _odd_mix(x2, m1, p, o_ln_w[0].reshape(1, d), o_ln_b[0].reshape(1, d),
                   o_w_s[0].astype(BF16), b_s, o_pool_w[0].astype(BF16),
                   o_pool_scale[0].reshape(1, d), o_w_out[0].astype(BF16),
                   final_norm_w.reshape(1, d))
    return out.reshape(bsz, seq, d)
```

```python
import functools
import math

import numpy as np
import jax
import jax.numpy as jnp
from jax import lax
from jax.experimental import pallas as pl
from jax.experimental.pallas import tpu as pltpu

F32 = jnp.float32
BF16 = jnp.bfloat16

D_MODEL = 2048
BATCH = 4
SEQ = 2048
CTX_LEN = 256
GRID_W = 64
EPS = 1e-6
HEADS = 16
HEAD_DIM = 128
QKV_W = 3 * D_MODEL
GATE_COL0 = QKV_W + D_MODEL
CHUNK = 64
SCAN_GROUPS = (12, 12, 12)
TOK_CHUNK = 128
POOL_GROUPS = 4
POOL_GW = D_MODEL // POOL_GROUPS
POOL_RADII = (1, 2, 4, 8)

LANES = 128
VMEM_LIMIT = 56 * 1024 * 1024
NORM_ROWS = 16
NORM_UNROLL = 8
SUB_N = 256


def _params(sem):
    return pltpu.CompilerParams(dimension_semantics=sem, vmem_limit_bytes=VMEM_LIMIT)


def _sigmoid(x):
    return 1.0 / (1.0 + jnp.exp(-x))


def _silu(x):
    return x * _sigmoid(x)


def _gelu_tanh(x):
    cdf = 0.5 * (1.0 + jnp.tanh(math.sqrt(2.0 / math.pi) * (x + 0.044715 * (x * x * x))))
    return x * cdf


def _modulated_norm_to(h_scr, x_ref, nw_ref, sc_ref, sh_ref):
    nw = nw_ref[...]
    sc = sc_ref[0]
    sh = sh_ref[0]

    step = NORM_ROWS * NORM_UNROLL

    def body(r, carry):
        rows = [pl.ds(pl.multiple_of(r * step + k * NORM_ROWS, NORM_ROWS), NORM_ROWS)
                for k in range(NORM_UNROLL)]
        scales = []
        for rw in rows:
            x = x_ref[rw, :]
            scales.append(lax.rsqrt(jnp.mean(x * x, axis=-1, keepdims=True) + EPS))
        for rw, rs in zip(rows, scales):
            y = x_ref[rw, :] * rs * nw
            h_scr[rw, :] = (y * (1.0 + sc) + sh).astype(h_scr.dtype)
        return carry

    lax.fori_loop(0, x_ref.shape[0] // step, body, 0)


def _row_conv3(p, cw, group):
    tm = p.shape[0]
    row = lax.broadcasted_iota(jnp.int32, (tm, 1), 0) % group
    prev = jnp.where(row == 0, 0.0, pltpu.roll(p, 1, 0))
    nxt = jnp.where(row == group - 1, 0.0, pltpu.roll(p, tm - 1, 0))
    return prev * cw[0:1] + p * cw[1:2] + nxt * cw[2:3]


def _adaln_kernel(c_ref, w_ref, b_ref, o_ref):
    a = _silu(c_ref[...]).astype(BF16)
    o_ref[0] = jnp.dot(a, w_ref[0].astype(BF16), preferred_element_type=F32) + b_ref[0]


def _adaln(cc, ada_w, ada_b):
    depth, d, n = ada_w.shape
    tn = 1024
    return pl.pallas_call(
        _adaln_kernel,
        grid=(depth, n // tn),
        in_specs=[pl.BlockSpec((8, d), lambda l, j: (0, 0)),
                  pl.BlockSpec((1, d, tn), lambda l, j: (l, 0, j)),
                  pl.BlockSpec((1, 1, tn), lambda l, j: (l, 0, j))],
        out_specs=pl.BlockSpec((1, 8, tn), lambda l, j: (l, 0, j)),
        out_shape=jax.ShapeDtypeStruct((depth, 8, n), F32),
        compiler_params=_params(("parallel", "parallel")),
        name="adaln",
    )(cc, ada_w, ada_b.reshape(depth, 1, n))


def _realign_kernel(a_ref, b_ref, o_ref, *, shift):
    both = jnp.concatenate([a_ref[0], b_ref[0]], axis=1)
    o_ref[...] = both[:, shift:shift + o_ref.shape[1]].astype(o_ref.dtype)


def _mixb_weights(w_in):
    _, d, n_all = w_in.shape
    col0 = GATE_COL0 + 4 * HEADS
    n = n_all - col0
    tn = 512
    shift = col0 % LANES
    base = col0 - shift
    return pl.pallas_call(
        functools.partial(_realign_kernel, shift=shift),
        grid=(n // tn,),
        in_specs=[pl.BlockSpec((1, d, tn), lambda j: (0, 0, base // tn + j)),
                  pl.BlockSpec((1, d, LANES),
                               lambda j: (0, 0, (base + tn) // LANES + j * (tn // LANES)))],
        out_specs=pl.BlockSpec((d, tn), lambda j: (0, j)),
        out_shape=jax.ShapeDtypeStruct((d, n), BF16),
        compiler_params=_params(("parallel",)),
        name="mixb_weights",
    )(w_in, w_in)


def _qkv_kernel(x_ref, nw_ref, sh_ref, sc_ref, w_ref, wg_ref, cw_ref, gp_ref, qkv_ref, g_ref, h_scr,
                *, group, n_tiles):
    j = pl.program_id(1)

    @pl.when(j == 0)
    def _():
        _modulated_norm_to(h_scr, x_ref, nw_ref, sc_ref, sh_ref)

    @pl.when(j < n_tiles)
    def _():
        p = jnp.dot(h_scr[...], w_ref[...], preferred_element_type=F32)
        y = _silu(_row_conv3(p, cw_ref[...], group))
        for hh in range(y.shape[1] // HEAD_DIM):
            qkv_ref[hh] = y[:, hh * HEAD_DIM:(hh + 1) * HEAD_DIM]

    @pl.when(j == n_tiles)
    def _():
        raw = jnp.dot(h_scr[...], wg_ref[...], preferred_element_type=F32)
        z = raw + gp_ref[1:2]
        softplus = jnp.maximum(z, 0.0) + jnp.log1p(jnp.exp(-jnp.abs(z)))
        lane = lax.broadcasted_iota(jnp.int32, (1, LANES), 1)
        gates = jnp.where(lane < 2 * HEADS, -jnp.exp(gp_ref[0:1]) * softplus, _sigmoid(raw))
        ri = lax.broadcasted_iota(jnp.int32, (LANES, LANES), 0)
        ci = lax.broadcasted_iota(jnp.int32, (LANES, LANES), 1)
        same = (ri // CHUNK) == (ci // CHUNK)
        pre01 = jnp.where(same & (ci <= ri), 1.0, 0.0).astype(BF16)
        suf01 = jnp.where(same & (ci >= ri), 1.0, 0.0).astype(BF16)
        pieces = _split3_bf16(gates)
        for r0 in range(0, gates.shape[0], LANES):
            blk = [p[r0:r0 + LANES] for p in pieces]
            pre = [jnp.dot(pre01, b, preferred_element_type=F32) for b in blk]
            suf = [jnp.dot(suf01, b, preferred_element_type=F32) for b in blk]
            pre = (pre[0] + pre[1]) + pre[2]
            suf = (suf[0] + suf[1]) + suf[2]
            g_ref[r0:r0 + LANES] = jnp.where(lane < HEADS, pre,
                                             jnp.where(lane < 2 * HEADS, suf, gates[r0:r0 + LANES]))


def _qkv_proj(xin, mods, nw, w_in, cw, gate_par, *, rows_per_batch, mod_row0, group):
    r, d = xin.shape
    tm, tn = 1024, 768
    n_tiles = QKV_W // tn
    hpt = tn // HEAD_DIM
    tiles_per_batch = max(rows_per_batch // tm, 1)

    def mod_idx(part):
        if rows_per_batch >= tm:
            return lambda i, j: (mod_row0 + i // tiles_per_batch, 0, part)
        return lambda i, j: (mod_row0, 0, part)

    last = n_tiles - 1
    return pl.pallas_call(
        functools.partial(_qkv_kernel, group=group, n_tiles=n_tiles),
        grid=(r // tm, n_tiles + 1),
        in_specs=[pl.BlockSpec((tm, d), lambda i, j: (i, 0)),
                  pl.BlockSpec((1, d), lambda i, j: (0, 0)),
                  pl.BlockSpec((1, 1, d), mod_idx(0)),
                  pl.BlockSpec((1, 1, d), mod_idx(1)),
                  pl.BlockSpec((d, tn), lambda i, j: (0, jnp.minimum(j, last))),
                  pl.BlockSpec((d, LANES), lambda i, j: (0, GATE_COL0 // LANES)),
                  pl.BlockSpec((3, tn), lambda i, j: (0, jnp.minimum(j, last))),
                  pl.BlockSpec((2, LANES), lambda i, j: (0, 0))],
        out_specs=[pl.BlockSpec((hpt, tm, HEAD_DIM), lambda i, j: (jnp.minimum(j, last), i, 0)),
                   pl.BlockSpec((tm, LANES), lambda i, j: (i, 0))],
        out_shape=[jax.ShapeDtypeStruct((3 * HEADS, r, HEAD_DIM), F32),
                   jax.ShapeDtypeStruct((r, LANES), F32)],
        scratch_shapes=[pltpu.VMEM((tm, d), BF16)],
        compiler_params=_params(("parallel", "arbitrary")),
        name="qkv_proj",
    )(xin, nw, mods, mods, w_in, w_in, cw, gate_par)


def _mixb_kernel(x_ref, nw_ref, sh_ref, sc_ref, wz_ref, wb_ref, wc_ref, wh_ref, wzb_ref, cw_ref,
                 gz_ref, yb_ref, h_scr, *, group):
    j = pl.program_id(1)

    @pl.when(j == 0)
    def _():
        _modulated_norm_to(h_scr, x_ref, nw_ref, sc_ref, sh_ref)

    h = h_scr[...]
    dot = lambda w: jnp.dot(h, w[...], preferred_element_type=F32)
    gz_ref[...] = _silu(dot(wz_ref)).astype(gz_ref.dtype)
    conv = _row_conv3(dot(wc_ref) * dot(wh_ref), cw_ref[...], group)
    yb_ref[...] = (dot(wb_ref) * conv * _silu(dot(wzb_ref))).astype(yb_ref.dtype)


def _mixb_proj(xin, mods, nw, w_in, w_bchz, cw, *, rows_per_batch, mod_row0, group):
    r, d = xin.shape
    tm, tn = 1024, 256
    nj = d // tn
    za_blk0 = QKV_W // tn
    tiles_per_batch = max(rows_per_batch // tm, 1)

    def mod_idx(part):
        if rows_per_batch >= tm:
            return lambda i, j: (mod_row0 + i // tiles_per_batch, 0, part)
        return lambda i, j: (mod_row0, 0, part)

    wspec = lambda reg: pl.BlockSpec((d, tn), lambda i, j: (0, reg * nj + j))
    return pl.pallas_call(
        functools.partial(_mixb_kernel, group=group),
        grid=(r // tm, nj),
        in_specs=[pl.BlockSpec((tm, d), lambda i, j: (i, 0)),
                  pl.BlockSpec((1, d), lambda i, j: (0, 0)),
                  pl.BlockSpec((1, 1, d), mod_idx(0)),
                  pl.BlockSpec((1, 1, d), mod_idx(1)),
                  pl.BlockSpec((d, tn), lambda i, j: (0, za_blk0 + j)),
                  wspec(0), wspec(1), wspec(2), wspec(3),
                  pl.BlockSpec((3, tn), lambda i, j: (0, j))],
        out_specs=[pl.BlockSpec((tm, tn), lambda i, j: (i, j)),
                   pl.BlockSpec((tm, tn), lambda i, j: (i, j))],
        out_shape=[jax.ShapeDtypeStruct((r, d), BF16), jax.ShapeDtypeStruct((r, d), BF16)],
        scratch_shapes=[pltpu.VMEM((tm, d), BF16)],
        compiler_params=_params(("parallel", "arbitrary")),
        name="mixb_proj",
    )(xin, nw, mods, mods, w_in, w_bchz, w_bchz, w_bchz, w_bchz, cw)


def _split3_bf16(x):
    x1 = x.astype(BF16)
    r1 = x - x1.astype(F32)
    x2 = r1.astype(BF16)
    x3 = (r1 - x2.astype(F32)).astype(BF16)
    return x1, x2, x3


def _block_diag2(x, left):
    return jnp.concatenate([jnp.where(left, x, 0.0), jnp.where(left, 0.0, x)], axis=0).astype(BF16)


def _unit_tri_inverse_minus_eye(ms, left, ri, cj):
    n = ms[0].shape[0]
    ys = None
    s = 1
    while s < n:
        shift = s.bit_length() - 1
        br = lax.shift_right_logical(ri, shift)
        bc = lax.shift_right_logical(cj, shift)
        mask_lo = ((br & 1) == 1) & (bc == br - 1)
        mask_up = ((bc & 1) == 1) & (br == bc - 1)
        mask = (left & mask_lo) | (jnp.logical_not(left) & mask_up)
        cms = [jnp.where(mask, m, 0.0) for m in ms]
        if ys is None:
            ys = [-cm for cm in cms]
        else:
            yc = [jnp.dot(y.astype(BF16), _block_diag2(cm, left), preferred_element_type=F32)
                  for cm, y in zip(cms, ys)]
            yield
            ps = [cm + t for cm, t in zip(cms, yc)]
            py = [jnp.dot(p.astype(BF16), _block_diag2(y, left), preferred_element_type=F32)
                  for y, p in zip(ys, ps)]
            yield
            ys = [y - p - t for y, p, t in zip(ys, ps, py)]
        s *= 2
    return ys


def _run_interleaved(*gens):
    active = list(gens)
    while active:
        for g in list(active):
            try:
                next(g)
            except StopIteration:
                active.remove(g)


def _gdn_kernel(qc_ref, kc_ref, vc_ref, qx_ref, kx_ref, vx_ref, gc_ref, gx_ref,
                hn_ref, oc_ref, ox_ref,
                qs, ks, vs, gsc, bsc, nq_s, b_s, op_s, gl_s, oacc, st_scr):
    k = pl.program_id(0)
    h = jnp.minimum(k, pl.num_programs(0) - 2) % HEADS
    cur_slot = k % 2
    prev_slot = 1 - cur_slot
    tc = qc_ref.shape[1]
    tx = qx_ref.shape[1]
    t = tc + tx
    nsteps = t // CHUNK
    nc_ctx = tc // CHUNK
    c = CHUNK

    def l2n(a):
        return a * lax.rsqrt(jnp.sum(a * a, axis=-1, keepdims=True) + EPS)

    @pl.when(k == 0)
    def _():
        for ref in (nq_s, b_s, op_s, gl_s, oacc, st_scr):
            ref[...] = jnp.zeros_like(ref)

    qscale = HEAD_DIM ** -0.5
    qs[0:tc] = l2n(qc_ref[0]) * qscale
    qs[tc:t] = l2n(qx_ref[0]) * qscale
    ks[0:tc] = l2n(kc_ref[0])
    ks[tc:t] = l2n(kx_ref[0])
    vs[0:tc] = vc_ref[0]
    vs[tc:t] = vx_ref[0]

    lane = lax.broadcasted_iota(jnp.int32, (1, LANES), 1)
    for d in range(2):
        for ref, lo, hi in ((gc_ref, 0, tc), (gx_ref, tc, t)):
            gates = ref[...]
            g = jnp.sum(jnp.where(lane == d * HEADS + h, gates, 0.0), axis=-1, keepdims=True)
            b = jnp.sum(jnp.where(lane == 2 * HEADS + d * HEADS + h, gates, 0.0), axis=-1,
                        keepdims=True)
            gsc[d, lo:hi] = jnp.broadcast_to(g, (hi - lo, LANES))
            bsc[d, lo:hi] = jnp.broadcast_to(b, (hi - lo, LANES))

    def rev_chunk(s):
        return nc_ctx - 1 - s if s < nc_ctx else nsteps + nc_ctx - 1 - s

    ri = lax.broadcasted_iota(jnp.int32, (c, 2 * c), 0)
    lane2 = lax.broadcasted_iota(jnp.int32, (c, 2 * c), 1)
    left = lane2 < c
    cj = lane2 & (c - 1)
    strict = (left & (cj < ri)) | (jnp.logical_not(left) & (cj > ri))
    incl = (left & (cj <= ri)) | (jnp.logical_not(left) & (cj >= ri))
    left_sq = lax.broadcasted_iota(jnp.int32, (2 * c, 2 * c), 1) < c

    def prep(steps):
        units = range(len(steps))
        rows = [[slice(n * c, (n + 1) * c) for n in (s, rev_chunk(s))] for s in steps]
        qv = [[qs[r] for r in rows[uu]] for uu in units]
        kv = [[ks[r] for r in rows[uu]] for uu in units]
        vv = [[vs[r] for r in rows[uu]] for uu in units]
        beta = [[bsc[d, rows[uu][d]] for d in range(2)] for uu in units]
        kb = [[kv[uu][d] * beta[uu][d] for d in range(2)] for uu in units]
        gk = [lax.dot_general(
                  jnp.concatenate([kb[uu][0], qv[uu][0], kb[uu][1], qv[uu][1]], axis=0).astype(BF16),
                  jnp.concatenate(kv[uu], axis=0).astype(BF16),
                  (((1,), (1,)), ((), ())), preferred_element_type=F32)
              for uu in units]
        yield
        gcb = [[gsc[d, rows[uu][d]] for d in range(2)] for uu in units]
        ms, avs = [], []
        for uu in units:
            col = jnp.where(left, gcb[uu][0], gcb[uu][1])
            row = jnp.transpose(jnp.concatenate(gcb[uu], axis=0))[0:c]
            decay = jnp.exp(jnp.where(incl, col - row, 0.0))
            kk = jnp.where(left, gk[uu][0:c], gk[uu][2 * c:3 * c])
            qk = jnp.where(left, gk[uu][c:2 * c], gk[uu][3 * c:4 * c])
            ms.append(jnp.where(strict, kk * decay, 0.0))
            avs.append(jnp.where(incl, qk * decay, 0.0))
        ys = yield from _unit_tri_inverse_minus_eye(ms, left, ri, cj)
        egc = [[jnp.exp(g) for g in gcb[uu]] for uu in units]
        rs = [[jnp.concatenate([kb[uu][d] * egc[uu][d], vv[uu][d] * beta[uu][d]], axis=1)
               for d in range(2)] for uu in units]
        corr = [jnp.dot(_block_diag2(ys[uu], left),
                        jnp.concatenate(rs[uu], axis=0).astype(BF16), preferred_element_type=F32)
                for uu in units]
        yield
        wus, aks, g_edges = [], [], []
        for uu in units:
            kgs, edges = [], []
            for d in range(2):
                edge = c - 1 if d == 0 else 0
                edges.append(gcb[uu][d][edge:edge + 1])
                kgs.append(kv[uu][d] * jnp.exp(edges[d] - gcb[uu][d]))
            g_edges.append(edges)
            wus.append([rs[uu][d] + corr[uu][d * c:(d + 1) * c] for d in range(2)])
            kgt = jnp.transpose(jnp.concatenate(kgs, axis=0))
            aks.append(jnp.concatenate(
                [jnp.where(left, avs[uu], 0.0), jnp.where(left_sq, kgt, 0.0),
                 jnp.where(left, 0.0, avs[uu]), jnp.where(left_sq, 0.0, kgt)], axis=0).astype(BF16))
        prods = [jnp.dot(aks[uu], jnp.concatenate(wus[uu], axis=0).astype(BF16),
                         preferred_element_type=F32) for uu in units]
        yield
        for uu in units:
            for d in range(2):
                x = prods[uu][d * (c + HEAD_DIM):(d + 1) * (c + HEAD_DIM)]
                a_w, a_u = x[0:c, 0:HEAD_DIM], x[0:c, HEAD_DIM:2 * HEAD_DIM]
                k_w, k_u = x[c:c + HEAD_DIM, 0:HEAD_DIM], x[c:c + HEAD_DIM, HEAD_DIM:2 * HEAD_DIM]
                idx = 2 * steps[uu] + d
                nq_s[idx] = jnp.concatenate([-k_w, qv[uu][d] * egc[uu][d] - a_w],
                                            axis=0).astype(BF16)
                b_s[idx] = k_u
                op_s[idx] = a_u
                gl_s[idx] = jnp.broadcast_to(jnp.exp(g_edges[uu][d]), (8, LANES))

    def scan(steps, states, slot, written):
        for s in steps:
            idxs = (2 * s, 2 * s + 1)
            xs = [jnp.dot(nq_s[idx], st.astype(BF16), preferred_element_type=F32)
                  for idx, st in zip(idxs, states)]
            yield
            for n, idx, x in zip((s, rev_chunk(s)), idxs, xs):
                o = x[HEAD_DIM:HEAD_DIM + c] + op_s[idx]
                rows = slice(n * c, (n + 1) * c)
                oacc[slot, rows] = oacc[slot, rows] + o if n in written else o
                written.add(n)
            states[:] = [st * gl_s[idx][0:1] + x[0:HEAD_DIM] + b_s[idx]
                         for idx, st, x in zip(idxs, states, xs)]

    bounds = np.cumsum((0,) + SCAN_GROUPS)
    assert bounds[-1] == nsteps
    groups = [list(range(lo, hi)) for lo, hi in zip(bounds[:-1], bounds[1:])]
    early = [s for g in groups[:-1] for s in g]
    written_early = {n for s in early for n in (s, rev_chunk(s))}

    prev_states = [st_scr[0], st_scr[1]]
    _run_interleaved(prep(groups[0]), scan(groups[-1], prev_states, prev_slot, set(written_early)))
    o = oacc[prev_slot]
    on = o * lax.rsqrt(jnp.mean(o * o, axis=-1, keepdims=True) + EPS) * hn_ref[...]
    oc_ref[0] = on[0:tc].astype(oc_ref.dtype)
    ox_ref[0] = on[tc:t].astype(ox_ref.dtype)

    states = [jnp.zeros((HEAD_DIM, HEAD_DIM), F32)] * 2
    written = set()
    for gi in range(1, len(groups)):
        _run_interleaved(prep(groups[gi]), scan(groups[gi - 1], states, cur_slot, written))
    assert written == written_early
    st_scr[0] = states[0]
    st_scr[1] = states[1]


def _gdn(qkv_c, qkv_x, gates_c, gates_x, head_norm):
    tc, tx = CTX_LEN, SEQ
    t = tc + tx
    nidx = 2 * (t // CHUNK)
    nprob = BATCH * HEADS
    cur = lambda k: jnp.minimum(k, nprob - 1)
    prev = lambda k: jnp.maximum(k - 1, 0)
    part = lambda rows, p: pl.BlockSpec(
        (1, rows, HEAD_DIM), lambda k: (p * HEADS + cur(k) % HEADS, cur(k) // HEADS, 0))
    return pl.pallas_call(
        _gdn_kernel,
        grid=(nprob + 1,),
        in_specs=[part(tc, 0), part(tc, 1), part(tc, 2),
                  part(tx, 0), part(tx, 1), part(tx, 2),
                  pl.BlockSpec((tc, LANES), lambda k: (cur(k) // HEADS, 0)),
                  pl.BlockSpec((tx, LANES), lambda k: (cur(k) // HEADS, 0)),
                  pl.BlockSpec((1, HEAD_DIM), lambda k: (0, 0))],
        out_specs=[pl.BlockSpec((1, tc, HEAD_DIM), lambda k: (prev(k) // HEADS, 0, prev(k) % HEADS)),
                   pl.BlockSpec((1, tx, HEAD_DIM), lambda k: (prev(k) // HEADS, 0, prev(k) % HEADS))],
        out_shape=[jax.ShapeDtypeStruct((BATCH, tc, D_MODEL), BF16),
                   jax.ShapeDtypeStruct((BATCH, tx, D_MODEL), BF16)],
        scratch_shapes=[pltpu.VMEM((t, HEAD_DIM), F32), pltpu.VMEM((t, HEAD_DIM), F32),
                        pltpu.VMEM((t, HEAD_DIM), F32),
                        pltpu.VMEM((2, t, LANES), F32), pltpu.VMEM((2, t, LANES), F32),
                        pltpu.VMEM((nidx, HEAD_DIM + CHUNK, HEAD_DIM), BF16),
                        pltpu.VMEM((nidx, HEAD_DIM, HEAD_DIM), F32),
                        pltpu.VMEM((nidx, CHUNK, HEAD_DIM), F32),
                        pltpu.VMEM((nidx, 8, LANES), F32),
                        pltpu.VMEM((2, t, HEAD_DIM), F32),
                        pltpu.VMEM((2, HEAD_DIM, HEAD_DIM), F32)],
        compiler_params=_params(("arbitrary",)),
        name="gdn",
    )(qkv_c, qkv_c, qkv_c, qkv_x, qkv_x, qkv_x, gates_c, gates_x, head_norm)


def _even_out_kernel(x_ref, gt_ref, o_ref, gz_ref, yb_ref, w_ref, out_ref, a_scr):
    j = pl.program_id(1)

    @pl.when(j == 0)
    def _():
        d = o_ref.shape[1]
        a_scr[:, 0:d] = o_ref[...] * gz_ref[...]
        a_scr[:, d:2 * d] = yb_ref[...]

    y = jnp.dot(a_scr[...], w_ref[...], preferred_element_type=F32)
    out_ref[...] = x_ref[...] + gt_ref[0] * y


def _even_out(xin, mods, o, gz, yb, w_out, *, rows_per_batch, mod_row0):
    r, d = xin.shape
    tm = min(1024, rows_per_batch)
    tn = 512
    nj = d // tn
    tiles_per_batch = rows_per_batch // tm
    fixed = mod_row0 >= BATCH
    gt_idx = ((lambda i, j: (mod_row0, 0, 2 * nj + j)) if fixed
              else (lambda i, j: (i // tiles_per_batch, 0, 2 * nj + j)))
    return pl.pallas_call(
        _even_out_kernel,
        grid=(r // tm, nj),
        in_specs=[pl.BlockSpec((tm, tn), lambda i, j: (i, j)),
                  pl.BlockSpec((1, 1, tn), gt_idx),
                  pl.BlockSpec((tm, d), lambda i, j: (i, 0)),
                  pl.BlockSpec((tm, d), lambda i, j: (i, 0)),
                  pl.BlockSpec((tm, d), lambda i, j: (i, 0)),
                  pl.BlockSpec((2 * d, tn), lambda i, j: (0, j))],
        out_specs=pl.BlockSpec((tm, tn), lambda i, j: (i, j)),
        out_shape=jax.ShapeDtypeStruct((r, d), F32),
        scratch_shapes=[pltpu.VMEM((tm, 2 * d), BF16)],
        compiler_params=_params(("parallel", "arbitrary")),
        name="even_out",
    )(xin, mods, o, gz, yb, w_out)


def _odd_in_kernel(x_ref, nw_ref, sh_ref, sc_ref, w_ref, out_ref, h_scr, *, tiles_per_region):
    j = pl.program_id(1)

    @pl.when(j == 0)
    def _():
        _modulated_norm_to(h_scr, x_ref, nw_ref, sc_ref, sh_ref)

    region = j // tiles_per_region

    def tile(act):
        for lo in range(0, w_ref.shape[1], SUB_N):
            p = jnp.dot(h_scr[...], w_ref[:, lo:lo + SUB_N], preferred_element_type=F32)
            out_ref[:, lo:lo + SUB_N] = act(p).astype(out_ref.dtype)

    @pl.when(region <= 1)
    def _():
        tile(_gelu_tanh)

    @pl.when((region == 2) | (region == 4))
    def _():
        tile(_silu)

    @pl.when(region == 3)
    def _():
        tile(lambda p: p)


def _odd_in(xin, mods, nw, w):
    r, d = xin.shape
    n = w.shape[1]
    tm, tn = 1024, 1024
    tiles_per_batch = SEQ // tm
    mod_idx = lambda part: (lambda i, j: (i // tiles_per_batch, 0, part))
    return pl.pallas_call(
        functools.partial(_odd_in_kernel, tiles_per_region=d // tn),
        grid=(r // tm, n // tn),
        in_specs=[pl.BlockSpec((tm, d), lambda i, j: (i, 0)),
                  pl.BlockSpec((1, d), lambda i, j: (0, 0)),
                  pl.BlockSpec((1, 1, d), mod_idx(0)),
                  pl.BlockSpec((1, 1, d), mod_idx(1)),
                  pl.BlockSpec((d, tn), lambda i, j: (0, j))],
        out_specs=pl.BlockSpec((tm, tn), lambda i, j: (i, j)),
        out_shape=jax.ShapeDtypeStruct((r, n), BF16),
        scratch_shapes=[pltpu.VMEM((tm, d), BF16)],
        compiler_params=_params(("parallel", "arbitrary")),
        name="odd_in",
    )(xin, nw, mods, mods, w)


def _odd_mix_kernel(x_ref, gt_ref, u_ref, v_ref, zc_ref, pd_ref, zd_ref, lnw_ref, lnb_ref, ws_ref,
                    bs_ref, band_ref, icnt_ref, pw_ref, ps_ref, wo_ref, fnw_ref, out_ref,
                    acc, ln_scr, *, nk_c):
    kk = pl.program_id(1)
    tm = x_ref.shape[0]
    tk = u_ref.shape[1]
    gpb = tk // HEAD_DIM

    @pl.when(kk == 0)
    def _():
        v = v_ref[...].astype(F32)
        mu = jnp.mean(v, axis=-1, keepdims=True)
        xc = v - mu
        var = jnp.mean(xc * xc, axis=-1, keepdims=True)
        ln = (xc * lax.rsqrt(var + EPS) * lnw_ref[...] + lnb_ref[...]).astype(BF16)
        for blk in range(nk_c):
            ln_scr[blk] = ln[:, blk * tk:(blk + 1) * tk]
        acc[...] = jnp.zeros_like(acc)

    @pl.when(kk < nk_c)
    def _():
        ln = ln_scr[kk]
        cols = []
        for gi in range(gpb):
            w_g = ws_ref[gi]
            b_g = bs_ref[gi]
            rows = []
            for cc in range(tm // TOK_CHUNK):
                blk = ln[cc * TOK_CHUNK:(cc + 1) * TOK_CHUNK, gi * HEAD_DIM:(gi + 1) * HEAD_DIM]
                rows.append(jnp.dot(w_g, blk, preferred_element_type=F32) + b_g)
            cols.append(jnp.concatenate(rows, axis=0))
        s = jnp.concatenate(cols, axis=1)
        yc = u_ref[...] * s * zc_ref[...]
        acc[...] += jnp.dot(yc.astype(BF16), wo_ref[...], preferred_element_type=F32)

    @pl.when(kk >= nk_c)
    def _():
        pd = pd_ref[...]
        wsum = jnp.dot(band_ref[0], pd, preferred_element_type=F32)
        icnt = icnt_ref[0]
        icnt = jnp.concatenate([icnt] * (tk // LANES), axis=1)
        diff = wsum * icnt - pd.astype(F32)
        y = jnp.dot(diff.astype(BF16), pw_ref[0], preferred_element_type=F32)
        yd = y * ps_ref[...] * zd_ref[...]
        acc[...] += jnp.dot(yd.astype(BF16), wo_ref[...], preferred_element_type=F32)

    @pl.when(kk == pl.num_programs(1) - 1)
    def _():
        xn = x_ref[...] + gt_ref[0] * acc[...]
        ms = jnp.mean(xn * xn, axis=-1, keepdims=True)
        out_ref[...] = xn * lax.rsqrt(ms + EPS) * fnw_ref[...]


def _pool_constants(tm):
    bands = np.zeros((POOL_GROUPS, tm, tm), np.float32)
    icnt = np.zeros((POOL_GROUPS, tm, LANES), np.float32)
    pos = np.arange(tm)
    row = pos // GRID_W
    col = pos % GRID_W
    for g, rad in enumerate(POOL_RADII):
        same = row[:, None] == row[None, :]
        bands[g] = (same & (np.abs(col[:, None] - col[None, :]) <= rad)).astype(np.float32)
        cnt = np.minimum(col + rad + 1, GRID_W) - np.maximum(col - rad, 0)
        icnt[g] = (1.0 / cnt.astype(np.float32))[:, None]
    return jnp.asarray(bands, BF16), jnp.asarray(icnt, F32)


def _odd_mix(xin, mods, p, ln_w, ln_b, w_s, b_s, pool_w, pool_scale, w_out, fnw):
    r, d = xin.shape
    tm, tk = 512, 512
    nk_c = d // tk
    nk = 2 * nk_c
    gpb = tk // HEAD_DIM
    tiles_per_batch = SEQ // tm
    bands, icnt = _pool_constants(tm)
    lo = lambda kk: jnp.minimum(kk, nk_c - 1)
    hi = lambda kk: jnp.maximum(kk - nk_c, 0)
    cblk = d // tk
    return pl.pallas_call(
        functools.partial(_odd_mix_kernel, nk_c=nk_c),
        grid=(r // tm, nk),
        in_specs=[pl.BlockSpec((tm, d), lambda i, kk: (i, 0)),
                  pl.BlockSpec((1, 1, d), lambda i, kk: (i // tiles_per_batch, 0, 2)),
                  pl.BlockSpec((tm, tk), lambda i, kk: (i, lo(kk))),
                  pl.BlockSpec((tm, d), lambda i, kk: (i, 1)),
                  pl.BlockSpec((tm, tk), lambda i, kk: (i, 2 * cblk + lo(kk))),
                  pl.BlockSpec((tm, tk), lambda i, kk: (i, 3 * cblk + hi(kk))),
                  pl.BlockSpec((tm, tk), lambda i, kk: (i, 4 * cblk + hi(kk))),
                  pl.BlockSpec((1, d), lambda i, kk: (0, 0)),
                  pl.BlockSpec((1, d), lambda i, kk: (0, 0)),
                  pl.BlockSpec((gpb, TOK_CHUNK, TOK_CHUNK), lambda i, kk: (lo(kk), 0, 0)),
                  pl.BlockSpec((gpb, TOK_CHUNK, LANES), lambda i, kk: (lo(kk), 0, 0)),
                  pl.BlockSpec((1, tm, tm), lambda i, kk: (hi(kk), 0, 0)),
                  pl.BlockSpec((1, tm, LANES), lambda i, kk: (hi(kk), 0, 0)),
                  pl.BlockSpec((1, POOL_GW, POOL_GW), lambda i, kk: (hi(kk), 0, 0)),
                  pl.BlockSpec((1, tk), lambda i, kk: (0, hi(kk))),
                  pl.BlockSpec((tk, d), lambda i, kk: (kk, 0)),
                  pl.BlockSpec((1, d), lambda i, kk: (0, 0))],
        out_specs=pl.BlockSpec((tm, d), lambda i, kk: (i, 0)),
        out_shape=jax.ShapeDtypeStruct((r, d), F32),
        scratch_shapes=[pltpu.VMEM((tm, d), F32), pltpu.VMEM((nk_c, tm, tk), BF16)],
        compiler_params=_params(("parallel", "arbitrary")),
        name="odd_mix",
    )(xin, mods, p, p, p, p, p, ln_w, ln_b, w_s, b_s, bands, icnt, pool_w, pool_scale, w_out, fnw)


def kernel(x, c, ctx, c_ctx, ada_w, ada_b, norm_w, e_w_in, e_conv_qkv, e_a_log, e_dt_bias, e_head_norm,
           e_conv_b, e_w_out, o_w_in, o_ln_w, o_ln_b, o_w_s, o_b_s, o_pool_w, o_pool_scale, o_w_out,
           final_norm_w):
    bsz, seq, d = x.shape
    rx = bsz * seq
    rc = bsz * ctx.shape[1]
    x2 = x.reshape(rx, d)
    ctx2 = ctx.reshape(rc, d)

    cc = jnp.concatenate([c, c_ctx[None, :], jnp.zeros((8 - bsz - 1, d), F32)], axis=0)
    mods = _adaln(cc, ada_w, ada_b)
    ctx_row = bsz

    w_in = e_w_in[0, :, :GATE_COL0 + LANES].astype(BF16)
    w_bchz = _mixb_weights(e_w_in[0:1])
    m0 = mods[0].reshape(8, 1, 3 * d)
    nw0 = norm_w[0].reshape(1, d)

    gate_par = jnp.pad(jnp.stack([e_a_log[0].reshape(-1), e_dt_bias[0].reshape(-1)]),
                       ((0, 0), (0, LANES - 2 * HEADS)))
    qkv_x, gates_x = _qkv_proj(x2, m0, nw0, w_in, e_conv_qkv[0], gate_par,
                               rows_per_batch=seq, mod_row0=0, group=GRID_W)
    qkv_c, gates_c = _qkv_proj(ctx2, m0, nw0, w_in, e_conv_qkv[0], gate_par,
                               rows_per_batch=ctx.shape[1], mod_row0=ctx_row, group=ctx.shape[1])
    gz_x, yb_x = _mixb_proj(x2, m0, nw0, w_in, w_bchz, e_conv_b[0],
                            rows_per_batch=seq, mod_row0=0, group=GRID_W)
    gz_c, yb_c = _mixb_proj(ctx2, m0, nw0, w_in, w_bchz, e_conv_b[0],
                            rows_per_batch=ctx.shape[1], mod_row0=ctx_row, group=ctx.shape[1])
    o_c, o_x = _gdn(qkv_c, qkv_x, gates_c, gates_x,
                    e_head_norm[0].reshape(1, HEAD_DIM))
    w_out0 = e_w_out[0].astype(BF16)
    x2 = _even_out(x2, m0, o_x.reshape(rx, d), gz_x, yb_x, w_out0, rows_per_batch=seq, mod_row0=0)
    ctx2 = _even_out(ctx2, m0, o_c.reshape(rc, d), gz_c, yb_c, w_out0,
                     rows_per_batch=ctx.shape[1], mod_row0=ctx_row)

    m1 = mods[1].reshape(8, 1, 3 * d)
    p = _odd_in(x2, m1, norm_w[1].reshape(1, d), o_w_in[0].astype(BF16))
    b_s = jnp.broadcast_to(o_b_s[0][:, :, None], (o_b_s.shape[1], TOK_CHUNK, LANES))
    out = _odd_mix(x2, m1, p, o_ln_w[0].reshape(1, d), o_ln_b[0].reshape(1, d),
                   o_w_s[0].astype(BF16), b_s, o_pool_w[0].astype(BF16),
                   o_pool_scale[0].reshape(1, d), o_w_out[0].astype(BF16),
                   final_norm_w.reshape(1, d))
    return out.reshape(bsz, seq, d)
```

```python
import functools
import math

import numpy as np
import jax
import jax.numpy as jnp
from jax import lax
from jax.experimental import pallas as pl
from jax.experimental.pallas import tpu as pltpu

F32 = jnp.float32
BF16 = jnp.bfloat16

D_MODEL = 2048
BATCH = 4
SEQ = 2048
CTX_LEN = 256
GRID_W = 64
EPS = 1e-6
HEADS = 16
HEAD_DIM = 128
QKV_W = 3 * D_MODEL
GATE_COL0 = QKV_W + D_MODEL
CHUNK = 64
SCAN_GROUPS = (12, 12, 12)
TOK_CHUNK = 128
POOL_GROUPS = 4
POOL_GW = D_MODEL // POOL_GROUPS
POOL_RADII = (1, 2, 4, 8)

LANES = 128
VMEM_LIMIT = 56 * 1024 * 1024
NORM_ROWS = 16
NORM_UNROLL = 8
SUB_N = 256


def _params(sem):
    return pltpu.CompilerParams(dimension_semantics=sem, vmem_limit_bytes=VMEM_LIMIT)


def _sigmoid(x):
    return 1.0 / (1.0 + jnp.exp(-x))


def _silu(x):
    return x * _sigmoid(x)


def _gelu_tanh(x):
    cdf = 0.5 * (1.0 + jnp.tanh(math.sqrt(2.0 / math.pi) * (x + 0.044715 * (x * x * x))))
    return x * cdf


def _modulated_norm_to(h_scr, x_ref, nw_ref, sc_ref, sh_ref):
    nw = nw_ref[...]
    sc = sc_ref[0]
    sh = sh_ref[0]

    step = NORM_ROWS * NORM_UNROLL

    def body(r, carry):
        rows = [pl.ds(pl.multiple_of(r * step + k * NORM_ROWS, NORM_ROWS), NORM_ROWS)
                for k in range(NORM_UNROLL)]
        scales = []
        for rw in rows:
            x = x_ref[rw, :]
            scales.append(lax.rsqrt(jnp.mean(x * x, axis=-1, keepdims=True) + EPS))
        for rw, rs in zip(rows, scales):
            y = x_ref[rw, :] * rs * nw
            h_scr[rw, :] = (y * (1.0 + sc) + sh).astype(h_scr.dtype)
        return carry

    lax.fori_loop(0, x_ref.shape[0] // step, body, 0)


def _row_conv3(p, cw, group):
    tm = p.shape[0]
    row = lax.broadcasted_iota(jnp.int32, (tm, 1), 0) % group
    prev = jnp.where(row == 0, 0.0, pltpu.roll(p, 1, 0))
    nxt = jnp.where(row == group - 1, 0.0, pltpu.roll(p, tm - 1, 0))
    return prev * cw[0:1] + p * cw[1:2] + nxt * cw[2:3]


def _adaln_kernel(c_ref, w_ref, b_ref, o_ref):
    a = _silu(c_ref[...]).astype(BF16)
    o_ref[0] = jnp.dot(a, w_ref[0].astype(BF16), preferred_element_type=F32) + b_ref[0]


def _adaln(cc, ada_w, ada_b):
    depth, d, n = ada_w.shape
    tn = 1024
    return pl.pallas_call(
        _adaln_kernel,
        grid=(depth, n // tn),
        in_specs=[pl.BlockSpec((8, d), lambda l, j: (0, 0)),
                  pl.BlockSpec((1, d, tn), lambda l, j: (l, 0, j)),
                  pl.BlockSpec((1, 1, tn), lambda l, j: (l, 0, j))],
        out_specs=pl.BlockSpec((1, 8, tn), lambda l, j: (l, 0, j)),
        out_shape=jax.ShapeDtypeStruct((depth, 8, n), F32),
        compiler_params=_params(("parallel", "parallel")),
        name="adaln",
    )(cc, ada_w, ada_b.reshape(depth, 1, n))


def _qkv_kernel(x_ref, nw_ref, sh_ref, sc_ref, w_ref, wg_ref, cw_ref, gp_ref, qkv_ref, g_ref, h_scr,
                *, group, n_tiles):
    j = pl.program_id(1)

    @pl.when(j == 0)
    def _():
        _modulated_norm_to(h_scr, x_ref, nw_ref, sc_ref, sh_ref)

    @pl.when(j < n_tiles)
    def _():
        p = jnp.dot(h_scr[...], w_ref[...], preferred_element_type=F32)
        y = _silu(_row_conv3(p, cw_ref[...], group))
        for hh in range(y.shape[1] // HEAD_DIM):
            qkv_ref[hh] = y[:, hh * HEAD_DIM:(hh + 1) * HEAD_DIM]

    @pl.when(j == n_tiles)
    def _():
        raw = jnp.dot(h_scr[...], wg_ref[...], preferred_element_type=F32)
        z = raw + gp_ref[1:2]
        softplus = jnp.maximum(z, 0.0) + jnp.log1p(jnp.exp(-jnp.abs(z)))
        lane = lax.broadcasted_iota(jnp.int32, (1, LANES), 1)
        gates = jnp.where(lane < 2 * HEADS, -jnp.exp(gp_ref[0:1]) * softplus, _sigmoid(raw))
        ri = lax.broadcasted_iota(jnp.int32, (LANES, LANES), 0)
        ci = lax.broadcasted_iota(jnp.int32, (LANES, LANES), 1)
        same = (ri // CHUNK) == (ci // CHUNK)
        pre01 = jnp.where(same & (ci <= ri), 1.0, 0.0).astype(BF16)
        suf01 = jnp.where(same & (ci >= ri), 1.0, 0.0).astype(BF16)
        pieces = _split3_bf16(gates)
        for r0 in range(0, gates.shape[0], LANES):
            blk = [p[r0:r0 + LANES] for p in pieces]
            pre = [jnp.dot(pre01, b, preferred_element_type=F32) for b in blk]
            suf = [jnp.dot(suf01, b, preferred_element_type=F32) for b in blk]
            pre = (pre[0] + pre[1]) + pre[2]
            suf = (suf[0] + suf[1]) + suf[2]
            g_ref[r0:r0 + LANES] = jnp.where(lane < HEADS, pre,
                                             jnp.where(lane < 2 * HEADS, suf, gates[r0:r0 + LANES]))


def _qkv_proj(xin, mods, nw, w_in, cw, gate_par, *, rows_per_batch, mod_row0, group):
    r, d = xin.shape
    tm, tn = 1024, 768
    n_tiles = QKV_W // tn
    hpt = tn // HEAD_DIM
    tiles_per_batch = max(rows_per_batch // tm, 1)

    def mod_idx(part):
        if rows_per_batch >= tm:
            return lambda i, j: (mod_row0 + i // tiles_per_batch, 0, part)
        return lambda i, j: (mod_row0, 0, part)

    last = n_tiles - 1
    return pl.pallas_call(
        functools.partial(_qkv_kernel, group=group, n_tiles=n_tiles),
        grid=(r // tm, n_tiles + 1),
        in_specs=[pl.BlockSpec((tm, d), lambda i, j: (i, 0)),
                  pl.BlockSpec((1, d), lambda i, j: (0, 0)),
                  pl.BlockSpec((1, 1, d), mod_idx(0)),
                  pl.BlockSpec((1, 1, d), mod_idx(1)),
                  pl.BlockSpec((d, tn), lambda i, j: (0, jnp.minimum(j, last))),
                  pl.BlockSpec((d, LANES), lambda i, j: (0, GATE_COL0 // LANES)),
                  pl.BlockSpec((3, tn), lambda i, j: (0, jnp.minimum(j, last))),
                  pl.BlockSpec((2, LANES), lambda i, j: (0, 0))],
        out_specs=[pl.BlockSpec((hpt, tm, HEAD_DIM), lambda i, j: (jnp.minimum(j, last), i, 0)),
                   pl.BlockSpec((tm, LANES), lambda i, j: (i, 0))],
        out_shape=[jax.ShapeDtypeStruct((3 * HEADS, r, HEAD_DIM), F32),
                   jax.ShapeDtypeStruct((r, LANES), F32)],
        scratch_shapes=[pltpu.VMEM((tm, d), BF16)],
        compiler_params=_params(("parallel", "arbitrary")),
        name="qkv_proj",
    )(xin, nw, mods, mods, w_in, w_in, cw, gate_par)


def _mixb_kernel(x_ref, nw_ref, sh_ref, sc_ref, wz_ref, wb_ref, wc_ref, wh_ref, wzb_ref, cw_ref,
                 gz_ref, yb_ref, h_scr, *, group):
    j = pl.program_id(1)

    @pl.when(j == 0)
    def _():
        _modulated_norm_to(h_scr, x_ref, nw_ref, sc_ref, sh_ref)

    h = h_scr[...]
    dot = lambda w: jnp.dot(h, w[...], preferred_element_type=F32)
    gz_ref[...] = _silu(dot(wz_ref)).astype(gz_ref.dtype)
    conv = _row_conv3(dot(wc_ref) * dot(wh_ref), cw_ref[...], group)
    yb_ref[...] = (dot(wb_ref) * conv * _silu(dot(wzb_ref))).astype(yb_ref.dtype)


def _mixb_proj(xin, mods, nw, w_in, w_bchz, cw, *, rows_per_batch, mod_row0, group):
    r, d = xin.shape
    tm, tn = 1024, 256
    nj = d // tn
    za_blk0 = QKV_W // tn
    tiles_per_batch = max(rows_per_batch // tm, 1)

    def mod_idx(part):
        if rows_per_batch >= tm:
            return lambda i, j: (mod_row0 + i // tiles_per_batch, 0, part)
        return lambda i, j: (mod_row0, 0, part)

    wspec = lambda reg: pl.BlockSpec((d, tn), lambda i, j: (0, reg * nj + j))
    return pl.pallas_call(
        functools.partial(_mixb_kernel, group=group),
        grid=(r // tm, nj),
        in_specs=[pl.BlockSpec((tm, d), lambda i, j: (i, 0)),
                  pl.BlockSpec((1, d), lambda i, j: (0, 0)),
                  pl.BlockSpec((1, 1, d), mod_idx(0)),
                  pl.BlockSpec((1, 1, d), mod_idx(1)),
                  pl.BlockSpec((d, tn), lambda i, j: (0, za_blk0 + j)),
                  wspec(0), wspec(1), wspec(2), wspec(3),
                  pl.BlockSpec((3, tn), lambda i, j: (0, j))],
        out_specs=[pl.BlockSpec((tm, tn), lambda i, j: (i, j)),
                   pl.BlockSpec((tm, tn), lambda i, j: (i, j))],
        out_shape=[jax.ShapeDtypeStruct((r, d), BF16), jax.ShapeDtypeStruct((r, d), BF16)],
        scratch_shapes=[pltpu.VMEM((tm, d), BF16)],
        compiler_params=_params(("parallel", "arbitrary")),
        name="mixb_proj",
    )(xin, nw, mods, mods, w_in, w_bchz, w_bchz, w_bchz, w_bchz, cw)


def _split3_bf16(x):
    x1 = x.astype(BF16)
    r1 = x - x1.astype(F32)
    x2 = r1.astype(BF16)
    x3 = (r1 - x2.astype(F32)).astype(BF16)
    return x1, x2, x3


def _block_diag2(x, left):
    return jnp.concatenate([jnp.where(left, x, 0.0), jnp.where(left, 0.0, x)], axis=0).astype(BF16)


def _unit_tri_inverse_minus_eye(ms, left, ri, cj):
    n = ms[0].shape[0]
    ys = None
    s = 1
    while s < n:
        shift = s.bit_length() - 1
        br = lax.shift_right_logical(ri, shift)
        bc = lax.shift_right_logical(cj, shift)
        mask_lo = ((br & 1) == 1) & (bc == br - 1)
        mask_up = ((bc & 1) == 1) & (br == bc - 1)
        mask = (left & mask_lo) | (jnp.logical_not(left) & mask_up)
        cms = [jnp.where(mask, m, 0.0) for m in ms]
        if ys is None:
            ys = [-cm for cm in cms]
        else:
            yc = [jnp.dot(y.astype(BF16), _block_diag2(cm, left), preferred_element_type=F32)
                  for cm, y in zip(cms, ys)]
            yield
            ps = [cm + t for cm, t in zip(cms, yc)]
            py = [jnp.dot(p.astype(BF16), _block_diag2(y, left), preferred_element_type=F32)
                  for y, p in zip(ys, ps)]
            yield
            ys = [y - p - t for y, p, t in zip(ys, ps, py)]
        s *= 2
    return ys


def _run_interleaved(*gens):
    active = list(gens)
    while active:
        for g in list(active):
            try:
                next(g)
            except StopIteration:
                active.remove(g)


def _gdn_kernel(qc_ref, kc_ref, vc_ref, qx_ref, kx_ref, vx_ref, gc_ref, gx_ref,
                hn_ref, oc_ref, ox_ref,
                qs, ks, vs, gsc, bsc, nq_s, b_s, op_s, gl_s, oacc, st_scr):
    k = pl.program_id(0)
    h = jnp.minimum(k, pl.num_programs(0) - 2) % HEADS
    cur_slot = k % 2
    prev_slot = 1 - cur_slot
    tc = qc_ref.shape[1]
    tx = qx_ref.shape[1]
    t = tc + tx
    nsteps = t // CHUNK
    nc_ctx = tc // CHUNK
    c = CHUNK

    def l2n(a):
        return a * lax.rsqrt(jnp.sum(a * a, axis=-1, keepdims=True) + EPS)

    @pl.when(k == 0)
    def _():
        for ref in (nq_s, b_s, op_s, gl_s, oacc, st_scr):
            ref[...] = jnp.zeros_like(ref)

    qscale = HEAD_DIM ** -0.5
    qs[0:tc] = l2n(qc_ref[0]) * qscale
    qs[tc:t] = l2n(qx_ref[0]) * qscale
    ks[0:tc] = l2n(kc_ref[0])
    ks[tc:t] = l2n(kx_ref[0])
    vs[0:tc] = vc_ref[0]
    vs[tc:t] = vx_ref[0]

    lane = lax.broadcasted_iota(jnp.int32, (1, LANES), 1)
    for d in range(2):
        for ref, lo, hi in ((gc_ref, 0, tc), (gx_ref, tc, t)):
            gates = ref[...]
            g = jnp.sum(jnp.where(lane == d * HEADS + h, gates, 0.0), axis=-1, keepdims=True)
            b = jnp.sum(jnp.where(lane == 2 * HEADS + d * HEADS + h, gates, 0.0), axis=-1,
                        keepdims=True)
            gsc[d, lo:hi] = jnp.broadcast_to(g, (hi - lo, LANES))
            bsc[d, lo:hi] = jnp.broadcast_to(b, (hi - lo, LANES))

    def rev_chunk(s):
        return nc_ctx - 1 - s if s < nc_ctx else nsteps + nc_ctx - 1 - s

    ri = lax.broadcasted_iota(jnp.int32, (c, 2 * c), 0)
    lane2 = lax.broadcasted_iota(jnp.int32, (c, 2 * c), 1)
    left = lane2 < c
    cj = lane2 & (c - 1)
    strict = (left & (cj < ri)) | (jnp.logical_not(left) & (cj > ri))
    incl = (left & (cj <= ri)) | (jnp.logical_not(left) & (cj >= ri))
    left_sq = lax.broadcasted_iota(jnp.int32, (2 * c, 2 * c), 1) < c

    def prep(steps):
        units = range(len(steps))
        rows = [[slice(n * c, (n + 1) * c) for n in (s, rev_chunk(s))] for s in steps]
        qv = [[qs[r] for r in rows[uu]] for uu in units]
        kv = [[ks[r] for r in rows[uu]] for uu in units]
        vv = [[vs[r] for r in rows[uu]] for uu in units]
        beta = [[bsc[d, rows[uu][d]] for d in range(2)] for uu in units]
        kb = [[kv[uu][d] * beta[uu][d] for d in range(2)] for uu in units]
        gk = [lax.dot_general(
                  jnp.concatenate([kb[uu][0], qv[uu][0], kb[uu][1], qv[uu][1]], axis=0).astype(BF16),
                  jnp.concatenate(kv[uu], axis=0).astype(BF16),
                  (((1,), (1,)), ((), ())), preferred_element_type=F32)
              for uu in units]
        yield
        gcb = [[gsc[d, rows[uu][d]] for d in range(2)] for uu in units]
        ms, avs = [], []
        for uu in units:
            col = jnp.where(left, gcb[uu][0], gcb[uu][1])
            row = jnp.transpose(jnp.concatenate(gcb[uu], axis=0))[0:c]
            decay = jnp.exp(jnp.where(incl, col - row, 0.0))
            kk = jnp.where(left, gk[uu][0:c], gk[uu][2 * c:3 * c])
            qk = jnp.where(left, gk[uu][c:2 * c], gk[uu][3 * c:4 * c])
            ms.append(jnp.where(strict, kk * decay, 0.0))
            avs.append(jnp.where(incl, qk * decay, 0.0))
        ys = yield from _unit_tri_inverse_minus_eye(ms, left, ri, cj)
        egc = [[jnp.exp(g) for g in gcb[uu]] for uu in units]
        rs = [[jnp.concatenate([kb[uu][d] * egc[uu][d], vv[uu][d] * beta[uu][d]], axis=1)
               for d in range(2)] for uu in units]
        corr = [jnp.dot(_block_diag2(ys[uu], left),
                        jnp.concatenate(rs[uu], axis=0).astype(BF16), preferred_element_type=F32)
                for uu in units]
        yield
        wus, aks, g_edges = [], [], []
        for uu in units:
            kgs, edges = [], []
            for d in range(2):
                edge = c - 1 if d == 0 else 0
                edges.append(gcb[uu][d][edge:edge + 1])
                kgs.append(kv[uu][d] * jnp.exp(edges[d] - gcb[uu][d]))
            g_edges.append(edges)
            wus.append([rs[uu][d] + corr[uu][d * c:(d + 1) * c] for d in range(2)])
            kgt = jnp.transpose(jnp.concatenate(kgs, axis=0))
            aks.append(jnp.concatenate(
                [jnp.where(left, avs[uu], 0.0), jnp.where(left_sq, kgt, 0.0),
                 jnp.where(left, 0.0, avs[uu]), jnp.where(left_sq, 0.0, kgt)], axis=0).astype(BF16))
        prods = [jnp.dot(aks[uu], jnp.concatenate(wus[uu], axis=0).astype(BF16),
                         preferred_element_type=F32) for uu in units]
        yield
        for uu in units:
            for d in range(2):
                x = prods[uu][d * (c + HEAD_DIM):(d + 1) * (c + HEAD_DIM)]
                a_w, a_u = x[0:c, 0:HEAD_DIM], x[0:c, HEAD_DIM:2 * HEAD_DIM]
                k_w, k_u = x[c:c + HEAD_DIM, 0:HEAD_DIM], x[c:c + HEAD_DIM, HEAD_DIM:2 * HEAD_DIM]
                idx = 2 * steps[uu] + d
                nq_s[idx] = jnp.concatenate([-k_w, qv[uu][d] * egc[uu][d] - a_w],
                                            axis=0).astype(BF16)
                b_s[idx] = k_u
                op_s[idx] = a_u
                gl_s[idx] = jnp.broadcast_to(jnp.exp(g_edges[uu][d]), (8, LANES))

    def scan(steps, states, slot, written):
        for s in steps:
            idxs = (2 * s, 2 * s + 1)
            xs = [jnp.dot(nq_s[idx], st.astype(BF16), preferred_element_type=F32)
                  for idx, st in zip(idxs, states)]
            yield
            for n, idx, x in zip((s, rev_chunk(s)), idxs, xs):
                o = x[HEAD_DIM:HEAD_DIM + c] + op_s[idx]
                rows = slice(n * c, (n + 1) * c)
                oacc[slot, rows] = oacc[slot, rows] + o if n in written else o
                written.add(n)
            states[:] = [st * gl_s[idx][0:1] + x[0:HEAD_DIM] + b_s[idx]
                         for idx, st, x in zip(idxs, states, xs)]

    bounds = np.cumsum((0,) + SCAN_GROUPS)
    assert bounds[-1] == nsteps
    groups = [list(range(lo, hi)) for lo, hi in zip(bounds[:-1], bounds[1:])]
    early = [s for g in groups[:-1] for s in g]
    written_early = {n for s in early for n in (s, rev_chunk(s))}

    prev_states = [st_scr[0], st_scr[1]]
    _run_interleaved(prep(groups[0]), scan(groups[-1], prev_states, prev_slot, set(written_early)))
    o = oacc[prev_slot]
    on = o * lax.rsqrt(jnp.mean(o * o, axis=-1, keepdims=True) + EPS) * hn_ref[...]
    oc_ref[0] = on[0:tc].astype(oc_ref.dtype)
    ox_ref[0] = on[tc:t].astype(ox_ref.dtype)

    states = [jnp.zeros((HEAD_DIM, HEAD_DIM), F32)] * 2
    written = set()
    for gi in range(1, len(groups)):
        _run_interleaved(prep(groups[gi]), scan(groups[gi - 1], states, cur_slot, written))
    assert written == written_early
    st_scr[0] = states[0]
    st_scr[1] = states[1]


def _gdn(qkv_c, qkv_x, gates_c, gates_x, head_norm):
    tc, tx = CTX_LEN, SEQ
    t = tc + tx
    nidx = 2 * (t // CHUNK)
    nprob = BATCH * HEADS
    cur = lambda k: jnp.minimum(k, nprob - 1)
    prev = lambda k: jnp.maximum(k - 1, 0)
    part = lambda rows, p: pl.BlockSpec(
        (1, rows, HEAD_DIM), lambda k: (p * HEADS + cur(k) % HEADS, cur(k) // HEADS, 0))
    return pl.pallas_call(
        _gdn_kernel,
        grid=(nprob + 1,),
        in_specs=[part(tc, 0), part(tc, 1), part(tc, 2),
                  part(tx, 0), part(tx, 1), part(tx, 2),
                  pl.BlockSpec((tc, LANES), lambda k: (cur(k) // HEADS, 0)),
                  pl.BlockSpec((tx, LANES), lambda k: (cur(k) // HEADS, 0)),
                  pl.BlockSpec((1, HEAD_DIM), lambda k: (0, 0))],
        out_specs=[pl.BlockSpec((1, tc, HEAD_DIM), lambda k: (prev(k) // HEADS, 0, prev(k) % HEADS)),
                   pl.BlockSpec((1, tx, HEAD_DIM), lambda k: (prev(k) // HEADS, 0, prev(k) % HEADS))],
        out_shape=[jax.ShapeDtypeStruct((BATCH, tc, D_MODEL), BF16),
                   jax.ShapeDtypeStruct((BATCH, tx, D_MODEL), BF16)],
        scratch_shapes=[pltpu.VMEM((t, HEAD_DIM), F32), pltpu.VMEM((t, HEAD_DIM), F32),
                        pltpu.VMEM((t, HEAD_DIM), F32),
                        pltpu.VMEM((2, t, LANES), F32), pltpu.VMEM((2, t, LANES), F32),
                        pltpu.VMEM((nidx, HEAD_DIM + CHUNK, HEAD_DIM), BF16),
                        pltpu.VMEM((nidx, HEAD_DIM, HEAD_DIM), F32),
                        pltpu.VMEM((nidx, CHUNK, HEAD_DIM), F32),
                        pltpu.VMEM((nidx, 8, LANES), F32),
                        pltpu.VMEM((2, t, HEAD_DIM), F32),
                        pltpu.VMEM((2, HEAD_DIM, HEAD_DIM), F32)],
        compiler_params=_params(("arbitrary",)),
        name="gdn",
    )(qkv_c, qkv_c, qkv_c, qkv_x, qkv_x, qkv_x, gates_c, gates_x, head_norm)


def _even_out_kernel(x_ref, gt_ref, o_ref, gz_ref, yb_ref, w_ref, out_ref, a_scr):
    j = pl.program_id(1)

    @pl.when(j == 0)
    def _():
        d = o_ref.shape[1]
        a_scr[:, 0:d] = o_ref[...] * gz_ref[...]
        a_scr[:, d:2 * d] = yb_ref[...]

    y = jnp.dot(a_scr[...], w_ref[...], preferred_element_type=F32)
    out_ref[...] = x_ref[...] + gt_ref[0] * y


def _even_out(xin, mods, o, gz, yb, w_out, *, rows_per_batch, mod_row0):
    r, d = xin.shape
    tm = min(1024, rows_per_batch)
    tn = 512
    nj = d // tn
    tiles_per_batch = rows_per_batch // tm
    fixed = mod_row0 >= BATCH
    gt_idx = ((lambda i, j: (mod_row0, 0, 2 * nj + j)) if fixed
              else (lambda i, j: (i // tiles_per_batch, 0, 2 * nj + j)))
    return pl.pallas_call(
        _even_out_kernel,
        grid=(r // tm, nj),
        in_specs=[pl.BlockSpec((tm, tn), lambda i, j: (i, j)),
                  pl.BlockSpec((1, 1, tn), gt_idx),
                  pl.BlockSpec((tm, d), lambda i, j: (i, 0)),
                  pl.BlockSpec((tm, d), lambda i, j: (i, 0)),
                  pl.BlockSpec((tm, d), lambda i, j: (i, 0)),
                  pl.BlockSpec((2 * d, tn), lambda i, j: (0, j))],
        out_specs=pl.BlockSpec((tm, tn), lambda i, j: (i, j)),
        out_shape=jax.ShapeDtypeStruct((r, d), F32),
        scratch_shapes=[pltpu.VMEM((tm, 2 * d), BF16)],
        compiler_params=_params(("parallel", "arbitrary")),
        name="even_out",
    )(xin, mods, o, gz, yb, w_out)


def _odd_in_kernel(x_ref, nw_ref, sh_ref, sc_ref, w_ref, out_ref, h_scr, *, tiles_per_region):
    j = pl.program_id(1)

    @pl.when(j == 0)
    def _():
        _modulated_norm_to(h_scr, x_ref, nw_ref, sc_ref, sh_ref)

    region = j // tiles_per_region

    def tile(act):
        for lo in range(0, w_ref.shape[1], SUB_N):
            p = jnp.dot(h_scr[...], w_ref[:, lo:lo + SUB_N], preferred_element_type=F32)
            out_ref[:, lo:lo + SUB_N] = act(p).astype(out_ref.dtype)

    @pl.when(region <= 1)
    def _():
        tile(_gelu_tanh)

    @pl.when((region == 2) | (region == 4))
    def _():
        tile(_silu)

    @pl.when(region == 3)
    def _():
        tile(lambda p: p)


def _odd_in(xin, mods, nw, w):
    r, d = xin.shape
    n = w.shape[1]
    tm, tn = 1024, 1024
    tiles_per_batch = SEQ // tm
    mod_idx = lambda part: (lambda i, j: (i // tiles_per_batch, 0, part))
    return pl.pallas_call(
        functools.partial(_odd_in_kernel, tiles_per_region=d // tn),
        grid=(r // tm, n // tn),
        in_specs=[pl.BlockSpec((tm, d), lambda i, j: (i, 0)),
                  pl.BlockSpec((1, d), lambda i, j: (0, 0)),
                  pl.BlockSpec((1, 1, d), mod_idx(0)),
                  pl.BlockSpec((1, 1, d), mod_idx(1)),
                  pl.BlockSpec((d, tn), lambda i, j: (0, j))],
        out_specs=pl.BlockSpec((tm, tn), lambda i, j: (i, j)),
        out_shape=jax.ShapeDtypeStruct((r, n), BF16),
        scratch_shapes=[pltpu.VMEM((tm, d), BF16)],
        compiler_params=_params(("parallel", "arbitrary")),
        name="odd_in",
    )(xin, nw, mods, mods, w)


def _odd_mix_kernel(x_ref, gt_ref, u_ref, v_ref, zc_ref, pd_ref, zd_ref, lnw_ref, lnb_ref, ws_ref,
                    bs_ref, band_ref, icnt_ref, pw_ref, ps_ref, wo_ref, fnw_ref, out_ref,
                    acc, ln_scr, *, nk_c):
    kk = pl.program_id(1)
    tm = x_ref.shape[0]
    tk = u_ref.shape[1]
    gpb = tk // HEAD_DIM

    @pl.when(kk == 0)
    def _():
        v = v_ref[...].astype(F32)
        mu = jnp.mean(v, axis=-1, keepdims=True)
        xc = v - mu
        var = jnp.mean(xc * xc, axis=-1, keepdims=True)
        ln = (xc * lax.rsqrt(var + EPS) * lnw_ref[...] + lnb_ref[...]).astype(BF16)
        for blk in range(nk_c):
            ln_scr[blk] = ln[:, blk * tk:(blk + 1) * tk]
        acc[...] = jnp.zeros_like(acc)

    @pl.when(kk < nk_c)
    def _():
        ln = ln_scr[kk]
        cols = []
        for gi in range(gpb):
            w_g = ws_ref[gi]
            b_g = bs_ref[gi]
            rows = []
            for cc in range(tm // TOK_CHUNK):
                blk = ln[cc * TOK_CHUNK:(cc + 1) * TOK_CHUNK, gi * HEAD_DIM:(gi + 1) * HEAD_DIM]
                rows.append(jnp.dot(w_g, blk, preferred_element_type=F32) + b_g)
            cols.append(jnp.concatenate(rows, axis=0))
        s = jnp.concatenate(cols, axis=1)
        yc = u_ref[...] * s * zc_ref[...]
        acc[...] += jnp.dot(yc.astype(BF16), wo_ref[...], preferred_element_type=F32)

    @pl.when(kk >= nk_c)
    def _():
        parts = []
        for pi in range(tk // POOL_GW):
            cols = slice(pi * POOL_GW, (pi + 1) * POOL_GW)
            pd = pd_ref[:, cols]
            wsum = jnp.dot(band_ref[pi], pd, preferred_element_type=F32)
            icnt = jnp.concatenate([icnt_ref[pi]] * (POOL_GW // LANES), axis=1)
            diff = wsum * icnt - pd.astype(F32)
            y = jnp.dot(diff.astype(BF16), pw_ref[pi], preferred_element_type=F32)
            parts.append((y * ps_ref[:, cols] * zd_ref[:, cols]).astype(BF16))
        yd = jnp.concatenate(parts, axis=1)
        acc[...] += jnp.dot(yd, wo_ref[...], preferred_element_type=F32)

    @pl.when(kk == pl.num_programs(1) - 1)
    def _():
        xn = x_ref[...] + gt_ref[0] * acc[...]
        ms = jnp.mean(xn * xn, axis=-1, keepdims=True)
        out_ref[...] = xn * lax.rsqrt(ms + EPS) * fnw_ref[...]


def _pool_constants(tm):
    bands = np.zeros((POOL_GROUPS, tm, tm), np.float32)
    icnt = np.zeros((POOL_GROUPS, tm, LANES), np.float32)
    pos = np.arange(tm)
    row = pos // GRID_W
    col = pos % GRID_W
    for g, rad in enumerate(POOL_RADII):
        same = row[:, None] == row[None, :]
        bands[g] = (same & (np.abs(col[:, None] - col[None, :]) <= rad)).astype(np.float32)
        cnt = np.minimum(col + rad + 1, GRID_W) - np.maximum(col - rad, 0)
        icnt[g] = (1.0 / cnt.astype(np.float32))[:, None]
    return jnp.asarray(bands, BF16), jnp.asarray(icnt, F32)


def _odd_mix(xin, mods, p, ln_w, ln_b, w_s, b_s, pool_w, pool_scale, w_out, fnw):
    r, d = xin.shape
    tm, tk = 512, 1024
    nk_c = d // tk
    nk = 2 * nk_c
    gpb = tk // HEAD_DIM
    ppb = tk // POOL_GW
    tiles_per_batch = SEQ // tm
    bands, icnt = _pool_constants(tm)
    lo = lambda kk: jnp.minimum(kk, nk_c - 1)
    hi = lambda kk: jnp.maximum(kk - nk_c, 0)
    cblk = d // tk
    return pl.pallas_call(
        functools.partial(_odd_mix_kernel, nk_c=nk_c),
        grid=(r // tm, nk),
        in_specs=[pl.BlockSpec((tm, d), lambda i, kk: (i, 0)),
                  pl.BlockSpec((1, 1, d), lambda i, kk: (i // tiles_per_batch, 0, 2)),
                  pl.BlockSpec((tm, tk), lambda i, kk: (i, lo(kk))),
                  pl.BlockSpec((tm, d), lambda i, kk: (i, 1)),
                  pl.BlockSpec((tm, tk), lambda i, kk: (i, 2 * cblk + lo(kk))),
                  pl.BlockSpec((tm, tk), lambda i, kk: (i, 3 * cblk + hi(kk))),
                  pl.BlockSpec((tm, tk), lambda i, kk: (i, 4 * cblk + hi(kk))),
                  pl.BlockSpec((1, d), lambda i, kk: (0, 0)),
                  pl.BlockSpec((1, d), lambda i, kk: (0, 0)),
                  pl.BlockSpec((gpb, TOK_CHUNK, TOK_CHUNK), lambda i, kk: (lo(kk), 0, 0)),
                  pl.BlockSpec((gpb, TOK_CHUNK, LANES), lambda i, kk: (lo(kk), 0, 0)),
                  pl.BlockSpec((ppb, tm, tm), lambda i, kk: (hi(kk), 0, 0)),
                  pl.BlockSpec((ppb, tm, LANES), lambda i, kk: (hi(kk), 0, 0)),
                  pl.BlockSpec((ppb, POOL_GW, POOL_GW), lambda i, kk: (hi(kk), 0, 0)),
                  pl.BlockSpec((1, tk), lambda i, kk: (0, hi(kk))),
                  pl.BlockSpec((tk, d), lambda i, kk: (kk, 0)),
                  pl.BlockSpec((1, d), lambda i, kk: (0, 0))],
        out_specs=pl.BlockSpec((tm, d), lambda i, kk: (i, 0)),
        out_shape=jax.ShapeDtypeStruct((r, d), F32),
        scratch_shapes=[pltpu.VMEM((tm, d), F32), pltpu.VMEM((nk_c, tm, tk), BF16)],
        compiler_params=_params(("parallel", "arbitrary")),
        name="odd_mix",
    )(xin, mods, p, p, p, p, p, ln_w, ln_b, w_s, b_s, bands, icnt, pool_w, pool_scale, w_out, fnw)


def kernel(x, c, ctx, c_ctx, ada_w, ada_b, norm_w, e_w_in, e_conv_qkv, e_a_log, e_dt_bias, e_head_norm,
           e_conv_b, e_w_out, o_w_in, o_ln_w, o_ln_b, o_w_s, o_b_s, o_pool_w, o_pool_scale, o_w_out,
           final_norm_w):
    bsz, seq, d = x.shape
    rx = bsz * seq
    rc = bsz * ctx.shape[1]
    x2 = x.reshape(rx, d)
    ctx2 = ctx.reshape(rc, d)

    cc = jnp.concatenate([c, c_ctx[None, :], jnp.zeros((8 - bsz - 1, d), F32)], axis=0)
    mods = _adaln(cc, ada_w, ada_b)
    ctx_row = bsz

    w_in = e_w_in[0].astype(BF16)
    w_bchz = w_in[:, GATE_COL0 + 4 * HEADS:]
    m0 = mods[0].reshape(8, 1, 3 * d)
    nw0 = norm_w[0].reshape(1, d)

    gate_par = jnp.pad(jnp.stack([e_a_log[0].reshape(-1), e_dt_bias[0].reshape(-1)]),
                       ((0, 0), (0, LANES - 2 * HEADS)))
    qkv_x, gates_x = _qkv_proj(x2, m0, nw0, w_in, e_conv_qkv[0], gate_par,
                               rows_per_batch=seq, mod_row0=0, group=GRID_W)
    qkv_c, gates_c = _qkv_proj(ctx2, m0, nw0, w_in, e_conv_qkv[0], gate_par,
                               rows_per_batch=ctx.shape[1], mod_row0=ctx_row, group=ctx.shape[1])
    gz_x, yb_x = _mixb_proj(x2, m0, nw0, w_in, w_bchz, e_conv_b[0],
                            rows_per_batch=seq, mod_row0=0, group=GRID_W)
    gz_c, yb_c = _mixb_proj(ctx2, m0, nw0, w_in, w_bchz, e_conv_b[0],
                            rows_per_batch=ctx.shape[1], mod_row0=ctx_row, group=ctx.shape[1])
    o_c, o_x = _gdn(qkv_c, qkv_x, gates_c, gates_x,
                    e_head_norm[0].reshape(1, HEAD_DIM))
    w_out0 = e_w_out[0].astype(BF16)
    x2 = _even_out(x2, m0, o_x.reshape(rx, d), gz_x, yb_x, w_out0, rows_per_batch=seq, mod_row0=0)
    ctx2 = _even_out(ctx2, m0, o_c.reshape(rc, d), gz_c, yb_c, w_out0,
                     rows_per_batch=ctx.shape[1], mod_row0=ctx_row)

    m1 = mods[1].reshape(8, 1, 3 * d)
    p = _odd_in(x2, m1, norm_w[1].reshape(1, d), o_w_in[0].astype(BF16))
    b_s = jnp.broadcast_to(o_b_s[0][:, :, None], (o_b_s.shape[1], TOK_CHUNK, LANES))
    out = _odd_mix(x2, m1, p, o_ln_w[0].reshape(1, d), o_ln_b[0].reshape(1, d),
                   o_w_s[0].astype(BF16), b_s, o_pool_w[0].astype(BF16),
                   o_pool_scale[0].reshape(1, d), o_w_out[0].astype(BF16),
                   final_norm_w.reshape(1, d))
    return out.reshape(bsz, seq, d)
```

```python
import functools
import math

import numpy as np
import jax
import jax.numpy as jnp
from jax import lax
from jax.experimental import pallas as pl
from jax.experimental.pallas import tpu as pltpu

F32 = jnp.float32
BF16 = jnp.bfloat16

D_MODEL = 2048
BATCH = 4
SEQ = 2048
CTX_LEN = 256
GRID_W = 64
EPS = 1e-6
HEADS = 16
HEAD_DIM = 128
QKV_W = 3 * D_MODEL
GATE_COL0 = QKV_W + D_MODEL
CHUNK = 64
SCAN_GROUPS = (12, 12, 12)
TOK_CHUNK = 128
POOL_GROUPS = 4
POOL_GW = D_MODEL // POOL_GROUPS
POOL_RADII = (1, 2, 4, 8)

LANES = 128
VMEM_LIMIT = 56 * 1024 * 1024
NORM_ROWS = 16
NORM_UNROLL = 8
SUB_N = 256


def _params(sem):
    return pltpu.CompilerParams(dimension_semantics=sem, vmem_limit_bytes=VMEM_LIMIT)


def _sigmoid(x):
    return 1.0 / (1.0 + jnp.exp(-x))


def _silu(x):
    return x * _sigmoid(x)


def _gelu_tanh(x):
    cdf = 0.5 * (1.0 + jnp.tanh(math.sqrt(2.0 / math.pi) * (x + 0.044715 * (x * x * x))))
    return x * cdf


def _modulated_norm_to(h_scr, x_ref, nw_ref, sc_ref, sh_ref):
    nw = nw_ref[...]
    sc = sc_ref[0]
    sh = sh_ref[0]

    step = NORM_ROWS * NORM_UNROLL

    def body(r, carry):
        rows = [pl.ds(pl.multiple_of(r * step + k * NORM_ROWS, NORM_ROWS), NORM_ROWS)
                for k in range(NORM_UNROLL)]
        scales = []
        for rw in rows:
            x = x_ref[rw, :]
            scales.append(lax.rsqrt(jnp.mean(x * x, axis=-1, keepdims=True) + EPS))
        for rw, rs in zip(rows, scales):
            y = x_ref[rw, :] * rs * nw
            h_scr[rw, :] = (y * (1.0 + sc) + sh).astype(h_scr.dtype)
        return carry

    lax.fori_loop(0, x_ref.shape[0] // step, body, 0)


def _row_conv3(p, cw, group):
    tm = p.shape[0]
    row = lax.broadcasted_iota(jnp.int32, (tm, 1), 0) % group
    prev = jnp.where(row == 0, 0.0, pltpu.roll(p, 1, 0))
    nxt = jnp.where(row == group - 1, 0.0, pltpu.roll(p, tm - 1, 0))
    return prev * cw[0:1] + p * cw[1:2] + nxt * cw[2:3]


def _adaln_kernel(c_ref, w_ref, b_ref, o_ref):
    a = _silu(c_ref[...]).astype(BF16)
    o_ref[0] = jnp.dot(a, w_ref[0].astype(BF16), preferred_element_type=F32) + b_ref[0]


def _adaln(cc, ada_w, ada_b):
    depth, d, n = ada_w.shape
    tn = 1024
    return pl.pallas_call(
        _adaln_kernel,
        grid=(depth, n // tn),
        in_specs=[pl.BlockSpec((8, d), lambda l, j: (0, 0)),
                  pl.BlockSpec((1, d, tn), lambda l, j: (l, 0, j)),
                  pl.BlockSpec((1, 1, tn), lambda l, j: (l, 0, j))],
        out_specs=pl.BlockSpec((1, 8, tn), lambda l, j: (l, 0, j)),
        out_shape=jax.ShapeDtypeStruct((depth, 8, n), F32),
        compiler_params=_params(("parallel", "parallel")),
        name="adaln",
    )(cc, ada_w, ada_b.reshape(depth, 1, n))


def _qkv_kernel(x_ref, nw_ref, sh_ref, sc_ref, w_ref, wg_ref, cw_ref, gp_ref, qkv_ref, g_ref, h_scr,
                *, group, n_tiles):
    j = pl.program_id(1)

    @pl.when(j == 0)
    def _():
        _modulated_norm_to(h_scr, x_ref, nw_ref, sc_ref, sh_ref)

    @pl.when(j < n_tiles)
    def _():
        p = jnp.dot(h_scr[...], w_ref[...], preferred_element_type=F32)
        y = _silu(_row_conv3(p, cw_ref[...], group))
        for hh in range(y.shape[1] // HEAD_DIM):
            qkv_ref[hh] = y[:, hh * HEAD_DIM:(hh + 1) * HEAD_DIM]

    @pl.when(j == n_tiles)
    def _():
        raw = jnp.dot(h_scr[...], wg_ref[...], preferred_element_type=F32)
        z = raw + gp_ref[1:2]
        softplus = jnp.maximum(z, 0.0) + jnp.log1p(jnp.exp(-jnp.abs(z)))
        lane = lax.broadcasted_iota(jnp.int32, (1, LANES), 1)
        gates = jnp.where(lane < 2 * HEADS, -jnp.exp(gp_ref[0:1]) * softplus, _sigmoid(raw))
        ri = lax.broadcasted_iota(jnp.int32, (LANES, LANES), 0)
        ci = lax.broadcasted_iota(jnp.int32, (LANES, LANES), 1)
        same = (ri // CHUNK) == (ci // CHUNK)
        pre01 = jnp.where(same & (ci <= ri), 1.0, 0.0).astype(BF16)
        suf01 = jnp.where(same & (ci >= ri), 1.0, 0.0).astype(BF16)
        pieces = _split3_bf16(gates)
        for r0 in range(0, gates.shape[0], LANES):
            blk = [p[r0:r0 + LANES] for p in pieces]
            pre = [jnp.dot(pre01, b, preferred_element_type=F32) for b in blk]
            suf = [jnp.dot(suf01, b, preferred_element_type=F32) for b in blk]
            pre = (pre[0] + pre[1]) + pre[2]
            suf = (suf[0] + suf[1]) + suf[2]
            g_ref[r0:r0 + LANES] = jnp.where(lane < HEADS, pre,
                                             jnp.where(lane < 2 * HEADS, suf, gates[r0:r0 + LANES]))


def _qkv_proj(xin, mods, nw, w_in, cw, gate_par, *, rows_per_batch, mod_row0, group):
    r, d = xin.shape
    tm, tn = 1024, 768
    n_tiles = QKV_W // tn
    hpt = tn // HEAD_DIM
    tiles_per_batch = max(rows_per_batch // tm, 1)

    def mod_idx(part):
        if rows_per_batch >= tm:
            return lambda i, j: (mod_row0 + i // tiles_per_batch, 0, part)
        return lambda i, j: (mod_row0, 0, part)

    last = n_tiles - 1
    return pl.pallas_call(
        functools.partial(_qkv_kernel, group=group, n_tiles=n_tiles),
        grid=(r // tm, n_tiles + 1),
        in_specs=[pl.BlockSpec((tm, d), lambda i, j: (i, 0)),
                  pl.BlockSpec((1, d), lambda i, j: (0, 0)),
                  pl.BlockSpec((1, 1, d), mod_idx(0)),
                  pl.BlockSpec((1, 1, d), mod_idx(1)),
                  pl.BlockSpec((d, tn), lambda i, j: (0, jnp.minimum(j, last))),
                  pl.BlockSpec((d, LANES), lambda i, j: (0, GATE_COL0 // LANES)),
                  pl.BlockSpec((3, tn), lambda i, j: (0, jnp.minimum(j, last))),
                  pl.BlockSpec((2, LANES), lambda i, j: (0, 0))],
        out_specs=[pl.BlockSpec((hpt, tm, HEAD_DIM), lambda i, j: (jnp.minimum(j, last), i, 0)),
                   pl.BlockSpec((tm, LANES), lambda i, j: (i, 0))],
        out_shape=[jax.ShapeDtypeStruct((3 * HEADS, r, HEAD_DIM), F32),
                   jax.ShapeDtypeStruct((r, LANES), F32)],
        scratch_shapes=[pltpu.VMEM((tm, d), BF16)],
        compiler_params=_params(("parallel", "arbitrary")),
        name="qkv_proj",
    )(xin, nw, mods, mods, w_in, w_in, cw, gate_par)


def _mixb_kernel(x_ref, nw_ref, sh_ref, sc_ref, wz_ref, wb_ref, wc_ref, wh_ref, wzb_ref, cw_ref,
                 gz_ref, yb_ref, h_scr, *, group):
    j = pl.program_id(1)

    @pl.when(j == 0)
    def _():
        _modulated_norm_to(h_scr, x_ref, nw_ref, sc_ref, sh_ref)

    h = h_scr[...]
    dot = lambda w: jnp.dot(h, w[...], preferred_element_type=F32)
    gz_ref[...] = _silu(dot(wz_ref)).astype(gz_ref.dtype)
    conv = _row_conv3(dot(wc_ref) * dot(wh_ref), cw_ref[...], group)
    yb_ref[...] = (dot(wb_ref) * conv * _silu(dot(wzb_ref))).astype(yb_ref.dtype)


def _mixb_proj(xin, mods, nw, w_in, w_bchz, cw, *, rows_per_batch, mod_row0, group):
    r, d = xin.shape
    tm, tn = 1024, 256
    nj = d // tn
    za_blk0 = QKV_W // tn
    tiles_per_batch = max(rows_per_batch // tm, 1)

    def mod_idx(part):
        if rows_per_batch >= tm:
            return lambda i, j: (mod_row0 + i // tiles_per_batch, 0, part)
        return lambda i, j: (mod_row0, 0, part)

    wspec = lambda reg: pl.BlockSpec((d, tn), lambda i, j: (0, reg * nj + j))
    return pl.pallas_call(
        functools.partial(_mixb_kernel, group=group),
        grid=(r // tm, nj),
        in_specs=[pl.BlockSpec((tm, d), lambda i, j: (i, 0)),
                  pl.BlockSpec((1, d), lambda i, j: (0, 0)),
                  pl.BlockSpec((1, 1, d), mod_idx(0)),
                  pl.BlockSpec((1, 1, d), mod_idx(1)),
                  pl.BlockSpec((d, tn), lambda i, j: (0, za_blk0 + j)),
                  wspec(0), wspec(1), wspec(2), wspec(3),
                  pl.BlockSpec((3, tn), lambda i, j: (0, j))],
        out_specs=[pl.BlockSpec((tm, tn), lambda i, j: (i, j)),
                   pl.BlockSpec((tm, tn), lambda i, j: (i, j))],
        out_shape=[jax.ShapeDtypeStruct((r, d), BF16), jax.ShapeDtypeStruct((r, d), BF16)],
        scratch_shapes=[pltpu.VMEM((tm, d), BF16)],
        compiler_params=_params(("parallel", "arbitrary")),
        name="mixb_proj",
    )(xin, nw, mods, mods, w_in, w_bchz, w_bchz, w_bchz, w_bchz, cw)


def _split3_bf16(x):
    x1 = x.astype(BF16)
    r1 = x - x1.astype(F32)
    x2 = r1.astype(BF16)
    x3 = (r1 - x2.astype(F32)).astype(BF16)
    return x1, x2, x3


def _block_diag2(x, left):
    return jnp.concatenate([jnp.where(left, x, 0.0), jnp.where(left, 0.0, x)], axis=0).astype(BF16)


def _unit_tri_inverse_minus_eye(ms, left, ri, cj):
    n = ms[0].shape[0]
    ys = None
    s = 1
    while s < n:
        shift = s.bit_length() - 1
        br = lax.shift_right_logical(ri, shift)
        bc = lax.shift_right_logical(cj, shift)
        mask_lo = ((br & 1) == 1) & (bc == br - 1)
        mask_up = ((bc & 1) == 1) & (br == bc - 1)
        mask = (left & mask_lo) | (jnp.logical_not(left) & mask_up)
        cms = [jnp.where(mask, m, 0.0) for m in ms]
        if ys is None:
            ys = [-cm for cm in cms]
        else:
            yc = [jnp.dot(y.astype(BF16), _block_diag2(cm, left), preferred_element_type=F32)
                  for cm, y in zip(cms, ys)]
            yield
            ps = [cm + t for cm, t in zip(cms, yc)]
            py = [jnp.dot(p.astype(BF16), _block_diag2(y, left), preferred_element_type=F32)
                  for y, p in zip(ys, ps)]
            yield
            ys = [y - p - t for y, p, t in zip(ys, ps, py)]
        s *= 2
    return ys


def _run_interleaved(*gens):
    active = list(gens)
    while active:
        for g in list(active):
            try:
                next(g)
            except StopIteration:
                active.remove(g)


def _gdn_kernel(qc_ref, kc_ref, vc_ref, qx_ref, kx_ref, vx_ref, gc_ref, gx_ref,
                hn_ref, oc_ref, ox_ref,
                qs, ks, vs, gsc, bsc, nq_s, b_s, op_s, gl_s, oacc, st_scr):
    k = pl.program_id(0)
    h = jnp.minimum(k, pl.num_programs(0) - 2) % HEADS
    cur_slot = k % 2
    prev_slot = 1 - cur_slot
    tc = qc_ref.shape[1]
    tx = qx_ref.shape[1]
    t = tc + tx
    nsteps = t // CHUNK
    nc_ctx = tc // CHUNK
    c = CHUNK

    def l2n(a):
        return a * lax.rsqrt(jnp.sum(a * a, axis=-1, keepdims=True) + EPS)

    @pl.when(k == 0)
    def _():
        for ref in (nq_s, b_s, op_s, gl_s, oacc, st_scr):
            ref[...] = jnp.zeros_like(ref)

    qscale = HEAD_DIM ** -0.5
    qs[0:tc] = l2n(qc_ref[0]) * qscale
    qs[tc:t] = l2n(qx_ref[0]) * qscale
    ks[0:tc] = l2n(kc_ref[0])
    ks[tc:t] = l2n(kx_ref[0])
    vs[0:tc] = vc_ref[0]
    vs[tc:t] = vx_ref[0]

    lane = lax.broadcasted_iota(jnp.int32, (1, LANES), 1)
    for d in range(2):
        for ref, lo, hi in ((gc_ref, 0, tc), (gx_ref, tc, t)):
            gates = ref[...]
            g = jnp.sum(jnp.where(lane == d * HEADS + h, gates, 0.0), axis=-1, keepdims=True)
            b = jnp.sum(jnp.where(lane == 2 * HEADS + d * HEADS + h, gates, 0.0), axis=-1,
                        keepdims=True)
            gsc[d, lo:hi] = jnp.broadcast_to(g, (hi - lo, LANES))
            bsc[d, lo:hi] = jnp.broadcast_to(b, (hi - lo, LANES))

    def rev_chunk(s):
        return nc_ctx - 1 - s if s < nc_ctx else nsteps + nc_ctx - 1 - s

    ri = lax.broadcasted_iota(jnp.int32, (c, 2 * c), 0)
    lane2 = lax.broadcasted_iota(jnp.int32, (c, 2 * c), 1)
    left = lane2 < c
    cj = lane2 & (c - 1)
    strict = (left & (cj < ri)) | (jnp.logical_not(left) & (cj > ri))
    incl = (left & (cj <= ri)) | (jnp.logical_not(left) & (cj >= ri))
    left_sq = lax.broadcasted_iota(jnp.int32, (2 * c, 2 * c), 1) < c

    def prep(steps):
        units = range(len(steps))
        rows = [[slice(n * c, (n + 1) * c) for n in (s, rev_chunk(s))] for s in steps]
        qv = [[qs[r] for r in rows[uu]] for uu in units]
        kv = [[ks[r] for r in rows[uu]] for uu in units]
        vv = [[vs[r] for r in rows[uu]] for uu in units]
        beta = [[bsc[d, rows[uu][d]] for d in range(2)] for uu in units]
        kb = [[kv[uu][d] * beta[uu][d] for d in range(2)] for uu in units]
        gk = [lax.dot_general(
                  jnp.concatenate([kb[uu][0], qv[uu][0], kb[uu][1], qv[uu][1]], axis=0).astype(BF16),
                  jnp.concatenate(kv[uu], axis=0).astype(BF16),
                  (((1,), (1,)), ((), ())), preferred_element_type=F32)
              for uu in units]
        yield
        gcb = [[gsc[d, rows[uu][d]] for d in range(2)] for uu in units]
        ms, avs = [], []
        for uu in units:
            col = jnp.where(left, gcb[uu][0], gcb[uu][1])
            row = jnp.transpose(jnp.concatenate(gcb[uu], axis=0))[0:c]
            decay = jnp.exp(jnp.where(incl, col - row, 0.0))
            kk = jnp.where(left, gk[uu][0:c], gk[uu][2 * c:3 * c])
            qk = jnp.where(left, gk[uu][c:2 * c], gk[uu][3 * c:4 * c])
            ms.append(jnp.where(strict, kk * decay, 0.0))
            avs.append(jnp.where(incl, qk * decay, 0.0))
        ys = yield from _unit_tri_inverse_minus_eye(ms, left, ri, cj)
        egc = [[jnp.exp(g) for g in gcb[uu]] for uu in units]
        rs = [[jnp.concatenate([kb[uu][d] * egc[uu][d], vv[uu][d] * beta[uu][d]], axis=1)
               for d in range(2)] for uu in units]
        corr = [jnp.dot(_block_diag2(ys[uu], left),
                        jnp.concatenate(rs[uu], axis=0).astype(BF16), preferred_element_type=F32)
                for uu in units]
        yield
        wus, aks, g_edges = [], [], []
        for uu in units:
            kgs, edges = [], []
            for d in range(2):
                edge = c - 1 if d == 0 else 0
                edges.append(gcb[uu][d][edge:edge + 1])
                kgs.append(kv[uu][d] * jnp.exp(edges[d] - gcb[uu][d]))
            g_edges.append(edges)
            wus.append([rs[uu][d] + corr[uu][d * c:(d + 1) * c] for d in range(2)])
            kgt = jnp.transpose(jnp.concatenate(kgs, axis=0))
            aks.append(jnp.concatenate(
                [jnp.where(left, avs[uu], 0.0), jnp.where(left_sq, kgt, 0.0),
                 jnp.where(left, 0.0, avs[uu]), jnp.where(left_sq, 0.0, kgt)], axis=0).astype(BF16))
        prods = [jnp.dot(aks[uu], jnp.concatenate(wus[uu], axis=0).astype(BF16),
                         preferred_element_type=F32) for uu in units]
        yield
        for uu in units:
            for d in range(2):
                x = prods[uu][d * (c + HEAD_DIM):(d + 1) * (c + HEAD_DIM)]
                a_w, a_u = x[0:c, 0:HEAD_DIM], x[0:c, HEAD_DIM:2 * HEAD_DIM]
                k_w, k_u = x[c:c + HEAD_DIM, 0:HEAD_DIM], x[c:c + HEAD_DIM, HEAD_DIM:2 * HEAD_DIM]
                idx = 2 * steps[uu] + d
                nq_s[idx] = jnp.concatenate([-k_w, qv[uu][d] * egc[uu][d] - a_w],
                                            axis=0).astype(BF16)
                b_s[idx] = k_u
                op_s[idx] = a_u
                gl_s[idx] = jnp.broadcast_to(jnp.exp(g_edges[uu][d]), (8, LANES))

    def scan(steps, states, slot, written):
        for s in steps:
            idxs = (2 * s, 2 * s + 1)
            xs = [jnp.dot(nq_s[idx], st.astype(BF16), preferred_element_type=F32)
                  for idx, st in zip(idxs, states)]
            yield
            for n, idx, x in zip((s, rev_chunk(s)), idxs, xs):
                o = x[HEAD_DIM:HEAD_DIM + c] + op_s[idx]
                rows = slice(n * c, (n + 1) * c)
                oacc[slot, rows] = oacc[slot, rows] + o if n in written else o
                written.add(n)
            states[:] = [st * gl_s[idx][0:1] + x[0:HEAD_DIM] + b_s[idx]
                         for idx, st, x in zip(idxs, states, xs)]

    bounds = np.cumsum((0,) + SCAN_GROUPS)
    assert bounds[-1] == nsteps
    groups = [list(range(lo, hi)) for lo, hi in zip(bounds[:-1], bounds[1:])]
    early = [s for g in groups[:-1] for s in g]
    written_early = {n for s in early for n in (s, rev_chunk(s))}

    prev_states = [st_scr[0], st_scr[1]]
    _run_interleaved(prep(groups[0]), scan(groups[-1], prev_states, prev_slot, set(written_early)))
    o = oacc[prev_slot]
    on = o * lax.rsqrt(jnp.mean(o * o, axis=-1, keepdims=True) + EPS) * hn_ref[...]
    oc_ref[0] = on[0:tc].astype(oc_ref.dtype)
    ox_ref[0] = on[tc:t].astype(ox_ref.dtype)

    states = [jnp.zeros((HEAD_DIM, HEAD_DIM), F32)] * 2
    written = set()
    for gi in range(1, len(groups)):
        _run_interleaved(prep(groups[gi]), scan(groups[gi - 1], states, cur_slot, written))
    assert written == written_early
    st_scr[0] = states[0]
    st_scr[1] = states[1]


def _gdn(qkv_c, qkv_x, gates_c, gates_x, head_norm):
    tc, tx = CTX_LEN, SEQ
    t = tc + tx
    nidx = 2 * (t // CHUNK)
    nprob = BATCH * HEADS
    cur = lambda k: jnp.minimum(k, nprob - 1)
    prev = lambda k: jnp.maximum(k - 1, 0)
    part = lambda rows, p: pl.BlockSpec(
        (1, rows, HEAD_DIM), lambda k: (p * HEADS + cur(k) % HEADS, cur(k) // HEADS, 0))
    return pl.pallas_call(
        _gdn_kernel,
        grid=(nprob + 1,),
        in_specs=[part(tc, 0), part(tc, 1), part(tc, 2),
                  part(tx, 0), part(tx, 1), part(tx, 2),
                  pl.BlockSpec((tc, LANES), lambda k: (cur(k) // HEADS, 0)),
                  pl.BlockSpec((tx, LANES), lambda k: (cur(k) // HEADS, 0)),
                  pl.BlockSpec((1, HEAD_DIM), lambda k: (0, 0))],
        out_specs=[pl.BlockSpec((1, tc, HEAD_DIM), lambda k: (prev(k) // HEADS, 0, prev(k) % HEADS)),
                   pl.BlockSpec((1, tx, HEAD_DIM), lambda k: (prev(k) // HEADS, 0, prev(k) % HEADS))],
        out_shape=[jax.ShapeDtypeStruct((BATCH, tc, D_MODEL), BF16),
                   jax.ShapeDtypeStruct((BATCH, tx, D_MODEL), BF16)],
        scratch_shapes=[pltpu.VMEM((t, HEAD_DIM), F32), pltpu.VMEM((t, HEAD_DIM), F32),
                        pltpu.VMEM((t, HEAD_DIM), F32),
                        pltpu.VMEM((2, t, LANES), F32), pltpu.VMEM((2, t, LANES), F32),
                        pltpu.VMEM((nidx, HEAD_DIM + CHUNK, HEAD_DIM), BF16),
                        pltpu.VMEM((nidx, HEAD_DIM, HEAD_DIM), F32),
                        pltpu.VMEM((nidx, CHUNK, HEAD_DIM), F32),
                        pltpu.VMEM((nidx, 8, LANES), F32),
                        pltpu.VMEM((2, t, HEAD_DIM), F32),
                        pltpu.VMEM((2, HEAD_DIM, HEAD_DIM), F32)],
        compiler_params=_params(("arbitrary",)),
        name="gdn",
    )(qkv_c, qkv_c, qkv_c, qkv_x, qkv_x, qkv_x, gates_c, gates_x, head_norm)


def _even_out_kernel(x_ref, gt_ref, o_ref, gz_ref, yb_ref, w_ref, out_ref, a_scr):
    j = pl.program_id(1)

    @pl.when(j == 0)
    def _():
        d = o_ref.shape[1]
        a_scr[:, 0:d] = o_ref[...] * gz_ref[...]
        a_scr[:, d:2 * d] = yb_ref[...]

    y = jnp.dot(a_scr[...], w_ref[...], preferred_element_type=F32)
    out_ref[...] = x_ref[...] + gt_ref[0] * y


def _even_out(xin, mods, o, gz, yb, w_out, *, rows_per_batch, mod_row0):
    r, d = xin.shape
    tm = min(1024, rows_per_batch)
    tn = 512
    nj = d // tn
    tiles_per_batch = rows_per_batch // tm
    fixed = mod_row0 >= BATCH
    gt_idx = ((lambda i, j: (mod_row0, 0, 2 * nj + j)) if fixed
              else (lambda i, j: (i // tiles_per_batch, 0, 2 * nj + j)))
    return pl.pallas_call(
        _even_out_kernel,
        grid=(r // tm, nj),
        in_specs=[pl.BlockSpec((tm, tn), lambda i, j: (i, j)),
                  pl.BlockSpec((1, 1, tn), gt_idx),
                  pl.BlockSpec((tm, d), lambda i, j: (i, 0)),
                  pl.BlockSpec((tm, d), lambda i, j: (i, 0)),
                  pl.BlockSpec((tm, d), lambda i, j: (i, 0)),
                  pl.BlockSpec((2 * d, tn), lambda i, j: (0, j))],
        out_specs=pl.BlockSpec((tm, tn), lambda i, j: (i, j)),
        out_shape=jax.ShapeDtypeStruct((r, d), F32),
        scratch_shapes=[pltpu.VMEM((tm, 2 * d), BF16)],
        compiler_params=_params(("parallel", "arbitrary")),
        name="even_out",
    )(xin, mods, o, gz, yb, w_out)


def _odd_in_kernel(x_ref, nw_ref, sh_ref, sc_ref, w_ref, out_ref, h_scr, *, tiles_per_region):
    j = pl.program_id(1)

    @pl.when(j == 0)
    def _():
        _modulated_norm_to(h_scr, x_ref, nw_ref, sc_ref, sh_ref)

    region = j // tiles_per_region

    def tile(act):
        for lo in range(0, w_ref.shape[1], SUB_N):
            w = w_ref[:, lo:lo + SUB_N].astype(BF16)
            p = jnp.dot(h_scr[...], w, preferred_element_type=F32)
            out_ref[:, lo:lo + SUB_N] = act(p).astype(out_ref.dtype)

    @pl.when(region <= 1)
    def _():
        tile(_gelu_tanh)

    @pl.when((region == 2) | (region == 4))
    def _():
        tile(_silu)

    @pl.when(region == 3)
    def _():
        tile(lambda p: p)


def _odd_in(xin, mods, nw, w):
    r, d = xin.shape
    n = w.shape[1]
    tm, tn = 1024, 1024
    tiles_per_batch = SEQ // tm
    mod_idx = lambda part: (lambda i, j: (i // tiles_per_batch, 0, part))
    return pl.pallas_call(
        functools.partial(_odd_in_kernel, tiles_per_region=d // tn),
        grid=(r // tm, n // tn),
        in_specs=[pl.BlockSpec((tm, d), lambda i, j: (i, 0)),
                  pl.BlockSpec((1, d), lambda i, j: (0, 0)),
                  pl.BlockSpec((1, 1, d), mod_idx(0)),
                  pl.BlockSpec((1, 1, d), mod_idx(1)),
                  pl.BlockSpec((d, tn), lambda i, j: (0, j))],
        out_specs=pl.BlockSpec((tm, tn), lambda i, j: (i, j)),
        out_shape=jax.ShapeDtypeStruct((r, n), BF16),
        scratch_shapes=[pltpu.VMEM((tm, d), BF16)],
        compiler_params=_params(("parallel", "arbitrary")),
        name="odd_in",
    )(xin, nw, mods, mods, w)


def _odd_mix_kernel(x_ref, gt_ref, u_ref, v_ref, zc_ref, pd_ref, zd_ref, lnw_ref, lnb_ref, ws_ref,
                    bs_ref, band_ref, icnt_ref, pw_ref, ps_ref, wo_ref, fnw_ref, out_ref,
                    acc, ln_scr, *, nk_c):
    kk = pl.program_id(1)
    tm = x_ref.shape[0]
    tk = u_ref.shape[1]
    gpb = tk // HEAD_DIM

    @pl.when(kk == 0)
    def _():
        v = v_ref[...].astype(F32)
        mu = jnp.mean(v, axis=-1, keepdims=True)
        xc = v - mu
        var = jnp.mean(xc * xc, axis=-1, keepdims=True)
        ln = (xc * lax.rsqrt(var + EPS) * lnw_ref[...] + lnb_ref[...]).astype(BF16)
        for blk in range(nk_c):
            ln_scr[blk] = ln[:, blk * tk:(blk + 1) * tk]
        acc[...] = jnp.zeros_like(acc)

    @pl.when(kk < nk_c)
    def _():
        ln = ln_scr[kk]
        cols = []
        for gi in range(gpb):
            w_g = ws_ref[gi]
            b_g = bs_ref[gi]
            rows = []
            for cc in range(tm // TOK_CHUNK):
                blk = ln[cc * TOK_CHUNK:(cc + 1) * TOK_CHUNK, gi * HEAD_DIM:(gi + 1) * HEAD_DIM]
                rows.append(jnp.dot(w_g, blk, preferred_element_type=F32) + b_g)
            cols.append(jnp.concatenate(rows, axis=0))
        s = jnp.concatenate(cols, axis=1)
        yc = u_ref[...] * s * zc_ref[...]
        acc[...] += jnp.dot(yc.astype(BF16), wo_ref[...], preferred_element_type=F32)

    @pl.when(kk >= nk_c)
    def _():
        parts = []
        for pi in range(tk // POOL_GW):
            cols = slice(pi * POOL_GW, (pi + 1) * POOL_GW)
            pd = pd_ref[:, cols]
            wsum = jnp.dot(band_ref[pi], pd, preferred_element_type=F32)
            icnt = jnp.concatenate([icnt_ref[pi]] * (POOL_GW // LANES), axis=1)
            diff = wsum * icnt - pd.astype(F32)
            y = jnp.dot(diff.astype(BF16), pw_ref[pi], preferred_element_type=F32)
            parts.append((y * ps_ref[:, cols] * zd_ref[:, cols]).astype(BF16))
        yd = jnp.concatenate(parts, axis=1)
        acc[...] += jnp.dot(yd, wo_ref[...], preferred_element_type=F32)

    @pl.when(kk == pl.num_programs(1) - 1)
    def _():
        xn = x_ref[...] + gt_ref[0] * acc[...]
        ms = jnp.mean(xn * xn, axis=-1, keepdims=True)
        out_ref[...] = xn * lax.rsqrt(ms + EPS) * fnw_ref[...]


def _pool_constants(tm):
    bands = np.zeros((POOL_GROUPS, tm, tm), np.float32)
    icnt = np.zeros((POOL_GROUPS, tm, LANES), np.float32)
    pos = np.arange(tm)
    row = pos // GRID_W
    col = pos % GRID_W
    for g, rad in enumerate(POOL_RADII):
        same = row[:, None] == row[None, :]
        bands[g] = (same & (np.abs(col[:, None] - col[None, :]) <= rad)).astype(np.float32)
        cnt = np.minimum(col + rad + 1, GRID_W) - np.maximum(col - rad, 0)
        icnt[g] = (1.0 / cnt.astype(np.float32))[:, None]
    return jnp.asarray(bands, BF16), jnp.asarray(icnt, F32)


def _odd_mix(xin, mods, p, ln_w, ln_b, w_s, b_s, pool_w, pool_scale, w_out, fnw):
    r, d = xin.shape
    tm, tk = 512, 1024
    nk_c = d // tk
    nk = 2 * nk_c
    gpb = tk // HEAD_DIM
    ppb = tk // POOL_GW
    tiles_per_batch = SEQ // tm
    bands, icnt = _pool_constants(tm)
    lo = lambda kk: jnp.minimum(kk, nk_c - 1)
    hi = lambda kk: jnp.maximum(kk - nk_c, 0)
    cblk = d // tk
    return pl.pallas_call(
        functools.partial(_odd_mix_kernel, nk_c=nk_c),
        grid=(r // tm, nk),
        in_specs=[pl.BlockSpec((tm, d), lambda i, kk: (i, 0)),
                  pl.BlockSpec((1, 1, d), lambda i, kk: (i // tiles_per_batch, 0, 2)),
                  pl.BlockSpec((tm, tk), lambda i, kk: (i, lo(kk))),
                  pl.BlockSpec((tm, d), lambda i, kk: (i, 1)),
                  pl.BlockSpec((tm, tk), lambda i, kk: (i, 2 * cblk + lo(kk))),
                  pl.BlockSpec((tm, tk), lambda i, kk: (i, 3 * cblk + hi(kk))),
                  pl.BlockSpec((tm, tk), lambda i, kk: (i, 4 * cblk + hi(kk))),
                  pl.BlockSpec((1, d), lambda i, kk: (0, 0)),
                  pl.BlockSpec((1, d), lambda i, kk: (0, 0)),
                  pl.BlockSpec((gpb, TOK_CHUNK, TOK_CHUNK), lambda i, kk: (lo(kk), 0, 0)),
                  pl.BlockSpec((gpb, TOK_CHUNK, LANES), lambda i, kk: (lo(kk), 0, 0)),
                  pl.BlockSpec((ppb, tm, tm), lambda i, kk: (hi(kk), 0, 0)),
                  pl.BlockSpec((ppb, tm, LANES), lambda i, kk: (hi(kk), 0, 0)),
                  pl.BlockSpec((ppb, POOL_GW, POOL_GW), lambda i, kk: (hi(kk), 0, 0)),
                  pl.BlockSpec((1, tk), lambda i, kk: (0, hi(kk))),
                  pl.BlockSpec((tk, d), lambda i, kk: (kk, 0)),
                  pl.BlockSpec((1, d), lambda i, kk: (0, 0))],
        out_specs=pl.BlockSpec((tm, d), lambda i, kk: (i, 0)),
        out_shape=jax.ShapeDtypeStruct((r, d), F32),
        scratch_shapes=[pltpu.VMEM((tm, d), F32), pltpu.VMEM((nk_c, tm, tk), BF16)],
        compiler_params=_params(("parallel", "arbitrary")),
        name="odd_mix",
    )(xin, mods, p, p, p, p, p, ln_w, ln_b, w_s, b_s, bands, icnt, pool_w, pool_scale, w_out, fnw)


def kernel(x, c, ctx, c_ctx, ada_w, ada_b, norm_w, e_w_in, e_conv_qkv, e_a_log, e_dt_bias, e_head_norm,
           e_conv_b, e_w_out, o_w_in, o_ln_w, o_ln_b, o_w_s, o_b_s, o_pool_w, o_pool_scale, o_w_out,
           final_norm_w):
    bsz, seq, d = x.shape
    rx = bsz * seq
    rc = bsz * ctx.shape[1]
    x2 = x.reshape(rx, d)
    ctx2 = ctx.reshape(rc, d)

    cc = jnp.concatenate([c, c_ctx[None, :], jnp.zeros((8 - bsz - 1, d), F32)], axis=0)
    mods = _adaln(cc, ada_w, ada_b)
    ctx_row = bsz

    w_in = e_w_in[0].astype(BF16)
    w_bchz = w_in[:, GATE_COL0 + 4 * HEADS:]
    m0 = mods[0].reshape(8, 1, 3 * d)
    nw0 = norm_w[0].reshape(1, d)

    gate_par = jnp.pad(jnp.stack([e_a_log[0].reshape(-1), e_dt_bias[0].reshape(-1)]),
                       ((0, 0), (0, LANES - 2 * HEADS)))
    qkv_x, gates_x = _qkv_proj(x2, m0, nw0, w_in, e_conv_qkv[0], gate_par,
                               rows_per_batch=seq, mod_row0=0, group=GRID_W)
    qkv_c, gates_c = _qkv_proj(ctx2, m0, nw0, w_in, e_conv_qkv[0], gate_par,
                               rows_per_batch=ctx.shape[1], mod_row0=ctx_row, group=ctx.shape[1])
    gz_x, yb_x = _mixb_proj(x2, m0, nw0, w_in, w_bchz, e_conv_b[0],
                            rows_per_batch=seq, mod_row0=0, group=GRID_W)
    gz_c, yb_c = _mixb_proj(ctx2, m0, nw0, w_in, w_bchz, e_conv_b[0],
                            rows_per_batch=ctx.shape[1], mod_row0=ctx_row, group=ctx.shape[1])
    o_c, o_x = _gdn(qkv_c, qkv_x, gates_c, gates_x,
                    e_head_norm[0].reshape(1, HEAD_DIM))
    w_out0 = e_w_out[0].astype(BF16)
    x2 = _even_out(x2, m0, o_x.reshape(rx, d), gz_x, yb_x, w_out0, rows_per_batch=seq, mod_row0=0)
    ctx2 = _even_out(ctx2, m0, o_c.reshape(rc, d), gz_c, yb_c, w_out0,
                     rows_per_batch=ctx.shape[1], mod_row0=ctx_row)

    m1 = mods[1].reshape(8, 1, 3 * d)
    p = _odd_in(x2, m1, norm_w[1].reshape(1, d), o_w_in[0])
    b_s = jnp.broadcast_to(o_b_s[0][:, :, None], (o_b_s.shape[1], TOK_CHUNK, LANES))
    out = _odd_mix(x2, m1, p, o_ln_w[0].reshape(1, d), o_ln_b[0].reshape(1, d),
                   o_w_s[0].astype(BF16), b_s, o_pool_w[0].astype(BF16),
                   o_pool_scale[0].reshape(1, d), o_w_out[0].astype(BF16),
                   final_norm_w.reshape(1, d))
    return out.reshape(bsz, seq, d)
```

```python
import functools
import math

import numpy as np
import jax
import jax.numpy as jnp
from jax import lax
from jax.experimental import pallas as pl
from jax.experimental.pallas import tpu as pltpu

F32 = jnp.float32
BF16 = jnp.bfloat16

D_MODEL = 2048
BATCH = 4
SEQ = 2048
CTX_LEN = 256
GRID_W = 64
EPS = 1e-6
HEADS = 16
HEAD_DIM = 128
QKV_W = 3 * D_MODEL
GATE_COL0 = QKV_W + D_MODEL
CHUNK = 64
SCAN_GROUPS = (12, 13, 11)
TOK_CHUNK = 128
POOL_GROUPS = 4
POOL_GW = D_MODEL // POOL_GROUPS
POOL_RADII = (1, 2, 4, 8)

LANES = 128
VMEM_LIMIT = 56 * 1024 * 1024
NORM_ROWS = 16
NORM_UNROLL = 16
SUB_N = 256


def _params(sem):
    return pltpu.CompilerParams(dimension_semantics=sem, vmem_limit_bytes=VMEM_LIMIT)


def _sigmoid(x):
    return 1.0 / (1.0 + jnp.exp(-x))


def _silu(x):
    return x * _sigmoid(x)


def _gelu_tanh(x):
    cdf = 0.5 * (1.0 + jnp.tanh(math.sqrt(2.0 / math.pi) * (x + 0.044715 * (x * x * x))))
    return x * cdf


def _modulated_norm_to(h_scr, x_ref, nw_ref, sc_ref, sh_ref):
    nw = nw_ref[...]
    sc = sc_ref[0]
    sh = sh_ref[0]

    step = NORM_ROWS * NORM_UNROLL

    def body(r, carry):
        rows = [pl.ds(pl.multiple_of(r * step + k * NORM_ROWS, NORM_ROWS), NORM_ROWS)
                for k in range(NORM_UNROLL)]
        scales = []
        for rw in rows:
            x = x_ref[rw, :]
            scales.append(lax.rsqrt(jnp.mean(x * x, axis=-1, keepdims=True) + EPS))
        for rw, rs in zip(rows, scales):
            y = x_ref[rw, :] * rs * nw
            h_scr[rw, :] = (y * (1.0 + sc) + sh).astype(h_scr.dtype)
        return carry

    lax.fori_loop(0, x_ref.shape[0] // step, body, 0)


def _row_conv3(p, cw, group):
    tm = p.shape[0]
    row = lax.broadcasted_iota(jnp.int32, (tm, 1), 0) % group
    prev = jnp.where(row == 0, 0.0, pltpu.roll(p, 1, 0))
    nxt = jnp.where(row == group - 1, 0.0, pltpu.roll(p, tm - 1, 0))
    return prev * cw[0:1] + p * cw[1:2] + nxt * cw[2:3]


def _adaln_kernel(c_ref, w_ref, b_ref, o_ref):
    a = _silu(c_ref[...]).astype(BF16)
    o_ref[0] = jnp.dot(a, w_ref[0].astype(BF16), preferred_element_type=F32) + b_ref[0]


def _adaln(cc, ada_w, ada_b):
    depth, d, n = ada_w.shape
    tn = 1024
    return pl.pallas_call(
        _adaln_kernel,
        grid=(depth, n // tn),
        in_specs=[pl.BlockSpec((8, d), lambda l, j: (0, 0)),
                  pl.BlockSpec((1, d, tn), lambda l, j: (l, 0, j)),
                  pl.BlockSpec((1, 1, tn), lambda l, j: (l, 0, j))],
        out_specs=pl.BlockSpec((1, 8, tn), lambda l, j: (l, 0, j)),
        out_shape=jax.ShapeDtypeStruct((depth, 8, n), F32),
        compiler_params=_params(("parallel", "parallel")),
        name="adaln",
    )(cc, ada_w, ada_b.reshape(depth, 1, n))


def _qkv_kernel(x_ref, nw_ref, sh_ref, sc_ref, w_ref, wg_ref, cw_ref, gp_ref, qkv_ref, g_ref, h_scr,
                *, group, n_tiles):
    j = pl.program_id(1)

    @pl.when(j == 0)
    def _():
        _modulated_norm_to(h_scr, x_ref, nw_ref, sc_ref, sh_ref)

    def tile():
        p = jnp.dot(h_scr[...], w_ref[...], preferred_element_type=F32)
        y = _silu(_row_conv3(p, cw_ref[...], group))
        for hh in range(y.shape[1] // HEAD_DIM):
            qkv_ref[hh] = y[:, hh * HEAD_DIM:(hh + 1) * HEAD_DIM]

    def gates():
        raw = jnp.dot(h_scr[...], wg_ref[...], preferred_element_type=F32)
        z = raw + gp_ref[1:2]
        softplus = jnp.maximum(z, 0.0) + jnp.log1p(jnp.exp(-jnp.abs(z)))
        lane = lax.broadcasted_iota(jnp.int32, (1, LANES), 1)
        gates = jnp.where(lane < 2 * HEADS, -jnp.exp(gp_ref[0:1]) * softplus, _sigmoid(raw))
        ri = lax.broadcasted_iota(jnp.int32, (LANES, LANES), 0)
        ci = lax.broadcasted_iota(jnp.int32, (LANES, LANES), 1)
        same = (ri // CHUNK) == (ci // CHUNK)
        pre01 = jnp.where(same & (ci <= ri), 1.0, 0.0).astype(BF16)
        suf01 = jnp.where(same & (ci >= ri), 1.0, 0.0).astype(BF16)
        pieces = _split3_bf16(gates)
        for r0 in range(0, gates.shape[0], LANES):
            blk = [p[r0:r0 + LANES] for p in pieces]
            pre = [jnp.dot(pre01, b, preferred_element_type=F32) for b in blk]
            suf = [jnp.dot(suf01, b, preferred_element_type=F32) for b in blk]
            pre = (pre[0] + pre[1]) + pre[2]
            suf = (suf[0] + suf[1]) + suf[2]
            g_ref[r0:r0 + LANES] = jnp.where(lane < HEADS, pre,
                                             jnp.where(lane < 2 * HEADS, suf, gates[r0:r0 + LANES]))

    @pl.when(j < n_tiles - 1)
    def _():
        tile()

    @pl.when(j == n_tiles - 1)
    def _():
        tile()
        gates()


def _qkv_proj(xin, mods, nw, w_in, cw, gate_par, *, rows_per_batch, mod_row0, group):
    r, d = xin.shape
    tm, tn = 1024, 768
    n_tiles = QKV_W // tn
    hpt = tn // HEAD_DIM
    tiles_per_batch = max(rows_per_batch // tm, 1)

    def mod_idx(part):
        if rows_per_batch >= tm:
            return lambda i, j: (mod_row0 + i // tiles_per_batch, 0, part)
        return lambda i, j: (mod_row0, 0, part)

    return pl.pallas_call(
        functools.partial(_qkv_kernel, group=group, n_tiles=n_tiles),
        grid=(r // tm, n_tiles),
        in_specs=[pl.BlockSpec((tm, d), lambda i, j: (i, 0)),
                  pl.BlockSpec((1, d), lambda i, j: (0, 0)),
                  pl.BlockSpec((1, 1, d), mod_idx(0)),
                  pl.BlockSpec((1, 1, d), mod_idx(1)),
                  pl.BlockSpec((d, tn), lambda i, j: (0, j)),
                  pl.BlockSpec((d, LANES), lambda i, j: (0, GATE_COL0 // LANES)),
                  pl.BlockSpec((3, tn), lambda i, j: (0, j)),
                  pl.BlockSpec((2, LANES), lambda i, j: (0, 0))],
        out_specs=[pl.BlockSpec((hpt, tm, HEAD_DIM), lambda i, j: (j, i, 0)),
                   pl.BlockSpec((tm, LANES), lambda i, j: (i, 0))],
        out_shape=[jax.ShapeDtypeStruct((3 * HEADS, r, HEAD_DIM), F32),
                   jax.ShapeDtypeStruct((r, LANES), F32)],
        scratch_shapes=[pltpu.VMEM((tm, d), BF16)],
        compiler_params=_params(("parallel", "arbitrary")),
        name="qkv_proj",
    )(xin, nw, mods, mods, w_in, w_in, cw, gate_par)


def _mixb_kernel(x_ref, nw_ref, sh_ref, sc_ref, wz_ref, wb_ref, wc_ref, wh_ref, wzb_ref, cw_ref,
                 gz_ref, yb_ref, h_scr, *, group):
    j = pl.program_id(1)

    @pl.when(j == 0)
    def _():
        _modulated_norm_to(h_scr, x_ref, nw_ref, sc_ref, sh_ref)

    h = h_scr[...]
    dot = lambda w: jnp.dot(h, w[...], preferred_element_type=F32)
    gz_ref[...] = _silu(dot(wz_ref)).astype(gz_ref.dtype)
    conv = _row_conv3(dot(wc_ref) * dot(wh_ref), cw_ref[...], group)
    yb_ref[...] = (dot(wb_ref) * conv * _silu(dot(wzb_ref))).astype(yb_ref.dtype)


def _mixb_proj(xin, mods, nw, w_in, w_bchz, cw, *, rows_per_batch, mod_row0, group):
    r, d = xin.shape
    tm, tn = 1024, 256
    nj = d // tn
    za_blk0 = QKV_W // tn
    tiles_per_batch = max(rows_per_batch // tm, 1)

    def mod_idx(part):
        if rows_per_batch >= tm:
            return lambda i, j: (mod_row0 + i // tiles_per_batch, 0, part)
        return lambda i, j: (mod_row0, 0, part)

    wspec = lambda reg: pl.BlockSpec((d, tn), lambda i, j: (0, reg * nj + j))
    return pl.pallas_call(
        functools.partial(_mixb_kernel, group=group),
        grid=(r // tm, nj),
        in_specs=[pl.BlockSpec((tm, d), lambda i, j: (i, 0)),
                  pl.BlockSpec((1, d), lambda i, j: (0, 0)),
                  pl.BlockSpec((1, 1, d), mod_idx(0)),
                  pl.BlockSpec((1, 1, d), mod_idx(1)),
                  pl.BlockSpec((d, tn), lambda i, j: (0, za_blk0 + j)),
                  wspec(0), wspec(1), wspec(2), wspec(3),
                  pl.BlockSpec((3, tn), lambda i, j: (0, j))],
        out_specs=[pl.BlockSpec((tm, tn), lambda i, j: (i, j)),
                   pl.BlockSpec((tm, tn), lambda i, j: (i, j))],
        out_shape=[jax.ShapeDtypeStruct((r, d), BF16), jax.ShapeDtypeStruct((r, d), BF16)],
        scratch_shapes=[pltpu.VMEM((tm, d), BF16)],
        compiler_params=_params(("parallel", "arbitrary")),
        name="mixb_proj",
    )(xin, nw, mods, mods, w_in, w_bchz, w_bchz, w_bchz, w_bchz, cw)


def _split3_bf16(x):
    x1 = x.astype(BF16)
    r1 = x - x1.astype(F32)
    x2 = r1.astype(BF16)
    x3 = (r1 - x2.astype(F32)).astype(BF16)
    return x1, x2, x3


def _block_diag2(x, left):
    return jnp.concatenate([jnp.where(left, x, 0.0), jnp.where(left, 0.0, x)], axis=0).astype(BF16)


def _unit_tri_inverse_minus_eye(ms, left, ri, cj):
    n = ms[0].shape[0]
    ys = None
    s = 1
    while s < n:
        shift = s.bit_length() - 1
        br = lax.shift_right_logical(ri, shift)
        bc = lax.shift_right_logical(cj, shift)
        mask_lo = ((br & 1) == 1) & (bc == br - 1)
        mask_up = ((bc & 1) == 1) & (br == bc - 1)
        mask = (left & mask_lo) | (jnp.logical_not(left) & mask_up)
        cms = [jnp.where(mask, m, 0.0) for m in ms]
        if ys is None:
            ys = [-cm for cm in cms]
        else:
            yc = [jnp.dot(y.astype(BF16), _block_diag2(cm, left), preferred_element_type=F32)
                  for cm, y in zip(cms, ys)]
            yield
            ps = [cm + t for cm, t in zip(cms, yc)]
            py = [jnp.dot(p.astype(BF16), _block_diag2(y, left), preferred_element_type=F32)
                  for y, p in zip(ys, ps)]
            yield
            ys = [y - p - t for y, p, t in zip(ys, ps, py)]
        s *= 2
    return ys


def _run_interleaved(*gens):
    active = list(gens)
    while active:
        for g in list(active):
            try:
                next(g)
            except StopIteration:
                active.remove(g)


def _gdn_kernel(qc_ref, kc_ref, vc_ref, qx_ref, kx_ref, vx_ref, gc_ref, gx_ref,
                hn_ref, oc_ref, ox_ref,
                qs, ks, vs, gsc, bsc, nq_s, b_s, op_s, gl_s, oacc, st_scr):
    k = pl.program_id(0)
    h = jnp.minimum(k, pl.num_programs(0) - 2) % HEADS
    cur_slot = k % 2
    prev_slot = 1 - cur_slot
    tc = qc_ref.shape[1]
    tx = qx_ref.shape[1]
    t = tc + tx
    nsteps = t // CHUNK
    nc_ctx = tc // CHUNK
    c = CHUNK

    def l2n(a):
        return a * lax.rsqrt(jnp.sum(a * a, axis=-1, keepdims=True) + EPS)

    @pl.when(k == 0)
    def _():
        for ref in (nq_s, b_s, op_s, gl_s, oacc, st_scr):
            ref[...] = jnp.zeros_like(ref)

    qscale = HEAD_DIM ** -0.5
    qs[0:tc] = l2n(qc_ref[0]) * qscale
    qs[tc:t] = l2n(qx_ref[0]) * qscale
    ks[0:tc] = l2n(kc_ref[0])
    ks[tc:t] = l2n(kx_ref[0])
    vs[0:tc] = vc_ref[0]
    vs[tc:t] = vx_ref[0]

    lane = lax.broadcasted_iota(jnp.int32, (1, LANES), 1)
    for d in range(2):
        for ref, lo, hi in ((gc_ref, 0, tc), (gx_ref, tc, t)):
            gates = ref[...]
            g = jnp.sum(jnp.where(lane == d * HEADS + h, gates, 0.0), axis=-1, keepdims=True)
            b = jnp.sum(jnp.where(lane == 2 * HEADS + d * HEADS + h, gates, 0.0), axis=-1,
                        keepdims=True)
            gsc[d, lo:hi] = jnp.broadcast_to(g, (hi - lo, LANES))
            bsc[d, lo:hi] = jnp.broadcast_to(b, (hi - lo, LANES))

    def rev_chunk(s):
        return nc_ctx - 1 - s if s < nc_ctx else nsteps + nc_ctx - 1 - s

    ri = lax.broadcasted_iota(jnp.int32, (c, 2 * c), 0)
    lane2 = lax.broadcasted_iota(jnp.int32, (c, 2 * c), 1)
    left = lane2 < c
    cj = lane2 & (c - 1)
    strict = (left & (cj < ri)) | (jnp.logical_not(left) & (cj > ri))
    incl = (left & (cj <= ri)) | (jnp.logical_not(left) & (cj >= ri))
    left_sq = lax.broadcasted_iota(jnp.int32, (2 * c, 2 * c), 1) < c

    def prep(steps):
        units = range(len(steps))
        rows = [[slice(n * c, (n + 1) * c) for n in (s, rev_chunk(s))] for s in steps]
        qv = [[qs[r] for r in rows[uu]] for uu in units]
        kv = [[ks[r] for r in rows[uu]] for uu in units]
        vv = [[vs[r] for r in rows[uu]] for uu in units]
        beta = [[bsc[d, rows[uu][d]] for d in range(2)] for uu in units]
        kb = [[kv[uu][d] * beta[uu][d] for d in range(2)] for uu in units]
        gk = [lax.dot_general(
                  jnp.concatenate([kb[uu][0], qv[uu][0], kb[uu][1], qv[uu][1]], axis=0).astype(BF16),
                  jnp.concatenate(kv[uu], axis=0).astype(BF16),
                  (((1,), (1,)), ((), ())), preferred_element_type=F32)
              for uu in units]
        yield
        gcb = [[gsc[d, rows[uu][d]] for d in range(2)] for uu in units]
        ms, avs = [], []
        for uu in units:
            col = jnp.where(left, gcb[uu][0], gcb[uu][1])
            row = jnp.transpose(jnp.concatenate(gcb[uu], axis=0))[0:c]
            decay = jnp.exp(jnp.where(incl, col - row, 0.0))
            kk = jnp.where(left, gk[uu][0:c], gk[uu][2 * c:3 * c])
            qk = jnp.where(left, gk[uu][c:2 * c], gk[uu][3 * c:4 * c])
            ms.append(jnp.where(strict, kk * decay, 0.0))
            avs.append(jnp.where(incl, qk * decay, 0.0))
        ys = yield from _unit_tri_inverse_minus_eye(ms, left, ri, cj)
        egc = [[jnp.exp(g) for g in gcb[uu]] for uu in units]
        rs = [[jnp.concatenate([kb[uu][d] * egc[uu][d], vv[uu][d] * beta[uu][d]], axis=1)
               for d in range(2)] for uu in units]
        corr = [jnp.dot(_block_diag2(ys[uu], left),
                        jnp.concatenate(rs[uu], axis=0).astype(BF16), preferred_element_type=F32)
                for uu in units]
        yield
        wus, aks, g_edges = [], [], []
        for uu in units:
            kgs, edges = [], []
            for d in range(2):
                edge = c - 1 if d == 0 else 0
                edges.append(gcb[uu][d][edge:edge + 1])
                kgs.append(kv[uu][d] * jnp.exp(edges[d] - gcb[uu][d]))
            g_edges.append(edges)
            wus.append([rs[uu][d] + corr[uu][d * c:(d + 1) * c] for d in range(2)])
            kgt = jnp.transpose(jnp.concatenate(kgs, axis=0))
            aks.append(jnp.concatenate(
                [jnp.where(left, avs[uu], 0.0), jnp.where(left_sq, kgt, 0.0),
                 jnp.where(left, 0.0, avs[uu]), jnp.where(left_sq, 0.0, kgt)], axis=0).astype(BF16))
        prods = [jnp.dot(aks[uu], jnp.concatenate(wus[uu], axis=0).astype(BF16),
                         preferred_element_type=F32) for uu in units]
        yield
        for uu in units:
            for d in range(2):
                x = prods[uu][d * (c + HEAD_DIM):(d + 1) * (c + HEAD_DIM)]
                a_w, a_u = x[0:c, 0:HEAD_DIM], x[0:c, HEAD_DIM:2 * HEAD_DIM]
                k_w, k_u = x[c:c + HEAD_DIM, 0:HEAD_DIM], x[c:c + HEAD_DIM, HEAD_DIM:2 * HEAD_DIM]
                idx = 2 * steps[uu] + d
                nq_s[idx] = jnp.concatenate([-k_w, qv[uu][d] * egc[uu][d] - a_w],
                                            axis=0).astype(BF16)
                b_s[idx] = k_u
                op_s[idx] = a_u
                gl_s[idx] = jnp.broadcast_to(jnp.exp(g_edges[uu][d]), (8, LANES))

    def scan(steps, states, slot, written):
        for s in steps:
            idxs = (2 * s, 2 * s + 1)
            xs = [jnp.dot(nq_s[idx], st.astype(BF16), preferred_element_type=F32)
                  for idx, st in zip(idxs, states)]
            yield
            for n, idx, x in zip((s, rev_chunk(s)), idxs, xs):
                o = x[HEAD_DIM:HEAD_DIM + c] + op_s[idx]
                rows = slice(n * c, (n + 1) * c)
                oacc[slot, rows] = oacc[slot, rows] + o if n in written else o
                written.add(n)
            states[:] = [st * gl_s[idx][0:1] + x[0:HEAD_DIM] + b_s[idx]
                         for idx, st, x in zip(idxs, states, xs)]

    bounds = np.cumsum((0,) + SCAN_GROUPS)
    assert bounds[-1] == nsteps
    groups = [list(range(lo, hi)) for lo, hi in zip(bounds[:-1], bounds[1:])]
    early = [s for g in groups[:-1] for s in g]
    written_early = {n for s in early for n in (s, rev_chunk(s))}

    prev_states = [st_scr[0], st_scr[1]]
    _run_interleaved(prep(groups[0]), scan(groups[-1], prev_states, prev_slot, set(written_early)))
    o = oacc[prev_slot]
    on = o * lax.rsqrt(jnp.mean(o * o, axis=-1, keepdims=True) + EPS) * hn_ref[...]
    oc_ref[0] = on[0:tc].astype(oc_ref.dtype)
    ox_ref[0] = on[tc:t].astype(ox_ref.dtype)

    states = [jnp.zeros((HEAD_DIM, HEAD_DIM), F32)] * 2
    written = set()
    for gi in range(1, len(groups)):
        _run_interleaved(prep(groups[gi]), scan(groups[gi - 1], states, cur_slot, written))
    assert written == written_early
    st_scr[0] = states[0]
    st_scr[1] = states[1]


def _gdn(qkv_c, qkv_x, gates_c, gates_x, head_norm):
    tc, tx = CTX_LEN, SEQ
    t = tc + tx
    nidx = 2 * (t // CHUNK)
    nprob = BATCH * HEADS
    cur = lambda k: jnp.minimum(k, nprob - 1)
    prev = lambda k: jnp.maximum(k - 1, 0)
    part = lambda rows, p: pl.BlockSpec(
        (1, rows, HEAD_DIM), lambda k: (p * HEADS + cur(k) % HEADS, cur(k) // HEADS, 0))
    return pl.pallas_call(
        _gdn_kernel,
        grid=(nprob + 1,),
        in_specs=[part(tc, 0), part(tc, 1), part(tc, 2),
                  part(tx, 0), part(tx, 1), part(tx, 2),
                  pl.BlockSpec((tc, LANES), lambda k: (cur(k) // HEADS, 0)),
                  pl.BlockSpec((tx, LANES), lambda k: (cur(k) // HEADS, 0)),
                  pl.BlockSpec((1, HEAD_DIM), lambda k: (0, 0))],
        out_specs=[pl.BlockSpec((1, tc, HEAD_DIM), lambda k: (prev(k) // HEADS, 0, prev(k) % HEADS)),
                   pl.BlockSpec((1, tx, HEAD_DIM), lambda k: (prev(k) // HEADS, 0, prev(k) % HEADS))],
        out_shape=[jax.ShapeDtypeStruct((BATCH, tc, D_MODEL), BF16),
                   jax.ShapeDtypeStruct((BATCH, tx, D_MODEL), BF16)],
        scratch_shapes=[pltpu.VMEM((t, HEAD_DIM), F32), pltpu.VMEM((t, HEAD_DIM), F32),
                        pltpu.VMEM((t, HEAD_DIM), F32),
                        pltpu.VMEM((2, t, LANES), F32), pltpu.VMEM((2, t, LANES), F32),
                        pltpu.VMEM((nidx, HEAD_DIM + CHUNK, HEAD_DIM), BF16),
                        pltpu.VMEM((nidx, HEAD_DIM, HEAD_DIM), F32),
                        pltpu.VMEM((nidx, CHUNK, HEAD_DIM), F32),
                        pltpu.VMEM((nidx, 8, LANES), F32),
                        pltpu.VMEM((2, t, HEAD_DIM), F32),
                        pltpu.VMEM((2, HEAD_DIM, HEAD_DIM), F32)],
        compiler_params=_params(("arbitrary",)),
        name="gdn",
    )(qkv_c, qkv_c, qkv_c, qkv_x, qkv_x, qkv_x, gates_c, gates_x, head_norm)


def _even_out_kernel(x_ref, gt_ref, o_ref, gz_ref, yb_ref, w_ref, out_ref, a_scr):
    j = pl.program_id(1)

    @pl.when(j == 0)
    def _():
        d = o_ref.shape[1]
        a_scr[:, 0:d] = o_ref[...] * gz_ref[...]
        a_scr[:, d:2 * d] = yb_ref[...]

    y = jnp.dot(a_scr[...], w_ref[...], preferred_element_type=F32)
    out_ref[...] = x_ref[...] + gt_ref[0] * y


def _even_out(xin, mods, o, gz, yb, w_out, *, rows_per_batch, mod_row0):
    r, d = xin.shape
    tm = min(1024, rows_per_batch)
    tn = 512
    nj = d // tn
    tiles_per_batch = rows_per_batch // tm
    fixed = mod_row0 >= BATCH
    gt_idx = ((lambda i, j: (mod_row0, 0, 2 * nj + j)) if fixed
              else (lambda i, j: (i // tiles_per_batch, 0, 2 * nj + j)))
    return pl.pallas_call(
        _even_out_kernel,
        grid=(r // tm, nj),
        in_specs=[pl.BlockSpec((tm, tn), lambda i, j: (i, j)),
                  pl.BlockSpec((1, 1, tn), gt_idx),
                  pl.BlockSpec((tm, d), lambda i, j: (i, 0)),
                  pl.BlockSpec((tm, d), lambda i, j: (i, 0)),
                  pl.BlockSpec((tm, d), lambda i, j: (i, 0)),
                  pl.BlockSpec((2 * d, tn), lambda i, j: (0, j))],
        out_specs=pl.BlockSpec((tm, tn), lambda i, j: (i, j)),
        out_shape=jax.ShapeDtypeStruct((r, d), F32),
        scratch_shapes=[pltpu.VMEM((tm, 2 * d), BF16)],
        compiler_params=_params(("parallel", "arbitrary")),
        name="even_out",
    )(xin, mods, o, gz, yb, w_out)


def _odd_in_kernel(x_ref, nw_ref, sh_ref, sc_ref, w_ref, out_ref, h_scr, *, tiles_per_region):
    j = pl.program_id(1)

    @pl.when(j == 0)
    def _():
        _modulated_norm_to(h_scr, x_ref, nw_ref, sc_ref, sh_ref)

    region = j // tiles_per_region

    def tile(act):
        for lo in range(0, w_ref.shape[1], SUB_N):
            w = w_ref[:, lo:lo + SUB_N].astype(BF16)
            p = jnp.dot(h_scr[...], w, preferred_element_type=F32)
            out_ref[:, lo:lo + SUB_N] = act(p).astype(out_ref.dtype)

    @pl.when(region <= 1)
    def _():
        tile(_gelu_tanh)

    @pl.when((region == 2) | (region == 4))
    def _():
        tile(_silu)

    @pl.when(region == 3)
    def _():
        tile(lambda p: p)


def _odd_in(xin, mods, nw, w):
    r, d = xin.shape
    n = w.shape[1]
    tm, tn = 1024, 1024
    tiles_per_batch = SEQ // tm
    mod_idx = lambda part: (lambda i, j: (i // tiles_per_batch, 0, part))
    return pl.pallas_call(
        functools.partial(_odd_in_kernel, tiles_per_region=d // tn),
        grid=(r // tm, n // tn),
        in_specs=[pl.BlockSpec((tm, d), lambda i, j: (i, 0)),
                  pl.BlockSpec((1, d), lambda i, j: (0, 0)),
                  pl.BlockSpec((1, 1, d), mod_idx(0)),
                  pl.BlockSpec((1, 1, d), mod_idx(1)),
                  pl.BlockSpec((d, tn), lambda i, j: (0, j))],
        out_specs=pl.BlockSpec((tm, tn), lambda i, j: (i, j)),
        out_shape=jax.ShapeDtypeStruct((r, n), BF16),
        scratch_shapes=[pltpu.VMEM((tm, d), BF16)],
        compiler_params=_params(("parallel", "arbitrary")),
        name="odd_in",
    )(xin, nw, mods, mods, w)


def _odd_mix_kernel(x_ref, gt_ref, u_ref, v_ref, zc_ref, pd_ref, zd_ref, lnw_ref, lnb_ref, ws_ref,
                    bs_ref, band_ref, icnt_ref, pw_ref, ps_ref, wo_ref, fnw_ref, out_ref,
                    acc, ln_scr, *, nk_c):
    kk = pl.program_id(1)
    tm = x_ref.shape[0]
    tk = u_ref.shape[1]
    gpb = tk // HEAD_DIM

    @pl.when(kk == 0)
    def _():
        v = v_ref[...].astype(F32)
        mu = jnp.mean(v, axis=-1, keepdims=True)
        xc = v - mu
        var = jnp.mean(xc * xc, axis=-1, keepdims=True)
        ln = (xc * lax.rsqrt(var + EPS) * lnw_ref[...] + lnb_ref[...]).astype(BF16)
        for blk in range(nk_c):
            ln_scr[blk] = ln[:, blk * tk:(blk + 1) * tk]
        acc[...] = jnp.zeros_like(acc)

    @pl.when(kk < nk_c)
    def _():
        ln = ln_scr[kk]
        cols = []
        for gi in range(gpb):
            w_g = ws_ref[gi]
            b_g = bs_ref[gi]
            rows = []
            for cc in range(tm // TOK_CHUNK):
                blk = ln[cc * TOK_CHUNK:(cc + 1) * TOK_CHUNK, gi * HEAD_DIM:(gi + 1) * HEAD_DIM]
                rows.append(jnp.dot(w_g, blk, preferred_element_type=F32) + b_g)
            cols.append(jnp.concatenate(rows, axis=0))
        s = jnp.concatenate(cols, axis=1)
        yc = u_ref[...] * s * zc_ref[...]
        acc[...] += jnp.dot(yc.astype(BF16), wo_ref[...], preferred_element_type=F32)

    @pl.when(kk >= nk_c)
    def _():
        parts = []
        for pi in range(tk // POOL_GW):
            cols = slice(pi * POOL_GW, (pi + 1) * POOL_GW)
            pd = pd_ref[:, cols]
            wsum = jnp.dot(band_ref[pi], pd, preferred_element_type=F32)
            icnt = jnp.concatenate([icnt_ref[pi]] * (POOL_GW // LANES), axis=1)
            diff = wsum * icnt - pd.astype(F32)
            y = jnp.dot(diff.astype(BF16), pw_ref[pi], preferred_element_type=F32)
            parts.append((y * ps_ref[:, cols] * zd_ref[:, cols]).astype(BF16))
        yd = jnp.concatenate(parts, axis=1)
        acc[...] += jnp.dot(yd, wo_ref[...], preferred_element_type=F32)

    @pl.when(kk == pl.num_programs(1) - 1)
    def _():
        xn = x_ref[...] + gt_ref[0] * acc[...]
        ms = jnp.mean(xn * xn, axis=-1, keepdims=True)
        out_ref[...] = xn * lax.rsqrt(ms + EPS) * fnw_ref[...]


def _pool_constants(tm):
    bands = np.zeros((POOL_GROUPS, tm, tm), np.float32)
    icnt = np.zeros((POOL_GROUPS, tm, LANES), np.float32)
    pos = np.arange(tm)
    row = pos // GRID_W
    col = pos % GRID_W
    for g, rad in enumerate(POOL_RADII):
        same = row[:, None] == row[None, :]
        bands[g] = (same & (np.abs(col[:, None] - col[None, :]) <= rad)).astype(np.float32)
        cnt = np.minimum(col + rad + 1, GRID_W) - np.maximum(col - rad, 0)
        icnt[g] = (1.0 / cnt.astype(np.float32))[:, None]
    return jnp.asarray(bands, BF16), jnp.asarray(icnt, F32)


def _odd_mix(xin, mods, p, ln_w, ln_b, w_s, b_s, pool_w, pool_scale, w_out, fnw):
    r, d = xin.shape
    tm, tk = 512, 1024
    nk_c = d // tk
    nk = 2 * nk_c
    gpb = tk // HEAD_DIM
    ppb = tk // POOL_GW
    tiles_per_batch = SEQ // tm
    bands, icnt = _pool_constants(tm)
    lo = lambda kk: jnp.minimum(kk, nk_c - 1)
    hi = lambda kk: jnp.maximum(kk - nk_c, 0)
    cblk = d // tk
    return pl.pallas_call(
        functools.partial(_odd_mix_kernel, nk_c=nk_c),
        grid=(r // tm, nk),
        in_specs=[pl.BlockSpec((tm, d), lambda i, kk: (i, 0)),
                  pl.BlockSpec((1, 1, d), lambda i, kk: (i // tiles_per_batch, 0, 2)),
                  pl.BlockSpec((tm, tk), lambda i, kk: (i, lo(kk))),
                  pl.BlockSpec((tm, d), lambda i, kk: (i, 1)),
                  pl.BlockSpec((tm, tk), lambda i, kk: (i, 2 * cblk + lo(kk))),
                  pl.BlockSpec((tm, tk), lambda i, kk: (i, 3 * cblk + hi(kk))),
                  pl.BlockSpec((tm, tk), lambda i, kk: (i, 4 * cblk + hi(kk))),
                  pl.BlockSpec((1, d), lambda i, kk: (0, 0)),
                  pl.BlockSpec((1, d), lambda i, kk: (0, 0)),
                  pl.BlockSpec((gpb, TOK_CHUNK, TOK_CHUNK), lambda i, kk: (lo(kk), 0, 0)),
                  pl.BlockSpec((gpb, TOK_CHUNK, LANES), lambda i, kk: (lo(kk), 0, 0)),
                  pl.BlockSpec((ppb, tm, tm), lambda i, kk: (hi(kk), 0, 0)),
                  pl.BlockSpec((ppb, tm, LANES), lambda i, kk: (hi(kk), 0, 0)),
                  pl.BlockSpec((ppb, POOL_GW, POOL_GW), lambda i, kk: (hi(kk), 0, 0)),
                  pl.BlockSpec((1, tk), lambda i, kk: (0, hi(kk))),
                  pl.BlockSpec((tk, d), lambda i, kk: (kk, 0)),
                  pl.BlockSpec((1, d), lambda i, kk: (0, 0))],
        out_specs=pl.BlockSpec((tm, d), lambda i, kk: (i, 0)),
        out_shape=jax.ShapeDtypeStruct((r, d), F32),
        scratch_shapes=[pltpu.VMEM((tm, d), F32), pltpu.VMEM((nk_c, tm, tk), BF16)],
        compiler_params=_params(("parallel", "arbitrary")),
        name="odd_mix",
    )(xin, mods, p, p, p, p, p, ln_w, ln_b, w_s, b_s, bands, icnt, pool_w, pool_scale, w_out, fnw)


def kernel(x, c, ctx, c_ctx, ada_w, ada_b, norm_w, e_w_in, e_conv_qkv, e_a_log, e_dt_bias, e_head_norm,
           e_conv_b, e_w_out, o_w_in, o_ln_w, o_ln_b, o_w_s, o_b_s, o_pool_w, o_pool_scale, o_w_out,
           final_norm_w):
    bsz, seq, d = x.shape
    rx = bsz * seq
    rc = bsz * ctx.shape[1]
    x2 = x.reshape(rx, d)
    ctx2 = ctx.reshape(rc, d)

    cc = jnp.concatenate([c, c_ctx[None, :], jnp.zeros((8 - bsz - 1, d), F32)], axis=0)
    mods = _adaln(cc, ada_w, ada_b)
    ctx_row = bsz

    w_in = e_w_in[0].astype(BF16)
    w_bchz = w_in[:, GATE_COL0 + 4 * HEADS:]
    m0 = mods[0].reshape(8, 1, 3 * d)
    nw0 = norm_w[0].reshape(1, d)

    gate_par = jnp.pad(jnp.stack([e_a_log[0].reshape(-1), e_dt_bias[0].reshape(-1)]),
                       ((0, 0), (0, LANES - 2 * HEADS)))
    qkv_x, gates_x = _qkv_proj(x2, m0, nw0, w_in, e_conv_qkv[0], gate_par,
                               rows_per_batch=seq, mod_row0=0, group=GRID_W)
    qkv_c, gates_c = _qkv_proj(ctx2, m0, nw0, w_in, e_conv_qkv[0], gate_par,
                               rows_per_batch=ctx.shape[1], mod_row0=ctx_row, group=ctx.shape[1])
    gz_x, yb_x = _mixb_proj(x2, m0, nw0, w_in, w_bchz, e_conv_b[0],
                            rows_per_batch=seq, mod_row0=0, group=GRID_W)
    gz_c, yb_c = _mixb_proj(ctx2, m0, nw0, w_in, w_bchz, e_conv_b[0],
                            rows_per_batch=ctx.shape[1], mod_row0=ctx_row, group=ctx.shape[1])
    o_c, o_x = _gdn(qkv_c, qkv_x, gates_c, gates_x,
                    e_head_norm[0].reshape(1, HEAD_DIM))
    w_out0 = e_w_out[0].astype(BF16)
    x2 = _even_out(x2, m0, o_x.reshape(rx, d), gz_x, yb_x, w_out0, rows_per_batch=seq, mod_row0=0)
    ctx2 = _even_out(ctx2, m0, o_c.reshape(rc, d), gz_c, yb_c, w_out0,
                     rows_per_batch=ctx.shape[1], mod_row0=ctx_row)

    m1 = mods[1].reshape(8, 1, 3 * d)
    p = _odd_in(x2, m1, norm_w[1].reshape(1, d), o_w_in[0])
    b_s = jnp.broadcast_to(o_b_s[0][:, :, None], (o_b_s.shape[1], TOK_CHUNK, LANES))
    out = _odd_mix(x2, m1, p, o_ln_w[0].reshape(1, d), o_ln_b[0].reshape(1, d),
                   o_w_s[0].astype(BF16), b_s, o_pool_w[0].astype(BF16),
                   o_pool_scale[0].reshape(1, d), o_w_out[0].astype(BF16),
                   final_norm_w.reshape(1, d))
    return out.reshape(bsz, seq, d)
```

```python
import functools
import math

import numpy as np
import jax
import jax.numpy as jnp
from jax import lax
from jax.experimental import pallas as pl
from jax.experimental.pallas import tpu as pltpu

F32 = jnp.float32
BF16 = jnp.bfloat16

D_MODEL = 2048
BATCH = 4
SEQ = 2048
CTX_LEN = 256
GRID_W = 64
EPS = 1e-6
HEADS = 16
HEAD_DIM = 128
QKV_W = 3 * D_MODEL
GATE_COL0 = QKV_W + D_MODEL
CHUNK = 64
SCAN_GROUPS = (12, 13, 11)
TOK_CHUNK = 128
POOL_GROUPS = 4
POOL_GW = D_MODEL // POOL_GROUPS
POOL_RADII = (1, 2, 4, 8)

LANES = 128
VMEM_LIMIT = 56 * 1024 * 1024
NORM_ROWS = 16
NORM_UNROLL = 16
SUB_N = 256


def _params(sem):
    return pltpu.CompilerParams(dimension_semantics=sem, vmem_limit_bytes=VMEM_LIMIT)


def _sigmoid(x):
    return 1.0 / (1.0 + jnp.exp(-x))


def _silu(x):
    return x * _sigmoid(x)


def _gelu_tanh(x):
    cdf = 0.5 * (1.0 + jnp.tanh(math.sqrt(2.0 / math.pi) * (x + 0.044715 * (x * x * x))))
    return x * cdf


def _modulated_norm_to(h_scr, x_ref, nw_ref, sc_ref, sh_ref):
    nw = nw_ref[...]
    sc = sc_ref[0]
    sh = sh_ref[0]

    step = NORM_ROWS * NORM_UNROLL

    def body(r, carry):
        rows = [pl.ds(pl.multiple_of(r * step + k * NORM_ROWS, NORM_ROWS), NORM_ROWS)
                for k in range(NORM_UNROLL)]
        scales = []
        for rw in rows:
            x = x_ref[rw, :]
            scales.append(lax.rsqrt(jnp.mean(x * x, axis=-1, keepdims=True) + EPS))
        for rw, rs in zip(rows, scales):
            y = x_ref[rw, :] * rs * nw
            h_scr[rw, :] = (y * (1.0 + sc) + sh).astype(h_scr.dtype)
        return carry

    lax.fori_loop(0, x_ref.shape[0] // step, body, 0)


def _row_conv3(p, cw, group):
    tm = p.shape[0]
    row = lax.broadcasted_iota(jnp.int32, (tm, 1), 0) % group
    prev = jnp.where(row == 0, 0.0, pltpu.roll(p, 1, 0))
    nxt = jnp.where(row == group - 1, 0.0, pltpu.roll(p, tm - 1, 0))
    return prev * cw[0:1] + p * cw[1:2] + nxt * cw[2:3]


def _adaln_kernel(c_ref, w_ref, b_ref, o_ref):
    a = _silu(c_ref[...]).astype(BF16)
    o_ref[0] = jnp.dot(a, w_ref[0].astype(BF16), preferred_element_type=F32) + b_ref[0]


def _adaln(cc, ada_w, ada_b):
    depth, d, n = ada_w.shape
    tn = 1024
    return pl.pallas_call(
        _adaln_kernel,
        grid=(depth, n // tn),
        in_specs=[pl.BlockSpec((8, d), lambda l, j: (0, 0)),
                  pl.BlockSpec((1, d, tn), lambda l, j: (l, 0, j)),
                  pl.BlockSpec((1, 1, tn), lambda l, j: (l, 0, j))],
        out_specs=pl.BlockSpec((1, 8, tn), lambda l, j: (l, 0, j)),
        out_shape=jax.ShapeDtypeStruct((depth, 8, n), F32),
        compiler_params=_params(("parallel", "parallel")),
        name="adaln",
    )(cc, ada_w, ada_b.reshape(depth, 1, n))


def _qkv_kernel(x_ref, nw_ref, sh_ref, sc_ref, w_ref, wg_ref, cw_ref, gp_ref, qkv_ref, g_ref, h_scr,
                *, group, n_tiles):
    j = pl.program_id(1)

    @pl.when(j == 0)
    def _():
        _modulated_norm_to(h_scr, x_ref, nw_ref, sc_ref, sh_ref)

    def tile():
        p = jnp.dot(h_scr[...], w_ref[...], preferred_element_type=F32)
        y = _silu(_row_conv3(p, cw_ref[...], group))
        for hh in range(y.shape[1] // HEAD_DIM):
            qkv_ref[hh] = y[:, hh * HEAD_DIM:(hh + 1) * HEAD_DIM]

    def gates():
        raw = jnp.dot(h_scr[...], wg_ref[...], preferred_element_type=F32)
        z = raw + gp_ref[1:2]
        softplus = jnp.maximum(z, 0.0) + jnp.log1p(jnp.exp(-jnp.abs(z)))
        lane = lax.broadcasted_iota(jnp.int32, (1, LANES), 1)
        gates = jnp.where(lane < 2 * HEADS, -jnp.exp(gp_ref[0:1]) * softplus, _sigmoid(raw))
        ri = lax.broadcasted_iota(jnp.int32, (LANES, LANES), 0)
        ci = lax.broadcasted_iota(jnp.int32, (LANES, LANES), 1)
        same = (ri // CHUNK) == (ci // CHUNK)
        pre01 = jnp.where(same & (ci <= ri), 1.0, 0.0).astype(BF16)
        suf01 = jnp.where(same & (ci >= ri), 1.0, 0.0).astype(BF16)
        pieces = _split3_bf16(gates)
        for r0 in range(0, gates.shape[0], LANES):
            blk = [p[r0:r0 + LANES] for p in pieces]
            pre = [jnp.dot(pre01, b, preferred_element_type=F32) for b in blk]
            suf = [jnp.dot(suf01, b, preferred_element_type=F32) for b in blk]
            pre = (pre[0] + pre[1]) + pre[2]
            suf = (suf[0] + suf[1]) + suf[2]
            g_ref[r0:r0 + LANES] = jnp.where(lane < HEADS, pre,
                                             jnp.where(lane < 2 * HEADS, suf, gates[r0:r0 + LANES]))

    @pl.when(j < n_tiles - 1)
    def _():
        tile()

    @pl.when(j == n_tiles - 1)
    def _():
        tile()
        gates()


def _qkv_proj(xin, mods, nw, w_in, cw, gate_par, *, rows_per_batch, mod_row0, group):
    r, d = xin.shape
    tm, tn = 1024, 768
    n_tiles = QKV_W // tn
    hpt = tn // HEAD_DIM
    tiles_per_batch = max(rows_per_batch // tm, 1)

    def mod_idx(part):
        if rows_per_batch >= tm:
            return lambda i, j: (mod_row0 + i // tiles_per_batch, 0, part)
        return lambda i, j: (mod_row0, 0, part)

    return pl.pallas_call(
        functools.partial(_qkv_kernel, group=group, n_tiles=n_tiles),
        grid=(r // tm, n_tiles),
        in_specs=[pl.BlockSpec((tm, d), lambda i, j: (i, 0)),
                  pl.BlockSpec((1, d), lambda i, j: (0, 0)),
                  pl.BlockSpec((1, 1, d), mod_idx(0)),
                  pl.BlockSpec((1, 1, d), mod_idx(1)),
                  pl.BlockSpec((d, tn), lambda i, j: (0, j)),
                  pl.BlockSpec((d, LANES), lambda i, j: (0, GATE_COL0 // LANES)),
                  pl.BlockSpec((3, tn), lambda i, j: (0, j)),
                  pl.BlockSpec((2, LANES), lambda i, j: (0, 0))],
        out_specs=[pl.BlockSpec((hpt, tm, HEAD_DIM), lambda i, j: (j, i, 0)),
                   pl.BlockSpec((tm, LANES), lambda i, j: (i, 0))],
        out_shape=[jax.ShapeDtypeStruct((3 * HEADS, r, HEAD_DIM), F32),
                   jax.ShapeDtypeStruct((r, LANES), F32)],
        scratch_shapes=[pltpu.VMEM((tm, d), BF16)],
        compiler_params=_params(("parallel", "arbitrary")),
        name="qkv_proj",
    )(xin, nw, mods, mods, w_in, w_in, cw, gate_par)


def _mixb_kernel(x_ref, nw_ref, sh_ref, sc_ref, wz_ref, wb_ref, wc_ref, wh_ref, wzb_ref, cw_ref,
                 gz_ref, yb_ref, h_scr, *, group):
    j = pl.program_id(1)

    @pl.when(j == 0)
    def _():
        _modulated_norm_to(h_scr, x_ref, nw_ref, sc_ref, sh_ref)

    h = h_scr[...]
    dot = lambda w: jnp.dot(h, w[...], preferred_element_type=F32)
    gz_ref[...] = _silu(dot(wz_ref)).astype(gz_ref.dtype)
    conv = _row_conv3(dot(wc_ref) * dot(wh_ref), cw_ref[...], group)
    yb_ref[...] = (dot(wb_ref) * conv * _silu(dot(wzb_ref))).astype(yb_ref.dtype)


def _mixb_proj(xin, mods, nw, w_in, w_bchz, cw, *, rows_per_batch, mod_row0, group):
    r, d = xin.shape
    tm, tn = 1024, 256
    nj = d // tn
    za_blk0 = QKV_W // tn
    tiles_per_batch = max(rows_per_batch // tm, 1)

    def mod_idx(part):
        if rows_per_batch >= tm:
            return lambda i, j: (mod_row0 + i // tiles_per_batch, 0, part)
        return lambda i, j: (mod_row0, 0, part)

    wspec = lambda reg: pl.BlockSpec((d, tn), lambda i, j: (0, reg * nj + j))
    return pl.pallas_call(
        functools.partial(_mixb_kernel, group=group),
        grid=(r // tm, nj),
        in_specs=[pl.BlockSpec((tm, d), lambda i, j: (i, 0)),
                  pl.BlockSpec((1, d), lambda i, j: (0, 0)),
                  pl.BlockSpec((1, 1, d), mod_idx(0)),
                  pl.BlockSpec((1, 1, d), mod_idx(1)),
                  pl.BlockSpec((d, tn), lambda i, j: (0, za_blk0 + j)),
                  wspec(0), wspec(1), wspec(2), wspec(3),
                  pl.BlockSpec((3, tn), lambda i, j: (0, j))],
        out_specs=[pl.BlockSpec((tm, tn), lambda i, j: (i, j)),
                   pl.BlockSpec((tm, tn), lambda i, j: (i, j))],
        out_shape=[jax.ShapeDtypeStruct((r, d), BF16), jax.ShapeDtypeStruct((r, d), BF16)],
        scratch_shapes=[pltpu.VMEM((tm, d), BF16)],
        compiler_params=_params(("parallel", "arbitrary")),
        name="mixb_proj",
    )(xin, nw, mods, mods, w_in, w_bchz, w_bchz, w_bchz, w_bchz, cw)


def _split3_bf16(x):
    x1 = x.astype(BF16)
    r1 = x - x1.astype(F32)
    x2 = r1.astype(BF16)
    x3 = (r1 - x2.astype(F32)).astype(BF16)
    return x1, x2, x3


def _block_diag2(x, left):
    return jnp.concatenate([jnp.where(left, x, 0.0), jnp.where(left, 0.0, x)], axis=0).astype(BF16)


def _unit_tri_inverse_minus_eye(ms, left, ri, cj):
    n = ms[0].shape[0]
    ys = None
    s = 1
    while s < n:
        shift = s.bit_length() - 1
        br = lax.shift_right_logical(ri, shift)
        bc = lax.shift_right_logical(cj, shift)
        mask_lo = ((br & 1) == 1) & (bc == br - 1)
        mask_up = ((bc & 1) == 1) & (br == bc - 1)
        mask = (left & mask_lo) | (jnp.logical_not(left) & mask_up)
        cms = [jnp.where(mask, m, 0.0) for m in ms]
        if ys is None:
            ys = [-cm for cm in cms]
        else:
            yc = [jnp.dot(y.astype(BF16), _block_diag2(cm, left), preferred_element_type=F32)
                  for cm, y in zip(cms, ys)]
            yield
            ps = [cm + t for cm, t in zip(cms, yc)]
            py = [jnp.dot(p.astype(BF16), _block_diag2(y, left), preferred_element_type=F32)
                  for y, p in zip(ys, ps)]
            yield
            ys = [y - p - t for y, p, t in zip(ys, ps, py)]
        s *= 2
    return ys


def _run_interleaved(*gens):
    active = list(gens)
    while active:
        for g in list(active):
            try:
                next(g)
            except StopIteration:
                active.remove(g)


def _gdn_kernel(qc_ref, kc_ref, vc_ref, qx_ref, kx_ref, vx_ref, gc_ref, gx_ref,
                hn_ref, oc_ref, ox_ref,
                qs, ks, vs, gsc, bsc, nq_s, b_s, op_s, gl_s, oacc, st_scr):
    k = pl.program_id(0)
    h = jnp.minimum(k, pl.num_programs(0) - 2) % HEADS
    cur_slot = k % 2
    prev_slot = 1 - cur_slot
    tc = qc_ref.shape[1]
    tx = qx_ref.shape[1]
    t = tc + tx
    nsteps = t // CHUNK
    nc_ctx = tc // CHUNK
    c = CHUNK

    def l2n(a):
        return a * lax.rsqrt(jnp.sum(a * a, axis=-1, keepdims=True) + EPS)

    @pl.when(k == 0)
    def _():
        for ref in (nq_s, b_s, op_s, gl_s, oacc, st_scr):
            ref[...] = jnp.zeros_like(ref)

    qscale = HEAD_DIM ** -0.5
    qs[0:tc] = l2n(qc_ref[0]) * qscale
    qs[tc:t] = l2n(qx_ref[0]) * qscale
    ks[0:tc] = l2n(kc_ref[0])
    ks[tc:t] = l2n(kx_ref[0])
    vs[0:tc] = vc_ref[0]
    vs[tc:t] = vx_ref[0]

    lane = lax.broadcasted_iota(jnp.int32, (1, LANES), 1)
    for d in range(2):
        for ref, lo, hi in ((gc_ref, 0, tc), (gx_ref, tc, t)):
            gates = ref[...]
            g = jnp.sum(jnp.where(lane == d * HEADS + h, gates, 0.0), axis=-1, keepdims=True)
            b = jnp.sum(jnp.where(lane == 2 * HEADS + d * HEADS + h, gates, 0.0), axis=-1,
                        keepdims=True)
            gsc[d, lo:hi] = jnp.broadcast_to(g, (hi - lo, LANES))
            bsc[d, lo:hi] = jnp.broadcast_to(b, (hi - lo, LANES))

    def rev_chunk(s):
        return nc_ctx - 1 - s if s < nc_ctx else nsteps + nc_ctx - 1 - s

    ri = lax.broadcasted_iota(jnp.int32, (c, 2 * c), 0)
    lane2 = lax.broadcasted_iota(jnp.int32, (c, 2 * c), 1)
    left = lane2 < c
    cj = lane2 & (c - 1)
    strict = (left & (cj < ri)) | (jnp.logical_not(left) & (cj > ri))
    incl = (left & (cj <= ri)) | (jnp.logical_not(left) & (cj >= ri))
    left_sq = lax.broadcasted_iota(jnp.int32, (2 * c, 2 * c), 1) < c

    def prep(steps):
        units = range(len(steps))
        rows = [[slice(n * c, (n + 1) * c) for n in (s, rev_chunk(s))] for s in steps]
        qv = [[qs[r] for r in rows[uu]] for uu in units]
        kv = [[ks[r] for r in rows[uu]] for uu in units]
        vv = [[vs[r] for r in rows[uu]] for uu in units]
        beta = [[bsc[d, rows[uu][d]] for d in range(2)] for uu in units]
        kb = [[kv[uu][d] * beta[uu][d] for d in range(2)] for uu in units]
        gk = [lax.dot_general(
                  jnp.concatenate([kb[uu][0], qv[uu][0], kb[uu][1], qv[uu][1]], axis=0).astype(BF16),
                  jnp.concatenate(kv[uu], axis=0).astype(BF16),
                  (((1,), (1,)), ((), ())), preferred_element_type=F32)
              for uu in units]
        yield
        gcb = [[gsc[d, rows[uu][d]] for d in range(2)] for uu in units]
        ms, avs = [], []
        for uu in units:
            col = jnp.where(left, gcb[uu][0], gcb[uu][1])
            row = jnp.transpose(jnp.concatenate(gcb[uu], axis=0))[0:c]
            decay = jnp.exp(jnp.where(incl, col - row, 0.0))
            kk = jnp.where(left, gk[uu][0:c], gk[uu][2 * c:3 * c])
            qk = jnp.where(left, gk[uu][c:2 * c], gk[uu][3 * c:4 * c])
            ms.append(jnp.where(strict, kk * decay, 0.0))
            avs.append(jnp.where(incl, qk * decay, 0.0))
        ys = yield from _unit_tri_inverse_minus_eye(ms, left, ri, cj)
        egc = [[jnp.exp(g) for g in gcb[uu]] for uu in units]
        rs = [[jnp.concatenate([kb[uu][d] * egc[uu][d], vv[uu][d] * beta[uu][d]], axis=1)
               for d in range(2)] for uu in units]
        corr = [jnp.dot(_block_diag2(ys[uu], left),
                        jnp.concatenate(rs[uu], axis=0).astype(BF16), preferred_element_type=F32)
                for uu in units]
        yield
        wus, aks, g_edges = [], [], []
        for uu in units:
            kgs, edges = [], []
            for d in range(2):
                edge = c - 1 if d == 0 else 0
                edges.append(gcb[uu][d][edge:edge + 1])
                kgs.append(kv[uu][d] * jnp.exp(edges[d] - gcb[uu][d]))
            g_edges.append(edges)
            wus.append([rs[uu][d] + corr[uu][d * c:(d + 1) * c] for d in range(2)])
            kgt = jnp.transpose(jnp.concatenate(kgs, axis=0))
            aks.append(jnp.concatenate(
                [jnp.where(left, avs[uu], 0.0), jnp.where(left_sq, kgt, 0.0),
                 jnp.where(left, 0.0, avs[uu]), jnp.where(left_sq, 0.0, kgt)], axis=0).astype(BF16))
        prods = [jnp.dot(aks[uu], jnp.concatenate(wus[uu], axis=0).astype(BF16),
                         preferred_element_type=F32) for uu in units]
        yield
        for uu in units:
            for d in range(2):
                x = prods[uu][d * (c + HEAD_DIM):(d + 1) * (c + HEAD_DIM)]
                a_w, a_u = x[0:c, 0:HEAD_DIM], x[0:c, HEAD_DIM:2 * HEAD_DIM]
                k_w, k_u = x[c:c + HEAD_DIM, 0:HEAD_DIM], x[c:c + HEAD_DIM, HEAD_DIM:2 * HEAD_DIM]
                idx = 2 * steps[uu] + d
                nq_s[idx] = jnp.concatenate([-k_w, qv[uu][d] * egc[uu][d] - a_w],
                                            axis=0).astype(BF16)
                b_s[idx] = k_u
                op_s[idx] = a_u
                gl_s[idx] = jnp.broadcast_to(jnp.exp(g_edges[uu][d]), (8, LANES))

    def scan(steps, states, slot, written):
        for s in steps:
            idxs = (2 * s, 2 * s + 1)
            xs = [jnp.dot(nq_s[idx], st.astype(BF16), preferred_element_type=F32)
                  for idx, st in zip(idxs, states)]
            yield
            for n, idx, x in zip((s, rev_chunk(s)), idxs, xs):
                o = x[HEAD_DIM:HEAD_DIM + c] + op_s[idx]
                rows = slice(n * c, (n + 1) * c)
                oacc[slot, rows] = oacc[slot, rows] + o if n in written else o
                written.add(n)
            states[:] = [st * gl_s[idx][0:1] + x[0:HEAD_DIM] + b_s[idx]
                         for idx, st, x in zip(idxs, states, xs)]

    bounds = np.cumsum((0,) + SCAN_GROUPS)
    assert bounds[-1] == nsteps
    groups = [list(range(lo, hi)) for lo, hi in zip(bounds[:-1], bounds[1:])]
    early = [s for g in groups[:-1] for s in g]
    written_early = {n for s in early for n in (s, rev_chunk(s))}

    prev_states = [st_scr[0], st_scr[1]]
    _run_interleaved(prep(groups[0]), scan(groups[-1], prev_states, prev_slot, set(written_early)))
    o = oacc[prev_slot]
    on = o * lax.rsqrt(jnp.mean(o * o, axis=-1, keepdims=True) + EPS) * hn_ref[...]
    oc_ref[0] = on[0:tc].astype(oc_ref.dtype)
    ox_ref[0] = on[tc:t].astype(ox_ref.dtype)

    states = [jnp.zeros((HEAD_DIM, HEAD_DIM), F32)] * 2
    written = set()
    for gi in range(1, len(groups)):
        _run_interleaved(prep(groups[gi]), scan(groups[gi - 1], states, cur_slot, written))
    assert written == written_early
    st_scr[0] = states[0]
    st_scr[1] = states[1]


def _gdn(qkv_c, qkv_x, gates_c, gates_x, head_norm):
    tc, tx = CTX_LEN, SEQ
    t = tc + tx
    nidx = 2 * (t // CHUNK)
    nprob = BATCH * HEADS
    cur = lambda k: jnp.minimum(k, nprob - 1)
    prev = lambda k: jnp.maximum(k - 1, 0)
    part = lambda rows, p: pl.BlockSpec(
        (1, rows, HEAD_DIM), lambda k: (p * HEADS + cur(k) % HEADS, cur(k) // HEADS, 0))
    return pl.pallas_call(
        _gdn_kernel,
        grid=(nprob + 1,),
        in_specs=[part(tc, 0), part(tc, 1), part(tc, 2),
                  part(tx, 0), part(tx, 1), part(tx, 2),
                  pl.BlockSpec((tc, LANES), lambda k: (cur(k) // HEADS, 0)),
                  pl.BlockSpec((tx, LANES), lambda k: (cur(k) // HEADS, 0)),
                  pl.BlockSpec((1, HEAD_DIM), lambda k: (0, 0))],
        out_specs=[pl.BlockSpec((1, tc, HEAD_DIM), lambda k: (prev(k) // HEADS, 0, prev(k) % HEADS)),
                   pl.BlockSpec((1, tx, HEAD_DIM), lambda k: (prev(k) // HEADS, 0, prev(k) % HEADS))],
        out_shape=[jax.ShapeDtypeStruct((BATCH, tc, D_MODEL), BF16),
                   jax.ShapeDtypeStruct((BATCH, tx, D_MODEL), BF16)],
        scratch_shapes=[pltpu.VMEM((t, HEAD_DIM), F32), pltpu.VMEM((t, HEAD_DIM), F32),
                        pltpu.VMEM((t, HEAD_DIM), F32),
                        pltpu.VMEM((2, t, LANES), F32), pltpu.VMEM((2, t, LANES), F32),
                        pltpu.VMEM((nidx, HEAD_DIM + CHUNK, HEAD_DIM), BF16),
                        pltpu.VMEM((nidx, HEAD_DIM, HEAD_DIM), F32),
                        pltpu.VMEM((nidx, CHUNK, HEAD_DIM), F32),
                        pltpu.VMEM((nidx, 8, LANES), F32),
                        pltpu.VMEM((2, t, HEAD_DIM), F32),
                        pltpu.VMEM((2, HEAD_DIM, HEAD_DIM), F32)],
        compiler_params=_params(("arbitrary",)),
        name="gdn",
    )(qkv_c, qkv_c, qkv_c, qkv_x, qkv_x, qkv_x, gates_c, gates_x, head_norm)


def _even_out_kernel(x_ref, gt_ref, o_ref, gz_ref, yb_ref, w_ref, out_ref, a_scr):
    j = pl.program_id(1)

    @pl.when(j == 0)
    def _():
        d = o_ref.shape[1]
        a_scr[:, 0:d] = o_ref[...] * gz_ref[...]
        a_scr[:, d:2 * d] = yb_ref[...]

    y = jnp.dot(a_scr[...], w_ref[...], preferred_element_type=F32)
    out_ref[...] = x_ref[...] + gt_ref[0] * y


def _even_out(xin, mods, o, gz, yb, w_out, *, rows_per_batch, mod_row0):
    r, d = xin.shape
    tm = min(1024, rows_per_batch)
    tn = 512
    nj = d // tn
    tiles_per_batch = rows_per_batch // tm
    fixed = mod_row0 >= BATCH
    gt_idx = ((lambda i, j: (mod_row0, 0, 2 * nj + j)) if fixed
              else (lambda i, j: (i // tiles_per_batch, 0, 2 * nj + j)))
    return pl.pallas_call(
        _even_out_kernel,
        grid=(r // tm, nj),
        in_specs=[pl.BlockSpec((tm, tn), lambda i, j: (i, j)),
                  pl.BlockSpec((1, 1, tn), gt_idx),
                  pl.BlockSpec((tm, d), lambda i, j: (i, 0)),
                  pl.BlockSpec((tm, d), lambda i, j: (i, 0)),
                  pl.BlockSpec((tm, d), lambda i, j: (i, 0)),
                  pl.BlockSpec((2 * d, tn), lambda i, j: (0, j))],
        out_specs=pl.BlockSpec((tm, tn), lambda i, j: (i, j)),
        out_shape=jax.ShapeDtypeStruct((r, d), F32),
        scratch_shapes=[pltpu.VMEM((tm, 2 * d), BF16)],
        compiler_params=_params(("parallel", "arbitrary")),
        name="even_out",
    )(xin, mods, o, gz, yb, w_out)


def _odd_in_kernel(x_ref, nw_ref, sh_ref, sc_ref, wa_ref, wp_ref, act_ref, pd_ref, h_scr,
                   *, n_gelu):
    j = pl.program_id(1)

    @pl.when(j == 0)
    def _():
        _modulated_norm_to(h_scr, x_ref, nw_ref, sc_ref, sh_ref)

    def tile(act):
        for lo in range(0, wa_ref.shape[1], SUB_N):
            w = wa_ref[:, lo:lo + SUB_N].astype(BF16)
            p = jnp.dot(h_scr[...], w, preferred_element_type=F32)
            act_ref[:, lo:lo + SUB_N] = act(p).astype(act_ref.dtype)
        p = jnp.dot(h_scr[...], wp_ref[...].astype(BF16), preferred_element_type=F32)
        pd_ref[...] = p.astype(pd_ref.dtype)

    @pl.when(j < n_gelu)
    def _():
        tile(_gelu_tanh)

    @pl.when(j >= n_gelu)
    def _():
        tile(_silu)


def _odd_in(xin, mods, nw, w):
    r, d = xin.shape
    tm, tn = 1024, 1024
    n_act = 4 * d // tn
    assert d // SUB_N == n_act
    pd_blk0 = 3 * d // SUB_N
    skip = d // tn
    tiles_per_batch = SEQ // tm
    mod_idx = lambda part: (lambda i, j: (i // tiles_per_batch, 0, part))
    act_col = lambda j: jnp.where(j < 3 * d // tn, j, j + skip)
    return pl.pallas_call(
        functools.partial(_odd_in_kernel, n_gelu=2 * d // tn),
        grid=(r // tm, n_act),
        in_specs=[pl.BlockSpec((tm, d), lambda i, j: (i, 0)),
                  pl.BlockSpec((1, d), lambda i, j: (0, 0)),
                  pl.BlockSpec((1, 1, d), mod_idx(0)),
                  pl.BlockSpec((1, 1, d), mod_idx(1)),
                  pl.BlockSpec((d, tn), lambda i, j: (0, act_col(j))),
                  pl.BlockSpec((d, SUB_N), lambda i, j: (0, pd_blk0 + j))],
        out_specs=[pl.BlockSpec((tm, tn), lambda i, j: (i, j)),
                   pl.BlockSpec((tm, SUB_N), lambda i, j: (i, j))],
        out_shape=[jax.ShapeDtypeStruct((r, 4 * d), BF16), jax.ShapeDtypeStruct((r, d), BF16)],
        scratch_shapes=[pltpu.VMEM((tm, d), BF16)],
        compiler_params=_params(("parallel", "arbitrary")),
        name="odd_in",
    )(xin, nw, mods, mods, w, w)


def _odd_mix_kernel(x_ref, gt_ref, u_ref, v_ref, zc_ref, pd_ref, zd_ref, lnw_ref, lnb_ref, ws_ref,
                    bs_ref, band_ref, icnt_ref, pw_ref, ps_ref, wo_ref, fnw_ref, out_ref,
                    acc, ln_scr, *, nk_c):
    kk = pl.program_id(1)
    tm = x_ref.shape[0]
    tk = u_ref.shape[1]
    gpb = tk // HEAD_DIM

    @pl.when(kk == 0)
    def _():
        v = v_ref[...].astype(F32)
        mu = jnp.mean(v, axis=-1, keepdims=True)
        xc = v - mu
        var = jnp.mean(xc * xc, axis=-1, keepdims=True)
        ln = (xc * lax.rsqrt(var + EPS) * lnw_ref[...] + lnb_ref[...]).astype(BF16)
        for blk in range(nk_c):
            ln_scr[blk] = ln[:, blk * tk:(blk + 1) * tk]
        acc[...] = jnp.zeros_like(acc)

    @pl.when(kk < nk_c)
    def _():
        ln = ln_scr[kk]
        cols = []
        for gi in range(gpb):
            w_g = ws_ref[gi]
            b_g = bs_ref[gi]
            rows = []
            for cc in range(tm // TOK_CHUNK):
                blk = ln[cc * TOK_CHUNK:(cc + 1) * TOK_CHUNK, gi * HEAD_DIM:(gi + 1) * HEAD_DIM]
                rows.append(jnp.dot(w_g, blk, preferred_element_type=F32) + b_g)
            cols.append(jnp.concatenate(rows, axis=0))
        s = jnp.concatenate(cols, axis=1)
        yc = u_ref[...] * s * zc_ref[...]
        acc[...] += jnp.dot(yc.astype(BF16), wo_ref[...], preferred_element_type=F32)

    @pl.when(kk >= nk_c)
    def _():
        parts = []
        for pi in range(tk // POOL_GW):
            cols = slice(pi * POOL_GW, (pi + 1) * POOL_GW)
            pd = pd_ref[:, cols]
            wsum = jnp.dot(band_ref[pi], pd, preferred_element_type=F32)
            icnt = jnp.concatenate([icnt_ref[pi]] * (POOL_GW // LANES), axis=1)
            diff = wsum * icnt - pd.astype(F32)
            y = jnp.dot(diff.astype(BF16), pw_ref[pi], preferred_element_type=F32)
            parts.append((y * ps_ref[:, cols] * zd_ref[:, cols]).astype(BF16))
        yd = jnp.concatenate(parts, axis=1)
        acc[...] += jnp.dot(yd, wo_ref[...], preferred_element_type=F32)

    @pl.when(kk == pl.num_programs(1) - 1)
    def _():
        xn = x_ref[...] + gt_ref[0] * acc[...]
        ms = jnp.mean(xn * xn, axis=-1, keepdims=True)
        out_ref[...] = xn * lax.rsqrt(ms + EPS) * fnw_ref[...]


def _pool_constants(tm):
    bands = np.zeros((POOL_GROUPS, tm, tm), np.float32)
    icnt = np.zeros((POOL_GROUPS, tm, LANES), np.float32)
    pos = np.arange(tm)
    row = pos // GRID_W
    col = pos % GRID_W
    for g, rad in enumerate(POOL_RADII):
        same = row[:, None] == row[None, :]
        bands[g] = (same & (np.abs(col[:, None] - col[None, :]) <= rad)).astype(np.float32)
        cnt = np.minimum(col + rad + 1, GRID_W) - np.maximum(col - rad, 0)
        icnt[g] = (1.0 / cnt.astype(np.float32))[:, None]
    return jnp.asarray(bands, BF16), jnp.asarray(icnt, F32)


def _odd_mix(xin, mods, p, p_d, ln_w, ln_b, w_s, b_s, pool_w, pool_scale, w_out, fnw):
    r, d = xin.shape
    tm, tk = 512, 1024
    nk_c = d // tk
    nk = 2 * nk_c
    gpb = tk // HEAD_DIM
    ppb = tk // POOL_GW
    tiles_per_batch = SEQ // tm
    bands, icnt = _pool_constants(tm)
    lo = lambda kk: jnp.minimum(kk, nk_c - 1)
    hi = lambda kk: jnp.maximum(kk - nk_c, 0)
    cblk = d // tk
    return pl.pallas_call(
        functools.partial(_odd_mix_kernel, nk_c=nk_c),
        grid=(r // tm, nk),
        in_specs=[pl.BlockSpec((tm, d), lambda i, kk: (i, 0)),
                  pl.BlockSpec((1, 1, d), lambda i, kk: (i // tiles_per_batch, 0, 2)),
                  pl.BlockSpec((tm, tk), lambda i, kk: (i, lo(kk))),
                  pl.BlockSpec((tm, d), lambda i, kk: (i, 1)),
                  pl.BlockSpec((tm, tk), lambda i, kk: (i, 2 * cblk + lo(kk))),
                  pl.BlockSpec((tm, tk), lambda i, kk: (i, hi(kk))),
                  pl.BlockSpec((tm, tk), lambda i, kk: (i, 3 * cblk + hi(kk))),
                  pl.BlockSpec((1, d), lambda i, kk: (0, 0)),
                  pl.BlockSpec((1, d), lambda i, kk: (0, 0)),
                  pl.BlockSpec((gpb, TOK_CHUNK, TOK_CHUNK), lambda i, kk: (lo(kk), 0, 0)),
                  pl.BlockSpec((gpb, TOK_CHUNK, LANES), lambda i, kk: (lo(kk), 0, 0)),
                  pl.BlockSpec((ppb, tm, tm), lambda i, kk: (hi(kk), 0, 0)),
                  pl.BlockSpec((ppb, tm, LANES), lambda i, kk: (hi(kk), 0, 0)),
                  pl.BlockSpec((ppb, POOL_GW, POOL_GW), lambda i, kk: (hi(kk), 0, 0)),
                  pl.BlockSpec((1, tk), lambda i, kk: (0, hi(kk))),
                  pl.BlockSpec((tk, d), lambda i, kk: (kk, 0)),
                  pl.BlockSpec((1, d), lambda i, kk: (0, 0))],
        out_specs=pl.BlockSpec((tm, d), lambda i, kk: (i, 0)),
        out_shape=jax.ShapeDtypeStruct((r, d), F32),
        scratch_shapes=[pltpu.VMEM((tm, d), F32), pltpu.VMEM((nk_c, tm, tk), BF16)],
        compiler_params=_params(("parallel", "arbitrary")),
        name="odd_mix",
    )(xin, mods, p, p, p, p_d, p, ln_w, ln_b, w_s, b_s, bands, icnt, pool_w, pool_scale, w_out, fnw)


def kernel(x, c, ctx, c_ctx, ada_w, ada_b, norm_w, e_w_in, e_conv_qkv, e_a_log, e_dt_bias, e_head_norm,
           e_conv_b, e_w_out, o_w_in, o_ln_w, o_ln_b, o_w_s, o_b_s, o_pool_w, o_pool_scale, o_w_out,
           final_norm_w):
    bsz, seq, d = x.shape
    rx = bsz * seq
    rc = bsz * ctx.shape[1]
    x2 = x.reshape(rx, d)
    ctx2 = ctx.reshape(rc, d)

    cc = jnp.concatenate([c, c_ctx[None, :], jnp.zeros((8 - bsz - 1, d), F32)], axis=0)
    mods = _adaln(cc, ada_w, ada_b)
    ctx_row = bsz

    w_in = e_w_in[0].astype(BF16)
    w_bchz = w_in[:, GATE_COL0 + 4 * HEADS:]
    m0 = mods[0].reshape(8, 1, 3 * d)
    nw0 = norm_w[0].reshape(1, d)

    gate_par = jnp.pad(jnp.stack([e_a_log[0].reshape(-1), e_dt_bias[0].reshape(-1)]),
                       ((0, 0), (0, LANES - 2 * HEADS)))
    qkv_x, gates_x = _qkv_proj(x2, m0, nw0, w_in, e_conv_qkv[0], gate_par,
                               rows_per_batch=seq, mod_row0=0, group=GRID_W)
    qkv_c, gates_c = _qkv_proj(ctx2, m0, nw0, w_in, e_conv_qkv[0], gate_par,
                               rows_per_batch=ctx.shape[1], mod_row0=ctx_row, group=ctx.shape[1])
    gz_x, yb_x = _mixb_proj(x2, m0, nw0, w_in, w_bchz, e_conv_b[0],
                            rows_per_batch=seq, mod_row0=0, group=GRID_W)
    gz_c, yb_c = _mixb_proj(ctx2, m0, nw0, w_in, w_bchz, e_conv_b[0],
                            rows_per_batch=ctx.shape[1], mod_row0=ctx_row, group=ctx.shape[1])
    o_c, o_x = _gdn(qkv_c, qkv_x, gates_c, gates_x,
                    e_head_norm[0].reshape(1, HEAD_DIM))
    w_out0 = e_w_out[0].astype(BF16)
    x2 = _even_out(x2, m0, o_x.reshape(rx, d), gz_x, yb_x, w_out0, rows_per_batch=seq, mod_row0=0)
    ctx2 = _even_out(ctx2, m0, o_c.reshape(rc, d), gz_c, yb_c, w_out0,
                     rows_per_batch=ctx.shape[1], mod_row0=ctx_row)

    m1 = mods[1].reshape(8, 1, 3 * d)
    p, p_d = _odd_in(x2, m1, norm_w[1].reshape(1, d), o_w_in[0])
    b_s = jnp.broadcast_to(o_b_s[0][:, :, None], (o_b_s.shape[1], TOK_CHUNK, LANES))
    out = _odd_mix(x2, m1, p, p_d, o_ln_w[0].reshape(1, d), o_ln_b[0].reshape(1, d),
                   o_w_s[0].astype(BF16), b_s, o_pool_w[0].astype(BF16),
                   o_pool_scale[0].reshape(1, d), o_w_out[0].astype(BF16),
                   final_norm_w.reshape(1, d))
    return out.reshape(bsz, seq, d)
```

```python
import functools
import math

import numpy as np
import jax
import jax.numpy as jnp
from jax import lax
from jax.experimental import pallas as pl
from jax.experimental.pallas import tpu as pltpu

F32 = jnp.float32
BF16 = jnp.bfloat16

D_MODEL = 2048
BATCH = 4
SEQ = 2048
CTX_LEN = 256
GRID_W = 64
EPS = 1e-6
HEADS = 16
HEAD_DIM = 128
QKV_W = 3 * D_MODEL
GATE_COL0 = QKV_W + D_MODEL
CHUNK = 64
SCAN_GROUPS = (12, 13, 11)
TOK_CHUNK = 128
POOL_GROUPS = 4
POOL_GW = D_MODEL // POOL_GROUPS
POOL_RADII = (1, 2, 4, 8)

LANES = 128
VMEM_LIMIT = 56 * 1024 * 1024
NORM_ROWS = 16
NORM_UNROLL = 16
SUB_N = 256


def _params(sem):
    return pltpu.CompilerParams(dimension_semantics=sem, vmem_limit_bytes=VMEM_LIMIT)


def _sigmoid(x):
    return 1.0 / (1.0 + jnp.exp(-x))


def _silu(x):
    return x * _sigmoid(x)


def _gelu_tanh(x):
    cdf = 0.5 * (1.0 + jnp.tanh(math.sqrt(2.0 / math.pi) * (x + 0.044715 * (x * x * x))))
    return x * cdf


def _modulated_norm_to(h_scr, x_ref, nw_ref, sc_ref, sh_ref):
    nw = nw_ref[...]
    sc = sc_ref[0]
    sh = sh_ref[0]

    step = NORM_ROWS * NORM_UNROLL

    def body(r, carry):
        rows = [pl.ds(pl.multiple_of(r * step + k * NORM_ROWS, NORM_ROWS), NORM_ROWS)
                for k in range(NORM_UNROLL)]
        scales = []
        for rw in rows:
            x = x_ref[rw, :]
            scales.append(lax.rsqrt(jnp.mean(x * x, axis=-1, keepdims=True) + EPS))
        for rw, rs in zip(rows, scales):
            y = x_ref[rw, :] * rs * nw
            h_scr[rw, :] = (y * (1.0 + sc) + sh).astype(h_scr.dtype)
        return carry

    lax.fori_loop(0, x_ref.shape[0] // step, body, 0)


def _row_conv3(p, cw, group):
    tm = p.shape[0]
    row = lax.broadcasted_iota(jnp.int32, (tm, 1), 0) % group
    prev = jnp.where(row == 0, 0.0, pltpu.roll(p, 1, 0))
    nxt = jnp.where(row == group - 1, 0.0, pltpu.roll(p, tm - 1, 0))
    return prev * cw[0:1] + p * cw[1:2] + nxt * cw[2:3]


def _adaln_kernel(c_ref, w_ref, b_ref, o_ref):
    a = _silu(c_ref[...]).astype(BF16)
    o_ref[0] = jnp.dot(a, w_ref[0].astype(BF16), preferred_element_type=F32) + b_ref[0]


def _adaln(cc, ada_w, ada_b):
    depth, d, n = ada_w.shape
    tn = 1024
    return pl.pallas_call(
        _adaln_kernel,
        grid=(depth, n // tn),
        in_specs=[pl.BlockSpec((8, d), lambda l, j: (0, 0)),
                  pl.BlockSpec((1, d, tn), lambda l, j: (l, 0, j)),
                  pl.BlockSpec((1, 1, tn), lambda l, j: (l, 0, j))],
        out_specs=pl.BlockSpec((1, 8, tn), lambda l, j: (l, 0, j)),
        out_shape=jax.ShapeDtypeStruct((depth, 8, n), F32),
        compiler_params=_params(("parallel", "parallel")),
        name="adaln",
    )(cc, ada_w, ada_b.reshape(depth, 1, n))


def _qkv_kernel(x_ref, nw_ref, sh_ref, sc_ref, w_ref, wg_ref, cw_ref, gp_ref, qkv_ref, g_ref, h_scr,
                *, group, n_tiles):
    j = pl.program_id(1)

    @pl.when(j == 0)
    def _():
        _modulated_norm_to(h_scr, x_ref, nw_ref, sc_ref, sh_ref)

    def tile():
        p = jnp.dot(h_scr[...], w_ref[...], preferred_element_type=F32)
        y = _silu(_row_conv3(p, cw_ref[...], group))
        for hh in range(y.shape[1] // HEAD_DIM):
            qkv_ref[hh] = y[:, hh * HEAD_DIM:(hh + 1) * HEAD_DIM]

    def gates():
        raw = jnp.dot(h_scr[...], wg_ref[...], preferred_element_type=F32)
        z = raw + gp_ref[1:2]
        softplus = jnp.maximum(z, 0.0) + jnp.log1p(jnp.exp(-jnp.abs(z)))
        lane = lax.broadcasted_iota(jnp.int32, (1, LANES), 1)
        gates = jnp.where(lane < 2 * HEADS, -jnp.exp(gp_ref[0:1]) * softplus, _sigmoid(raw))
        ri = lax.broadcasted_iota(jnp.int32, (LANES, LANES), 0)
        ci = lax.broadcasted_iota(jnp.int32, (LANES, LANES), 1)
        same = (ri // CHUNK) == (ci // CHUNK)
        pre01 = jnp.where(same & (ci <= ri), 1.0, 0.0).astype(BF16)
        suf01 = jnp.where(same & (ci >= ri), 1.0, 0.0).astype(BF16)
        pieces = _split3_bf16(gates)
        for r0 in range(0, gates.shape[0], LANES):
            blk = [p[r0:r0 + LANES] for p in pieces]
            pre = [jnp.dot(pre01, b, preferred_element_type=F32) for b in blk]
            suf = [jnp.dot(suf01, b, preferred_element_type=F32) for b in blk]
            pre = (pre[0] + pre[1]) + pre[2]
            suf = (suf[0] + suf[1]) + suf[2]
            g_ref[r0:r0 + LANES] = jnp.where(lane < HEADS, pre,
                                             jnp.where(lane < 2 * HEADS, suf, gates[r0:r0 + LANES]))

    @pl.when(j < n_tiles - 1)
    def _():
        tile()

    @pl.when(j == n_tiles - 1)
    def _():
        tile()
        gates()


def _qkv_proj(xin, mods, nw, w_in, cw, gate_par, *, rows_per_batch, mod_row0, group):
    r, d = xin.shape
    tm, tn = 1024, 768
    n_tiles = QKV_W // tn
    hpt = tn // HEAD_DIM
    tiles_per_batch = max(rows_per_batch // tm, 1)

    def mod_idx(part):
        if rows_per_batch >= tm:
            return lambda i, j: (mod_row0 + i // tiles_per_batch, 0, part)
        return lambda i, j: (mod_row0, 0, part)

    return pl.pallas_call(
        functools.partial(_qkv_kernel, group=group, n_tiles=n_tiles),
        grid=(r // tm, n_tiles),
        in_specs=[pl.BlockSpec((tm, d), lambda i, j: (i, 0)),
                  pl.BlockSpec((1, d), lambda i, j: (0, 0)),
                  pl.BlockSpec((1, 1, d), mod_idx(0)),
                  pl.BlockSpec((1, 1, d), mod_idx(1)),
                  pl.BlockSpec((d, tn), lambda i, j: (0, j)),
                  pl.BlockSpec((d, LANES), lambda i, j: (0, GATE_COL0 // LANES)),
                  pl.BlockSpec((3, tn), lambda i, j: (0, j)),
                  pl.BlockSpec((2, LANES), lambda i, j: (0, 0))],
        out_specs=[pl.BlockSpec((hpt, tm, HEAD_DIM), lambda i, j: (j, i, 0)),
                   pl.BlockSpec((tm, LANES), lambda i, j: (i, 0))],
        out_shape=[jax.ShapeDtypeStruct((3 * HEADS, r, HEAD_DIM), F32),
                   jax.ShapeDtypeStruct((r, LANES), F32)],
        scratch_shapes=[pltpu.VMEM((tm, d), BF16)],
        compiler_params=_params(("parallel", "arbitrary")),
        name="qkv_proj",
    )(xin, nw, mods, mods, w_in, w_in, cw, gate_par)


def _mixb_kernel(x_ref, nw_ref, sh_ref, sc_ref, wz_ref, wb_ref, wc_ref, wh_ref, wzb_ref, cw_ref,
                 gz_ref, yb_ref, h_scr, *, group):
    j = pl.program_id(1)

    @pl.when(j == 0)
    def _():
        _modulated_norm_to(h_scr, x_ref, nw_ref, sc_ref, sh_ref)

    h = h_scr[...]
    dot = lambda w: jnp.dot(h, w[...], preferred_element_type=F32)
    gz_ref[...] = _silu(dot(wz_ref)).astype(gz_ref.dtype)
    conv = _row_conv3(dot(wc_ref) * dot(wh_ref), cw_ref[...], group)
    yb_ref[...] = (dot(wb_ref) * conv * _silu(dot(wzb_ref))).astype(yb_ref.dtype)


def _mixb_proj(xin, mods, nw, w_in, w_bchz, cw, *, rows_per_batch, mod_row0, group):
    r, d = xin.shape
    tm, tn = 1024, 256
    nj = d // tn
    za_blk0 = QKV_W // tn
    tiles_per_batch = max(rows_per_batch // tm, 1)

    def mod_idx(part):
        if rows_per_batch >= tm:
            return lambda i, j: (mod_row0 + i // tiles_per_batch, 0, part)
        return lambda i, j: (mod_row0, 0, part)

    wspec = lambda reg: pl.BlockSpec((d, tn), lambda i, j: (0, reg * nj + j))
    return pl.pallas_call(
        functools.partial(_mixb_kernel, group=group),
        grid=(r // tm, nj),
        in_specs=[pl.BlockSpec((tm, d), lambda i, j: (i, 0)),
                  pl.BlockSpec((1, d), lambda i, j: (0, 0)),
                  pl.BlockSpec((1, 1, d), mod_idx(0)),
                  pl.BlockSpec((1, 1, d), mod_idx(1)),
                  pl.BlockSpec((d, tn), lambda i, j: (0, za_blk0 + j)),
                  wspec(0), wspec(1), wspec(2), wspec(3),
                  pl.BlockSpec((3, tn), lambda i, j: (0, j))],
        out_specs=[pl.BlockSpec((tm, tn), lambda i, j: (i, j)),
                   pl.BlockSpec((tm, tn), lambda i, j: (i, j))],
        out_shape=[jax.ShapeDtypeStruct((r, d), BF16), jax.ShapeDtypeStruct((r, d), BF16)],
        scratch_shapes=[pltpu.VMEM((tm, d), BF16)],
        compiler_params=_params(("parallel", "arbitrary")),
        name="mixb_proj",
    )(xin, nw, mods, mods, w_in, w_bchz, w_bchz, w_bchz, w_bchz, cw)


def _split3_bf16(x):
    x1 = x.astype(BF16)
    r1 = x - x1.astype(F32)
    x2 = r1.astype(BF16)
    x3 = (r1 - x2.astype(F32)).astype(BF16)
    return x1, x2, x3


def _block_diag2(x, left):
    return jnp.concatenate([jnp.where(left, x, 0.0), jnp.where(left, 0.0, x)], axis=0).astype(BF16)


def _unit_tri_inverse_minus_eye(ms, left, ri, cj):
    n = ms[0].shape[0]
    ys = None
    s = 1
    while s < n:
        shift = s.bit_length() - 1
        br = lax.shift_right_logical(ri, shift)
        bc = lax.shift_right_logical(cj, shift)
        mask_lo = ((br & 1) == 1) & (bc == br - 1)
        mask_up = ((bc & 1) == 1) & (br == bc - 1)
        mask = (left & mask_lo) | (jnp.logical_not(left) & mask_up)
        cms = [jnp.where(mask, m, 0.0) for m in ms]
        if ys is None:
            ys = [-cm for cm in cms]
        else:
            yc = [jnp.dot(y.astype(BF16), _block_diag2(cm, left), preferred_element_type=F32)
                  for cm, y in zip(cms, ys)]
            yield
            ps = [cm + t for cm, t in zip(cms, yc)]
            py = [jnp.dot(p.astype(BF16), _block_diag2(y, left), preferred_element_type=F32)
                  for y, p in zip(ys, ps)]
            yield
            ys = [y - p - t for y, p, t in zip(ys, ps, py)]
        s *= 2
    return ys


def _run_interleaved(*gens):
    active = list(gens)
    while active:
        for g in list(active):
            try:
                next(g)
            except StopIteration:
                active.remove(g)


def _gdn_kernel(qc_ref, kc_ref, vc_ref, qx_ref, kx_ref, vx_ref, gc_ref, gx_ref,
                hn_ref, oc_ref, ox_ref,
                qs, ks, vs, gsc, bsc, nq_s, b_s, op_s, gl_s, oacc, st_scr):
    k = pl.program_id(0)
    h = jnp.minimum(k, pl.num_programs(0) - 2) % HEADS
    cur_slot = k % 2
    prev_slot = 1 - cur_slot
    tc = qc_ref.shape[1]
    tx = qx_ref.shape[1]
    t = tc + tx
    nsteps = t // CHUNK
    nc_ctx = tc // CHUNK
    c = CHUNK

    def l2n(a):
        return a * lax.rsqrt(jnp.sum(a * a, axis=-1, keepdims=True) + EPS)

    @pl.when(k == 0)
    def _():
        for ref in (nq_s, b_s, op_s, gl_s, oacc, st_scr):
            ref[...] = jnp.zeros_like(ref)

    qscale = HEAD_DIM ** -0.5
    qs[0:tc] = l2n(qc_ref[0]) * qscale
    qs[tc:t] = l2n(qx_ref[0]) * qscale
    ks[0:tc] = l2n(kc_ref[0])
    ks[tc:t] = l2n(kx_ref[0])
    vs[0:tc] = vc_ref[0]
    vs[tc:t] = vx_ref[0]

    lane = lax.broadcasted_iota(jnp.int32, (1, LANES), 1)
    for d in range(2):
        for ref, lo, hi in ((gc_ref, 0, tc), (gx_ref, tc, t)):
            gates = ref[...]
            g = jnp.sum(jnp.where(lane == d * HEADS + h, gates, 0.0), axis=-1, keepdims=True)
            b = jnp.sum(jnp.where(lane == 2 * HEADS + d * HEADS + h, gates, 0.0), axis=-1,
                        keepdims=True)
            gsc[d, lo:hi] = jnp.broadcast_to(g, (hi - lo, LANES))
            bsc[d, lo:hi] = jnp.broadcast_to(b, (hi - lo, LANES))

    def rev_chunk(s):
        return nc_ctx - 1 - s if s < nc_ctx else nsteps + nc_ctx - 1 - s

    ri = lax.broadcasted_iota(jnp.int32, (c, 2 * c), 0)
    lane2 = lax.broadcasted_iota(jnp.int32, (c, 2 * c), 1)
    left = lane2 < c
    cj = lane2 & (c - 1)
    strict = (left & (cj < ri)) | (jnp.logical_not(left) & (cj > ri))
    incl = (left & (cj <= ri)) | (jnp.logical_not(left) & (cj >= ri))
    left_sq = lax.broadcasted_iota(jnp.int32, (2 * c, 2 * c), 1) < c

    def prep(steps):
        units = range(len(steps))
        rows = [[slice(n * c, (n + 1) * c) for n in (s, rev_chunk(s))] for s in steps]
        qv = [[qs[r] for r in rows[uu]] for uu in units]
        kv = [[ks[r] for r in rows[uu]] for uu in units]
        vv = [[vs[r] for r in rows[uu]] for uu in units]
        beta = [[bsc[d, rows[uu][d]] for d in range(2)] for uu in units]
        kb = [[kv[uu][d] * beta[uu][d] for d in range(2)] for uu in units]
        gk = [lax.dot_general(
                  jnp.concatenate([kb[uu][0], qv[uu][0], kb[uu][1], qv[uu][1]], axis=0).astype(BF16),
                  jnp.concatenate(kv[uu], axis=0).astype(BF16),
                  (((1,), (1,)), ((), ())), preferred_element_type=F32)
              for uu in units]
        yield
        gcb = [[gsc[d, rows[uu][d]] for d in range(2)] for uu in units]
        ms, avs = [], []
        for uu in units:
            col = jnp.where(left, gcb[uu][0], gcb[uu][1])
            row = jnp.transpose(jnp.concatenate(gcb[uu], axis=0))[0:c]
            decay = jnp.exp(jnp.where(incl, col - row, 0.0))
            kk = jnp.where(left, gk[uu][0:c], gk[uu][2 * c:3 * c])
            qk = jnp.where(left, gk[uu][c:2 * c], gk[uu][3 * c:4 * c])
            ms.append(jnp.where(strict, kk * decay, 0.0))
            avs.append(jnp.where(incl, qk * decay, 0.0))
        ys = yield from _unit_tri_inverse_minus_eye(ms, left, ri, cj)
        egc = [[jnp.exp(g) for g in gcb[uu]] for uu in units]
        rs = [[jnp.concatenate([kb[uu][d] * egc[uu][d], vv[uu][d] * beta[uu][d]], axis=1)
               for d in range(2)] for uu in units]
        corr = [jnp.dot(_block_diag2(ys[uu], left),
                        jnp.concatenate(rs[uu], axis=0).astype(BF16), preferred_element_type=F32)
                for uu in units]
        yield
        wus, aks, g_edges = [], [], []
        for uu in units:
            kgs, edges = [], []
            for d in range(2):
                edge = c - 1 if d == 0 else 0
                edges.append(gcb[uu][d][edge:edge + 1])
                kgs.append(kv[uu][d] * jnp.exp(edges[d] - gcb[uu][d]))
            g_edges.append(edges)
            wus.append([rs[uu][d] + corr[uu][d * c:(d + 1) * c] for d in range(2)])
            kgt = jnp.transpose(jnp.concatenate(kgs, axis=0))
            aks.append(jnp.concatenate(
                [jnp.where(left, avs[uu], 0.0), jnp.where(left_sq, kgt, 0.0),
                 jnp.where(left, 0.0, avs[uu]), jnp.where(left_sq, 0.0, kgt)], axis=0).astype(BF16))
        prods = [jnp.dot(aks[uu], jnp.concatenate(wus[uu], axis=0).astype(BF16),
                         preferred_element_type=F32) for uu in units]
        yield
        for uu in units:
            for d in range(2):
                x = prods[uu][d * (c + HEAD_DIM):(d + 1) * (c + HEAD_DIM)]
                a_w, a_u = x[0:c, 0:HEAD_DIM], x[0:c, HEAD_DIM:2 * HEAD_DIM]
                k_w, k_u = x[c:c + HEAD_DIM, 0:HEAD_DIM], x[c:c + HEAD_DIM, HEAD_DIM:2 * HEAD_DIM]
                idx = 2 * steps[uu] + d
                nq_s[idx] = jnp.concatenate([-k_w, qv[uu][d] * egc[uu][d] - a_w],
                                            axis=0).astype(BF16)
                b_s[idx] = k_u
                op_s[idx] = a_u
                gl_s[idx] = jnp.broadcast_to(jnp.exp(g_edges[uu][d]), (8, LANES))

    def scan(steps, states, slot, written):
        for s in steps:
            idxs = (2 * s, 2 * s + 1)
            xs = [jnp.dot(nq_s[idx], st.astype(BF16), preferred_element_type=F32)
                  for idx, st in zip(idxs, states)]
            yield
            for n, idx, x in zip((s, rev_chunk(s)), idxs, xs):
                o = x[HEAD_DIM:HEAD_DIM + c] + op_s[idx]
                rows = slice(n * c, (n + 1) * c)
                oacc[slot, rows] = oacc[slot, rows] + o if n in written else o
                written.add(n)
            states[:] = [st * gl_s[idx][0:1] + x[0:HEAD_DIM] + b_s[idx]
                         for idx, st, x in zip(idxs, states, xs)]

    bounds = np.cumsum((0,) + SCAN_GROUPS)
    assert bounds[-1] == nsteps
    groups = [list(range(lo, hi)) for lo, hi in zip(bounds[:-1], bounds[1:])]
    early = [s for g in groups[:-1] for s in g]
    written_early = {n for s in early for n in (s, rev_chunk(s))}

    prev_states = [st_scr[0], st_scr[1]]
    _run_interleaved(prep(groups[0]), scan(groups[-1], prev_states, prev_slot, set(written_early)))
    o = oacc[prev_slot]
    on = o * lax.rsqrt(jnp.mean(o * o, axis=-1, keepdims=True) + EPS) * hn_ref[...]
    oc_ref[0] = on[0:tc].astype(oc_ref.dtype)
    ox_ref[0] = on[tc:t].astype(ox_ref.dtype)

    states = [jnp.zeros((HEAD_DIM, HEAD_DIM), F32)] * 2
    written = set()
    for gi in range(1, len(groups)):
        _run_interleaved(prep(groups[gi]), scan(groups[gi - 1], states, cur_slot, written))
    assert written == written_early
    st_scr[0] = states[0]
    st_scr[1] = states[1]


def _gdn(qkv_c, qkv_x, gates_c, gates_x, head_norm):
    tc, tx = CTX_LEN, SEQ
    t = tc + tx
    nidx = 2 * (t // CHUNK)
    nprob = BATCH * HEADS
    cur = lambda k: jnp.minimum(k, nprob - 1)
    prev = lambda k: jnp.maximum(k - 1, 0)
    part = lambda rows, p: pl.BlockSpec(
        (1, rows, HEAD_DIM), lambda k: (p * HEADS + cur(k) % HEADS, cur(k) // HEADS, 0))
    return pl.pallas_call(
        _gdn_kernel,
        grid=(nprob + 1,),
        in_specs=[part(tc, 0), part(tc, 1), part(tc, 2),
                  part(tx, 0), part(tx, 1), part(tx, 2),
                  pl.BlockSpec((tc, LANES), lambda k: (cur(k) // HEADS, 0)),
                  pl.BlockSpec((tx, LANES), lambda k: (cur(k) // HEADS, 0)),
                  pl.BlockSpec((1, HEAD_DIM), lambda k: (0, 0))],
        out_specs=[pl.BlockSpec((1, tc, HEAD_DIM), lambda k: (prev(k) // HEADS, 0, prev(k) % HEADS)),
                   pl.BlockSpec((1, tx, HEAD_DIM), lambda k: (prev(k) // HEADS, 0, prev(k) % HEADS))],
        out_shape=[jax.ShapeDtypeStruct((BATCH, tc, D_MODEL), BF16),
                   jax.ShapeDtypeStruct((BATCH, tx, D_MODEL), BF16)],
        scratch_shapes=[pltpu.VMEM((t, HEAD_DIM), F32), pltpu.VMEM((t, HEAD_DIM), F32),
                        pltpu.VMEM((t, HEAD_DIM), F32),
                        pltpu.VMEM((2, t, LANES), F32), pltpu.VMEM((2, t, LANES), F32),
                        pltpu.VMEM((nidx, HEAD_DIM + CHUNK, HEAD_DIM), BF16),
                        pltpu.VMEM((nidx, HEAD_DIM, HEAD_DIM), F32),
                        pltpu.VMEM((nidx, CHUNK, HEAD_DIM), F32),
                        pltpu.VMEM((nidx, 8, LANES), F32),
                        pltpu.VMEM((2, t, HEAD_DIM), F32),
                        pltpu.VMEM((2, HEAD_DIM, HEAD_DIM), F32)],
        compiler_params=_params(("arbitrary",)),
        name="gdn",
    )(qkv_c, qkv_c, qkv_c, qkv_x, qkv_x, qkv_x, gates_c, gates_x, head_norm)


def _even_out_kernel(x_ref, gt_ref, o_ref, gz_ref, yb_ref, w_ref, out_ref, a_scr):
    j = pl.program_id(1)

    def project():
        y = jnp.dot(a_scr[...], w_ref[...], preferred_element_type=F32)
        out_ref[...] = x_ref[...] + gt_ref[0] * y

    @pl.when(j == 0)
    def _():
        d = o_ref.shape[1]
        a_scr[:, 0:d] = o_ref[...] * gz_ref[...]
        a_scr[:, d:2 * d] = yb_ref[...]
        project()

    @pl.when(j > 0)
    def _():
        project()


def _even_out(xin, mods, o, gz, yb, w_out, *, rows_per_batch, mod_row0):
    r, d = xin.shape
    tm = min(1024, rows_per_batch)
    tn = 512
    nj = d // tn
    tiles_per_batch = rows_per_batch // tm
    fixed = mod_row0 >= BATCH
    gt_idx = ((lambda i, j: (mod_row0, 0, 2 * nj + j)) if fixed
              else (lambda i, j: (i // tiles_per_batch, 0, 2 * nj + j)))
    return pl.pallas_call(
        _even_out_kernel,
        grid=(r // tm, nj),
        in_specs=[pl.BlockSpec((tm, tn), lambda i, j: (i, j)),
                  pl.BlockSpec((1, 1, tn), gt_idx),
                  pl.BlockSpec((tm, d), lambda i, j: (i, 0)),
                  pl.BlockSpec((tm, d), lambda i, j: (i, 0)),
                  pl.BlockSpec((tm, d), lambda i, j: (i, 0)),
                  pl.BlockSpec((2 * d, tn), lambda i, j: (0, j))],
        out_specs=pl.BlockSpec((tm, tn), lambda i, j: (i, j)),
        out_shape=jax.ShapeDtypeStruct((r, d), F32),
        scratch_shapes=[pltpu.VMEM((tm, 2 * d), BF16)],
        compiler_params=_params(("parallel", "arbitrary")),
        name="even_out",
    )(xin, mods, o, gz, yb, w_out)


def _odd_in_kernel(x_ref, nw_ref, sh_ref, sc_ref, wa_ref, wp_ref, act_ref, pd_ref, h_scr,
                   *, n_gelu):
    j = pl.program_id(1)

    @pl.when(j == 0)
    def _():
        _modulated_norm_to(h_scr, x_ref, nw_ref, sc_ref, sh_ref)

    def tile(act):
        for lo in range(0, wa_ref.shape[1], SUB_N):
            w = wa_ref[:, lo:lo + SUB_N].astype(BF16)
            p = jnp.dot(h_scr[...], w, preferred_element_type=F32)
            act_ref[:, lo:lo + SUB_N] = act(p).astype(act_ref.dtype)
        p = jnp.dot(h_scr[...], wp_ref[...].astype(BF16), preferred_element_type=F32)
        pd_ref[...] = p.astype(pd_ref.dtype)

    @pl.when(j < n_gelu)
    def _():
        tile(_gelu_tanh)

    @pl.when(j >= n_gelu)
    def _():
        tile(_silu)


def _odd_in(xin, mods, nw, w):
    r, d = xin.shape
    tm, tn = 1024, 1024
    n_act = 4 * d // tn
    assert d // SUB_N == n_act
    pd_blk0 = 3 * d // SUB_N
    skip = d // tn
    tiles_per_batch = SEQ // tm
    mod_idx = lambda part: (lambda i, j: (i // tiles_per_batch, 0, part))
    act_col = lambda j: jnp.where(j < 3 * d // tn, j, j + skip)
    return pl.pallas_call(
        functools.partial(_odd_in_kernel, n_gelu=2 * d // tn),
        grid=(r // tm, n_act),
        in_specs=[pl.BlockSpec((tm, d), lambda i, j: (i, 0)),
                  pl.BlockSpec((1, d), lambda i, j: (0, 0)),
                  pl.BlockSpec((1, 1, d), mod_idx(0)),
                  pl.BlockSpec((1, 1, d), mod_idx(1)),
                  pl.BlockSpec((d, tn), lambda i, j: (0, act_col(j))),
                  pl.BlockSpec((d, SUB_N), lambda i, j: (0, pd_blk0 + j))],
        out_specs=[pl.BlockSpec((tm, tn), lambda i, j: (i, j)),
                   pl.BlockSpec((tm, SUB_N), lambda i, j: (i, j))],
        out_shape=[jax.ShapeDtypeStruct((r, 4 * d), BF16), jax.ShapeDtypeStruct((r, d), BF16)],
        scratch_shapes=[pltpu.VMEM((tm, d), BF16)],
        compiler_params=_params(("parallel", "arbitrary")),
        name="odd_in",
    )(xin, nw, mods, mods, w, w)


def _odd_mix_kernel(x_ref, gt_ref, u_ref, v_ref, zc_ref, pd_ref, zd_ref, lnw_ref, lnb_ref, ws_ref,
                    bs_ref, band_ref, icnt_ref, pw_ref, ps_ref, wo_ref, fnw_ref, out_ref,
                    acc, ln_scr, *, nk_c):
    kk = pl.program_id(1)
    tm = x_ref.shape[0]
    tk = u_ref.shape[1]
    gpb = tk // HEAD_DIM

    @pl.when(kk == 0)
    def _():
        v = v_ref[...].astype(F32)
        mu = jnp.mean(v, axis=-1, keepdims=True)
        xc = v - mu
        var = jnp.mean(xc * xc, axis=-1, keepdims=True)
        ln = (xc * lax.rsqrt(var + EPS) * lnw_ref[...] + lnb_ref[...]).astype(BF16)
        for blk in range(nk_c):
            ln_scr[blk] = ln[:, blk * tk:(blk + 1) * tk]
        acc[...] = jnp.zeros_like(acc)

    @pl.when(kk < nk_c)
    def _():
        ln = ln_scr[kk]
        cols = []
        for gi in range(gpb):
            w_g = ws_ref[gi]
            b_g = bs_ref[gi]
            rows = []
            for cc in range(tm // TOK_CHUNK):
                blk = ln[cc * TOK_CHUNK:(cc + 1) * TOK_CHUNK, gi * HEAD_DIM:(gi + 1) * HEAD_DIM]
                rows.append(jnp.dot(w_g, blk, preferred_element_type=F32) + b_g)
            cols.append(jnp.concatenate(rows, axis=0))
        s = jnp.concatenate(cols, axis=1)
        yc = u_ref[...] * s * zc_ref[...]
        acc[...] += jnp.dot(yc.astype(BF16), wo_ref[...], preferred_element_type=F32)

    @pl.when(kk >= nk_c)
    def _():
        parts = []
        for pi in range(tk // POOL_GW):
            cols = slice(pi * POOL_GW, (pi + 1) * POOL_GW)
            pd = pd_ref[:, cols]
            wsum = jnp.dot(band_ref[pi], pd, preferred_element_type=F32)
            icnt = jnp.concatenate([icnt_ref[pi]] * (POOL_GW // LANES), axis=1)
            diff = wsum * icnt - pd.astype(F32)
            y = jnp.dot(diff.astype(BF16), pw_ref[pi], preferred_element_type=F32)
            parts.append((y * ps_ref[:, cols] * zd_ref[:, cols]).astype(BF16))
        yd = jnp.concatenate(parts, axis=1)
        acc[...] += jnp.dot(yd, wo_ref[...], preferred_element_type=F32)

    @pl.when(kk == pl.num_programs(1) - 1)
    def _():
        xn = x_ref[...] + gt_ref[0] * acc[...]
        ms = jnp.mean(xn * xn, axis=-1, keepdims=True)
        out_ref[...] = xn * lax.rsqrt(ms + EPS) * fnw_ref[...]


def _pool_constants(tm):
    bands = np.zeros((POOL_GROUPS, tm, tm), np.float32)
    icnt = np.zeros((POOL_GROUPS, tm, LANES), np.float32)
    pos = np.arange(tm)
    row = pos // GRID_W
    col = pos % GRID_W
    for g, rad in enumerate(POOL_RADII):
        same = row[:, None] == row[None, :]
        bands[g] = (same & (np.abs(col[:, None] - col[None, :]) <= rad)).astype(np.float32)
        cnt = np.minimum(col + rad + 1, GRID_W) - np.maximum(col - rad, 0)
        icnt[g] = (1.0 / cnt.astype(np.float32))[:, None]
    return jnp.asarray(bands, BF16), jnp.asarray(icnt, F32)


def _odd_mix(xin, mods, p, p_d, ln_w, ln_b, w_s, b_s, pool_w, pool_scale, w_out, fnw):
    r, d = xin.shape
    tm, tk = 512, 1024
    nk_c = d // tk
    nk = 2 * nk_c
    gpb = tk // HEAD_DIM
    ppb = tk // POOL_GW
    tiles_per_batch = SEQ // tm
    bands, icnt = _pool_constants(tm)
    lo = lambda kk: jnp.minimum(kk, nk_c - 1)
    hi = lambda kk: jnp.maximum(kk - nk_c, 0)
    cblk = d // tk
    return pl.pallas_call(
        functools.partial(_odd_mix_kernel, nk_c=nk_c),
        grid=(r // tm, nk),
        in_specs=[pl.BlockSpec((tm, d), lambda i, kk: (i, 0)),
                  pl.BlockSpec((1, 1, d), lambda i, kk: (i // tiles_per_batch, 0, 2)),
                  pl.BlockSpec((tm, tk), lambda i, kk: (i, lo(kk))),
                  pl.BlockSpec((tm, d), lambda i, kk: (i, 1)),
                  pl.BlockSpec((tm, tk), lambda i, kk: (i, 2 * cblk + lo(kk))),
                  pl.BlockSpec((tm, tk), lambda i, kk: (i, hi(kk))),
                  pl.BlockSpec((tm, tk), lambda i, kk: (i, 3 * cblk + hi(kk))),
                  pl.BlockSpec((1, d), lambda i, kk: (0, 0)),
                  pl.BlockSpec((1, d), lambda i, kk: (0, 0)),
                  pl.BlockSpec((gpb, TOK_CHUNK, TOK_CHUNK), lambda i, kk: (lo(kk), 0, 0)),
                  pl.BlockSpec((gpb, TOK_CHUNK, LANES), lambda i, kk: (lo(kk), 0, 0)),
                  pl.BlockSpec((ppb, tm, tm), lambda i, kk: (hi(kk), 0, 0)),
                  pl.BlockSpec((ppb, tm, LANES), lambda i, kk: (hi(kk), 0, 0)),
                  pl.BlockSpec((ppb, POOL_GW, POOL_GW), lambda i, kk: (hi(kk), 0, 0)),
                  pl.BlockSpec((1, tk), lambda i, kk: (0, hi(kk))),
                  pl.BlockSpec((tk, d), lambda i, kk: (kk, 0)),
                  pl.BlockSpec((1, d), lambda i, kk: (0, 0))],
        out_specs=pl.BlockSpec((tm, d), lambda i, kk: (i, 0)),
        out_shape=jax.ShapeDtypeStruct((r, d), F32),
        scratch_shapes=[pltpu.VMEM((tm, d), F32), pltpu.VMEM((nk_c, tm, tk), BF16)],
        compiler_params=_params(("parallel", "arbitrary")),
        name="odd_mix",
    )(xin, mods, p, p, p, p_d, p, ln_w, ln_b, w_s, b_s, bands, icnt, pool_w, pool_scale, w_out, fnw)


def kernel(x, c, ctx, c_ctx, ada_w, ada_b, norm_w, e_w_in, e_conv_qkv, e_a_log, e_dt_bias, e_head_norm,
           e_conv_b, e_w_out, o_w_in, o_ln_w, o_ln_b, o_w_s, o_b_s, o_pool_w, o_pool_scale, o_w_out,
           final_norm_w):
    bsz, seq, d = x.shape
    rx = bsz * seq
    rc = bsz * ctx.shape[1]
    x2 = x.reshape(rx, d)
    ctx2 = ctx.reshape(rc, d)

    cc = jnp.concatenate([c, c_ctx[None, :], jnp.zeros((8 - bsz - 1, d), F32)], axis=0)
    mods = _adaln(cc, ada_w, ada_b)
    ctx_row = bsz

    w_in = e_w_in[0].astype(BF16)
    w_bchz = w_in[:, GATE_COL0 + 4 * HEADS:]
    m0 = mods[0].reshape(8, 1, 3 * d)
    nw0 = norm_w[0].reshape(1, d)

    gate_par = jnp.pad(jnp.stack([e_a_log[0].reshape(-1), e_dt_bias[0].reshape(-1)]),
                       ((0, 0), (0, LANES - 2 * HEADS)))
    qkv_x, gates_x = _qkv_proj(x2, m0, nw0, w_in, e_conv_qkv[0], gate_par,
                               rows_per_batch=seq, mod_row0=0, group=GRID_W)
    qkv_c, gates_c = _qkv_proj(ctx2, m0, nw0, w_in, e_conv_qkv[0], gate_par,
                               rows_per_batch=ctx.shape[1], mod_row0=ctx_row, group=ctx.shape[1])
    gz_x, yb_x = _mixb_proj(x2, m0, nw0, w_in, w_bchz, e_conv_b[0],
                            rows_per_batch=seq, mod_row0=0, group=GRID_W)
    gz_c, yb_c = _mixb_proj(ctx2, m0, nw0, w_in, w_bchz, e_conv_b[0],
                            rows_per_batch=ctx.shape[1], mod_row0=ctx_row, group=ctx.shape[1])
    o_c, o_x = _gdn(qkv_c, qkv_x, gates_c, gates_x,
                    e_head_norm[0].reshape(1, HEAD_DIM))
    w_out0 = e_w_out[0].astype(BF16)
    x2 = _even_out(x2, m0, o_x.reshape(rx, d), gz_x, yb_x, w_out0, rows_per_batch=seq, mod_row0=0)
    ctx2 = _even_out(ctx2, m0, o_c.reshape(rc, d), gz_c, yb_c, w_out0,
                     rows_per_batch=ctx.shape[1], mod_row0=ctx_row)

    m1 = mods[1].reshape(8, 1, 3 * d)
    p, p_d = _odd_in(x2, m1, norm_w[1].reshape(1, d), o_w_in[0])
    b_s = jnp.broadcast_to(o_b_s[0][:, :, None], (o_b_s.shape[1], TOK_CHUNK, LANES))
    out = _odd_mix(x2, m1, p, p_d, o_ln_w[0].reshape(1, d), o_ln_b[0].reshape(1, d),
                   o_w_s[0].astype(BF16), b_s, o_pool_w[0].astype(BF16),
                   o_pool_scale[0].reshape(1, d), o_w_out[0].astype(BF16),
                   final_norm_w.reshape(1, d))
    return out.reshape(bsz, seq, d)
```

```python
import functools
import math

import numpy as np
import jax
import jax.numpy as jnp
from jax import lax
from jax.experimental import pallas as pl
from jax.experimental.pallas import tpu as pltpu

F32 = jnp.float32
BF16 = jnp.bfloat16

D_MODEL = 2048
BATCH = 4
SEQ = 2048
CTX_LEN = 256
GRID_W = 64
EPS = 1e-6
HEADS = 16
HEAD_DIM = 128
QKV_W = 3 * D_MODEL
GATE_COL0 = QKV_W + D_MODEL
CHUNK = 64
SCAN_GROUPS = (12, 13, 11)
TOK_CHUNK = 128
POOL_GROUPS = 4
POOL_GW = D_MODEL // POOL_GROUPS
POOL_RADII = (1, 2, 4, 8)

LANES = 128
VMEM_LIMIT = 56 * 1024 * 1024
NORM_ROWS = 16
NORM_UNROLL = 16
SUB_N = 256


def _params(sem):
    return pltpu.CompilerParams(dimension_semantics=sem, vmem_limit_bytes=VMEM_LIMIT)


def _sigmoid(x):
    return 1.0 / (1.0 + jnp.exp(-x))


def _silu(x):
    h = 0.5 * x
    return h + h * jnp.tanh(h)


def _gelu_tanh(x):
    cdf = 0.5 * (1.0 + jnp.tanh(math.sqrt(2.0 / math.pi) * (x + 0.044715 * (x * x * x))))
    return x * cdf


def _modulated_norm_to(h_scr, x_ref, nw_ref, sc_ref, sh_ref):
    gain = nw_ref[...] * (1.0 + sc_ref[0])
    sh = sh_ref[0]

    step = NORM_ROWS * NORM_UNROLL

    def body(r, carry):
        rows = [pl.ds(pl.multiple_of(r * step + k * NORM_ROWS, NORM_ROWS), NORM_ROWS)
                for k in range(NORM_UNROLL)]
        scales = []
        for rw in rows:
            x = x_ref[rw, :]
            scales.append(lax.rsqrt(jnp.mean(x * x, axis=-1, keepdims=True) + EPS))
        for rw, rs in zip(rows, scales):
            h_scr[rw, :] = (x_ref[rw, :] * rs * gain + sh).astype(h_scr.dtype)
        return carry

    lax.fori_loop(0, x_ref.shape[0] // step, body, 0)


def _row_conv3(p, cw, group):
    tm = p.shape[0]
    row = lax.broadcasted_iota(jnp.int32, (tm, 1), 0) % group
    prev = jnp.where(row == 0, 0.0, pltpu.roll(p, 1, 0))
    nxt = jnp.where(row == group - 1, 0.0, pltpu.roll(p, tm - 1, 0))
    return prev * cw[0:1] + p * cw[1:2] + nxt * cw[2:3]


def _adaln_kernel(c_ref, w_ref, b_ref, o_ref):
    a = _silu(c_ref[...]).astype(BF16)
    o_ref[0] = jnp.dot(a, w_ref[0].astype(BF16), preferred_element_type=F32) + b_ref[0]


def _adaln(cc, ada_w, ada_b):
    depth, d, n = ada_w.shape
    tn = 1024
    return pl.pallas_call(
        _adaln_kernel,
        grid=(depth, n // tn),
        in_specs=[pl.BlockSpec((8, d), lambda l, j: (0, 0)),
                  pl.BlockSpec((1, d, tn), lambda l, j: (l, 0, j)),
                  pl.BlockSpec((1, 1, tn), lambda l, j: (l, 0, j))],
        out_specs=pl.BlockSpec((1, 8, tn), lambda l, j: (l, 0, j)),
        out_shape=jax.ShapeDtypeStruct((depth, 8, n), F32),
        compiler_params=_params(("parallel", "parallel")),
        name="adaln",
    )(cc, ada_w, ada_b.reshape(depth, 1, n))


def _qkv_kernel(x_ref, nw_ref, sh_ref, sc_ref, w_ref, wg_ref, cw_ref, gp_ref, qkv_ref, g_ref, h_scr,
                *, group, n_tiles):
    j = pl.program_id(1)

    @pl.when(j == 0)
    def _():
        _modulated_norm_to(h_scr, x_ref, nw_ref, sc_ref, sh_ref)

    def tile():
        p = jnp.dot(h_scr[...], w_ref[...], preferred_element_type=F32)
        y = _silu(_row_conv3(p, cw_ref[...], group))
        for hh in range(y.shape[1] // HEAD_DIM):
            qkv_ref[hh] = y[:, hh * HEAD_DIM:(hh + 1) * HEAD_DIM]

    def gates():
        raw = jnp.dot(h_scr[...], wg_ref[...], preferred_element_type=F32)
        z = raw + gp_ref[1:2]
        softplus = jnp.maximum(z, 0.0) + jnp.log1p(jnp.exp(-jnp.abs(z)))
        lane = lax.broadcasted_iota(jnp.int32, (1, LANES), 1)
        gates = jnp.where(lane < 2 * HEADS, -jnp.exp(gp_ref[0:1]) * softplus, _sigmoid(raw))
        ri = lax.broadcasted_iota(jnp.int32, (LANES, LANES), 0)
        ci = lax.broadcasted_iota(jnp.int32, (LANES, LANES), 1)
        same = (ri // CHUNK) == (ci // CHUNK)
        pre01 = jnp.where(same & (ci <= ri), 1.0, 0.0).astype(BF16)
        suf01 = jnp.where(same & (ci >= ri), 1.0, 0.0).astype(BF16)
        pieces = _split3_bf16(gates)
        for r0 in range(0, gates.shape[0], LANES):
            blk = [p[r0:r0 + LANES] for p in pieces]
            pre = [jnp.dot(pre01, b, preferred_element_type=F32) for b in blk]
            suf = [jnp.dot(suf01, b, preferred_element_type=F32) for b in blk]
            pre = (pre[0] + pre[1]) + pre[2]
            suf = (suf[0] + suf[1]) + suf[2]
            g_ref[r0:r0 + LANES] = jnp.where(lane < HEADS, pre,
                                             jnp.where(lane < 2 * HEADS, suf, gates[r0:r0 + LANES]))

    @pl.when(j < n_tiles - 1)
    def _():
        tile()

    @pl.when(j == n_tiles - 1)
    def _():
        tile()
        gates()


def _qkv_proj(xin, mods, nw, w_in, cw, gate_par, *, rows_per_batch, mod_row0, group):
    r, d = xin.shape
    tm, tn = 1024, 768
    n_tiles = QKV_W // tn
    hpt = tn // HEAD_DIM
    tiles_per_batch = max(rows_per_batch // tm, 1)

    def mod_idx(part):
        if rows_per_batch >= tm:
            return lambda i, j: (mod_row0 + i // tiles_per_batch, 0, part)
        return lambda i, j: (mod_row0, 0, part)

    return pl.pallas_call(
        functools.partial(_qkv_kernel, group=group, n_tiles=n_tiles),
        grid=(r // tm, n_tiles),
        in_specs=[pl.BlockSpec((tm, d), lambda i, j: (i, 0)),
                  pl.BlockSpec((1, d), lambda i, j: (0, 0)),
                  pl.BlockSpec((1, 1, d), mod_idx(0)),
                  pl.BlockSpec((1, 1, d), mod_idx(1)),
                  pl.BlockSpec((d, tn), lambda i, j: (0, j)),
                  pl.BlockSpec((d, LANES), lambda i, j: (0, GATE_COL0 // LANES)),
                  pl.BlockSpec((3, tn), lambda i, j: (0, j)),
                  pl.BlockSpec((2, LANES), lambda i, j: (0, 0))],
        out_specs=[pl.BlockSpec((hpt, tm, HEAD_DIM), lambda i, j: (j, i, 0)),
                   pl.BlockSpec((tm, LANES), lambda i, j: (i, 0))],
        out_shape=[jax.ShapeDtypeStruct((3 * HEADS, r, HEAD_DIM), F32),
                   jax.ShapeDtypeStruct((r, LANES), F32)],
        scratch_shapes=[pltpu.VMEM((tm, d), BF16)],
        compiler_params=_params(("parallel", "arbitrary")),
        name="qkv_proj",
    )(xin, nw, mods, mods, w_in, w_in, cw, gate_par)


def _mixb_kernel(x_ref, nw_ref, sh_ref, sc_ref, wz_ref, wb_ref, wc_ref, wh_ref, wzb_ref, cw_ref,
                 gz_ref, yb_ref, h_scr, *, group):
    j = pl.program_id(1)

    @pl.when(j == 0)
    def _():
        _modulated_norm_to(h_scr, x_ref, nw_ref, sc_ref, sh_ref)

    h = h_scr[...]
    dot = lambda w: jnp.dot(h, w[...], preferred_element_type=F32)
    gz_ref[...] = _silu(dot(wz_ref)).astype(gz_ref.dtype)
    conv = _row_conv3(dot(wc_ref) * dot(wh_ref), cw_ref[...], group)
    yb_ref[...] = (dot(wb_ref) * conv * _silu(dot(wzb_ref))).astype(yb_ref.dtype)


def _mixb_proj(xin, mods, nw, w_in, w_bchz, cw, *, rows_per_batch, mod_row0, group):
    r, d = xin.shape
    tm, tn = 1024, 256
    nj = d // tn
    za_blk0 = QKV_W // tn
    tiles_per_batch = max(rows_per_batch // tm, 1)

    def mod_idx(part):
        if rows_per_batch >= tm:
            return lambda i, j: (mod_row0 + i // tiles_per_batch, 0, part)
        return lambda i, j: (mod_row0, 0, part)

    wspec = lambda reg: pl.BlockSpec((d, tn), lambda i, j: (0, reg * nj + j))
    return pl.pallas_call(
        functools.partial(_mixb_kernel, group=group),
        grid=(r // tm, nj),
        in_specs=[pl.BlockSpec((tm, d), lambda i, j: (i, 0)),
                  pl.BlockSpec((1, d), lambda i, j: (0, 0)),
                  pl.BlockSpec((1, 1, d), mod_idx(0)),
                  pl.BlockSpec((1, 1, d), mod_idx(1)),
                  pl.BlockSpec((d, tn), lambda i, j: (0, za_blk0 + j)),
                  wspec(0), wspec(1), wspec(2), wspec(3),
                  pl.BlockSpec((3, tn), lambda i, j: (0, j))],
        out_specs=[pl.BlockSpec((tm, tn), lambda i, j: (i, j)),
                   pl.BlockSpec((tm, tn), lambda i, j: (i, j))],
        out_shape=[jax.ShapeDtypeStruct((r, d), BF16), jax.ShapeDtypeStruct((r, d), BF16)],
        scratch_shapes=[pltpu.VMEM((tm, d), BF16)],
        compiler_params=_params(("parallel", "arbitrary")),
        name="mixb_proj",
    )(xin, nw, mods, mods, w_in, w_bchz, w_bchz, w_bchz, w_bchz, cw)


def _split3_bf16(x):
    x1 = x.astype(BF16)
    r1 = x - x1.astype(F32)
    x2 = r1.astype(BF16)
    x3 = (r1 - x2.astype(F32)).astype(BF16)
    return x1, x2, x3


def _block_diag2(x, left):
    return jnp.concatenate([jnp.where(left, x, 0.0), jnp.where(left, 0.0, x)], axis=0).astype(BF16)


def _unit_tri_inverse_minus_eye(ms, left, ri, cj):
    n = ms[0].shape[0]
    ys = None
    s = 1
    while s < n:
        shift = s.bit_length() - 1
        br = lax.shift_right_logical(ri, shift)
        bc = lax.shift_right_logical(cj, shift)
        mask_lo = ((br & 1) == 1) & (bc == br - 1)
        mask_up = ((bc & 1) == 1) & (br == bc - 1)
        mask = (left & mask_lo) | (jnp.logical_not(left) & mask_up)
        cms = [jnp.where(mask, m, 0.0) for m in ms]
        if ys is None:
            ys = [-cm for cm in cms]
        else:
            yc = [jnp.dot(y.astype(BF16), _block_diag2(cm, left), preferred_element_type=F32)
                  for cm, y in zip(cms, ys)]
            yield
            ps = [cm + t for cm, t in zip(cms, yc)]
            py = [jnp.dot(p.astype(BF16), _block_diag2(y, left), preferred_element_type=F32)
                  for y, p in zip(ys, ps)]
            yield
            ys = [y - p - t for y, p, t in zip(ys, ps, py)]
        s *= 2
    return ys


def _run_interleaved(*gens):
    active = list(gens)
    while active:
        for g in list(active):
            try:
                next(g)
            except StopIteration:
                active.remove(g)


def _gdn_kernel(qc_ref, kc_ref, vc_ref, qx_ref, kx_ref, vx_ref, gc_ref, gx_ref,
                hn_ref, oc_ref, ox_ref,
                qs, ks, vs, gsc, bsc, nq_s, b_s, op_s, gl_s, oacc, st_scr):
    k = pl.program_id(0)
    h = jnp.minimum(k, pl.num_programs(0) - 2) % HEADS
    cur_slot = k % 2
    prev_slot = 1 - cur_slot
    tc = qc_ref.shape[1]
    tx = qx_ref.shape[1]
    t = tc + tx
    nsteps = t // CHUNK
    nc_ctx = tc // CHUNK
    c = CHUNK

    def l2n(a):
        return a * lax.rsqrt(jnp.sum(a * a, axis=-1, keepdims=True) + EPS)

    @pl.when(k == 0)
    def _():
        for ref in (nq_s, b_s, op_s, gl_s, oacc, st_scr):
            ref[...] = jnp.zeros_like(ref)

    qscale = HEAD_DIM ** -0.5
    qs[0:tc] = l2n(qc_ref[0]) * qscale
    qs[tc:t] = l2n(qx_ref[0]) * qscale
    ks[0:tc] = l2n(kc_ref[0])
    ks[tc:t] = l2n(kx_ref[0])
    vs[0:tc] = vc_ref[0]
    vs[tc:t] = vx_ref[0]

    lane = lax.broadcasted_iota(jnp.int32, (1, LANES), 1)
    for d in range(2):
        for ref, lo, hi in ((gc_ref, 0, tc), (gx_ref, tc, t)):
            gates = ref[...]
            g = jnp.sum(jnp.where(lane == d * HEADS + h, gates, 0.0), axis=-1, keepdims=True)
            b = jnp.sum(jnp.where(lane == 2 * HEADS + d * HEADS + h, gates, 0.0), axis=-1,
                        keepdims=True)
            gsc[d, lo:hi] = jnp.broadcast_to(g, (hi - lo, LANES))
            bsc[d, lo:hi] = jnp.broadcast_to(b, (hi - lo, LANES))

    def rev_chunk(s):
        return nc_ctx - 1 - s if s < nc_ctx else nsteps + nc_ctx - 1 - s

    ri = lax.broadcasted_iota(jnp.int32, (c, 2 * c), 0)
    lane2 = lax.broadcasted_iota(jnp.int32, (c, 2 * c), 1)
    left = lane2 < c
    cj = lane2 & (c - 1)
    strict = (left & (cj < ri)) | (jnp.logical_not(left) & (cj > ri))
    incl = (left & (cj <= ri)) | (jnp.logical_not(left) & (cj >= ri))
    left_sq = lax.broadcasted_iota(jnp.int32, (2 * c, 2 * c), 1) < c

    def prep(steps):
        units = range(len(steps))
        rows = [[slice(n * c, (n + 1) * c) for n in (s, rev_chunk(s))] for s in steps]
        qv = [[qs[r] for r in rows[uu]] for uu in units]
        kv = [[ks[r] for r in rows[uu]] for uu in units]
        vv = [[vs[r] for r in rows[uu]] for uu in units]
        beta = [[bsc[d, rows[uu][d]] for d in range(2)] for uu in units]
        kb = [[kv[uu][d] * beta[uu][d] for d in range(2)] for uu in units]
        gk = [lax.dot_general(
                  jnp.concatenate([kb[uu][0], qv[uu][0], kb[uu][1], qv[uu][1]], axis=0).astype(BF16),
                  jnp.concatenate(kv[uu], axis=0).astype(BF16),
                  (((1,), (1,)), ((), ())), preferred_element_type=F32)
              for uu in units]
        yield
        gcb = [[gsc[d, rows[uu][d]] for d in range(2)] for uu in units]
        ms, avs = [], []
        for uu in units:
            col = jnp.where(left, gcb[uu][0], gcb[uu][1])
            row = jnp.transpose(jnp.concatenate(gcb[uu], axis=0))[0:c]
            decay = jnp.exp(jnp.where(incl, col - row, 0.0))
            kk = jnp.where(left, gk[uu][0:c], gk[uu][2 * c:3 * c])
            qk = jnp.where(left, gk[uu][c:2 * c], gk[uu][3 * c:4 * c])
            ms.append(jnp.where(strict, kk * decay, 0.0))
            avs.append(jnp.where(incl, qk * decay, 0.0))
        ys = yield from _unit_tri_inverse_minus_eye(ms, left, ri, cj)
        egc = [[jnp.exp(g) for g in gcb[uu]] for uu in units]
        rs = [[jnp.concatenate([kb[uu][d] * egc[uu][d], vv[uu][d] * beta[uu][d]], axis=1)
               for d in range(2)] for uu in units]
        corr = [jnp.dot(_block_diag2(ys[uu], left),
                        jnp.concatenate(rs[uu], axis=0).astype(BF16), preferred_element_type=F32)
                for uu in units]
        yield
        wus, aks, g_edges = [], [], []
        for uu in units:
            kgs, edges = [], []
            for d in range(2):
                edge = c - 1 if d == 0 else 0
                edges.append(gcb[uu][d][edge:edge + 1])
                kgs.append(kv[uu][d] * jnp.exp(edges[d] - gcb[uu][d]))
            g_edges.append(edges)
            wus.append([rs[uu][d] + corr[uu][d * c:(d + 1) * c] for d in range(2)])
            kgt = jnp.transpose(jnp.concatenate(kgs, axis=0))
            aks.append(jnp.concatenate(
                [jnp.where(left, avs[uu], 0.0), jnp.where(left_sq, kgt, 0.0),
                 jnp.where(left, 0.0, avs[uu]), jnp.where(left_sq, 0.0, kgt)], axis=0).astype(BF16))
        prods = [jnp.dot(aks[uu], jnp.concatenate(wus[uu], axis=0).astype(BF16),
                         preferred_element_type=F32) for uu in units]
        yield
        for uu in units:
            for d in range(2):
                x = prods[uu][d * (c + HEAD_DIM):(d + 1) * (c + HEAD_DIM)]
                a_w, a_u = x[0:c, 0:HEAD_DIM], x[0:c, HEAD_DIM:2 * HEAD_DIM]
                k_w, k_u = x[c:c + HEAD_DIM, 0:HEAD_DIM], x[c:c + HEAD_DIM, HEAD_DIM:2 * HEAD_DIM]
                idx = 2 * steps[uu] + d
                nq_s[idx] = jnp.concatenate([-k_w, qv[uu][d] * egc[uu][d] - a_w],
                                            axis=0).astype(BF16)
                b_s[idx] = k_u
                op_s[idx] = a_u
                gl_s[idx] = jnp.broadcast_to(jnp.exp(g_edges[uu][d]), (8, LANES))

    def scan(steps, states, slot, written):
        for s in steps:
            idxs = (2 * s, 2 * s + 1)
            xs = [jnp.dot(nq_s[idx], st.astype(BF16), preferred_element_type=F32)
                  for idx, st in zip(idxs, states)]
            yield
            for n, idx, x in zip((s, rev_chunk(s)), idxs, xs):
                o = x[HEAD_DIM:HEAD_DIM + c] + op_s[idx]
                rows = slice(n * c, (n + 1) * c)
                oacc[slot, rows] = oacc[slot, rows] + o if n in written else o
                written.add(n)
            states[:] = [st * gl_s[idx][0:1] + x[0:HEAD_DIM] + b_s[idx]
                         for idx, st, x in zip(idxs, states, xs)]

    bounds = np.cumsum((0,) + SCAN_GROUPS)
    assert bounds[-1] == nsteps
    groups = [list(range(lo, hi)) for lo, hi in zip(bounds[:-1], bounds[1:])]
    early = [s for g in groups[:-1] for s in g]
    written_early = {n for s in early for n in (s, rev_chunk(s))}

    prev_states = [st_scr[0], st_scr[1]]
    _run_interleaved(prep(groups[0]), scan(groups[-1], prev_states, prev_slot, set(written_early)))
    o = oacc[prev_slot]
    on = o * lax.rsqrt(jnp.mean(o * o, axis=-1, keepdims=True) + EPS) * hn_ref[...]
    oc_ref[0] = on[0:tc].astype(oc_ref.dtype)
    ox_ref[0] = on[tc:t].astype(ox_ref.dtype)

    states = [jnp.zeros((HEAD_DIM, HEAD_DIM), F32)] * 2
    written = set()
    for gi in range(1, len(groups)):
        _run_interleaved(prep(groups[gi]), scan(groups[gi - 1], states, cur_slot, written))
    assert written == written_early
    st_scr[0] = states[0]
    st_scr[1] = states[1]


def _gdn(qkv_c, qkv_x, gates_c, gates_x, head_norm):
    tc, tx = CTX_LEN, SEQ
    t = tc + tx
    nidx = 2 * (t // CHUNK)
    nprob = BATCH * HEADS
    cur = lambda k: jnp.minimum(k, nprob - 1)
    prev = lambda k: jnp.maximum(k - 1, 0)
    part = lambda rows, p: pl.BlockSpec(
        (1, rows, HEAD_DIM), lambda k: (p * HEADS + cur(k) % HEADS, cur(k) // HEADS, 0))
    return pl.pallas_call(
        _gdn_kernel,
        grid=(nprob + 1,),
        in_specs=[part(tc, 0), part(tc, 1), part(tc, 2),
                  part(tx, 0), part(tx, 1), part(tx, 2),
                  pl.BlockSpec((tc, LANES), lambda k: (cur(k) // HEADS, 0)),
                  pl.BlockSpec((tx, LANES), lambda k: (cur(k) // HEADS, 0)),
                  pl.BlockSpec((1, HEAD_DIM), lambda k: (0, 0))],
        out_specs=[pl.BlockSpec((1, tc, HEAD_DIM), lambda k: (prev(k) // HEADS, 0, prev(k) % HEADS)),
                   pl.BlockSpec((1, tx, HEAD_DIM), lambda k: (prev(k) // HEADS, 0, prev(k) % HEADS))],
        out_shape=[jax.ShapeDtypeStruct((BATCH, tc, D_MODEL), BF16),
                   jax.ShapeDtypeStruct((BATCH, tx, D_MODEL), BF16)],
        scratch_shapes=[pltpu.VMEM((t, HEAD_DIM), F32), pltpu.VMEM((t, HEAD_DIM), F32),
                        pltpu.VMEM((t, HEAD_DIM), F32),
                        pltpu.VMEM((2, t, LANES), F32), pltpu.VMEM((2, t, LANES), F32),
                        pltpu.VMEM((nidx, HEAD_DIM + CHUNK, HEAD_DIM), BF16),
                        pltpu.VMEM((nidx, HEAD_DIM, HEAD_DIM), F32),
                        pltpu.VMEM((nidx, CHUNK, HEAD_DIM), F32),
                        pltpu.VMEM((nidx, 8, LANES), F32),
                        pltpu.VMEM((2, t, HEAD_DIM), F32),
                        pltpu.VMEM((2, HEAD_DIM, HEAD_DIM), F32)],
        compiler_params=_params(("arbitrary",)),
        name="gdn",
    )(qkv_c, qkv_c, qkv_c, qkv_x, qkv_x, qkv_x, gates_c, gates_x, head_norm)


def _even_out_kernel(x_ref, gt_ref, o_ref, gz_ref, yb_ref, w_ref, out_ref, a_scr):
    j = pl.program_id(1)

    def project():
        y = jnp.dot(a_scr[...], w_ref[...], preferred_element_type=F32)
        out_ref[...] = x_ref[...] + gt_ref[0] * y

    @pl.when(j == 0)
    def _():
        d = o_ref.shape[1]
        a_scr[:, 0:d] = o_ref[...] * gz_ref[...]
        a_scr[:, d:2 * d] = yb_ref[...]
        project()

    @pl.when(j > 0)
    def _():
        project()


def _even_out(xin, mods, o, gz, yb, w_out, *, rows_per_batch, mod_row0):
    r, d = xin.shape
    tm = min(1024, rows_per_batch)
    tn = 512
    nj = d // tn
    tiles_per_batch = rows_per_batch // tm
    fixed = mod_row0 >= BATCH
    gt_idx = ((lambda i, j: (mod_row0, 0, 2 * nj + j)) if fixed
              else (lambda i, j: (i // tiles_per_batch, 0, 2 * nj + j)))
    return pl.pallas_call(
        _even_out_kernel,
        grid=(r // tm, nj),
        in_specs=[pl.BlockSpec((tm, tn), lambda i, j: (i, j)),
                  pl.BlockSpec((1, 1, tn), gt_idx),
                  pl.BlockSpec((tm, d), lambda i, j: (i, 0)),
                  pl.BlockSpec((tm, d), lambda i, j: (i, 0)),
                  pl.BlockSpec((tm, d), lambda i, j: (i, 0)),
                  pl.BlockSpec((2 * d, tn), lambda i, j: (0, j))],
        out_specs=pl.BlockSpec((tm, tn), lambda i, j: (i, j)),
        out_shape=jax.ShapeDtypeStruct((r, d), F32),
        scratch_shapes=[pltpu.VMEM((tm, 2 * d), BF16)],
        compiler_params=_params(("parallel", "arbitrary")),
        name="even_out",
    )(xin, mods, o, gz, yb, w_out)


def _odd_in_kernel(x_ref, nw_ref, sh_ref, sc_ref, wa_ref, wp_ref, act_ref, pd_ref, h_scr,
                   *, n_gelu):
    j = pl.program_id(1)

    @pl.when(j == 0)
    def _():
        _modulated_norm_to(h_scr, x_ref, nw_ref, sc_ref, sh_ref)

    def tile(act):
        for lo in range(0, wa_ref.shape[1], SUB_N):
            w = wa_ref[:, lo:lo + SUB_N].astype(BF16)
            p = jnp.dot(h_scr[...], w, preferred_element_type=F32)
            act_ref[:, lo:lo + SUB_N] = act(p).astype(act_ref.dtype)
        p = jnp.dot(h_scr[...], wp_ref[...].astype(BF16), preferred_element_type=F32)
        pd_ref[...] = p.astype(pd_ref.dtype)

    @pl.when(j < n_gelu)
    def _():
        tile(_gelu_tanh)

    @pl.when(j >= n_gelu)
    def _():
        tile(_silu)


def _odd_in(xin, mods, nw, w):
    r, d = xin.shape
    tm, tn = 1024, 1024
    n_act = 4 * d // tn
    assert d // SUB_N == n_act
    pd_blk0 = 3 * d // SUB_N
    skip = d // tn
    tiles_per_batch = SEQ // tm
    mod_idx = lambda part: (lambda i, j: (i // tiles_per_batch, 0, part))
    act_col = lambda j: jnp.where(j < 3 * d // tn, j, j + skip)
    return pl.pallas_call(
        functools.partial(_odd_in_kernel, n_gelu=2 * d // tn),
        grid=(r // tm, n_act),
        in_specs=[pl.BlockSpec((tm, d), lambda i, j: (i, 0)),
                  pl.BlockSpec((1, d), lambda i, j: (0, 0)),
                  pl.BlockSpec((1, 1, d), mod_idx(0)),
                  pl.BlockSpec((1, 1, d), mod_idx(1)),
                  pl.BlockSpec((d, tn), lambda i, j: (0, act_col(j))),
                  pl.BlockSpec((d, SUB_N), lambda i, j: (0, pd_blk0 + j))],
        out_specs=[pl.BlockSpec((tm, tn), lambda i, j: (i, j)),
                   pl.BlockSpec((tm, SUB_N), lambda i, j: (i, j))],
        out_shape=[jax.ShapeDtypeStruct((r, 4 * d), BF16), jax.ShapeDtypeStruct((r, d), BF16)],
        scratch_shapes=[pltpu.VMEM((tm, d), BF16)],
        compiler_params=_params(("parallel", "arbitrary")),
        name="odd_in",
    )(xin, nw, mods, mods, w, w)


def _odd_mix_kernel(x_ref, gt_ref, u_ref, v_ref, zc_ref, pd_ref, zd_ref, lnw_ref, lnb_ref, ws_ref,
                    bs_ref, band_ref, icnt_ref, pw_ref, ps_ref, wo_ref, fnw_ref, out_ref,
                    acc, ln_scr, *, nk_c):
    kk = pl.program_id(1)
    tm = x_ref.shape[0]
    tk = u_ref.shape[1]
    gpb = tk // HEAD_DIM

    @pl.when(kk == 0)
    def _():
        v = v_ref[...].astype(F32)
        mu = jnp.mean(v, axis=-1, keepdims=True)
        xc = v - mu
        var = jnp.mean(xc * xc, axis=-1, keepdims=True)
        ln = (xc * lax.rsqrt(var + EPS) * lnw_ref[...] + lnb_ref[...]).astype(BF16)
        for blk in range(nk_c):
            ln_scr[blk] = ln[:, blk * tk:(blk + 1) * tk]
        acc[...] = jnp.zeros_like(acc)

    @pl.when(kk < nk_c)
    def _():
        ln = ln_scr[kk]
        cols = []
        for gi in range(gpb):
            w_g = ws_ref[gi]
            b_g = bs_ref[gi]
            rows = []
            for cc in range(tm // TOK_CHUNK):
                blk = ln[cc * TOK_CHUNK:(cc + 1) * TOK_CHUNK, gi * HEAD_DIM:(gi + 1) * HEAD_DIM]
                rows.append(jnp.dot(w_g, blk, preferred_element_type=F32) + b_g)
            cols.append(jnp.concatenate(rows, axis=0))
        s = jnp.concatenate(cols, axis=1)
        yc = u_ref[...] * s * zc_ref[...]
        acc[...] += jnp.dot(yc.astype(BF16), wo_ref[...], preferred_element_type=F32)

    @pl.when(kk >= nk_c)
    def _():
        parts = []
        for pi in range(tk // POOL_GW):
            cols = slice(pi * POOL_GW, (pi + 1) * POOL_GW)
            pd = pd_ref[:, cols]
            wsum = jnp.dot(band_ref[pi], pd, preferred_element_type=F32)
            icnt = jnp.concatenate([icnt_ref[pi]] * (POOL_GW // LANES), axis=1)
            diff = wsum * icnt - pd.astype(F32)
            y = jnp.dot(diff.astype(BF16), pw_ref[pi], preferred_element_type=F32)
            parts.append((y * ps_ref[:, cols] * zd_ref[:, cols]).astype(BF16))
        yd = jnp.concatenate(parts, axis=1)
        acc[...] += jnp.dot(yd, wo_ref[...], preferred_element_type=F32)

    @pl.when(kk == pl.num_programs(1) - 1)
    def _():
        xn = x_ref[...] + gt_ref[0] * acc[...]
        ms = jnp.mean(xn * xn, axis=-1, keepdims=True)
        out_ref[...] = xn * lax.rsqrt(ms + EPS) * fnw_ref[...]


def _pool_constants(tm):
    bands = np.zeros((POOL_GROUPS, tm, tm), np.float32)
    icnt = np.zeros((POOL_GROUPS, tm, LANES), np.float32)
    pos = np.arange(tm)
    row = pos // GRID_W
    col = pos % GRID_W
    for g, rad in enumerate(POOL_RADII):
        same = row[:, None] == row[None, :]
        bands[g] = (same & (np.abs(col[:, None] - col[None, :]) <= rad)).astype(np.float32)
        cnt = np.minimum(col + rad + 1, GRID_W) - np.maximum(col - rad, 0)
        icnt[g] = (1.0 / cnt.astype(np.float32))[:, None]
    return jnp.asarray(bands, BF16), jnp.asarray(icnt, F32)


def _odd_mix(xin, mods, p, p_d, ln_w, ln_b, w_s, b_s, pool_w, pool_scale, w_out, fnw):
    r, d = xin.shape
    tm, tk = 512, 1024
    nk_c = d // tk
    nk = 2 * nk_c
    gpb = tk // HEAD_DIM
    ppb = tk // POOL_GW
    tiles_per_batch = SEQ // tm
    bands, icnt = _pool_constants(tm)
    lo = lambda kk: jnp.minimum(kk, nk_c - 1)
    hi = lambda kk: jnp.maximum(kk - nk_c, 0)
    cblk = d // tk
    return pl.pallas_call(
        functools.partial(_odd_mix_kernel, nk_c=nk_c),
        grid=(r // tm, nk),
        in_specs=[pl.BlockSpec((tm, d), lambda i, kk: (i, 0)),
                  pl.BlockSpec((1, 1, d), lambda i, kk: (i // tiles_per_batch, 0, 2)),
                  pl.BlockSpec((tm, tk), lambda i, kk: (i, lo(kk))),
                  pl.BlockSpec((tm, d), lambda i, kk: (i, 1)),
                  pl.BlockSpec((tm, tk), lambda i, kk: (i, 2 * cblk + lo(kk))),
                  pl.BlockSpec((tm, tk), lambda i, kk: (i, hi(kk))),
                  pl.BlockSpec((tm, tk), lambda i, kk: (i, 3 * cblk + hi(kk))),
                  pl.BlockSpec((1, d), lambda i, kk: (0, 0)),
                  pl.BlockSpec((1, d), lambda i, kk: (0, 0)),
                  pl.BlockSpec((gpb, TOK_CHUNK, TOK_CHUNK), lambda i, kk: (lo(kk), 0, 0)),
                  pl.BlockSpec((gpb, TOK_CHUNK, LANES), lambda i, kk: (lo(kk), 0, 0)),
                  pl.BlockSpec((ppb, tm, tm), lambda i, kk: (hi(kk), 0, 0)),
                  pl.BlockSpec((ppb, tm, LANES), lambda i, kk: (hi(kk), 0, 0)),
                  pl.BlockSpec((ppb, POOL_GW, POOL_GW), lambda i, kk: (hi(kk), 0, 0)),
                  pl.BlockSpec((1, tk), lambda i, kk: (0, hi(kk))),
                  pl.BlockSpec((tk, d), lambda i, kk: (kk, 0)),
                  pl.BlockSpec((1, d), lambda i, kk: (0, 0))],
        out_specs=pl.BlockSpec((tm, d), lambda i, kk: (i, 0)),
        out_shape=jax.ShapeDtypeStruct((r, d), F32),
        scratch_shapes=[pltpu.VMEM((tm, d), F32), pltpu.VMEM((nk_c, tm, tk), BF16)],
        compiler_params=_params(("parallel", "arbitrary")),
        name="odd_mix",
    )(xin, mods, p, p, p, p_d, p, ln_w, ln_b, w_s, b_s, bands, icnt, pool_w, pool_scale, w_out, fnw)


def kernel(x, c, ctx, c_ctx, ada_w, ada_b, norm_w, e_w_in, e_conv_qkv, e_a_log, e_dt_bias, e_head_norm,
           e_conv_b, e_w_out, o_w_in, o_ln_w, o_ln_b, o_w_s, o_b_s, o_pool_w, o_pool_scale, o_w_out,
           final_norm_w):
    bsz, seq, d = x.shape
    rx = bsz * seq
    rc = bsz * ctx.shape[1]
    x2 = x.reshape(rx, d)
    ctx2 = ctx.reshape(rc, d)

    cc = jnp.concatenate([c, c_ctx[None, :], jnp.zeros((8 - bsz - 1, d), F32)], axis=0)
    mods = _adaln(cc, ada_w, ada_b)
    ctx_row = bsz

    w_in = e_w_in[0].astype(BF16)
    w_bchz = w_in[:, GATE_COL0 + 4 * HEADS:]
    m0 = mods[0].reshape(8, 1, 3 * d)
    nw0 = norm_w[0].reshape(1, d)

    gate_par = jnp.pad(jnp.stack([e_a_log[0].reshape(-1), e_dt_bias[0].reshape(-1)]),
                       ((0, 0), (0, LANES - 2 * HEADS)))
    qkv_x, gates_x = _qkv_proj(x2, m0, nw0, w_in, e_conv_qkv[0], gate_par,
                               rows_per_batch=seq, mod_row0=0, group=GRID_W)
    qkv_c, gates_c = _qkv_proj(ctx2, m0, nw0, w_in, e_conv_qkv[0], gate_par,
                               rows_per_batch=ctx.shape[1], mod_row0=ctx_row, group=ctx.shape[1])
    gz_x, yb_x = _mixb_proj(x2, m0, nw0, w_in, w_bchz, e_conv_b[0],
                            rows_per_batch=seq, mod_row0=0, group=GRID_W)
    gz_c, yb_c = _mixb_proj(ctx2, m0, nw0, w_in, w_bchz, e_conv_b[0],
                            rows_per_batch=ctx.shape[1], mod_row0=ctx_row, group=ctx.shape[1])
    o_c, o_x = _gdn(qkv_c, qkv_x, gates_c, gates_x,
                    e_head_norm[0].reshape(1, HEAD_DIM))
    w_out0 = e_w_out[0].astype(BF16)
    x2 = _even_out(x2, m0, o_x.reshape(rx, d), gz_x, yb_x, w_out0, rows_per_batch=seq, mod_row0=0)
    ctx2 = _even_out(ctx2, m0, o_c.reshape(rc, d), gz_c, yb_c, w_out0,
                     rows_per_batch=ctx.shape[1], mod_row0=ctx_row)

    m1 = mods[1].reshape(8, 1, 3 * d)
    p, p_d = _odd_in(x2, m1, norm_w[1].reshape(1, d), o_w_in[0])
    b_s = jnp.broadcast_to(o_b_s[0][:, :, None], (o_b_s.shape[1], TOK_CHUNK, LANES))
    out = _odd_mix(x2, m1, p, p_d, o_ln_w[0].reshape(1, d), o_ln_b[0].reshape(1, d),
                   o_w_s[0].astype(BF16), b_s, o_pool_w[0].astype(BF16),
                   o_pool_scale[0].reshape(1, d), o_w_out[0].astype(BF16),
                   final_norm_w.reshape(1, d))
    return out.reshape(bsz, seq, d)
```

```python
import functools
import math

import numpy as np
import jax
import jax.numpy as jnp
from jax import lax
from jax.experimental import pallas as pl
from jax.experimental.pallas import tpu as pltpu

F32 = jnp.float32
BF16 = jnp.bfloat16

D_MODEL = 2048
BATCH = 4
SEQ = 2048
CTX_LEN = 256
GRID_W = 64
EPS = 1e-6
HEADS = 16
HEAD_DIM = 128
QKV_W = 3 * D_MODEL
GATE_COL0 = QKV_W + D_MODEL
CHUNK = 64
SCAN_GROUPS = (12, 13, 11)
TOK_CHUNK = 128
POOL_GROUPS = 4
POOL_GW = D_MODEL // POOL_GROUPS
POOL_RADII = (1, 2, 4, 8)

LANES = 128
VMEM_LIMIT = 56 * 1024 * 1024
NORM_ROWS = 16
NORM_UNROLL = 16
SUB_N = 256


def _params(sem):
    return pltpu.CompilerParams(dimension_semantics=sem, vmem_limit_bytes=VMEM_LIMIT)


def _sigmoid(x):
    return 1.0 / (1.0 + jnp.exp(-x))


def _silu(x):
    h = 0.5 * x
    return h + h * jnp.tanh(h)


def _gelu_tanh(x):
    c = math.sqrt(2.0 / math.pi)
    h = 0.5 * x
    return h + h * jnp.tanh(x * (c + (c * 0.044715) * (x * x)))


def _modulated_norm_to(h_scr, x_ref, nw_ref, sc_ref, sh_ref):
    gain = nw_ref[...] * (1.0 + sc_ref[0])
    sh = sh_ref[0]

    step = NORM_ROWS * NORM_UNROLL

    def body(r, carry):
        rows = [pl.ds(pl.multiple_of(r * step + k * NORM_ROWS, NORM_ROWS), NORM_ROWS)
                for k in range(NORM_UNROLL)]
        scales = []
        for rw in rows:
            x = x_ref[rw, :]
            scales.append(lax.rsqrt(jnp.mean(x * x, axis=-1, keepdims=True) + EPS))
        for rw, rs in zip(rows, scales):
            h_scr[rw, :] = (x_ref[rw, :] * rs * gain + sh).astype(h_scr.dtype)
        return carry

    lax.fori_loop(0, x_ref.shape[0] // step, body, 0)


def _row_conv3(p, cw, group):
    tm = p.shape[0]
    row = lax.broadcasted_iota(jnp.int32, (tm, 1), 0) % group
    prev = jnp.where(row == 0, 0.0, pltpu.roll(p, 1, 0))
    nxt = jnp.where(row == group - 1, 0.0, pltpu.roll(p, tm - 1, 0))
    return prev * cw[0:1] + p * cw[1:2] + nxt * cw[2:3]


def _adaln_kernel(c_ref, w_ref, b_ref, o_ref):
    a = _silu(c_ref[...]).astype(BF16)
    o_ref[0] = jnp.dot(a, w_ref[0].astype(BF16), preferred_element_type=F32) + b_ref[0]


def _adaln(cc, ada_w, ada_b):
    depth, d, n = ada_w.shape
    tn = 1024
    return pl.pallas_call(
        _adaln_kernel,
        grid=(depth, n // tn),
        in_specs=[pl.BlockSpec((8, d), lambda l, j: (0, 0)),
                  pl.BlockSpec((1, d, tn), lambda l, j: (l, 0, j)),
                  pl.BlockSpec((1, 1, tn), lambda l, j: (l, 0, j))],
        out_specs=pl.BlockSpec((1, 8, tn), lambda l, j: (l, 0, j)),
        out_shape=jax.ShapeDtypeStruct((depth, 8, n), F32),
        compiler_params=_params(("parallel", "parallel")),
        name="adaln",
    )(cc, ada_w, ada_b.reshape(depth, 1, n))


def _qkv_kernel(x_ref, nw_ref, sh_ref, sc_ref, w_ref, wg_ref, cw_ref, gp_ref, qkv_ref, g_ref, h_scr,
                *, group, n_tiles):
    j = pl.program_id(1)

    @pl.when(j == 0)
    def _():
        _modulated_norm_to(h_scr, x_ref, nw_ref, sc_ref, sh_ref)

    def tile():
        p = jnp.dot(h_scr[...], w_ref[...], preferred_element_type=F32)
        y = _silu(_row_conv3(p, cw_ref[...], group))
        for hh in range(y.shape[1] // HEAD_DIM):
            qkv_ref[hh] = y[:, hh * HEAD_DIM:(hh + 1) * HEAD_DIM]

    def gates():
        raw = jnp.dot(h_scr[...], wg_ref[...], preferred_element_type=F32)
        z = raw + gp_ref[1:2]
        softplus = jnp.maximum(z, 0.0) + jnp.log1p(jnp.exp(-jnp.abs(z)))
        lane = lax.broadcasted_iota(jnp.int32, (1, LANES), 1)
        gates = jnp.where(lane < 2 * HEADS, -jnp.exp(gp_ref[0:1]) * softplus, _sigmoid(raw))
        ri = lax.broadcasted_iota(jnp.int32, (LANES, LANES), 0)
        ci = lax.broadcasted_iota(jnp.int32, (LANES, LANES), 1)
        same = (ri // CHUNK) == (ci // CHUNK)
        pre01 = jnp.where(same & (ci <= ri), 1.0, 0.0).astype(BF16)
        suf01 = jnp.where(same & (ci >= ri), 1.0, 0.0).astype(BF16)
        pieces = _split3_bf16(gates)
        for r0 in range(0, gates.shape[0], LANES):
            blk = [p[r0:r0 + LANES] for p in pieces]
            pre = [jnp.dot(pre01, b, preferred_element_type=F32) for b in blk]
            suf = [jnp.dot(suf01, b, preferred_element_type=F32) for b in blk]
            pre = (pre[0] + pre[1]) + pre[2]
            suf = (suf[0] + suf[1]) + suf[2]
            g_ref[r0:r0 + LANES] = jnp.where(lane < HEADS, pre,
                                             jnp.where(lane < 2 * HEADS, suf, gates[r0:r0 + LANES]))

    @pl.when(j < n_tiles - 1)
    def _():
        tile()

    @pl.when(j == n_tiles - 1)
    def _():
        tile()
        gates()


def _qkv_proj(xin, mods, nw, w_in, cw, gate_par, *, rows_per_batch, mod_row0, group):
    r, d = xin.shape
    tm, tn = 1024, 768
    n_tiles = QKV_W // tn
    hpt = tn // HEAD_DIM
    tiles_per_batch = max(rows_per_batch // tm, 1)

    def mod_idx(part):
        if rows_per_batch >= tm:
            return lambda i, j: (mod_row0 + i // tiles_per_batch, 0, part)
        return lambda i, j: (mod_row0, 0, part)

    return pl.pallas_call(
        functools.partial(_qkv_kernel, group=group, n_tiles=n_tiles),
        grid=(r // tm, n_tiles),
        in_specs=[pl.BlockSpec((tm, d), lambda i, j: (i, 0)),
                  pl.BlockSpec((1, d), lambda i, j: (0, 0)),
                  pl.BlockSpec((1, 1, d), mod_idx(0)),
                  pl.BlockSpec((1, 1, d), mod_idx(1)),
                  pl.BlockSpec((d, tn), lambda i, j: (0, j)),
                  pl.BlockSpec((d, LANES), lambda i, j: (0, GATE_COL0 // LANES)),
                  pl.BlockSpec((3, tn), lambda i, j: (0, j)),
                  pl.BlockSpec((2, LANES), lambda i, j: (0, 0))],
        out_specs=[pl.BlockSpec((hpt, tm, HEAD_DIM), lambda i, j: (j, i, 0)),
                   pl.BlockSpec((tm, LANES), lambda i, j: (i, 0))],
        out_shape=[jax.ShapeDtypeStruct((3 * HEADS, r, HEAD_DIM), F32),
                   jax.ShapeDtypeStruct((r, LANES), F32)],
        scratch_shapes=[pltpu.VMEM((tm, d), BF16)],
        compiler_params=_params(("parallel", "arbitrary")),
        name="qkv_proj",
    )(xin, nw, mods, mods, w_in, w_in, cw, gate_par)


def _mixb_kernel(x_ref, nw_ref, sh_ref, sc_ref, wz_ref, wb_ref, wc_ref, wh_ref, wzb_ref, cw_ref,
                 gz_ref, yb_ref, h_scr, *, group):
    j = pl.program_id(1)

    @pl.when(j == 0)
    def _():
        _modulated_norm_to(h_scr, x_ref, nw_ref, sc_ref, sh_ref)

    h = h_scr[...]
    dot = lambda w: jnp.dot(h, w[...], preferred_element_type=F32)
    gz_ref[...] = _silu(dot(wz_ref)).astype(gz_ref.dtype)
    conv = _row_conv3(dot(wc_ref) * dot(wh_ref), cw_ref[...], group)
    yb_ref[...] = (dot(wb_ref) * conv * _silu(dot(wzb_ref))).astype(yb_ref.dtype)


def _mixb_proj(xin, mods, nw, w_in, w_bchz, cw, *, rows_per_batch, mod_row0, group):
    r, d = xin.shape
    tm, tn = 1024, 256
    nj = d // tn
    za_blk0 = QKV_W // tn
    tiles_per_batch = max(rows_per_batch // tm, 1)

    def mod_idx(part):
        if rows_per_batch >= tm:
            return lambda i, j: (mod_row0 + i // tiles_per_batch, 0, part)
        return lambda i, j: (mod_row0, 0, part)

    wspec = lambda reg: pl.BlockSpec((d, tn), lambda i, j: (0, reg * nj + j))
    return pl.pallas_call(
        functools.partial(_mixb_kernel, group=group),
        grid=(r // tm, nj),
        in_specs=[pl.BlockSpec((tm, d), lambda i, j: (i, 0)),
                  pl.BlockSpec((1, d), lambda i, j: (0, 0)),
                  pl.BlockSpec((1, 1, d), mod_idx(0)),
                  pl.BlockSpec((1, 1, d), mod_idx(1)),
                  pl.BlockSpec((d, tn), lambda i, j: (0, za_blk0 + j)),
                  wspec(0), wspec(1), wspec(2), wspec(3),
                  pl.BlockSpec((3, tn), lambda i, j: (0, j))],
        out_specs=[pl.BlockSpec((tm, tn), lambda i, j: (i, j)),
                   pl.BlockSpec((tm, tn), lambda i, j: (i, j))],
        out_shape=[jax.ShapeDtypeStruct((r, d), BF16), jax.ShapeDtypeStruct((r, d), BF16)],
        scratch_shapes=[pltpu.VMEM((tm, d), BF16)],
        compiler_params=_params(("parallel", "arbitrary")),
        name="mixb_proj",
    )(xin, nw, mods, mods, w_in, w_bchz, w_bchz, w_bchz, w_bchz, cw)


def _split3_bf16(x):
    x1 = x.astype(BF16)
    r1 = x - x1.astype(F32)
    x2 = r1.astype(BF16)
    x3 = (r1 - x2.astype(F32)).astype(BF16)
    return x1, x2, x3


def _block_diag2(x, left):
    return jnp.concatenate([jnp.where(left, x, 0.0), jnp.where(left, 0.0, x)], axis=0).astype(BF16)


def _unit_tri_inverse_minus_eye(ms, left, ri, cj):
    n = ms[0].shape[0]
    ys = None
    s = 1
    while s < n:
        shift = s.bit_length() - 1
        br = lax.shift_right_logical(ri, shift)
        bc = lax.shift_right_logical(cj, shift)
        mask_lo = ((br & 1) == 1) & (bc == br - 1)
        mask_up = ((bc & 1) == 1) & (br == bc - 1)
        mask = (left & mask_lo) | (jnp.logical_not(left) & mask_up)
        cms = [jnp.where(mask, m, 0.0) for m in ms]
        if ys is None:
            ys = [-cm for cm in cms]
        else:
            yc = [jnp.dot(y.astype(BF16), _block_diag2(cm, left), preferred_element_type=F32)
                  for cm, y in zip(cms, ys)]
            yield
            ps = [cm + t for cm, t in zip(cms, yc)]
            py = [jnp.dot(p.astype(BF16), _block_diag2(y, left), preferred_element_type=F32)
                  for y, p in zip(ys, ps)]
            yield
            ys = [y - p - t for y, p, t in zip(ys, ps, py)]
        s *= 2
    return ys


def _run_interleaved(*gens):
    active = list(gens)
    while active:
        for g in list(active):
            try:
                next(g)
            except StopIteration:
                active.remove(g)


def _gdn_kernel(qc_ref, kc_ref, vc_ref, qx_ref, kx_ref, vx_ref, gc_ref, gx_ref,
                hn_ref, oc_ref, ox_ref,
                qs, ks, vs, gsc, bsc, nq_s, b_s, op_s, gl_s, oacc, st_scr):
    k = pl.program_id(0)
    h = jnp.minimum(k, pl.num_programs(0) - 2) % HEADS
    cur_slot = k % 2
    prev_slot = 1 - cur_slot
    tc = qc_ref.shape[1]
    tx = qx_ref.shape[1]
    t = tc + tx
    nsteps = t // CHUNK
    nc_ctx = tc // CHUNK
    c = CHUNK

    def l2n(a):
        return a * lax.rsqrt(jnp.sum(a * a, axis=-1, keepdims=True) + EPS)

    @pl.when(k == 0)
    def _():
        for ref in (nq_s, b_s, op_s, gl_s, oacc, st_scr):
            ref[...] = jnp.zeros_like(ref)

    qscale = HEAD_DIM ** -0.5
    qs[0:tc] = l2n(qc_ref[0]) * qscale
    qs[tc:t] = l2n(qx_ref[0]) * qscale
    ks[0:tc] = l2n(kc_ref[0])
    ks[tc:t] = l2n(kx_ref[0])
    vs[0:tc] = vc_ref[0]
    vs[tc:t] = vx_ref[0]

    lane = lax.broadcasted_iota(jnp.int32, (1, LANES), 1)
    for d in range(2):
        for ref, lo, hi in ((gc_ref, 0, tc), (gx_ref, tc, t)):
            gates = ref[...]
            g = jnp.sum(jnp.where(lane == d * HEADS + h, gates, 0.0), axis=-1, keepdims=True)
            b = jnp.sum(jnp.where(lane == 2 * HEADS + d * HEADS + h, gates, 0.0), axis=-1,
                        keepdims=True)
            gsc[d, lo:hi] = jnp.broadcast_to(g, (hi - lo, LANES))
            bsc[d, lo:hi] = jnp.broadcast_to(b, (hi - lo, LANES))

    def rev_chunk(s):
        return nc_ctx - 1 - s if s < nc_ctx else nsteps + nc_ctx - 1 - s

    ri = lax.broadcasted_iota(jnp.int32, (c, 2 * c), 0)
    lane2 = lax.broadcasted_iota(jnp.int32, (c, 2 * c), 1)
    left = lane2 < c
    cj = lane2 & (c - 1)
    strict = (left & (cj < ri)) | (jnp.logical_not(left) & (cj > ri))
    incl = (left & (cj <= ri)) | (jnp.logical_not(left) & (cj >= ri))
    left_sq = lax.broadcasted_iota(jnp.int32, (2 * c, 2 * c), 1) < c

    def prep(steps):
        units = range(len(steps))
        rows = [[slice(n * c, (n + 1) * c) for n in (s, rev_chunk(s))] for s in steps]
        qv = [[qs[r] for r in rows[uu]] for uu in units]
        kv = [[ks[r] for r in rows[uu]] for uu in units]
        vv = [[vs[r] for r in rows[uu]] for uu in units]
        beta = [[bsc[d, rows[uu][d]] for d in range(2)] for uu in units]
        kb = [[kv[uu][d] * beta[uu][d] for d in range(2)] for uu in units]
        gk = [lax.dot_general(
                  jnp.concatenate([kb[uu][0], qv[uu][0], kb[uu][1], qv[uu][1]], axis=0).astype(BF16),
                  jnp.concatenate(kv[uu], axis=0).astype(BF16),
                  (((1,), (1,)), ((), ())), preferred_element_type=F32)
              for uu in units]
        yield
        gcb = [[gsc[d, rows[uu][d]] for d in range(2)] for uu in units]
        ms, avs = [], []
        for uu in units:
            col = jnp.where(left, gcb[uu][0], gcb[uu][1])
            row = jnp.transpose(jnp.concatenate(gcb[uu], axis=0))[0:c]
            decay = jnp.exp(jnp.where(incl, col - row, 0.0))
            kk = jnp.where(left, gk[uu][0:c], gk[uu][2 * c:3 * c])
            qk = jnp.where(left, gk[uu][c:2 * c], gk[uu][3 * c:4 * c])
            ms.append(jnp.where(strict, kk * decay, 0.0))
            avs.append(jnp.where(incl, qk * decay, 0.0))
        ys = yield from _unit_tri_inverse_minus_eye(ms, left, ri, cj)
        egc = [[jnp.exp(g) for g in gcb[uu]] for uu in units]
        rs = [[jnp.concatenate([kb[uu][d] * egc[uu][d], vv[uu][d] * beta[uu][d]], axis=1)
               for d in range(2)] for uu in units]
        corr = [jnp.dot(_block_diag2(ys[uu], left),
                        jnp.concatenate(rs[uu], axis=0).astype(BF16), preferred_element_type=F32)
                for uu in units]
        yield
        wus, aks, g_edges = [], [], []
        for uu in units:
            kgs, edges = [], []
            for d in range(2):
                edge = c - 1 if d == 0 else 0
                edges.append(gcb[uu][d][edge:edge + 1])
                kgs.append(kv[uu][d] * jnp.exp(edges[d] - gcb[uu][d]))
            g_edges.append(edges)
            wus.append([rs[uu][d] + corr[uu][d * c:(d + 1) * c] for d in range(2)])
            kgt = jnp.transpose(jnp.concatenate(kgs, axis=0))
            aks.append(jnp.concatenate(
                [jnp.where(left, avs[uu], 0.0), jnp.where(left_sq, kgt, 0.0),
                 jnp.where(left, 0.0, avs[uu]), jnp.where(left_sq, 0.0, kgt)], axis=0).astype(BF16))
        prods = [jnp.dot(aks[uu], jnp.concatenate(wus[uu], axis=0).astype(BF16),
                         preferred_element_type=F32) for uu in units]
        yield
        for uu in units:
            for d in range(2):
                x = prods[uu][d * (c + HEAD_DIM):(d + 1) * (c + HEAD_DIM)]
                a_w, a_u = x[0:c, 0:HEAD_DIM], x[0:c, HEAD_DIM:2 * HEAD_DIM]
                k_w, k_u = x[c:c + HEAD_DIM, 0:HEAD_DIM], x[c:c + HEAD_DIM, HEAD_DIM:2 * HEAD_DIM]
                idx = 2 * steps[uu] + d
                nq_s[idx] = jnp.concatenate([-k_w, qv[uu][d] * egc[uu][d] - a_w],
                                            axis=0).astype(BF16)
                b_s[idx] = k_u
                op_s[idx] = a_u
                gl_s[idx] = jnp.broadcast_to(jnp.exp(g_edges[uu][d]), (8, LANES))

    def scan(steps, states, slot, written):
        for s in steps:
            idxs = (2 * s, 2 * s + 1)
            xs = [jnp.dot(nq_s[idx], st.astype(BF16), preferred_element_type=F32)
                  for idx, st in zip(idxs, states)]
            yield
            for n, idx, x in zip((s, rev_chunk(s)), idxs, xs):
                o = x[HEAD_DIM:HEAD_DIM + c] + op_s[idx]
                rows = slice(n * c, (n + 1) * c)
                oacc[slot, rows] = oacc[slot, rows] + o if n in written else o
                written.add(n)
            states[:] = [st * gl_s[idx][0:1] + x[0:HEAD_DIM] + b_s[idx]
                         for idx, st, x in zip(idxs, states, xs)]

    bounds = np.cumsum((0,) + SCAN_GROUPS)
    assert bounds[-1] == nsteps
    groups = [list(range(lo, hi)) for lo, hi in zip(bounds[:-1], bounds[1:])]
    early = [s for g in groups[:-1] for s in g]
    written_early = {n for s in early for n in (s, rev_chunk(s))}

    prev_states = [st_scr[0], st_scr[1]]
    _run_interleaved(prep(groups[0]), scan(groups[-1], prev_states, prev_slot, set(written_early)))
    o = oacc[prev_slot]
    on = o * lax.rsqrt(jnp.mean(o * o, axis=-1, keepdims=True) + EPS) * hn_ref[...]
    oc_ref[0] = on[0:tc].astype(oc_ref.dtype)
    ox_ref[0] = on[tc:t].astype(ox_ref.dtype)

    states = [jnp.zeros((HEAD_DIM, HEAD_DIM), F32)] * 2
    written = set()
    for gi in range(1, len(groups)):
        _run_interleaved(prep(groups[gi]), scan(groups[gi - 1], states, cur_slot, written))
    assert written == written_early
    st_scr[0] = states[0]
    st_scr[1] = states[1]


def _gdn(qkv_c, qkv_x, gates_c, gates_x, head_norm):
    tc, tx = CTX_LEN, SEQ
    t = tc + tx
    nidx = 2 * (t // CHUNK)
    nprob = BATCH * HEADS
    cur = lambda k: jnp.minimum(k, nprob - 1)
    prev = lambda k: jnp.maximum(k - 1, 0)
    part = lambda rows, p: pl.BlockSpec(
        (1, rows, HEAD_DIM), lambda k: (p * HEADS + cur(k) % HEADS, cur(k) // HEADS, 0))
    return pl.pallas_call(
        _gdn_kernel,
        grid=(nprob + 1,),
        in_specs=[part(tc, 0), part(tc, 1), part(tc, 2),
                  part(tx, 0), part(tx, 1), part(tx, 2),
                  pl.BlockSpec((tc, LANES), lambda k: (cur(k) // HEADS, 0)),
                  pl.BlockSpec((tx, LANES), lambda k: (cur(k) // HEADS, 0)),
                  pl.BlockSpec((1, HEAD_DIM), lambda k: (0, 0))],
        out_specs=[pl.BlockSpec((1, tc, HEAD_DIM), lambda k: (prev(k) // HEADS, 0, prev(k) % HEADS)),
                   pl.BlockSpec((1, tx, HEAD_DIM), lambda k: (prev(k) // HEADS, 0, prev(k) % HEADS))],
        out_shape=[jax.ShapeDtypeStruct((BATCH, tc, D_MODEL), BF16),
                   jax.ShapeDtypeStruct((BATCH, tx, D_MODEL), BF16)],
        scratch_shapes=[pltpu.VMEM((t, HEAD_DIM), F32), pltpu.VMEM((t, HEAD_DIM), F32),
                        pltpu.VMEM((t, HEAD_DIM), F32),
                        pltpu.VMEM((2, t, LANES), F32), pltpu.VMEM((2, t, LANES), F32),
                        pltpu.VMEM((nidx, HEAD_DIM + CHUNK, HEAD_DIM), BF16),
                        pltpu.VMEM((nidx, HEAD_DIM, HEAD_DIM), F32),
                        pltpu.VMEM((nidx, CHUNK, HEAD_DIM), F32),
                        pltpu.VMEM((nidx, 8, LANES), F32),
                        pltpu.VMEM((2, t, HEAD_DIM), F32),
                        pltpu.VMEM((2, HEAD_DIM, HEAD_DIM), F32)],
        compiler_params=_params(("arbitrary",)),
        name="gdn",
    )(qkv_c, qkv_c, qkv_c, qkv_x, qkv_x, qkv_x, gates_c, gates_x, head_norm)


def _even_out_kernel(x_ref, gt_ref, o_ref, gz_ref, yb_ref, w_ref, out_ref, a_scr):
    j = pl.program_id(1)

    def project():
        y = jnp.dot(a_scr[...], w_ref[...], preferred_element_type=F32)
        out_ref[...] = x_ref[...] + gt_ref[0] * y

    @pl.when(j == 0)
    def _():
        d = o_ref.shape[1]
        a_scr[:, 0:d] = o_ref[...] * gz_ref[...]
        a_scr[:, d:2 * d] = yb_ref[...]
        project()

    @pl.when(j > 0)
    def _():
        project()


def _even_out(xin, mods, o, gz, yb, w_out, *, rows_per_batch, mod_row0):
    r, d = xin.shape
    tm = min(1024, rows_per_batch)
    tn = 512
    nj = d // tn
    tiles_per_batch = rows_per_batch // tm
    fixed = mod_row0 >= BATCH
    gt_idx = ((lambda i, j: (mod_row0, 0, 2 * nj + j)) if fixed
              else (lambda i, j: (i // tiles_per_batch, 0, 2 * nj + j)))
    return pl.pallas_call(
        _even_out_kernel,
        grid=(r // tm, nj),
        in_specs=[pl.BlockSpec((tm, tn), lambda i, j: (i, j)),
                  pl.BlockSpec((1, 1, tn), gt_idx),
                  pl.BlockSpec((tm, d), lambda i, j: (i, 0)),
                  pl.BlockSpec((tm, d), lambda i, j: (i, 0)),
                  pl.BlockSpec((tm, d), lambda i, j: (i, 0)),
                  pl.BlockSpec((2 * d, tn), lambda i, j: (0, j))],
        out_specs=pl.BlockSpec((tm, tn), lambda i, j: (i, j)),
        out_shape=jax.ShapeDtypeStruct((r, d), F32),
        scratch_shapes=[pltpu.VMEM((tm, 2 * d), BF16)],
        compiler_params=_params(("parallel", "arbitrary")),
        name="even_out",
    )(xin, mods, o, gz, yb, w_out)


def _odd_in_kernel(x_ref, nw_ref, sh_ref, sc_ref, wa_ref, wp_ref, act_ref, pd_ref, h_scr,
                   *, n_gelu):
    j = pl.program_id(1)

    @pl.when(j == 0)
    def _():
        _modulated_norm_to(h_scr, x_ref, nw_ref, sc_ref, sh_ref)

    def tile(act):
        for lo in range(0, wa_ref.shape[1], SUB_N):
            w = wa_ref[:, lo:lo + SUB_N].astype(BF16)
            p = jnp.dot(h_scr[...], w, preferred_element_type=F32)
            act_ref[:, lo:lo + SUB_N] = act(p).astype(act_ref.dtype)
        p = jnp.dot(h_scr[...], wp_ref[...].astype(BF16), preferred_element_type=F32)
        pd_ref[...] = p.astype(pd_ref.dtype)

    @pl.when(j < n_gelu)
    def _():
        tile(_gelu_tanh)

    @pl.when(j >= n_gelu)
    def _():
        tile(_silu)


def _odd_in(xin, mods, nw, w):
    r, d = xin.shape
    tm, tn = 1024, 1024
    n_act = 4 * d // tn
    assert d // SUB_N == n_act
    pd_blk0 = 3 * d // SUB_N
    skip = d // tn
    tiles_per_batch = SEQ // tm
    mod_idx = lambda part: (lambda i, j: (i // tiles_per_batch, 0, part))
    act_col = lambda j: jnp.where(j < 3 * d // tn, j, j + skip)
    return pl.pallas_call(
        functools.partial(_odd_in_kernel, n_gelu=2 * d // tn),
        grid=(r // tm, n_act),
        in_specs=[pl.BlockSpec((tm, d), lambda i, j: (i, 0)),
                  pl.BlockSpec((1, d), lambda i, j: (0, 0)),
                  pl.BlockSpec((1, 1, d), mod_idx(0)),
                  pl.BlockSpec((1, 1, d), mod_idx(1)),
                  pl.BlockSpec((d, tn), lambda i, j: (0, act_col(j))),
                  pl.BlockSpec((d, SUB_N), lambda i, j: (0, pd_blk0 + j))],
        out_specs=[pl.BlockSpec((tm, tn), lambda i, j: (i, j)),
                   pl.BlockSpec((tm, SUB_N), lambda i, j: (i, j))],
        out_shape=[jax.ShapeDtypeStruct((r, 4 * d), BF16), jax.ShapeDtypeStruct((r, d), BF16)],
        scratch_shapes=[pltpu.VMEM((tm, d), BF16)],
        compiler_params=_params(("parallel", "arbitrary")),
        name="odd_in",
    )(xin, nw, mods, mods, w, w)


def _odd_mix_kernel(x_ref, gt_ref, u_ref, v_ref, zc_ref, pd_ref, zd_ref, lnw_ref, lnb_ref, ws_ref,
                    bs_ref, band_ref, icnt_ref, pw_ref, ps_ref, wo_ref, fnw_ref, out_ref,
                    acc, ln_scr, *, nk_c):
    kk = pl.program_id(1)
    tm = x_ref.shape[0]
    tk = u_ref.shape[1]
    gpb = tk // HEAD_DIM

    @pl.when(kk == 0)
    def _():
        v = v_ref[...].astype(F32)
        mu = jnp.mean(v, axis=-1, keepdims=True)
        xc = v - mu
        var = jnp.mean(xc * xc, axis=-1, keepdims=True)
        ln = (xc * lax.rsqrt(var + EPS) * lnw_ref[...] + lnb_ref[...]).astype(BF16)
        for blk in range(nk_c):
            ln_scr[blk] = ln[:, blk * tk:(blk + 1) * tk]
        acc[...] = jnp.zeros_like(acc)

    @pl.when(kk < nk_c)
    def _():
        ln = ln_scr[kk]
        cols = []
        for gi in range(gpb):
            w_g = ws_ref[gi]
            b_g = bs_ref[gi]
            rows = []
            for cc in range(tm // TOK_CHUNK):
                blk = ln[cc * TOK_CHUNK:(cc + 1) * TOK_CHUNK, gi * HEAD_DIM:(gi + 1) * HEAD_DIM]
                rows.append(jnp.dot(w_g, blk, preferred_element_type=F32) + b_g)
            cols.append(jnp.concatenate(rows, axis=0))
        s = jnp.concatenate(cols, axis=1)
        yc = u_ref[...] * s * zc_ref[...]
        acc[...] += jnp.dot(yc.astype(BF16), wo_ref[...], preferred_element_type=F32)

    @pl.when(kk >= nk_c)
    def _():
        parts = []
        for pi in range(tk // POOL_GW):
            cols = slice(pi * POOL_GW, (pi + 1) * POOL_GW)
            pd = pd_ref[:, cols]
            wsum = jnp.dot(band_ref[pi], pd, preferred_element_type=F32)
            icnt = jnp.concatenate([icnt_ref[pi]] * (POOL_GW // LANES), axis=1)
            diff = wsum * icnt - pd.astype(F32)
            y = jnp.dot(diff.astype(BF16), pw_ref[pi], preferred_element_type=F32)
            parts.append((y * ps_ref[:, cols] * zd_ref[:, cols]).astype(BF16))
        yd = jnp.concatenate(parts, axis=1)
        acc[...] += jnp.dot(yd, wo_ref[...], preferred_element_type=F32)

    @pl.when(kk == pl.num_programs(1) - 1)
    def _():
        xn = x_ref[...] + gt_ref[0] * acc[...]
        ms = jnp.mean(xn * xn, axis=-1, keepdims=True)
        out_ref[...] = xn * lax.rsqrt(ms + EPS) * fnw_ref[...]


def _pool_constants(tm):
    bands = np.zeros((POOL_GROUPS, tm, tm), np.float32)
    icnt = np.zeros((POOL_GROUPS, tm, LANES), np.float32)
    pos = np.arange(tm)
    row = pos // GRID_W
    col = pos % GRID_W
    for g, rad in enumerate(POOL_RADII):
        same = row[:, None] == row[None, :]
        bands[g] = (same & (np.abs(col[:, None] - col[None, :]) <= rad)).astype(np.float32)
        cnt = np.minimum(col + rad + 1, GRID_W) - np.maximum(col - rad, 0)
        icnt[g] = (1.0 / cnt.astype(np.float32))[:, None]
    return jnp.asarray(bands, BF16), jnp.asarray(icnt, F32)


def _odd_mix(xin, mods, p, p_d, ln_w, ln_b, w_s, b_s, pool_w, pool_scale, w_out, fnw):
    r, d = xin.shape
    tm, tk = 512, 1024
    nk_c = d // tk
    nk = 2 * nk_c
    gpb = tk // HEAD_DIM
    ppb = tk // POOL_GW
    tiles_per_batch = SEQ // tm
    bands, icnt = _pool_constants(tm)
    lo = lambda kk: jnp.minimum(kk, nk_c - 1)
    hi = lambda kk: jnp.maximum(kk - nk_c, 0)
    cblk = d // tk
    return pl.pallas_call(
        functools.partial(_odd_mix_kernel, nk_c=nk_c),
        grid=(r // tm, nk),
        in_specs=[pl.BlockSpec((tm, d), lambda i, kk: (i, 0)),
                  pl.BlockSpec((1, 1, d), lambda i, kk: (i // tiles_per_batch, 0, 2)),
                  pl.BlockSpec((tm, tk), lambda i, kk: (i, lo(kk))),
                  pl.BlockSpec((tm, d), lambda i, kk: (i, 1)),
                  pl.BlockSpec((tm, tk), lambda i, kk: (i, 2 * cblk + lo(kk))),
                  pl.BlockSpec((tm, tk), lambda i, kk: (i, hi(kk))),
                  pl.BlockSpec((tm, tk), lambda i, kk: (i, 3 * cblk + hi(kk))),
                  pl.BlockSpec((1, d), lambda i, kk: (0, 0)),
                  pl.BlockSpec((1, d), lambda i, kk: (0, 0)),
                  pl.BlockSpec((gpb, TOK_CHUNK, TOK_CHUNK), lambda i, kk: (lo(kk), 0, 0)),
                  pl.BlockSpec((gpb, TOK_CHUNK, LANES), lambda i, kk: (lo(kk), 0, 0)),
                  pl.BlockSpec((ppb, tm, tm), lambda i, kk: (hi(kk), 0, 0)),
                  pl.BlockSpec((ppb, tm, LANES), lambda i, kk: (hi(kk), 0, 0)),
                  pl.BlockSpec((ppb, POOL_GW, POOL_GW), lambda i, kk: (hi(kk), 0, 0)),
                  pl.BlockSpec((1, tk), lambda i, kk: (0, hi(kk))),
                  pl.BlockSpec((tk, d), lambda i, kk: (kk, 0)),
                  pl.BlockSpec((1, d), lambda i, kk: (0, 0))],
        out_specs=pl.BlockSpec((tm, d), lambda i, kk: (i, 0)),
        out_shape=jax.ShapeDtypeStruct((r, d), F32),
        scratch_shapes=[pltpu.VMEM((tm, d), F32), pltpu.VMEM((nk_c, tm, tk), BF16)],
        compiler_params=_params(("parallel", "arbitrary")),
        name="odd_mix",
    )(xin, mods, p, p, p, p_d, p, ln_w, ln_b, w_s, b_s, bands, icnt, pool_w, pool_scale, w_out, fnw)


def kernel(x, c, ctx, c_ctx, ada_w, ada_b, norm_w, e_w_in, e_conv_qkv, e_a_log, e_dt_bias, e_head_norm,
           e_conv_b, e_w_out, o_w_in, o_ln_w, o_ln_b, o_w_s, o_b_s, o_pool_w, o_pool_scale, o_w_out,
           final_norm_w):
    bsz, seq, d = x.shape
    rx = bsz * seq
    rc = bsz * ctx.shape[1]
    x2 = x.reshape(rx, d)
    ctx2 = ctx.reshape(rc, d)

    cc = jnp.concatenate([c, c_ctx[None, :], jnp.zeros((8 - bsz - 1, d), F32)], axis=0)
    mods = _adaln(cc, ada_w, ada_b)
    ctx_row = bsz

    w_in = e_w_in[0].astype(BF16)
    w_bchz = w_in[:, GATE_COL0 + 4 * HEADS:]
    m0 = mods[0].reshape(8, 1, 3 * d)
    nw0 = norm_w[0].reshape(1, d)

    gate_par = jnp.pad(jnp.stack([e_a_log[0].reshape(-1), e_dt_bias[0].reshape(-1)]),
                       ((0, 0), (0, LANES - 2 * HEADS)))
    qkv_x, gates_x = _qkv_proj(x2, m0, nw0, w_in, e_conv_qkv[0], gate_par,
                               rows_per_batch=seq, mod_row0=0, group=GRID_W)
    qkv_c, gates_c = _qkv_proj(ctx2, m0, nw0, w_in, e_conv_qkv[0], gate_par,
                               rows_per_batch=ctx.shape[1], mod_row0=ctx_row, group=ctx.shape[1])
    gz_x, yb_x = _mixb_proj(x2, m0, nw0, w_in, w_bchz, e_conv_b[0],
                            rows_per_batch=seq, mod_row0=0, group=GRID_W)
    gz_c, yb_c = _mixb_proj(ctx2, m0, nw0, w_in, w_bchz, e_conv_b[0],
                            rows_per_batch=ctx.shape[1], mod_row0=ctx_row, group=ctx.shape[1])
    o_c, o_x = _gdn(qkv_c, qkv_x, gates_c, gates_x,
                    e_head_norm[0].reshape(1, HEAD_DIM))
    w_out0 = e_w_out[0].astype(BF16)
    x2 = _even_out(x2, m0, o_x.reshape(rx, d), gz_x, yb_x, w_out0, rows_per_batch=seq, mod_row0=0)
    ctx2 = _even_out(ctx2, m0, o_c.reshape(rc, d), gz_c, yb_c, w_out0,
                     rows_per_batch=ctx.shape[1], mod_row0=ctx_row)

    m1 = mods[1].reshape(8, 1, 3 * d)
    p, p_d = _odd_in(x2, m1, norm_w[1].reshape(1, d), o_w_in[0])
    b_s = jnp.broadcast_to(o_b_s[0][:, :, None], (o_b_s.shape[1], TOK_CHUNK, LANES))
    out = _odd_mix(x2, m1, p, p_d, o_ln_w[0].reshape(1, d), o_ln_b[0].reshape(1, d),
                   o_w_s[0].astype(BF16), b_s, o_pool_w[0].astype(BF16),
                   o_pool_scale[0].reshape(1, d), o_w_out[0].astype(BF16),
                   final_norm_w.reshape(1, d))
    return out.reshape(bsz, seq, d)
```
